```python
import math
import jax, jax.numpy as jnp
from jax import lax
import numpy as np

D_MODEL = 1024
BATCH = 2
SEQ = 16384
DEPTH = 4

F32 = jnp.float32
EPS = 1e-6
NEG = -1e30

N_BRANCH = 4
BRANCH_W = 256
Q_BLOCK = 128

MLA_HEADS = 4
MLA_NOPE = 64
MLA_ROPE = 32
MLA_V = 64
MLA_Q_LORA = 256
MLA_KV_LORA = 128
ROPE_THETA = 10000.0

SGU_GROUPS = 4
SGU_CHUNK = 128
SGU_W = BRANCH_W

POOL_WINDOWS = (2, 4, 8, 16)
N_POOL = 4
POOL_GROUP_W = BRANCH_W // N_POOL
POOL_MAXW = 16

MOBA_HEADS = 4
MOBA_HEAD_DIM = 64
MOBA_BLOCK = 256
MOBA_TOPK = 3

N_EXPERTS = 16
N_GROUPS = 4
EXPERTS_PER_GROUP = N_EXPERTS // N_GROUPS
MOE_TOPK = 2
EXPERT_FF = 512
SHARED_FF = 512

MLA_IN = MLA_Q_LORA + MLA_KV_LORA + MLA_ROPE
SGU_IN = 2 * SGU_W
POOL_IN = BRANCH_W
MOBA_IN = 3 * MOBA_HEADS * MOBA_HEAD_DIM
IN_W = MLA_IN + SGU_IN + POOL_IN + MOBA_IN

kernel_name = "hybrid_mla_sgu_pool_moba_moe"


def rms_norm(x, g):
    xf = x.astype(F32)
    y = xf * lax.rsqrt(jnp.mean(xf * xf, axis=-1, keepdims=True) + EPS)
    return (y * g.astype(F32)).astype(x.dtype)


def apply_rope(x, pos):
    half = x.shape[-1] // 2
    inv = ROPE_THETA ** (-jnp.arange(half, dtype=F32) / half)
    ang = pos.astype(F32)[:, None] * inv[None, :]
    cos = jnp.cos(ang)[None, :, None, :]
    sin = jnp.sin(ang)[None, :, None, :]
    x1 = x[..., :half].astype(F32)
    x2 = x[..., half:].astype(F32)
    return jnp.concatenate([x1 * cos - x2 * sin, x2 * cos + x1 * sin], axis=-1).astype(x.dtype)


def dense_causal_attention(q, k, v, scale):
    B, S, H, dk = q.shape
    dv = v.shape[-1]
    nq = S // Q_BLOCK
    qb = q.reshape(B, nq, Q_BLOCK, H, dk).transpose(1, 0, 3, 2, 4)
    kt = k.transpose(0, 2, 1, 3)
    vt = v.transpose(0, 2, 1, 3)
    kpos = jnp.arange(S)

    def one(args):
        i, qi = args
        qpos = i * Q_BLOCK + jnp.arange(Q_BLOCK)
        s = jnp.einsum('bhqd,bhkd->bhqk', qi, kt, preferred_element_type=F32) * scale
        s = jnp.where(kpos[None, :] <= qpos[:, None], s, NEG)
        p = jax.nn.softmax(s, axis=-1).astype(vt.dtype)
        return jnp.einsum('bhqk,bhkd->bhqd', p, vt)

    o = lax.map(one, (jnp.arange(nq), qb))
    return o.transpose(1, 0, 3, 2, 4).reshape(B, S, H, dv)


def mla_mixer(cols, q_norm, w_q_up, kv_norm, w_kv_up, q_gain, k_gain):
    B, S, _ = cols.shape
    H = MLA_HEADS
    c_q = rms_norm(cols[..., :MLA_Q_LORA], q_norm)
    q = (c_q @ w_q_up).reshape(B, S, H, MLA_NOPE + MLA_ROPE)
    c_kv = rms_norm(cols[..., MLA_Q_LORA:MLA_Q_LORA + MLA_KV_LORA], kv_norm)
    k_pe = cols[..., MLA_Q_LORA + MLA_KV_LORA:]
    kv = (c_kv @ w_kv_up).reshape(B, S, H, MLA_NOPE + MLA_V)
    k_nope, v = kv[..., :MLA_NOPE], kv[..., MLA_NOPE:]
    k = jnp.concatenate([k_nope, jnp.broadcast_to(k_pe[:, :, None, :], (B, S, H, MLA_ROPE))], axis=-1)
    q = rms_norm(q, q_gain)
    k = rms_norm(k, k_gain)
    pos = jnp.arange(S)
    q = jnp.concatenate([q[..., :MLA_NOPE], apply_rope(q[..., MLA_NOPE:], pos)], axis=-1)
    k = jnp.concatenate([k[..., :MLA_NOPE], apply_rope(k[..., MLA_NOPE:], pos)], axis=-1)
    o = dense_causal_attention(q, k, v, (MLA_NOPE + MLA_ROPE) ** -0.5)
    return o.reshape(B, S, H * MLA_V)


def sgu_mixer(cols, norm_g, w_s, b_s):
    B, S, _ = cols.shape
    z = jax.nn.gelu(cols)
    u, vv = z[..., :SGU_W], z[..., SGU_W:]
    vv = rms_norm(vv, norm_g)
    nc = S // SGU_CHUNK
    gw = SGU_W // SGU_GROUPS
    vc = vv.reshape(B, nc, SGU_CHUNK, SGU_GROUPS, gw)
    mask = jnp.tril(jnp.ones((SGU_CHUNK, SGU_CHUNK), dtype=bool))
    w = jnp.where(mask[None], w_s, jnp.zeros_like(w_s))
    y = jnp.einsum('gts,bcsgd->bctgd', w, vc) + b_s.T[None, None, :, :, None]
    return u * y.reshape(B, S, SGU_W)


def pool_mixer(cols, w_pool, b_pool, scale):
    B, S, _ = cols.shape
    xf = cols.reshape(B, S, N_POOL, POOL_GROUP_W).astype(F32)
    c = jnp.cumsum(xf, axis=1)
    c = jnp.pad(c, ((0, 0), (POOL_MAXW, 0), (0, 0), (0, 0)))
    t = jnp.arange(S)
    outs = []
    for gi, w in enumerate(POOL_WINDOWS):
        upper = c[:, POOL_MAXW:, gi]
        lower = c[:, POOL_MAXW - w:POOL_MAXW - w + S, gi]
        cnt = jnp.minimum(t + 1, w).astype(F32)[None, :, None]
        outs.append((upper - lower) / cnt - xf[:, :, gi])
    p = jnp.stack(outs, axis=2).astype(cols.dtype)
    y = jnp.einsum('bsgi,gio->bsgo', p, w_pool).reshape(B, S, BRANCH_W) + b_pool
    return y * scale


def moba_mixer(cols, q_gain, k_gain):
    B, S, _ = cols.shape
    H, hd, BLK = MOBA_HEADS, MOBA_HEAD_DIM, MOBA_BLOCK
    qkv = cols.reshape(B, S, 3, H, hd)
    q = rms_norm(qkv[:, :, 0], q_gain)
    k = rms_norm(qkv[:, :, 1], k_gain)
    v = qkv[:, :, 2]
    nkb = -(-S // BLK)
    pad = nkb * BLK - S
    kt = jnp.pad(k.transpose(0, 2, 1, 3), ((0, 0), (0, 0), (0, pad), (0, 0))).reshape(B, H, nkb, BLK, hd)
    vt = jnp.pad(v.transpose(0, 2, 1, 3), ((0, 0), (0, 0), (0, pad), (0, 0))).reshape(B, H, nkb, BLK, hd)
    kmean = jnp.mean(kt.astype(F32), axis=3)
    topk = min(MOBA_TOPK, nkb)
    slopes = 2.0 ** (-8.0 * jnp.arange(1, H + 1, dtype=F32) / H)
    scale = hd ** -0.5
    nq = S // Q_BLOCK
    qb = q.reshape(B, nq, Q_BLOCK, H, hd).transpose(1, 0, 3, 2, 4)
    bi = jnp.arange(B)[:, None, None, None]
    hi = jnp.arange(H)[None, :, None, None]
    offs = jnp.arange(BLK)
    blk_ids = jnp.arange(nkb)

    def one(args):
        i, qi = args
        qpos = i * Q_BLOCK + jnp.arange(Q_BLOCK)
        own = (i * Q_BLOCK) // BLK
        gate = jnp.einsum('bhqd,bhnd->bhqn', qi.astype(F32), kmean)
        gate = jnp.where(blk_ids < own, gate, NEG)
        _, idx = lax.top_k(gate, topk)
        valid = jnp.arange(topk) < own
        k_sel = kt[bi, hi, idx]
        v_sel = vt[bi, hi, idx]
        kpos_sel = idx[..., None] * BLK + offs
        s_sel = jnp.einsum('bhqd,bhqjkd->bhqjk', qi, k_sel, preferred_element_type=F32) * scale
        s_sel = s_sel - slopes[None, :, None, None, None] * (qpos[None, None, :, None, None] - kpos_sel).astype(F32)
        s_sel = jnp.where(valid[None, None, None, :, None], s_sel, NEG)
        k_own = lax.dynamic_index_in_dim(kt, own, axis=2, keepdims=False)
        v_own = lax.dynamic_index_in_dim(vt, own, axis=2, keepdims=False)
        kpos_own = own * BLK + offs
        s_own = jnp.einsum('bhqd,bhkd->bhqk', qi, k_own, preferred_element_type=F32) * scale
        s_own = s_own - slopes[:, None, None] * (qpos[:, None] - kpos_own[None, :]).astype(F32)[None]
        s_own = jnp.where((kpos_own[None, :] <= qpos[:, None])[None, None], s_own, NEG)
        s = jnp.concatenate([s_sel.reshape(B, H, Q_BLOCK, topk * BLK), s_own], axis=-1)
        p = jax.nn.softmax(s, axis=-1)
        p_sel = p[..., :topk * BLK].reshape(B, H, Q_BLOCK, topk, BLK).astype(vt.dtype)
        p_own = p[..., topk * BLK:].astype(vt.dtype)
        return (jnp.einsum('bhqjk,bhqjkd->bhqd', p_sel, v_sel)
                + jnp.einsum('bhqk,bhkd->bhqd', p_own, v_own))

    o = lax.map(one, (jnp.arange(nq), qb))
    return o.transpose(1, 0, 3, 2, 4).reshape(B, S, H * hd)


def swiglu(t, w_g, w_u, w_d):
    return (jax.nn.silu(t @ w_g) * (t @ w_u)) @ w_d


def moe_ffn(h, router_w, router_bias, w_g, w_u, w_d, s_g, s_u, s_d):
    B, S, Dm = h.shape
    t = h.reshape(-1, Dm)
    scores = jax.nn.sigmoid(jnp.dot(t, router_w, preferred_element_type=F32))
    biased = scores + router_bias.astype(F32)
    grouped = biased.reshape(-1, N_GROUPS, EXPERTS_PER_GROUP)
    group_score = jnp.sum(lax.top_k(grouped, MOE_TOPK)[0], axis=-1)
    gsel = jnp.argmax(group_score, axis=-1)
    in_group = jnp.take_along_axis(grouped, gsel[:, None, None], axis=1)[:, 0]
    _, local = lax.top_k(in_group, MOE_TOPK)
    eidx = gsel[:, None] * EXPERTS_PER_GROUP + local
    wsel = jnp.take_along_axis(scores, eidx, axis=1)
    wsel = wsel / jnp.sum(wsel, axis=-1, keepdims=True)
    gates = jnp.sum(jax.nn.one_hot(eidx, N_EXPERTS, dtype=F32) * wsel[..., None], axis=1).astype(t.dtype)
    y = swiglu(t, s_g, s_u, s_d)
    for e in range(N_EXPERTS):
        y = y + gates[:, e:e + 1] * swiglu(t, w_g[e], w_u[e], w_d[e])
    return y.reshape(B, S, Dm)


def setup_inputs(seed: int = 0) -> dict:
    key = jax.random.key(seed)
    ks = jax.random.split(key, 32)
    L = DEPTH
    res = (2 * DEPTH) ** -0.5

    def nrm(k, shape, fan_in, mult=1.0):
        return jax.random.normal(k, shape, F32) * (mult * fan_in ** -0.5)

    def gain(k, shape):
        return 1.0 + 0.05 * jax.random.normal(k, shape, F32)

    return {
        "x": jax.random.normal(ks[0], (BATCH, SEQ, D_MODEL), F32),
        "attn_norm": gain(ks[1], (L, D_MODEL)),
        "w_in": nrm(ks[2], (L, D_MODEL, IN_W), D_MODEL),
        "mla_q_norm": gain(ks[3], (L, MLA_Q_LORA)),
        "mla_w_q_up": nrm(ks[4], (L, MLA_Q_LORA, MLA_HEADS * (MLA_NOPE + MLA_ROPE)), MLA_Q_LORA),
        "mla_kv_norm": gain(ks[5], (L, MLA_KV_LORA)),
        "mla_w_kv_up": nrm(ks[6], (L, MLA_KV_LORA, MLA_HEADS * (MLA_NOPE + MLA_V)), MLA_KV_LORA),
        "mla_q_gain": gain(ks[7], (L, MLA_NOPE + MLA_ROPE)),
        "mla_k_gain": gain(ks[8], (L, MLA_NOPE + MLA_ROPE)),
        "sgu_norm": gain(ks[9], (L, SGU_W)),
        "sgu_w": nrm(ks[10], (L, SGU_GROUPS, SGU_CHUNK, SGU_CHUNK), SGU_CHUNK),
        "sgu_b": gain(ks[11], (L, SGU_GROUPS, SGU_CHUNK)),
        "pool_w": nrm(ks[12], (L, N_POOL, POOL_GROUP_W, POOL_GROUP_W), POOL_GROUP_W),
        "pool_b": 0.02 * jax.random.normal(ks[13], (L, BRANCH_W), F32),
        "pool_scale": gain(ks[14], (L, BRANCH_W)),
        "moba_q_gain": gain(ks[15], (L, MOBA_HEAD_DIM)),
        "moba_k_gain": gain(ks[16], (L, MOBA_HEAD_DIM)),
        "w_gate": nrm(ks[17], (L, N_BRANCH, D_MODEL, D_MODEL), D_MODEL),
        "w_branch": nrm(ks[18], (L, N_BRANCH, BRANCH_W, D_MODEL), BRANCH_W),
        "w_out": nrm(ks[19], (L, D_MODEL, D_MODEL), D_MODEL, res),
        "ffn_norm": gain(ks[20], (L, D_MODEL)),
        "router_w": nrm(ks[21], (D_MODEL, N_EXPERTS), D_MODEL),
        "router_bias": 0.01 * jax.random.normal(ks[22], (N_EXPERTS,), F32),
        "moe_w_gate": nrm(ks[23], (L, N_EXPERTS, D_MODEL, EXPERT_FF), D_MODEL),
        "moe_w_up": nrm(ks[24], (L, N_EXPERTS, D_MODEL, EXPERT_FF), D_MODEL),
        "moe_w_down": nrm(ks[25], (L, N_EXPERTS, EXPERT_FF, D_MODEL), EXPERT_FF, res),
        "shared_w_gate": nrm(ks[26], (L, D_MODEL, SHARED_FF), D_MODEL),
        "shared_w_up": nrm(ks[27], (L, D_MODEL, SHARED_FF), D_MODEL),
        "shared_w_down": nrm(ks[28], (L, SHARED_FF, D_MODEL), SHARED_FF, res),
    }


def reference(x, attn_norm, w_in, mla_q_norm, mla_w_q_up, mla_kv_norm, mla_w_kv_up,
              mla_q_gain, mla_k_gain, sgu_norm, sgu_w, sgu_b, pool_w, pool_b, pool_scale,
              moba_q_gain, moba_k_gain, w_gate, w_branch, w_out, ffn_norm, router_w,
              router_bias, moe_w_gate, moe_w_up, moe_w_down, shared_w_gate, shared_w_up,
              shared_w_down):
    o1 = MLA_IN
    o2 = o1 + SGU_IN
    o3 = o2 + POOL_IN
    for l in range(DEPTH):
        h = rms_norm(x, attn_norm[l])
        cols = h @ w_in[l]
        y_a = mla_mixer(cols[..., :o1], mla_q_norm[l], mla_w_q_up[l], mla_kv_norm[l],
                        mla_w_kv_up[l], mla_q_gain[l], mla_k_gain[l])
        y_b = sgu_mixer(cols[..., o1:o2], sgu_norm[l], sgu_w[l], sgu_b[l])
        y_c = pool_mixer(cols[..., o2:o3], pool_w[l], pool_b[l], pool_scale[l])
        y_d = moba_mixer(cols[..., o3:], moba_q_gain[l], moba_k_gain[l])
        merged = jax.nn.sigmoid(h @ w_gate[l, 0]) * (y_a @ w_branch[l, 0])
        merged = merged + jax.nn.sigmoid(h @ w_gate[l, 1]) * (y_b @ w_branch[l, 1])
        merged = merged + jax.nn.sigmoid(h @ w_gate[l, 2]) * (y_c @ w_branch[l, 2])
        merged = merged + jax.nn.sigmoid(h @ w_gate[l, 3]) * (y_d @ w_branch[l, 3])
        x = x + merged @ w_out[l]
        h2 = rms_norm(x, ffn_norm[l])
        x = x + moe_ffn(h2, router_w, router_bias, moe_w_gate[l], moe_w_up[l], moe_w_down[l],
                        shared_w_gate[l], shared_w_up[l], shared_w_down[l])
    return x
```

```python
import functools
import math

import jax
import jax.numpy as jnp
import numpy as np
from jax import lax
from jax.experimental import pallas as pl
from jax.experimental.pallas import tpu as pltpu

F32 = jnp.float32
BF16 = jnp.bfloat16
EPS = 1e-6
NEG = -1e30

LANES = 128
VMEM_LIMIT = 56 * 1024 * 1024

MLA_HEADS = 4
MLA_NOPE = 64
MLA_ROPE = 32
MLA_V = 64
MLA_Q_LORA = 256
MLA_KV_LORA = 128
ROPE_THETA = 10000.0
MLA_QK = MLA_NOPE + MLA_ROPE

SGU_GROUPS = 4
SGU_CHUNK = 128
SGU_W = 256

POOL_WINDOWS = (2, 4, 8, 16)
POOL_MAXW = 16
POOL_W = 256

MOBA_HEADS = 4
MOBA_HD = 64
MOBA_BLOCK = 256
MOBA_TOPK = 3

N_EXPERTS = 16
N_GROUPS = 4
EPG = 4

C_MLA = 0
C_SGU = 512
C_POOL = 1024
C_MOBA = 1280
C_END = 2048

TM_IN = 256
TQ = 256
TK = 512
TM_MERGE = 512
TM_MOE = 1024
SEL_L = 2048


def _cparams(sem):
    return pltpu.CompilerParams(dimension_semantics=sem, vmem_limit_bytes=VMEM_LIMIT)


def _full(shape):
    n = len(shape)
    return pl.BlockSpec(shape, lambda *_: (0,) * n)


def _rms(x, g):
    return x * lax.rsqrt(jnp.mean(x * x, axis=-1, keepdims=True) + EPS) * g


def _dot(a, b):
    return jnp.dot(a, b, preferred_element_type=F32)


def _in_proj_kernel(x_ref, g_ref, w_ref, qn_ref, wq_ref, kvn_ref, wkv_ref, pk_ref,
                    qg_ref, kg_ref, g512_ref, r512_ref, cos_ref, sin_ref,
                    sn_ref, sw_ref, sb_ref, pw_ref, pb_ref, ps_ref,
                    mqg_ref, mkg_ref, g256_ref,
                    qa_ref, ka_ref, va_ref, yb_ref, yc_ref, qd_ref, kd_ref, vd_ref, km_ref,
                    halo_ref, buf_ref, *, tiles_per_seq, q_scale):
    tm = x_ref.shape[0]
    i = pl.program_id(0)
    x = x_ref[...]
    h = _rms(x, g_ref[...]).astype(BF16)

    cm = _dot(h, w_ref[:, C_MLA:C_SGU])
    c_q = _rms(cm[:, :MLA_Q_LORA], qn_ref[...]).astype(BF16)
    c_kv = _rms(cm[:, MLA_Q_LORA:MLA_Q_LORA + MLA_KV_LORA], kvn_ref[...]).astype(BF16)
    kpe_grp = cm[:, MLA_Q_LORA + MLA_KV_LORA:].astype(BF16)
    q = _dot(c_q, wq_ref[...])
    kv = _dot(c_kv, wkv_ref[...])
    k = kv[:, :512] + _dot(kpe_grp, pk_ref[...])
    va_ref[...] = kv[:, 512:].astype(BF16)
    cos = jnp.concatenate([cos_ref[...]] * MLA_HEADS, axis=1)
    sin = jnp.concatenate([sin_ref[...]] * MLA_HEADS, axis=1)
    inv_d = 1.0 / MLA_QK

    def head_norm_rope(t, gain):
        ms = _dot((t * t).astype(BF16), g512_ref[...]) * inv_d
        t = t * lax.rsqrt(ms + EPS) * gain
        rot = _dot(t.astype(BF16), r512_ref[...])
        return t * cos + rot * sin

    qa_ref[...] = (head_norm_rope(q, qg_ref[...]) * q_scale).astype(BF16)
    ka_ref[...] = head_norm_rope(k, kg_ref[...]).astype(BF16)

    z = jax.nn.gelu(_dot(h, w_ref[:, C_SGU:C_POOL]))
    u = z[:, :SGU_W]
    vv = _rms(z[:, SGU_W:], sn_ref[...]).astype(BF16)
    lane_grp = lax.broadcasted_iota(jnp.int32, (SGU_CHUNK, SGU_W), 1) // (SGU_W // SGU_GROUPS)
    for c in range(tm // SGU_CHUNK):
        vc = vv[c * SGU_CHUNK:(c + 1) * SGU_CHUNK, :]
        y = sb_ref[...]
        for gi in range(SGU_GROUPS):
            y = y + jnp.where(lane_grp == gi, _dot(sw_ref[gi], vc), 0.0)
        yb_ref[c * SGU_CHUNK:(c + 1) * SGU_CHUNK, :] = (
            u[c * SGU_CHUNK:(c + 1) * SGU_CHUNK, :] * y).astype(BF16)

    xp = _dot(h, w_ref[:, C_POOL:C_MOBA])

    @pl.when(i % tiles_per_seq == 0)
    def _():
        halo_ref[...] = jnp.zeros_like(halo_ref)

    buf_ref[0:POOL_MAXW, :] = halo_ref[...]
    buf_ref[POOL_MAXW:, :] = xp
    halo_ref[...] = xp[tm - POOL_MAXW:, :]
    pos = (i % tiles_per_seq) * tm + lax.broadcasted_iota(jnp.int32, (tm, POOL_W), 0)
    lane_w = lax.broadcasted_iota(jnp.int32, (tm, POOL_W), 1) // (POOL_W // len(POOL_WINDOWS))
    acc = xp
    pooled = jnp.zeros((tm, POOL_W), F32)
    shift = 1
    for gi, w in enumerate(POOL_WINDOWS):
        while shift < w:
            acc = acc + buf_ref[POOL_MAXW - shift:POOL_MAXW - shift + tm, :]
            shift += 1
        cnt = jnp.minimum(pos + 1, w).astype(F32)
        pooled = jnp.where(lane_w == gi, acc / cnt, pooled)
    p = (pooled - xp).astype(BF16)
    yc_ref[...] = ((_dot(p, pw_ref[...]) + pb_ref[...]) * ps_ref[...]).astype(BF16)

    cd = _dot(h, w_ref[:, C_MOBA:C_END])
    inv_hd = 1.0 / MOBA_HD

    def head_norm(t, gain):
        ms = _dot((t * t).astype(BF16), g256_ref[...]) * inv_hd
        return t * lax.rsqrt(ms + EPS) * gain

    qd_ref[...] = head_norm(cd[:, :256], mqg_ref[...])
    kd = head_norm(cd[:, 256:512], mkg_ref[...])
    kd_ref[...] = kd.astype(BF16)
    vd_ref[...] = cd[:, 512:].astype(BF16)
    for c in range(tm // MOBA_BLOCK):
        km_ref[c] = jnp.mean(kd[c * MOBA_BLOCK:(c + 1) * MOBA_BLOCK, :], axis=0, keepdims=True)


def _in_proj(x2, lw, consts, S):
    T, D = x2.shape
    tm = TM_IN
    nt = T // tm
    tok = lambda w: pl.BlockSpec((tm, w), lambda i: (i, 0))
    pos_spec = pl.BlockSpec((tm, LANES), lambda i: (i % (S // tm), 0))
    ins = [
        (x2, tok(D)), (lw["attn_norm"], None), (lw["w_in"], None),
        (lw["mla_q_norm"], None), (lw["wq"], None), (lw["mla_kv_norm"], None), (lw["wkv"], None),
        (consts["pk"], None), (lw["mla_qg"], None), (lw["mla_kg"], None),
        (consts["g512"], None), (consts["r512"], None),
        (consts["cos"], pos_spec), (consts["sin"], pos_spec),
        (lw["sgu_norm"], None), (lw["sgu_w"], None), (lw["sgu_b"], None),
        (lw["pool_w"], None), (lw["pool_b"], None), (lw["pool_scale"], None),
        (lw["moba_qg"], None), (lw["moba_kg"], None), (consts["g256"], None),
    ]
    arrays = [a for a, _ in ins]
    specs = [s if s is not None else _full(a.shape) for a, s in ins]
    nb = tm // MOBA_BLOCK
    out_shape = [
        jax.ShapeDtypeStruct((T, 512), BF16), jax.ShapeDtypeStruct((T, 512), BF16),
        jax.ShapeDtypeStruct((T, 256), BF16), jax.ShapeDtypeStruct((T, 256), BF16),
        jax.ShapeDtypeStruct((T, 256), BF16), jax.ShapeDtypeStruct((T, 256), F32),
        jax.ShapeDtypeStruct((T, 256), BF16), jax.ShapeDtypeStruct((T, 256), BF16),
        jax.ShapeDtypeStruct((T // MOBA_BLOCK, 1, 256), F32),
    ]
    out_specs = [tok(512), tok(512), tok(256), tok(256), tok(256), tok(256), tok(256), tok(256),
                 pl.BlockSpec((nb, 1, 256), lambda i: (i, 0, 0))]
    q_scale = (MLA_QK ** -0.5) * math.log2(math.e)
    return pl.pallas_call(
        functools.partial(_in_proj_kernel, tiles_per_seq=S // tm, q_scale=q_scale),
        grid=(nt,), in_specs=specs, out_specs=out_specs, out_shape=out_shape,
        scratch_shapes=[pltpu.VMEM((POOL_MAXW, POOL_W), F32),
                        pltpu.VMEM((tm + POOL_MAXW, POOL_W), F32)],
        compiler_params=_cparams(("arbitrary",)), name="in_proj",
    )(*arrays)


def _moba_select_kernel(q_ref, km_ref, o_ref, *, slope_log2_step):
    hh = pl.program_id(1)
    li = pl.program_id(2)
    q = q_ref[0, 0]
    L = q.shape[1]
    nrow = km_ref.shape[2]
    g = jnp.dot(km_ref[0, 0], q, preferred_element_type=F32, precision=lax.Precision.HIGHEST)
    n_i = lax.broadcasted_iota(jnp.int32, (nrow, L), 0)
    own = (li * L + lax.broadcasted_iota(jnp.int32, (nrow, L), 1)) // MOBA_BLOCK
    g = jnp.where(n_i < own, g, NEG)
    sel = jnp.zeros((nrow, L), F32)
    for _ in range(MOBA_TOPK):
        mx = jnp.max(g, axis=0, keepdims=True)
        first = jnp.min(jnp.where(g == mx, n_i, nrow), axis=0, keepdims=True)
        hit = n_i == first
        sel = jnp.where(hit & (mx > 0.5 * NEG), 1.0, sel)
        g = jnp.where(hit, NEG, g)
    slope = jnp.exp2(-slope_log2_step * (hh + 1).astype(F32))
    rel = (n_i - own).astype(F32) * (slope * MOBA_BLOCK)
    bias = jnp.where((sel > 0.0) | (n_i == own), rel, NEG)
    bias = jnp.where(n_i == nrow - 1, slope, bias)
    o_ref[0, 0, 0:MOBA_HD, :] = (q * (MOBA_HD ** -0.5)).astype(BF16)
    o_ref[0, 0, MOBA_HD:, :] = bias.astype(BF16)


def _moba_select(qgT, kmean):
    B, H, hd, S = qgT.shape
    L = min(SEL_L, S)
    return pl.pallas_call(
        functools.partial(_moba_select_kernel, slope_log2_step=8.0 / MOBA_HEADS),
        grid=(B, H, S // L),
        in_specs=[pl.BlockSpec((1, 1, hd, L), lambda b, h, l: (b, h, 0, l)),
                  pl.BlockSpec((1, 1, 64, hd), lambda b, h, l: (b, h, 0, 0))],
        out_specs=pl.BlockSpec((1, 1, 2 * hd, L), lambda b, h, l: (b, h, 0, l)),
        out_shape=jax.ShapeDtypeStruct((B, H, 2 * hd, S), BF16),
        compiler_params=_cparams(("arbitrary",) * 3), name="moba_select",
    )(qgT, kmean)


def _flash_kernel(q_ref, k_ref, v_ref, o_ref, *, use_exp2):
    nq, _, tq = q_ref.shape[2:]
    tk = k_ref.shape[3]
    ratio = tk // tq
    ex = jnp.exp2 if use_exp2 else jnp.exp

    def q_body(i, carry):
        qt = q_ref[0, 0, i]
        jd = i // ratio
        s = _dot(k_ref[0, 0, jd], qt)
        kpos = jd * tk + lax.broadcasted_iota(jnp.int32, (tk, tq), 0)
        qpos = i * tq + lax.broadcasted_iota(jnp.int32, (tk, tq), 1)
        s = jnp.where(kpos <= qpos, s, NEG)
        m = jnp.max(s, axis=0, keepdims=True)
        p = ex(s - m)
        l = jnp.sum(p, axis=0, keepdims=True)
        acc = _dot(v_ref[0, 0, jd], p.astype(BF16))

        def k_body(j, c):
            m, l, acc = c
            s = _dot(k_ref[0, 0, j], qt)
            m_new = jnp.maximum(m, jnp.max(s, axis=0, keepdims=True))
            alpha = ex(m - m_new)
            p = ex(s - m_new)
            l = l * alpha + jnp.sum(p, axis=0, keepdims=True)
            acc = acc * alpha + _dot(v_ref[0, 0, j], p.astype(BF16))
            return m_new, l, acc

        m, l, acc = lax.fori_loop(0, jd, k_body, (m, l, acc))
        o_ref[0, 0, i] = (acc / l).astype(BF16)
        return carry

    lax.fori_loop(0, nq, q_body, 0)


def _flash(qT, k, vT, use_exp2, name):
    B, H, nq, dk, tq = qT.shape
    nk, tk = k.shape[2:4]
    dv = vT.shape[3]
    spec = lambda shp: pl.BlockSpec((1, 1) + shp, lambda b, h: (b, h, 0, 0, 0))
    return pl.pallas_call(
        functools.partial(_flash_kernel, use_exp2=use_exp2),
        grid=(B, H),
        in_specs=[spec((nq, dk, tq)), spec((nk, tk, dk)), spec((nk, dv, tk))],
        out_specs=spec((nq, dv, tq)),
        out_shape=jax.ShapeDtypeStruct((B, H, nq, dv, tq), BF16),
        compiler_params=_cparams(("arbitrary", "arbitrary")), name=name,
    )(qT, k, vT)


def _merge_kernel(x_ref, ya_ref, yb_ref, yc_ref, yd_ref, g_ref, wg_ref, wb_ref, wo_ref,
                  fg_ref, rw_ref, rb_ref, x1_ref, h2_ref, gt_ref):
    x = x_ref[...]
    h = _rms(x, g_ref[...]).astype(BF16)
    merged = None
    for bi, y_ref in enumerate((ya_ref, yb_ref, yc_ref, yd_ref)):
        t = jax.nn.sigmoid(_dot(h, wg_ref[bi])) * _dot(y_ref[...], wb_ref[bi])
        merged = t if merged is None else merged + t
    x1 = x + _dot(merged.astype(BF16), wo_ref[...])
    x1_ref[...] = x1
    h2 = _rms(x1, fg_ref[...])
    h2_ref[...] = h2.astype(BF16)

    logits = lax.dot_general(rw_ref[...], h2, (((1,), (1,)), ((), ())),
                             preferred_element_type=F32, precision=lax.Precision.HIGHEST)
    scores = jax.nn.sigmoid(logits)
    biased = scores + rb_ref[...]
    s_rows = [scores[e:e + 1, :] for e in range(N_EXPERTS)]
    b_rows = [biased[e:e + 1, :] for e in range(N_EXPERTS)]

    def top2(vals):
        m1 = vals[0]
        i1 = jnp.zeros_like(m1, dtype=jnp.int32)
        for j in range(1, len(vals)):
            better = vals[j] > m1
            i1 = jnp.where(better, j, i1)
            m1 = jnp.where(better, vals[j], m1)
        m2 = None
        i2 = None
        for j in range(len(vals)):
            cand = jnp.where(i1 == j, NEG, vals[j])
            if m2 is None:
                m2, i2 = cand, jnp.zeros_like(i1)
            else:
                better = cand > m2
                i2 = jnp.where(better, j, i2)
                m2 = jnp.where(better, cand, m2)
        return m1, i1, m2, i2

    best = None
    for gi in range(N_GROUPS):
        bm1, bi1, bm2, bi2 = top2(b_rows[gi * EPG:(gi + 1) * EPG])
        gs = bm1 + bm2
        e1 = bi1 + gi * EPG
        e2 = bi2 + gi * EPG
        if best is None:
            best = (gs, e1, e2)
        else:
            better = gs > best[0]
            best = (jnp.where(better, gs, best[0]), jnp.where(better, e1, best[1]),
                    jnp.where(better, e2, best[2]))
    _, e1, e2 = best
    w1 = jnp.zeros_like(s_rows[0])
    w2 = jnp.zeros_like(s_rows[0])
    for e in range(N_EXPERTS):
        w1 = jnp.where(e1 == e, s_rows[e], w1)
        w2 = jnp.where(e2 == e, s_rows[e], w2)
    tot = w1 + w2
    w1 = w1 / tot
    w2 = w2 / tot
    row = lax.broadcasted_iota(jnp.int32, scores.shape, 0)
    gates = jnp.where(row == e1, w1, 0.0) + jnp.where(row == e2, w2, 0.0)
    gt_ref[...] = gates


def _merge(x2, ya, yb, yc, yd, lw, router_w_t, router_b):
    T, D = x2.shape
    tm = TM_MERGE
    tok = lambda w: pl.BlockSpec((tm, w), lambda i: (i, 0))
    consts = [lw["attn_norm"], lw["w_gate"], lw["w_branch"], lw["w_out"], lw["ffn_norm"],
              router_w_t, router_b]
    return pl.pallas_call(
        _merge_kernel, grid=(T // tm,),
        in_specs=[tok(D), tok(256), tok(256), tok(256), tok(256)] + [_full(a.shape) for a in consts],
        out_specs=[tok(D), tok(D), pl.BlockSpec((N_EXPERTS, tm), lambda i: (0, i))],
        out_shape=[jax.ShapeDtypeStruct((T, D), F32), jax.ShapeDtypeStruct((T, D), BF16),
                   jax.ShapeDtypeStruct((N_EXPERTS, T), F32)],
        compiler_params=_cparams(("arbitrary",)), name="merge",
    )(x2, ya, yb, yc, yd, *consts)


def _moe_kernel(x1_ref, h2_ref, g_ref, wg_ref, wu_ref, wd_ref, o_ref):
    e = pl.program_id(1)
    h2 = h2_ref[...]
    a = jax.nn.silu(_dot(h2, wg_ref[0])) * _dot(h2, wu_ref[0])
    lane = lax.broadcasted_iota(jnp.int32, g_ref.shape, 1)
    gate = jnp.sum(jnp.where(lane == e, g_ref[...], 0.0), axis=1, keepdims=True)
    y = _dot((a * gate).astype(BF16), wd_ref[0])

    @pl.when(e == 0)
    def _():
        o_ref[...] = x1_ref[...] + y

    @pl.when(e > 0)
    def _():
        o_ref[...] += y


def _moe(x1, h2, gates, wg, wu, wd):
    T, D = x1.shape
    tm = min(TM_MOE, T)
    ne, _, ff = wg.shape
    return pl.pallas_call(
        _moe_kernel, grid=(T // tm, ne),
        in_specs=[pl.BlockSpec((tm, D), lambda i, e: (i, 0)),
                  pl.BlockSpec((tm, D), lambda i, e: (i, 0)),
                  pl.BlockSpec((tm, gates.shape[1]), lambda i, e: (i, 0)),
                  pl.BlockSpec((1, D, ff), lambda i, e: (e, 0, 0)),
                  pl.BlockSpec((1, D, ff), lambda i, e: (e, 0, 0)),
                  pl.BlockSpec((1, ff, D), lambda i, e: (e, 0, 0))],
        out_specs=pl.BlockSpec((tm, D), lambda i, e: (i, 0)),
        out_shape=jax.ShapeDtypeStruct((T, D), F32),
        compiler_params=_cparams(("arbitrary", "arbitrary")), name="moe",
    )(x1, h2, gates, wg, wu, wd)


def _constants(S):
    half = MLA_ROPE // 2
    inv = ROPE_THETA ** (-jnp.arange(half, dtype=F32) / half)
    ang = jnp.arange(S, dtype=F32)[:, None] * inv[None, :]
    cos = jnp.ones((S, LANES), F32)
    cos = cos.at[:, MLA_NOPE:MLA_NOPE + half].set(jnp.cos(ang))
    cos = cos.at[:, MLA_NOPE + half:MLA_QK].set(jnp.cos(ang))
    sin = jnp.zeros((S, LANES), F32)
    sin = sin.at[:, MLA_NOPE:MLA_NOPE + half].set(jnp.sin(ang))
    sin = sin.at[:, MLA_NOPE + half:MLA_QK].set(jnp.sin(ang))

    r = np.zeros((LANES, LANES), np.float32)
    for j in range(half):
        r[MLA_NOPE + half + j, MLA_NOPE + j] = -1.0
        r[MLA_NOPE + j, MLA_NOPE + half + j] = 1.0
    g = np.zeros((LANES, LANES), np.float32)
    g[:MLA_QK, :] = 1.0
    eye4 = np.eye(MLA_HEADS, dtype=np.float32)
    pk = np.zeros((LANES, 4 * LANES), np.float32)
    for hh in range(MLA_HEADS):
        for j in range(MLA_ROPE):
            pk[j, hh * LANES + MLA_NOPE + j] = 1.0
    g256 = np.kron(np.eye(MOBA_HEADS, dtype=np.float32), np.ones((MOBA_HD, MOBA_HD), np.float32))

    nkb = S // MOBA_BLOCK
    kpos = np.arange(S)
    kc = np.zeros((S, MOBA_HD), np.float32)
    blk = kpos // MOBA_BLOCK
    m = blk < min(nkb, MOBA_HD) - 1
    kc[kpos[m], blk[m]] = 1.0
    kc[:, MOBA_HD - 1] = kpos % MOBA_BLOCK
    return {
        "cos": cos, "sin": sin,
        "r512": jnp.asarray(np.kron(eye4, r), BF16), "g512": jnp.asarray(np.kron(eye4, g), BF16),
        "pk": jnp.asarray(pk, BF16), "g256": jnp.asarray(g256, BF16), "kc": jnp.asarray(kc, BF16),
    }


def _head_pad(v):
    one = jnp.concatenate([v, jnp.zeros((LANES - MLA_QK,), F32)])
    return jnp.tile(one, MLA_HEADS)[None, :]


def _layer_weights(l, p):
    D = p["w_in"].shape[1]
    w_in = p["w_in"][l]
    o1 = MLA_Q_LORA + MLA_KV_LORA + MLA_ROPE
    o2 = o1 + 2 * SGU_W
    o3 = o2 + POOL_W
    w = jnp.concatenate([w_in[:, :o1], jnp.zeros((D, C_SGU - o1), F32), w_in[:, o1:]], axis=1)

    wq = p["mla_w_q_up"][l].reshape(MLA_Q_LORA, MLA_HEADS, MLA_QK)
    wq = jnp.pad(wq, ((0, 0), (0, 0), (0, LANES - MLA_QK))).reshape(MLA_Q_LORA, MLA_HEADS * LANES)
    wkv = p["mla_w_kv_up"][l].reshape(MLA_KV_LORA, MLA_HEADS, MLA_NOPE + MLA_V)
    wk = jnp.pad(wkv[:, :, :MLA_NOPE], ((0, 0), (0, 0), (0, LANES - MLA_NOPE)))
    wk = wk.reshape(MLA_KV_LORA, MLA_HEADS * LANES)
    wv = wkv[:, :, MLA_NOPE:].reshape(MLA_KV_LORA, MLA_HEADS * MLA_V)

    tri = jnp.tril(jnp.ones((SGU_CHUNK, SGU_CHUNK), dtype=bool))
    sgu_w = jnp.where(tri[None], p["sgu_w"][l], 0.0)
    sgu_b = jnp.repeat(p["sgu_b"][l].T, SGU_W // SGU_GROUPS, axis=1)
    pool_w = jax.scipy.linalg.block_diag(*[p["pool_w"][l, gi] for gi in range(len(POOL_WINDOWS))])

    ne = p["moe_w_gate"].shape[1]
    return {
        "attn_norm": p["attn_norm"][l][None, :], "w_in": w.astype(BF16),
        "mla_q_norm": p["mla_q_norm"][l][None, :], "wq": wq.astype(BF16),
        "mla_kv_norm": p["mla_kv_norm"][l][None, :],
        "wkv": jnp.concatenate([wk, wv], axis=1).astype(BF16),
        "mla_qg": _head_pad(p["mla_q_gain"][l]), "mla_kg": _head_pad(p["mla_k_gain"][l]),
        "sgu_norm": p["sgu_norm"][l][None, :], "sgu_w": sgu_w.astype(BF16), "sgu_b": sgu_b,
        "pool_w": pool_w.astype(BF16), "pool_b": p["pool_b"][l][None, :],
        "pool_scale": p["pool_scale"][l][None, :],
        "moba_qg": jnp.tile(p["moba_q_gain"][l], MOBA_HEADS)[None, :],
        "moba_kg": jnp.tile(p["moba_k_gain"][l], MOBA_HEADS)[None, :],
        "w_gate": p["w_gate"][l].astype(BF16), "w_branch": p["w_branch"][l].astype(BF16),
        "w_out": p["w_out"][l].astype(BF16), "ffn_norm": p["ffn_norm"][l][None, :],
        "ffn_g": jnp.concatenate([p["shared_w_gate"][l][None], p["moe_w_gate"][l]]).astype(BF16),
        "ffn_u": jnp.concatenate([p["shared_w_up"][l][None], p["moe_w_up"][l]]).astype(BF16),
        "ffn_d": jnp.concatenate([p["shared_w_down"][l][None], p["moe_w_down"][l]]).astype(BF16),
    }


def _to_qT(a, B, S, H, d):
    return a.reshape(B, S // TQ, TQ, H, d).transpose(0, 3, 1, 4, 2)


def _to_k(a, B, S, H, d):
    return a.reshape(B, S // TK, TK, H, d).transpose(0, 3, 1, 2, 4)


def _to_vT(a, B, S, H, d):
    return a.reshape(B, S // TK, TK, H, d).transpose(0, 3, 1, 4, 2)


def _from_oT(o, B, S):
    _, H, nq, d, tq = o.shape
    return o.transpose(0, 2, 4, 1, 3).reshape(B * S, H * d)


def kernel(x, attn_norm, w_in, mla_q_norm, mla_w_q_up, mla_kv_norm, mla_w_kv_up, mla_q_gain, mla_k_gain, sgu_norm, sgu_w, sgu_b, pool_w, pool_b, pool_scale, moba_q_gain, moba_k_gain, w_gate, w_branch, w_out, ffn_norm, router_w, router_bias, moe_w_gate, moe_w_up, moe_w_down, shared_w_gate, shared_w_up, shared_w_down):
    p = dict(attn_norm=attn_norm, w_in=w_in, mla_q_norm=mla_q_norm, mla_w_q_up=mla_w_q_up,
             mla_kv_norm=mla_kv_norm, mla_w_kv_up=mla_w_kv_up, mla_q_gain=mla_q_gain,
             mla_k_gain=mla_k_gain, sgu_norm=sgu_norm, sgu_w=sgu_w, sgu_b=sgu_b, pool_w=pool_w,
             pool_b=pool_b, pool_scale=pool_scale, moba_q_gain=moba_q_gain,
             moba_k_gain=moba_k_gain, w_gate=w_gate, w_branch=w_branch, w_out=w_out,
             ffn_norm=ffn_norm, moe_w_gate=moe_w_gate, moe_w_up=moe_w_up, moe_w_down=moe_w_down,
             shared_w_gate=shared_w_gate, shared_w_up=shared_w_up, shared_w_down=shared_w_down)
    B, S, D = x.shape
    T = B * S
    depth = attn_norm.shape[0]
    nkb = S // MOBA_BLOCK
    assert S % TK == 0 and S % TM_IN == 0 and T % TM_MERGE == 0 and nkb <= MOBA_HD
    consts = _constants(S)
    router_w_t = router_w.T
    router_b = router_bias[:, None]
    kc = jnp.broadcast_to(consts["kc"][None, :, None, :], (B, S, MOBA_HEADS, MOBA_HD))

    x2 = x.reshape(T, D)
    for l in range(depth):
        lw = _layer_weights(l, p)
        qa, ka, va, yb, yc, qd, kd, vd, km = _in_proj(x2, lw, consts, S)

        oa = _flash(_to_qT(qa, B, S, MLA_HEADS, LANES), _to_k(ka, B, S, MLA_HEADS, LANES),
                    _to_vT(va, B, S, MLA_HEADS, MLA_V), True, "mla_attn")
        ya = _from_oT(oa, B, S)

        qgT = qd.reshape(B, S, MOBA_HEADS, MOBA_HD).transpose(0, 2, 3, 1)
        kmean = km.reshape(B, nkb, MOBA_HEADS, MOBA_HD).transpose(0, 2, 1, 3)
        kmean = jnp.pad(kmean, ((0, 0), (0, 0), (0, MOBA_HD - nkb), (0, 0)))
        q_aug = _moba_select(qgT, kmean)
        q_aug = q_aug.reshape(B, MOBA_HEADS, 2 * MOBA_HD, S // TQ, TQ).transpose(0, 1, 3, 2, 4)
        k_aug = jnp.concatenate([kd.reshape(B, S, MOBA_HEADS, MOBA_HD), kc], axis=-1)
        k_aug = k_aug.reshape(B, S // TK, TK, MOBA_HEADS, 2 * MOBA_HD).transpose(0, 3, 1, 2, 4)
        od = _flash(q_aug, k_aug, _to_vT(vd, B, S, MOBA_HEADS, MOBA_HD), False, "moba_attn")
        yd = _from_oT(od, B, S)

        x1, h2, gates_t = _merge(x2, ya, yb, yc, yd, lw, router_w_t, router_b)
        gates = jnp.concatenate([jnp.ones((T, 1), F32), gates_t.T], axis=1)
        x2 = _moe(x1, h2, gates, lw["ffn_g"], lw["ffn_u"], lw["ffn_d"])
    return x2.reshape(B, S, D)
```

```python
import functools
import math

import jax
import jax.numpy as jnp
import numpy as np
from jax import lax
from jax.experimental import pallas as pl
from jax.experimental.pallas import tpu as pltpu

F32 = jnp.float32
BF16 = jnp.bfloat16
EPS = 1e-6
NEG = -1e30

LANES = 128
VMEM_LIMIT = 56 * 1024 * 1024

MLA_HEADS = 4
MLA_NOPE = 64
MLA_ROPE = 32
MLA_V = 64
MLA_Q_LORA = 256
MLA_KV_LORA = 128
ROPE_THETA = 10000.0
MLA_QK = MLA_NOPE + MLA_ROPE

SGU_GROUPS = 4
SGU_CHUNK = 128
SGU_W = 256

POOL_WINDOWS = (2, 4, 8, 16)
POOL_MAXW = 16
POOL_W = 256

MOBA_HEADS = 4
MOBA_HD = 64
MOBA_BLOCK = 256
MOBA_TOPK = 3

N_EXPERTS = 16
N_GROUPS = 4
EPG = 4

C_MLA = 0
C_SGU = 512
C_POOL = 1024
C_MOBA = 1280
C_END = 2048

TM_IN = 256
TQ = 512
TK = 512
TM_MERGE = 512
TM_MOE = 1024
SEL_L = 2048
ONES_ROWS = 16


def _cparams(sem):
    return pltpu.CompilerParams(dimension_semantics=sem, vmem_limit_bytes=VMEM_LIMIT)


def _full(shape):
    n = len(shape)
    return pl.BlockSpec(shape, lambda *_: (0,) * n)


def _rms(x, g):
    return x * lax.rsqrt(jnp.mean(x * x, axis=-1, keepdims=True) + EPS) * g


def _dot(a, b):
    return jnp.dot(a, b, preferred_element_type=F32)


def _in_proj_kernel(x_ref, g_ref, w_ref, qn_ref, wq_ref, kvn_ref, wkv_ref, pk_ref,
                    qg_ref, kg_ref, g512_ref, r512_ref, cos_ref, sin_ref,
                    sn_ref, sw_ref, sb_ref, pw_ref, pb_ref, ps_ref,
                    mqg_ref, mkg_ref, g256_ref,
                    qa_ref, ka_ref, va_ref, yb_ref, yc_ref, qd_ref, kd_ref, vd_ref, km_ref,
                    halo_ref, buf_ref, *, tiles_per_seq, q_scale):
    tm = x_ref.shape[0]
    i = pl.program_id(0)
    x = x_ref[...]
    h = _rms(x, g_ref[...]).astype(BF16)

    cm = _dot(h, w_ref[:, C_MLA:C_SGU])
    c_q = _rms(cm[:, :MLA_Q_LORA], qn_ref[...]).astype(BF16)
    c_kv = _rms(cm[:, MLA_Q_LORA:MLA_Q_LORA + MLA_KV_LORA], kvn_ref[...]).astype(BF16)
    kpe_grp = cm[:, MLA_Q_LORA + MLA_KV_LORA:].astype(BF16)
    q = _dot(c_q, wq_ref[...])
    kv = _dot(c_kv, wkv_ref[...])
    k = kv[:, :512] + _dot(kpe_grp, pk_ref[...])
    va_ref[...] = kv[:, 512:].astype(BF16)
    cos = jnp.concatenate([cos_ref[...]] * MLA_HEADS, axis=1)
    sin = jnp.concatenate([sin_ref[...]] * MLA_HEADS, axis=1)
    inv_d = 1.0 / MLA_QK

    def head_norm_rope(t, gain):
        ms = _dot((t * t).astype(BF16), g512_ref[...]) * inv_d
        t = t * lax.rsqrt(ms + EPS) * gain
        rot = _dot(t.astype(BF16), r512_ref[...])
        return t * cos + rot * sin

    qa_ref[...] = (head_norm_rope(q, qg_ref[...]) * q_scale).astype(BF16)
    ka_ref[...] = head_norm_rope(k, kg_ref[...]).astype(BF16)

    z = jax.nn.gelu(_dot(h, w_ref[:, C_SGU:C_POOL]))
    u = z[:, :SGU_W]
    vv = _rms(z[:, SGU_W:], sn_ref[...]).astype(BF16)
    lane_grp = lax.broadcasted_iota(jnp.int32, (SGU_CHUNK, SGU_W), 1) // (SGU_W // SGU_GROUPS)
    for c in range(tm // SGU_CHUNK):
        vc = vv[c * SGU_CHUNK:(c + 1) * SGU_CHUNK, :]
        y = sb_ref[...]
        for gi in range(SGU_GROUPS):
            y = y + jnp.where(lane_grp == gi, _dot(sw_ref[gi], vc), 0.0)
        yb_ref[c * SGU_CHUNK:(c + 1) * SGU_CHUNK, :] = (
            u[c * SGU_CHUNK:(c + 1) * SGU_CHUNK, :] * y).astype(BF16)

    xp = _dot(h, w_ref[:, C_POOL:C_MOBA])

    @pl.when(i % tiles_per_seq == 0)
    def _():
        halo_ref[...] = jnp.zeros_like(halo_ref)

    buf_ref[0:POOL_MAXW, :] = halo_ref[...]
    buf_ref[POOL_MAXW:, :] = xp
    halo_ref[...] = xp[tm - POOL_MAXW:, :]
    pos = (i % tiles_per_seq) * tm + lax.broadcasted_iota(jnp.int32, (tm, POOL_W), 0)
    lane_w = lax.broadcasted_iota(jnp.int32, (tm, POOL_W), 1) // (POOL_W // len(POOL_WINDOWS))
    acc = xp
    pooled = jnp.zeros((tm, POOL_W), F32)
    shift = 1
    for gi, w in enumerate(POOL_WINDOWS):
        while shift < w:
            acc = acc + buf_ref[POOL_MAXW - shift:POOL_MAXW - shift + tm, :]
            shift += 1
        cnt = jnp.minimum(pos + 1, w).astype(F32)
        pooled = jnp.where(lane_w == gi, acc / cnt, pooled)
    p = (pooled - xp).astype(BF16)
    yc_ref[...] = ((_dot(p, pw_ref[...]) + pb_ref[...]) * ps_ref[...]).astype(BF16)

    cd = _dot(h, w_ref[:, C_MOBA:C_END])
    inv_hd = 1.0 / MOBA_HD

    def head_norm(t, gain):
        ms = _dot((t * t).astype(BF16), g256_ref[...]) * inv_hd
        return t * lax.rsqrt(ms + EPS) * gain

    qd_ref[...] = head_norm(cd[:, :256], mqg_ref[...])
    kd = head_norm(cd[:, 256:512], mkg_ref[...])
    kd_ref[...] = kd.astype(BF16)
    vd_ref[...] = cd[:, 512:].astype(BF16)
    for c in range(tm // MOBA_BLOCK):
        km_ref[c] = jnp.mean(kd[c * MOBA_BLOCK:(c + 1) * MOBA_BLOCK, :], axis=0, keepdims=True)


def _in_proj(x2, lw, consts, S):
    T, D = x2.shape
    tm = TM_IN
    nt = T // tm
    tok = lambda w: pl.BlockSpec((tm, w), lambda i: (i, 0))
    pos_spec = pl.BlockSpec((tm, LANES), lambda i: (i % (S // tm), 0))
    ins = [
        (x2, tok(D)), (lw["attn_norm"], None), (lw["w_in"], None),
        (lw["mla_q_norm"], None), (lw["wq"], None), (lw["mla_kv_norm"], None), (lw["wkv"], None),
        (consts["pk"], None), (lw["mla_qg"], None), (lw["mla_kg"], None),
        (consts["g512"], None), (consts["r512"], None),
        (consts["cos"], pos_spec), (consts["sin"], pos_spec),
        (lw["sgu_norm"], None), (lw["sgu_w"], None), (lw["sgu_b"], None),
        (lw["pool_w"], None), (lw["pool_b"], None), (lw["pool_scale"], None),
        (lw["moba_qg"], None), (lw["moba_kg"], None), (consts["g256"], None),
    ]
    arrays = [a for a, _ in ins]
    specs = [s if s is not None else _full(a.shape) for a, s in ins]
    nb = tm // MOBA_BLOCK
    out_shape = [
        jax.ShapeDtypeStruct((T, 512), BF16), jax.ShapeDtypeStruct((T, 512), BF16),
        jax.ShapeDtypeStruct((T, 256), BF16), jax.ShapeDtypeStruct((T, 256), BF16),
        jax.ShapeDtypeStruct((T, 256), BF16), jax.ShapeDtypeStruct((T, 256), F32),
        jax.ShapeDtypeStruct((T, 256), BF16), jax.ShapeDtypeStruct((T, 256), BF16),
        jax.ShapeDtypeStruct((T // MOBA_BLOCK, 1, 256), F32),
    ]
    out_specs = [tok(512), tok(512), tok(256), tok(256), tok(256), tok(256), tok(256), tok(256),
                 pl.BlockSpec((nb, 1, 256), lambda i: (i, 0, 0))]
    q_scale = (MLA_QK ** -0.5) * math.log2(math.e)
    return pl.pallas_call(
        functools.partial(_in_proj_kernel, tiles_per_seq=S // tm, q_scale=q_scale),
        grid=(nt,), in_specs=specs, out_specs=out_specs, out_shape=out_shape,
        scratch_shapes=[pltpu.VMEM((POOL_MAXW, POOL_W), F32),
                        pltpu.VMEM((tm + POOL_MAXW, POOL_W), F32)],
        compiler_params=_cparams(("arbitrary",)), name="in_proj",
    )(*arrays)


def _moba_select_kernel(q_ref, km_ref, o_ref, *, slope_log2_step):
    hh = pl.program_id(1)
    li = pl.program_id(2)
    q = q_ref[0, 0]
    L = q.shape[1]
    nrow = km_ref.shape[2]
    g = jnp.dot(km_ref[0, 0], q, preferred_element_type=F32, precision=lax.Precision.HIGHEST)
    n_i = lax.broadcasted_iota(jnp.int32, (nrow, L), 0)
    own = (li * L + lax.broadcasted_iota(jnp.int32, (nrow, L), 1)) // MOBA_BLOCK
    g = jnp.where(n_i < own, g, NEG)
    sel = jnp.zeros((nrow, L), F32)
    for _ in range(MOBA_TOPK):
        mx = jnp.max(g, axis=0, keepdims=True)
        first = jnp.min(jnp.where(g == mx, n_i, nrow), axis=0, keepdims=True)
        hit = n_i == first
        sel = jnp.where(hit & (mx > 0.5 * NEG), 1.0, sel)
        g = jnp.where(hit, NEG, g)
    slope = jnp.exp2(-slope_log2_step * (hh + 1).astype(F32))
    rel = (n_i - own).astype(F32) * (slope * MOBA_BLOCK)
    bias = jnp.where((sel > 0.0) | (n_i == own), rel, NEG)
    bias = jnp.where(n_i == nrow - 1, slope, bias)
    o_ref[0, 0, 0:MOBA_HD, :] = (q * (MOBA_HD ** -0.5)).astype(BF16)
    o_ref[0, 0, MOBA_HD:, :] = bias.astype(BF16)


def _moba_select(qgT, kmean):
    B, H, hd, S = qgT.shape
    L = min(SEL_L, S)
    return pl.pallas_call(
        functools.partial(_moba_select_kernel, slope_log2_step=8.0 / MOBA_HEADS),
        grid=(B, H, S // L),
        in_specs=[pl.BlockSpec((1, 1, hd, L), lambda b, h, l: (b, h, 0, l)),
                  pl.BlockSpec((1, 1, 64, hd), lambda b, h, l: (b, h, 0, 0))],
        out_specs=pl.BlockSpec((1, 1, 2 * hd, L), lambda b, h, l: (b, h, 0, l)),
        out_shape=jax.ShapeDtypeStruct((B, H, 2 * hd, S), BF16),
        compiler_params=_cparams(("arbitrary",) * 3), name="moba_select",
    )(qgT, kmean)


def _flash_kernel(q_ref, k_ref, v_ref, o_ref, s_scr, acc_scr, *, use_exp2):
    H, _, _, tq = q_ref.shape[1:]
    tk = k_ref.shape[3]
    ratio = tk // tq
    ex = jnp.exp2 if use_exp2 else jnp.exp
    i = pl.program_id(1)
    jd = i // ratio
    kpos = jd * tk + lax.broadcasted_iota(jnp.int32, (tk, tq), 0)
    qpos = i * tq + lax.broadcasted_iota(jnp.int32, (tk, tq), 1)
    causal = kpos <= qpos

    dv = o_ref.shape[3]

    def scores(jt, h, masked):
        s = _dot(k_ref[0, h, jt], q_ref[0, h, 0])
        if masked:
            s = jnp.where(causal, s, NEG)
        s_scr[h] = s
        return jnp.max(s, axis=0, keepdims=True)

    def update(jt, h, cmax, m):
        m_new = jnp.maximum(m, cmax)
        alpha = ex(m - m_new)
        p = ex(s_scr[h] - m_new).astype(BF16)
        acc_scr[h] = acc_scr[h] * alpha + _dot(v_ref[0, h, jt], p)
        return m_new

    acc_scr[...] = jnp.zeros_like(acc_scr)
    ms = [jnp.full((1, tq), NEG, F32) for _ in range(H)]
    cmax = scores(jd, 0, True)
    for h in range(H):
        nxt = scores(jd, h + 1, True) if h + 1 < H else scores(0, 0, False)
        ms[h] = update(jd, h, cmax, ms[h])
        cmax = nxt

    def k_body(j, carry):
        ms, cmax = carry
        ms = list(ms)
        jn = jnp.minimum(j + 1, jd - 1)
        for h in range(H):
            nxt = scores(j, h + 1, False) if h + 1 < H else scores(jn, 0, False)
            ms[h] = update(j, h, cmax, ms[h])
            cmax = nxt
        return tuple(ms), cmax

    lax.fori_loop(0, jd, k_body, (tuple(ms), cmax))
    for h in range(H):
        acc = acc_scr[h]
        o_ref[0, h, 0] = (acc[:dv] / acc[dv:dv + 1]).astype(BF16)


def _flash(qT, k, vT, use_exp2, name):
    B, H, nq, dk, tq = qT.shape
    nk, tk = k.shape[2:4]
    dva = vT.shape[3]
    dv = dva - ONES_ROWS
    whole = lambda shp: pl.BlockSpec((1, H) + shp, lambda b, i: (b, 0, 0, 0, 0),
                                     pipeline_mode=pl.Buffered(1))
    tile = lambda shp: pl.BlockSpec((1, H, 1) + shp, lambda b, i: (b, 0, i, 0, 0))
    return pl.pallas_call(
        functools.partial(_flash_kernel, use_exp2=use_exp2),
        grid=(B, nq),
        in_specs=[tile((dk, tq)), whole((nk, tk, dk)), whole((nk, dva, tk))],
        out_specs=tile((dv, tq)),
        out_shape=jax.ShapeDtypeStruct((B, H, nq, dv, tq), BF16),
        scratch_shapes=[pltpu.VMEM((H, tk, tq), F32), pltpu.VMEM((H, dva, tq), F32)],
        compiler_params=_cparams(("arbitrary", "arbitrary")), name=name,
    )(qT, k, vT)


def _merge_kernel(x_ref, ya_ref, yb_ref, yc_ref, yd_ref, g_ref, wg_ref, wb_ref, wo_ref,
                  fg_ref, rw_ref, rb_ref, x1_ref, h2_ref, gt_ref):
    x = x_ref[...]
    h = _rms(x, g_ref[...]).astype(BF16)
    merged = None
    for bi, y_ref in enumerate((ya_ref, yb_ref, yc_ref, yd_ref)):
        t = jax.nn.sigmoid(_dot(h, wg_ref[bi])) * _dot(y_ref[...], wb_ref[bi])
        merged = t if merged is None else merged + t
    x1 = x + _dot(merged.astype(BF16), wo_ref[...])
    x1_ref[...] = x1
    h2 = _rms(x1, fg_ref[...])
    h2_ref[...] = h2.astype(BF16)

    logits = lax.dot_general(rw_ref[...], h2, (((1,), (1,)), ((), ())),
                             preferred_element_type=F32, precision=lax.Precision.HIGHEST)
    scores = jax.nn.sigmoid(logits)
    biased = scores + rb_ref[...]
    s_rows = [scores[e:e + 1, :] for e in range(N_EXPERTS)]
    b_rows = [biased[e:e + 1, :] for e in range(N_EXPERTS)]

    def top2(vals):
        m1 = vals[0]
        i1 = jnp.zeros_like(m1, dtype=jnp.int32)
        for j in range(1, len(vals)):
            better = vals[j] > m1
            i1 = jnp.where(better, j, i1)
            m1 = jnp.where(better, vals[j], m1)
        m2 = None
        i2 = None
        for j in range(len(vals)):
            cand = jnp.where(i1 == j, NEG, vals[j])
            if m2 is None:
                m2, i2 = cand, jnp.zeros_like(i1)
            else:
                better = cand > m2
                i2 = jnp.where(better, j, i2)
                m2 = jnp.where(better, cand, m2)
        return m1, i1, m2, i2

    best = None
    for gi in range(N_GROUPS):
        bm1, bi1, bm2, bi2 = top2(b_rows[gi * EPG:(gi + 1) * EPG])
        gs = bm1 + bm2
        e1 = bi1 + gi * EPG
        e2 = bi2 + gi * EPG
        if best is None:
            best = (gs, e1, e2)
        else:
            better = gs > best[0]
            best = (jnp.where(better, gs, best[0]), jnp.where(better, e1, best[1]),
                    jnp.where(better, e2, best[2]))
    _, e1, e2 = best
    w1 = jnp.zeros_like(s_rows[0])
    w2 = jnp.zeros_like(s_rows[0])
    for e in range(N_EXPERTS):
        w1 = jnp.where(e1 == e, s_rows[e], w1)
        w2 = jnp.where(e2 == e, s_rows[e], w2)
    tot = w1 + w2
    w1 = w1 / tot
    w2 = w2 / tot
    row = lax.broadcasted_iota(jnp.int32, scores.shape, 0)
    gates = jnp.where(row == e1, w1, 0.0) + jnp.where(row == e2, w2, 0.0)
    gt_ref[...] = gates


def _merge(x2, ya, yb, yc, yd, lw, router_w_t, router_b):
    T, D = x2.shape
    tm = TM_MERGE
    tok = lambda w: pl.BlockSpec((tm, w), lambda i: (i, 0))
    consts = [lw["attn_norm"], lw["w_gate"], lw["w_branch"], lw["w_out"], lw["ffn_norm"],
              router_w_t, router_b]
    return pl.pallas_call(
        _merge_kernel, grid=(T // tm,),
        in_specs=[tok(D), tok(256), tok(256), tok(256), tok(256)] + [_full(a.shape) for a in consts],
        out_specs=[tok(D), tok(D), pl.BlockSpec((N_EXPERTS, tm), lambda i: (0, i))],
        out_shape=[jax.ShapeDtypeStruct((T, D), F32), jax.ShapeDtypeStruct((T, D), BF16),
                   jax.ShapeDtypeStruct((N_EXPERTS, T), F32)],
        compiler_params=_cparams(("arbitrary",)), name="merge",
    )(x2, ya, yb, yc, yd, *consts)


def _moe_kernel(x1_ref, h2_ref, g_ref, wg_ref, wu_ref, wd_ref, o_ref):
    e = pl.program_id(1)
    h2 = h2_ref[...]
    a = jax.nn.silu(_dot(h2, wg_ref[0])) * _dot(h2, wu_ref[0])
    lane = lax.broadcasted_iota(jnp.int32, g_ref.shape, 1)
    gate = jnp.sum(jnp.where(lane == e, g_ref[...], 0.0), axis=1, keepdims=True)
    y = _dot((a * gate).astype(BF16), wd_ref[0])

    @pl.when(e == 0)
    def _():
        o_ref[...] = x1_ref[...] + y

    @pl.when(e > 0)
    def _():
        o_ref[...] += y


def _moe(x1, h2, gates, wg, wu, wd):
    T, D = x1.shape
    tm = min(TM_MOE, T)
    ne, _, ff = wg.shape
    return pl.pallas_call(
        _moe_kernel, grid=(T // tm, ne),
        in_specs=[pl.BlockSpec((tm, D), lambda i, e: (i, 0)),
                  pl.BlockSpec((tm, D), lambda i, e: (i, 0)),
                  pl.BlockSpec((tm, gates.shape[1]), lambda i, e: (i, 0)),
                  pl.BlockSpec((1, D, ff), lambda i, e: (e, 0, 0)),
                  pl.BlockSpec((1, D, ff), lambda i, e: (e, 0, 0)),
                  pl.BlockSpec((1, ff, D), lambda i, e: (e, 0, 0))],
        out_specs=pl.BlockSpec((tm, D), lambda i, e: (i, 0)),
        out_shape=jax.ShapeDtypeStruct((T, D), F32),
        compiler_params=_cparams(("arbitrary", "arbitrary")), name="moe",
    )(x1, h2, gates, wg, wu, wd)


def _constants(S):
    half = MLA_ROPE // 2
    inv = ROPE_THETA ** (-jnp.arange(half, dtype=F32) / half)
    ang = jnp.arange(S, dtype=F32)[:, None] * inv[None, :]
    cos = jnp.ones((S, LANES), F32)
    cos = cos.at[:, MLA_NOPE:MLA_NOPE + half].set(jnp.cos(ang))
    cos = cos.at[:, MLA_NOPE + half:MLA_QK].set(jnp.cos(ang))
    sin = jnp.zeros((S, LANES), F32)
    sin = sin.at[:, MLA_NOPE:MLA_NOPE + half].set(jnp.sin(ang))
    sin = sin.at[:, MLA_NOPE + half:MLA_QK].set(jnp.sin(ang))

    r = np.zeros((LANES, LANES), np.float32)
    for j in range(half):
        r[MLA_NOPE + half + j, MLA_NOPE + j] = -1.0
        r[MLA_NOPE + j, MLA_NOPE + half + j] = 1.0
    g = np.zeros((LANES, LANES), np.float32)
    g[:MLA_QK, :] = 1.0
    eye4 = np.eye(MLA_HEADS, dtype=np.float32)
    pk = np.zeros((LANES, 4 * LANES), np.float32)
    for hh in range(MLA_HEADS):
        for j in range(MLA_ROPE):
            pk[j, hh * LANES + MLA_NOPE + j] = 1.0
    g256 = np.kron(np.eye(MOBA_HEADS, dtype=np.float32), np.ones((MOBA_HD, MOBA_HD), np.float32))

    nkb = S // MOBA_BLOCK
    kpos = np.arange(S)
    kc = np.zeros((S, MOBA_HD), np.float32)
    blk = kpos // MOBA_BLOCK
    m = blk < min(nkb, MOBA_HD) - 1
    kc[kpos[m], blk[m]] = 1.0
    kc[:, MOBA_HD - 1] = kpos % MOBA_BLOCK
    return {
        "cos": cos, "sin": sin,
        "r512": jnp.asarray(np.kron(eye4, r), BF16), "g512": jnp.asarray(np.kron(eye4, g), BF16),
        "pk": jnp.asarray(pk, BF16), "g256": jnp.asarray(g256, BF16), "kc": jnp.asarray(kc, BF16),
    }


def _head_pad(v):
    one = jnp.concatenate([v, jnp.zeros((LANES - MLA_QK,), F32)])
    return jnp.tile(one, MLA_HEADS)[None, :]


def _layer_weights(l, p):
    D = p["w_in"].shape[1]
    w_in = p["w_in"][l]
    o1 = MLA_Q_LORA + MLA_KV_LORA + MLA_ROPE
    o2 = o1 + 2 * SGU_W
    o3 = o2 + POOL_W
    w = jnp.concatenate([w_in[:, :o1], jnp.zeros((D, C_SGU - o1), F32), w_in[:, o1:]], axis=1)

    wq = p["mla_w_q_up"][l].reshape(MLA_Q_LORA, MLA_HEADS, MLA_QK)
    wq = jnp.pad(wq, ((0, 0), (0, 0), (0, LANES - MLA_QK))).reshape(MLA_Q_LORA, MLA_HEADS * LANES)
    wkv = p["mla_w_kv_up"][l].reshape(MLA_KV_LORA, MLA_HEADS, MLA_NOPE + MLA_V)
    wk = jnp.pad(wkv[:, :, :MLA_NOPE], ((0, 0), (0, 0), (0, LANES - MLA_NOPE)))
    wk = wk.reshape(MLA_KV_LORA, MLA_HEADS * LANES)
    wv = wkv[:, :, MLA_NOPE:].reshape(MLA_KV_LORA, MLA_HEADS * MLA_V)

    tri = jnp.tril(jnp.ones((SGU_CHUNK, SGU_CHUNK), dtype=bool))
    sgu_w = jnp.where(tri[None], p["sgu_w"][l], 0.0)
    sgu_b = jnp.repeat(p["sgu_b"][l].T, SGU_W // SGU_GROUPS, axis=1)
    pool_w = jax.scipy.linalg.block_diag(*[p["pool_w"][l, gi] for gi in range(len(POOL_WINDOWS))])

    ne = p["moe_w_gate"].shape[1]
    return {
        "attn_norm": p["attn_norm"][l][None, :], "w_in": w.astype(BF16),
        "mla_q_norm": p["mla_q_norm"][l][None, :], "wq": wq.astype(BF16),
        "mla_kv_norm": p["mla_kv_norm"][l][None, :],
        "wkv": jnp.concatenate([wk, wv], axis=1).astype(BF16),
        "mla_qg": _head_pad(p["mla_q_gain"][l]), "mla_kg": _head_pad(p["mla_k_gain"][l]),
        "sgu_norm": p["sgu_norm"][l][None, :], "sgu_w": sgu_w.astype(BF16), "sgu_b": sgu_b,
        "pool_w": pool_w.astype(BF16), "pool_b": p["pool_b"][l][None, :],
        "pool_scale": p["pool_scale"][l][None, :],
        "moba_qg": jnp.tile(p["moba_q_gain"][l], MOBA_HEADS)[None, :],
        "moba_kg": jnp.tile(p["moba_k_gain"][l], MOBA_HEADS)[None, :],
        "w_gate": p["w_gate"][l].astype(BF16), "w_branch": p["w_branch"][l].astype(BF16),
        "w_out": p["w_out"][l].astype(BF16), "ffn_norm": p["ffn_norm"][l][None, :],
        "ffn_g": jnp.concatenate([p["shared_w_gate"][l][None], p["moe_w_gate"][l]]).astype(BF16),
        "ffn_u": jnp.concatenate([p["shared_w_up"][l][None], p["moe_w_up"][l]]).astype(BF16),
        "ffn_d": jnp.concatenate([p["shared_w_down"][l][None], p["moe_w_down"][l]]).astype(BF16),
    }


def _to_qT(a, B, S, H, d):
    return a.reshape(B, S // TQ, TQ, H, d).transpose(0, 3, 1, 4, 2)


def _to_k(a, B, S, H, d):
    return a.reshape(B, S // TK, TK, H, d).transpose(0, 3, 1, 2, 4)


def _to_vT(a, B, S, H, d):
    v = a.reshape(B, S // TK, TK, H, d).transpose(0, 3, 1, 4, 2)
    return jnp.concatenate([v, jnp.ones((B, H, S // TK, ONES_ROWS, TK), v.dtype)], axis=3)


def _from_oT(o, B, S):
    _, H, nq, d, tq = o.shape
    return o.transpose(0, 2, 4, 1, 3).reshape(B * S, H * d)


def kernel(x, attn_norm, w_in, mla_q_norm, mla_w_q_up, mla_kv_norm, mla_w_kv_up, mla_q_gain, mla_k_gain, sgu_norm, sgu_w, sgu_b, pool_w, pool_b, pool_scale, moba_q_gain, moba_k_gain, w_gate, w_branch, w_out, ffn_norm, router_w, router_bias, moe_w_gate, moe_w_up, moe_w_down, shared_w_gate, shared_w_up, shared_w_down):
    p = dict(attn_norm=attn_norm, w_in=w_in, mla_q_norm=mla_q_norm, mla_w_q_up=mla_w_q_up,
             mla_kv_norm=mla_kv_norm, mla_w_kv_up=mla_w_kv_up, mla_q_gain=mla_q_gain,
             mla_k_gain=mla_k_gain, sgu_norm=sgu_norm, sgu_w=sgu_w, sgu_b=sgu_b, pool_w=pool_w,
             pool_b=pool_b, pool_scale=pool_scale, moba_q_gain=moba_q_gain,
             moba_k_gain=moba_k_gain, w_gate=w_gate, w_branch=w_branch, w_out=w_out,
             ffn_norm=ffn_norm, moe_w_gate=moe_w_gate, moe_w_up=moe_w_up, moe_w_down=moe_w_down,
             shared_w_gate=shared_w_gate, shared_w_up=shared_w_up, shared_w_down=shared_w_down)
    B, S, D = x.shape
    T = B * S
    depth = attn_norm.shape[0]
    nkb = S // MOBA_BLOCK
    assert S % TK == 0 and S % TM_IN == 0 and T % TM_MERGE == 0 and nkb <= MOBA_HD
    consts = _constants(S)
    router_w_t = router_w.T
    router_b = router_bias[:, None]
    kc = jnp.broadcast_to(consts["kc"][None, :, None, :], (B, S, MOBA_HEADS, MOBA_HD))

    x2 = x.reshape(T, D)
    for l in range(depth):
        lw = _layer_weights(l, p)
        qa, ka, va, yb, yc, qd, kd, vd, km = _in_proj(x2, lw, consts, S)

        oa = _flash(_to_qT(qa, B, S, MLA_HEADS, LANES), _to_k(ka, B, S, MLA_HEADS, LANES),
                    _to_vT(va, B, S, MLA_HEADS, MLA_V), True, "mla_attn")
        ya = _from_oT(oa, B, S)

        qgT = qd.reshape(B, S, MOBA_HEADS, MOBA_HD).transpose(0, 2, 3, 1)
        kmean = km.reshape(B, nkb, MOBA_HEADS, MOBA_HD).transpose(0, 2, 1, 3)
        kmean = jnp.pad(kmean, ((0, 0), (0, 0), (0, MOBA_HD - nkb), (0, 0)))
        q_aug = _moba_select(qgT, kmean)
        q_aug = q_aug.reshape(B, MOBA_HEADS, 2 * MOBA_HD, S // TQ, TQ).transpose(0, 1, 3, 2, 4)
        k_aug = jnp.concatenate([kd.reshape(B, S, MOBA_HEADS, MOBA_HD), kc], axis=-1)
        k_aug = k_aug.reshape(B, S // TK, TK, MOBA_HEADS, 2 * MOBA_HD).transpose(0, 3, 1, 2, 4)
        od = _flash(q_aug, k_aug, _to_vT(vd, B, S, MOBA_HEADS, MOBA_HD), False, "moba_attn")
        yd = _from_oT(od, B, S)

        x1, h2, gates_t = _merge(x2, ya, yb, yc, yd, lw, router_w_t, router_b)
        gates = jnp.concatenate([jnp.ones((T, 1), F32), gates_t.T], axis=1)
        x2 = _moe(x1, h2, gates, lw["ffn_g"], lw["ffn_u"], lw["ffn_d"])
    return x2.reshape(B, S, D)
```

```python
import functools
import math

import jax
import jax.numpy as jnp
import numpy as np
from jax import lax
from jax.experimental import pallas as pl
from jax.experimental.pallas import tpu as pltpu

F32 = jnp.float32
BF16 = jnp.bfloat16
EPS = 1e-6
NEG = -1e30

LANES = 128
VMEM_LIMIT = 56 * 1024 * 1024

MLA_HEADS = 4
MLA_NOPE = 64
MLA_ROPE = 32
MLA_V = 64
MLA_Q_LORA = 256
MLA_KV_LORA = 128
ROPE_THETA = 10000.0
MLA_QK = MLA_NOPE + MLA_ROPE

SGU_GROUPS = 4
SGU_CHUNK = 128
SGU_W = 256

POOL_WINDOWS = (2, 4, 8, 16)
POOL_MAXW = 16
POOL_W = 256

MOBA_HEADS = 4
MOBA_HD = 64
MOBA_BLOCK = 256
MOBA_TOPK = 3

N_EXPERTS = 16
N_GROUPS = 4
EPG = 4

C_MLA = 0
C_SGU = 512
C_POOL = 1024
C_MOBA = 1280
C_END = 2048

TM_IN = 256
TQ = 512
TK = 512
TM_MERGE = 512
TM_MOE = 1024
SEL_L = 2048
MOE_CHUNK = 160
ONES_ROWS = 16


def _cparams(sem):
    return pltpu.CompilerParams(dimension_semantics=sem, vmem_limit_bytes=VMEM_LIMIT)


def _full(shape):
    n = len(shape)
    return pl.BlockSpec(shape, lambda *_: (0,) * n)


def _rms(x, g):
    return x * lax.rsqrt(jnp.mean(x * x, axis=-1, keepdims=True) + EPS) * g


def _dot(a, b):
    return jnp.dot(a, b, preferred_element_type=F32)


def _in_proj_kernel(x_ref, g_ref, w_ref, qn_ref, wq_ref, kvn_ref, wkv_ref, pk_ref,
                    qg_ref, kg_ref, g512_ref, r512_ref, cos_ref, sin_ref,
                    sn_ref, sw_ref, sb_ref, pw_ref, pb_ref, ps_ref,
                    mqg_ref, mkg_ref, g256_ref,
                    qa_ref, ka_ref, va_ref, yb_ref, yc_ref, qd_ref, kd_ref, vd_ref, km_ref,
                    halo_ref, buf_ref, *, tiles_per_seq, q_scale):
    tm = x_ref.shape[0]
    i = pl.program_id(0)
    x = x_ref[...]
    h = _rms(x, g_ref[...]).astype(BF16)

    cm = _dot(h, w_ref[:, C_MLA:C_SGU])
    c_q = _rms(cm[:, :MLA_Q_LORA], qn_ref[...]).astype(BF16)
    c_kv = _rms(cm[:, MLA_Q_LORA:MLA_Q_LORA + MLA_KV_LORA], kvn_ref[...]).astype(BF16)
    kpe_grp = cm[:, MLA_Q_LORA + MLA_KV_LORA:].astype(BF16)
    q = _dot(c_q, wq_ref[...])
    kv = _dot(c_kv, wkv_ref[...])
    k = kv[:, :512] + _dot(kpe_grp, pk_ref[...])
    va_ref[...] = kv[:, 512:].astype(BF16)
    cos = jnp.concatenate([cos_ref[...]] * MLA_HEADS, axis=1)
    sin = jnp.concatenate([sin_ref[...]] * MLA_HEADS, axis=1)
    inv_d = 1.0 / MLA_QK

    def head_norm_rope(t, gain):
        ms = _dot((t * t).astype(BF16), g512_ref[...]) * inv_d
        t = t * lax.rsqrt(ms + EPS) * gain
        rot = _dot(t.astype(BF16), r512_ref[...])
        return t * cos + rot * sin

    qa_ref[...] = (head_norm_rope(q, qg_ref[...]) * q_scale).astype(BF16)
    ka_ref[...] = head_norm_rope(k, kg_ref[...]).astype(BF16)

    z = jax.nn.gelu(_dot(h, w_ref[:, C_SGU:C_POOL]))
    u = z[:, :SGU_W]
    vv = _rms(z[:, SGU_W:], sn_ref[...]).astype(BF16)
    lane_grp = lax.broadcasted_iota(jnp.int32, (SGU_CHUNK, SGU_W), 1) // (SGU_W // SGU_GROUPS)
    for c in range(tm // SGU_CHUNK):
        vc = vv[c * SGU_CHUNK:(c + 1) * SGU_CHUNK, :]
        y = sb_ref[...]
        for gi in range(SGU_GROUPS):
            y = y + jnp.where(lane_grp == gi, _dot(sw_ref[gi], vc), 0.0)
        yb_ref[c * SGU_CHUNK:(c + 1) * SGU_CHUNK, :] = (
            u[c * SGU_CHUNK:(c + 1) * SGU_CHUNK, :] * y).astype(BF16)

    xp = _dot(h, w_ref[:, C_POOL:C_MOBA])

    @pl.when(i % tiles_per_seq == 0)
    def _():
        halo_ref[...] = jnp.zeros_like(halo_ref)

    buf_ref[0:POOL_MAXW, :] = halo_ref[...]
    buf_ref[POOL_MAXW:, :] = xp
    halo_ref[...] = xp[tm - POOL_MAXW:, :]
    pos = (i % tiles_per_seq) * tm + lax.broadcasted_iota(jnp.int32, (tm, POOL_W), 0)
    lane_w = lax.broadcasted_iota(jnp.int32, (tm, POOL_W), 1) // (POOL_W // len(POOL_WINDOWS))
    acc = xp
    pooled = jnp.zeros((tm, POOL_W), F32)
    shift = 1
    for gi, w in enumerate(POOL_WINDOWS):
        while shift < w:
            acc = acc + buf_ref[POOL_MAXW - shift:POOL_MAXW - shift + tm, :]
            shift += 1
        cnt = jnp.minimum(pos + 1, w).astype(F32)
        pooled = jnp.where(lane_w == gi, acc / cnt, pooled)
    p = (pooled - xp).astype(BF16)
    yc_ref[...] = ((_dot(p, pw_ref[...]) + pb_ref[...]) * ps_ref[...]).astype(BF16)

    cd = _dot(h, w_ref[:, C_MOBA:C_END])
    inv_hd = 1.0 / MOBA_HD

    def head_norm(t, gain):
        ms = _dot((t * t).astype(BF16), g256_ref[...]) * inv_hd
        return t * lax.rsqrt(ms + EPS) * gain

    qd_ref[...] = head_norm(cd[:, :256], mqg_ref[...])
    kd = head_norm(cd[:, 256:512], mkg_ref[...])
    kd_ref[...] = kd.astype(BF16)
    vd_ref[...] = cd[:, 512:].astype(BF16)
    for c in range(tm // MOBA_BLOCK):
        km_ref[c] = jnp.mean(kd[c * MOBA_BLOCK:(c + 1) * MOBA_BLOCK, :], axis=0, keepdims=True)


def _in_proj(x2, lw, consts, S):
    T, D = x2.shape
    tm = TM_IN
    nt = T // tm
    tok = lambda w: pl.BlockSpec((tm, w), lambda i: (i, 0))
    pos_spec = pl.BlockSpec((tm, LANES), lambda i: (i % (S // tm), 0))
    ins = [
        (x2, tok(D)), (lw["attn_norm"], None), (lw["w_in"], None),
        (lw["mla_q_norm"], None), (lw["wq"], None), (lw["mla_kv_norm"], None), (lw["wkv"], None),
        (consts["pk"], None), (lw["mla_qg"], None), (lw["mla_kg"], None),
        (consts["g512"], None), (consts["r512"], None),
        (consts["cos"], pos_spec), (consts["sin"], pos_spec),
        (lw["sgu_norm"], None), (lw["sgu_w"], None), (lw["sgu_b"], None),
        (lw["pool_w"], None), (lw["pool_b"], None), (lw["pool_scale"], None),
        (lw["moba_qg"], None), (lw["moba_kg"], None), (consts["g256"], None),
    ]
    arrays = [a for a, _ in ins]
    specs = [s if s is not None else _full(a.shape) for a, s in ins]
    nb = tm // MOBA_BLOCK
    out_shape = [
        jax.ShapeDtypeStruct((T, 512), BF16), jax.ShapeDtypeStruct((T, 512), BF16),
        jax.ShapeDtypeStruct((T, 256), BF16), jax.ShapeDtypeStruct((T, 256), BF16),
        jax.ShapeDtypeStruct((T, 256), BF16), jax.ShapeDtypeStruct((T, 256), F32),
        jax.ShapeDtypeStruct((T, 256), BF16), jax.ShapeDtypeStruct((T, 256), BF16),
        jax.ShapeDtypeStruct((T // MOBA_BLOCK, 1, 256), F32),
    ]
    out_specs = [tok(512), tok(512), tok(256), tok(256), tok(256), tok(256), tok(256), tok(256),
                 pl.BlockSpec((nb, 1, 256), lambda i: (i, 0, 0))]
    q_scale = (MLA_QK ** -0.5) * math.log2(math.e)
    return pl.pallas_call(
        functools.partial(_in_proj_kernel, tiles_per_seq=S // tm, q_scale=q_scale),
        grid=(nt,), in_specs=specs, out_specs=out_specs, out_shape=out_shape,
        scratch_shapes=[pltpu.VMEM((POOL_MAXW, POOL_W), F32),
                        pltpu.VMEM((tm + POOL_MAXW, POOL_W), F32)],
        compiler_params=_cparams(("arbitrary",)), name="in_proj",
    )(*arrays)


def _moba_select_kernel(q_ref, km_ref, o_ref, *, slope_log2_step):
    hh = pl.program_id(1)
    li = pl.program_id(2)
    q = q_ref[0, 0]
    L = q.shape[1]
    nrow = km_ref.shape[2]
    g = jnp.dot(km_ref[0, 0], q, preferred_element_type=F32, precision=lax.Precision.HIGHEST)
    n_i = lax.broadcasted_iota(jnp.int32, (nrow, L), 0)
    own = (li * L + lax.broadcasted_iota(jnp.int32, (nrow, L), 1)) // MOBA_BLOCK
    g = jnp.where(n_i < own, g, NEG)
    sel = jnp.zeros((nrow, L), F32)
    for _ in range(MOBA_TOPK):
        mx = jnp.max(g, axis=0, keepdims=True)
        first = jnp.min(jnp.where(g == mx, n_i, nrow), axis=0, keepdims=True)
        hit = n_i == first
        sel = jnp.where(hit & (mx > 0.5 * NEG), 1.0, sel)
        g = jnp.where(hit, NEG, g)
    slope = jnp.exp2(-slope_log2_step * (hh + 1).astype(F32))
    rel = (n_i - own).astype(F32) * (slope * MOBA_BLOCK)
    bias = jnp.where((sel > 0.0) | (n_i == own), rel, NEG)
    bias = jnp.where(n_i == nrow - 1, slope, bias)
    o_ref[0, 0, 0:MOBA_HD, :] = (q * (MOBA_HD ** -0.5)).astype(BF16)
    o_ref[0, 0, MOBA_HD:, :] = bias.astype(BF16)


def _moba_select(qgT, kmean):
    B, H, hd, S = qgT.shape
    L = min(SEL_L, S)
    return pl.pallas_call(
        functools.partial(_moba_select_kernel, slope_log2_step=8.0 / MOBA_HEADS),
        grid=(B, H, S // L),
        in_specs=[pl.BlockSpec((1, 1, hd, L), lambda b, h, l: (b, h, 0, l)),
                  pl.BlockSpec((1, 1, 64, hd), lambda b, h, l: (b, h, 0, 0))],
        out_specs=pl.BlockSpec((1, 1, 2 * hd, L), lambda b, h, l: (b, h, 0, l)),
        out_shape=jax.ShapeDtypeStruct((B, H, 2 * hd, S), BF16),
        compiler_params=_cparams(("arbitrary",) * 3), name="moba_select",
    )(qgT, kmean)


def _flash_kernel(q_ref, k_ref, v_ref, o_ref, s_scr, acc_scr, *, use_exp2):
    H, _, _, tq = q_ref.shape[1:]
    tk = k_ref.shape[3]
    ratio = tk // tq
    ex = jnp.exp2 if use_exp2 else jnp.exp
    i = pl.program_id(1)
    jd = i // ratio
    kpos = jd * tk + lax.broadcasted_iota(jnp.int32, (tk, tq), 0)
    qpos = i * tq + lax.broadcasted_iota(jnp.int32, (tk, tq), 1)
    causal = kpos <= qpos

    dv = o_ref.shape[3]

    def scores(jt, h, masked):
        s = _dot(k_ref[0, h, jt], q_ref[0, h, 0])
        if masked:
            s = jnp.where(causal, s, NEG)
        s_scr[h] = s
        return jnp.max(s, axis=0, keepdims=True)

    def update(jt, h, cmax, m):
        m_new = jnp.maximum(m, cmax)
        alpha = ex(m - m_new)
        p = ex(s_scr[h] - m_new).astype(BF16)
        acc_scr[h] = acc_scr[h] * alpha + _dot(v_ref[0, h, jt], p)
        return m_new

    acc_scr[...] = jnp.zeros_like(acc_scr)
    ms = [jnp.full((1, tq), NEG, F32) for _ in range(H)]
    cmax = scores(jd, 0, True)
    for h in range(H):
        nxt = scores(jd, h + 1, True) if h + 1 < H else scores(0, 0, False)
        ms[h] = update(jd, h, cmax, ms[h])
        cmax = nxt

    def k_body(j, carry):
        ms, cmax = carry
        ms = list(ms)
        jn = jnp.minimum(j + 1, jd - 1)
        for h in range(H):
            nxt = scores(j, h + 1, False) if h + 1 < H else scores(jn, 0, False)
            ms[h] = update(j, h, cmax, ms[h])
            cmax = nxt
        return tuple(ms), cmax

    lax.fori_loop(0, jd, k_body, (tuple(ms), cmax))
    for h in range(H):
        acc = acc_scr[h]
        o_ref[0, h, 0] = (acc[:dv] / acc[dv:dv + 1]).astype(BF16)


def _flash(qT, k, vT, use_exp2, name):
    B, H, nq, dk, tq = qT.shape
    nk, tk = k.shape[2:4]
    dva = vT.shape[3]
    dv = dva - ONES_ROWS
    whole = lambda shp: pl.BlockSpec((1, H) + shp, lambda b, i: (b, 0, 0, 0, 0),
                                     pipeline_mode=pl.Buffered(1))
    tile = lambda shp: pl.BlockSpec((1, H, 1) + shp, lambda b, i: (b, 0, i, 0, 0))
    return pl.pallas_call(
        functools.partial(_flash_kernel, use_exp2=use_exp2),
        grid=(B, nq),
        in_specs=[tile((dk, tq)), whole((nk, tk, dk)), whole((nk, dva, tk))],
        out_specs=tile((dv, tq)),
        out_shape=jax.ShapeDtypeStruct((B, H, nq, dv, tq), BF16),
        scratch_shapes=[pltpu.VMEM((H, tk, tq), F32), pltpu.VMEM((H, dva, tq), F32)],
        compiler_params=_cparams(("arbitrary", "arbitrary")), name=name,
    )(qT, k, vT)


def _merge_kernel(x_ref, ya_ref, yb_ref, yc_ref, yd_ref, g_ref, wg_ref, wb_ref, wo_ref,
                  fg_ref, rw_ref, rb_ref, x1_ref, h2_ref, rt_ref, cnt_ref):
    x = x_ref[...]
    h = _rms(x, g_ref[...]).astype(BF16)
    merged = None
    for bi, y_ref in enumerate((ya_ref, yb_ref, yc_ref, yd_ref)):
        t = jax.nn.sigmoid(_dot(h, wg_ref[bi])) * _dot(y_ref[...], wb_ref[bi])
        merged = t if merged is None else merged + t
    x1 = x + _dot(merged.astype(BF16), wo_ref[...])
    x1_ref[...] = x1
    h2 = _rms(x1, fg_ref[...])
    h2_ref[...] = h2.astype(BF16)

    logits = lax.dot_general(rw_ref[...], h2, (((1,), (1,)), ((), ())),
                             preferred_element_type=F32, precision=lax.Precision.HIGHEST)
    scores = jax.nn.sigmoid(logits)
    biased = scores + rb_ref[...]
    s_rows = [scores[e:e + 1, :] for e in range(N_EXPERTS)]
    b_rows = [biased[e:e + 1, :] for e in range(N_EXPERTS)]

    def top2(vals):
        m1 = vals[0]
        i1 = jnp.zeros_like(m1, dtype=jnp.int32)
        for j in range(1, len(vals)):
            better = vals[j] > m1
            i1 = jnp.where(better, j, i1)
            m1 = jnp.where(better, vals[j], m1)
        m2 = None
        i2 = None
        for j in range(len(vals)):
            cand = jnp.where(i1 == j, NEG, vals[j])
            if m2 is None:
                m2, i2 = cand, jnp.zeros_like(i1)
            else:
                better = cand > m2
                i2 = jnp.where(better, j, i2)
                m2 = jnp.where(better, cand, m2)
        return m1, i1, m2, i2

    best = None
    for gi in range(N_GROUPS):
        bm1, bi1, bm2, bi2 = top2(b_rows[gi * EPG:(gi + 1) * EPG])
        gs = bm1 + bm2
        e1 = bi1 + gi * EPG
        e2 = bi2 + gi * EPG
        if best is None:
            best = (gs, e1, e2)
        else:
            better = gs > best[0]
            best = (jnp.where(better, gs, best[0]), jnp.where(better, e1, best[1]),
                    jnp.where(better, e2, best[2]))
    _, e1, e2 = best
    w1 = jnp.zeros_like(s_rows[0])
    w2 = jnp.zeros_like(s_rows[0])
    for e in range(N_EXPERTS):
        w1 = jnp.where(e1 == e, s_rows[e], w1)
        w2 = jnp.where(e2 == e, s_rows[e], w2)
    tot = w1 + w2
    w1 = w1 / tot
    w2 = w2 / tot
    r8 = lax.broadcasted_iota(jnp.int32, (8, scores.shape[1]), 0)
    rt_ref[...] = jnp.where(r8 == 0, e1.astype(F32), jnp.where(r8 == 1, e2.astype(F32),
                            jnp.where(r8 == 2, w1, jnp.where(r8 == 3, w2, 0.0))))
    row = lax.broadcasted_iota(jnp.int32, scores.shape, 0)
    assigned = jnp.where(row == e1, 1.0, jnp.where(row == e2, 1.0, 0.0))
    cnt = jnp.sum(assigned, axis=1, keepdims=True)
    cnt_ref[0] = jnp.broadcast_to(cnt, (N_EXPERTS, LANES)).astype(jnp.int32)


def _merge(x2, ya, yb, yc, yd, lw, router_w_t, router_b):
    T, D = x2.shape
    tm = TM_MERGE
    tok = lambda w: pl.BlockSpec((tm, w), lambda i: (i, 0))
    consts = [lw["attn_norm"], lw["w_gate"], lw["w_branch"], lw["w_out"], lw["ffn_norm"],
              router_w_t, router_b]
    return pl.pallas_call(
        _merge_kernel, grid=(T // tm,),
        in_specs=[tok(D), tok(256), tok(256), tok(256), tok(256)] + [_full(a.shape) for a in consts],
        out_specs=[tok(D), tok(D), pl.BlockSpec((8, tm), lambda i: (0, i)),
                   pl.BlockSpec((1, N_EXPERTS, LANES), lambda i: (i, 0, 0))],
        out_shape=[jax.ShapeDtypeStruct((T, D), F32), jax.ShapeDtypeStruct((T, D), BF16),
                   jax.ShapeDtypeStruct((8, T), F32),
                   jax.ShapeDtypeStruct((T // tm, N_EXPERTS, LANES), jnp.int32)],
        compiler_params=_cparams(("arbitrary",)), name="merge",
    )(x2, ya, yb, yc, yd, *consts)


def _moe_kernel(cnt_ref, x1_ref, h2_ref, rt_ref, tri_ref, wg_ref, wu_ref, wd_ref, o_ref,
                rank_scr, gate_scr):
    i = pl.program_id(0)
    e = pl.program_id(1)
    tm = h2_ref.shape[0]

    def ffn(rows, scale):
        a = jax.nn.silu(_dot(rows, wg_ref[0])) * _dot(rows, wu_ref[0])
        if scale is not None:
            a = a * scale
        return _dot(a.astype(BF16), wd_ref[0])

    @pl.when(e == 0)
    def _():
        o_ref[...] = x1_ref[...] + ffn(h2_ref[...], None)
        row = lax.broadcasted_iota(jnp.int32, (N_EXPERTS, tm), 0).astype(F32)
        e1, e2 = rt_ref[0:1, :], rt_ref[1:2, :]
        w1, w2 = rt_ref[2:3, :], rt_ref[3:4, :]
        assigned = jnp.where(row == e1, 1.0, jnp.where(row == e2, 1.0, 0.0))
        before = _dot(assigned.astype(BF16), tri_ref[...])
        rank_scr[...] = jnp.where(assigned > 0.0, before, -1.0)
        gate_scr[...] = jnp.where(row == e1, w1, jnp.where(row == e2, w2, 0.0))

    @pl.when(e > 0)
    def _():
        ex = e - 1
        cnt = cnt_ref[i * N_EXPERTS + ex]
        rank_row = rank_scr[pl.ds(ex, 1), :]
        gate_row = gate_scr[pl.ds(ex, 1), :]
        slot = lax.broadcasted_iota(jnp.int32, (MOE_CHUNK, tm), 0).astype(F32)

        def chunk(c, carry):
            hit = rank_row == slot + (c * MOE_CHUNK).astype(F32)
            onehot = jnp.where(hit, 1.0, 0.0).astype(BF16)
            rows = _dot(onehot, h2_ref[...]).astype(BF16)
            scale = jnp.sum(jnp.where(hit, gate_row, 0.0), axis=1, keepdims=True)
            y = ffn(rows, scale).astype(BF16)
            o_ref[...] += lax.dot_general(onehot, y, (((0,), (0,)), ((), ())),
                                          preferred_element_type=F32)
            return carry

        lax.fori_loop(0, (cnt + MOE_CHUNK - 1) // MOE_CHUNK, chunk, 0)


def _moe(x1, h2, route, counts, tri, wg, wu, wd):
    T, D = x1.shape
    tm = tri.shape[0]
    ne, _, ff = wg.shape
    grid_spec = pltpu.PrefetchScalarGridSpec(
        num_scalar_prefetch=1, grid=(T // tm, ne),
        in_specs=[pl.BlockSpec((tm, D), lambda i, e, c: (i, 0)),
                  pl.BlockSpec((tm, D), lambda i, e, c: (i, 0)),
                  pl.BlockSpec((8, tm), lambda i, e, c: (0, i)),
                  pl.BlockSpec((tm, tm), lambda i, e, c: (0, 0)),
                  pl.BlockSpec((1, D, ff), lambda i, e, c: (e, 0, 0)),
                  pl.BlockSpec((1, D, ff), lambda i, e, c: (e, 0, 0)),
                  pl.BlockSpec((1, ff, D), lambda i, e, c: (e, 0, 0))],
        out_specs=pl.BlockSpec((tm, D), lambda i, e, c: (i, 0)),
        scratch_shapes=[pltpu.VMEM((N_EXPERTS, tm), F32), pltpu.VMEM((N_EXPERTS, tm), F32)])
    return pl.pallas_call(
        _moe_kernel, grid_spec=grid_spec,
        out_shape=jax.ShapeDtypeStruct((T, D), F32),
        compiler_params=_cparams(("arbitrary", "arbitrary")), name="moe",
    )(counts, x1, h2, route, tri, wg, wu, wd)


def _constants(S):
    half = MLA_ROPE // 2
    inv = ROPE_THETA ** (-jnp.arange(half, dtype=F32) / half)
    ang = jnp.arange(S, dtype=F32)[:, None] * inv[None, :]
    cos = jnp.ones((S, LANES), F32)
    cos = cos.at[:, MLA_NOPE:MLA_NOPE + half].set(jnp.cos(ang))
    cos = cos.at[:, MLA_NOPE + half:MLA_QK].set(jnp.cos(ang))
    sin = jnp.zeros((S, LANES), F32)
    sin = sin.at[:, MLA_NOPE:MLA_NOPE + half].set(jnp.sin(ang))
    sin = sin.at[:, MLA_NOPE + half:MLA_QK].set(jnp.sin(ang))

    r = np.zeros((LANES, LANES), np.float32)
    for j in range(half):
        r[MLA_NOPE + half + j, MLA_NOPE + j] = -1.0
        r[MLA_NOPE + j, MLA_NOPE + half + j] = 1.0
    g = np.zeros((LANES, LANES), np.float32)
    g[:MLA_QK, :] = 1.0
    eye4 = np.eye(MLA_HEADS, dtype=np.float32)
    pk = np.zeros((LANES, 4 * LANES), np.float32)
    for hh in range(MLA_HEADS):
        for j in range(MLA_ROPE):
            pk[j, hh * LANES + MLA_NOPE + j] = 1.0
    g256 = np.kron(np.eye(MOBA_HEADS, dtype=np.float32), np.ones((MOBA_HD, MOBA_HD), np.float32))

    nkb = S // MOBA_BLOCK
    kpos = np.arange(S)
    kc = np.zeros((S, MOBA_HD), np.float32)
    blk = kpos // MOBA_BLOCK
    m = blk < min(nkb, MOBA_HD) - 1
    kc[kpos[m], blk[m]] = 1.0
    kc[:, MOBA_HD - 1] = kpos % MOBA_BLOCK
    return {
        "cos": cos, "sin": sin,
        "r512": jnp.asarray(np.kron(eye4, r), BF16), "g512": jnp.asarray(np.kron(eye4, g), BF16),
        "pk": jnp.asarray(pk, BF16), "g256": jnp.asarray(g256, BF16), "kc": jnp.asarray(kc, BF16),
    }


def _head_pad(v):
    one = jnp.concatenate([v, jnp.zeros((LANES - MLA_QK,), F32)])
    return jnp.tile(one, MLA_HEADS)[None, :]


def _layer_weights(l, p):
    D = p["w_in"].shape[1]
    w_in = p["w_in"][l]
    o1 = MLA_Q_LORA + MLA_KV_LORA + MLA_ROPE
    o2 = o1 + 2 * SGU_W
    o3 = o2 + POOL_W
    w = jnp.concatenate([w_in[:, :o1], jnp.zeros((D, C_SGU - o1), F32), w_in[:, o1:]], axis=1)

    wq = p["mla_w_q_up"][l].reshape(MLA_Q_LORA, MLA_HEADS, MLA_QK)
    wq = jnp.pad(wq, ((0, 0), (0, 0), (0, LANES - MLA_QK))).reshape(MLA_Q_LORA, MLA_HEADS * LANES)
    wkv = p["mla_w_kv_up"][l].reshape(MLA_KV_LORA, MLA_HEADS, MLA_NOPE + MLA_V)
    wk = jnp.pad(wkv[:, :, :MLA_NOPE], ((0, 0), (0, 0), (0, LANES - MLA_NOPE)))
    wk = wk.reshape(MLA_KV_LORA, MLA_HEADS * LANES)
    wv = wkv[:, :, MLA_NOPE:].reshape(MLA_KV_LORA, MLA_HEADS * MLA_V)

    tri = jnp.tril(jnp.ones((SGU_CHUNK, SGU_CHUNK), dtype=bool))
    sgu_w = jnp.where(tri[None], p["sgu_w"][l], 0.0)
    sgu_b = jnp.repeat(p["sgu_b"][l].T, SGU_W // SGU_GROUPS, axis=1)
    pool_w = jax.scipy.linalg.block_diag(*[p["pool_w"][l, gi] for gi in range(len(POOL_WINDOWS))])

    ne = p["moe_w_gate"].shape[1]
    return {
        "attn_norm": p["attn_norm"][l][None, :], "w_in": w.astype(BF16),
        "mla_q_norm": p["mla_q_norm"][l][None, :], "wq": wq.astype(BF16),
        "mla_kv_norm": p["mla_kv_norm"][l][None, :],
        "wkv": jnp.concatenate([wk, wv], axis=1).astype(BF16),
        "mla_qg": _head_pad(p["mla_q_gain"][l]), "mla_kg": _head_pad(p["mla_k_gain"][l]),
        "sgu_norm": p["sgu_norm"][l][None, :], "sgu_w": sgu_w.astype(BF16), "sgu_b": sgu_b,
        "pool_w": pool_w.astype(BF16), "pool_b": p["pool_b"][l][None, :],
        "pool_scale": p["pool_scale"][l][None, :],
        "moba_qg": jnp.tile(p["moba_q_gain"][l], MOBA_HEADS)[None, :],
        "moba_kg": jnp.tile(p["moba_k_gain"][l], MOBA_HEADS)[None, :],
        "w_gate": p["w_gate"][l].astype(BF16), "w_branch": p["w_branch"][l].astype(BF16),
        "w_out": p["w_out"][l].astype(BF16), "ffn_norm": p["ffn_norm"][l][None, :],
        "ffn_g": jnp.concatenate([p["shared_w_gate"][l][None], p["moe_w_gate"][l]]).astype(BF16),
        "ffn_u": jnp.concatenate([p["shared_w_up"][l][None], p["moe_w_up"][l]]).astype(BF16),
        "ffn_d": jnp.concatenate([p["shared_w_down"][l][None], p["moe_w_down"][l]]).astype(BF16),
    }


def _to_qT(a, B, S, H, d):
    return a.reshape(B, S // TQ, TQ, H, d).transpose(0, 3, 1, 4, 2)


def _to_k(a, B, S, H, d):
    return a.reshape(B, S // TK, TK, H, d).transpose(0, 3, 1, 2, 4)


def _to_vT(a, B, S, H, d):
    v = a.reshape(B, S // TK, TK, H, d).transpose(0, 3, 1, 4, 2)
    return jnp.concatenate([v, jnp.ones((B, H, S // TK, ONES_ROWS, TK), v.dtype)], axis=3)


def _from_oT(o, B, S):
    _, H, nq, d, tq = o.shape
    return o.transpose(0, 2, 4, 1, 3).reshape(B * S, H * d)


def kernel(x, attn_norm, w_in, mla_q_norm, mla_w_q_up, mla_kv_norm, mla_w_kv_up, mla_q_gain, mla_k_gain, sgu_norm, sgu_w, sgu_b, pool_w, pool_b, pool_scale, moba_q_gain, moba_k_gain, w_gate, w_branch, w_out, ffn_norm, router_w, router_bias, moe_w_gate, moe_w_up, moe_w_down, shared_w_gate, shared_w_up, shared_w_down):
    p = dict(attn_norm=attn_norm, w_in=w_in, mla_q_norm=mla_q_norm, mla_w_q_up=mla_w_q_up,
             mla_kv_norm=mla_kv_norm, mla_w_kv_up=mla_w_kv_up, mla_q_gain=mla_q_gain,
             mla_k_gain=mla_k_gain, sgu_norm=sgu_norm, sgu_w=sgu_w, sgu_b=sgu_b, pool_w=pool_w,
             pool_b=pool_b, pool_scale=pool_scale, moba_q_gain=moba_q_gain,
             moba_k_gain=moba_k_gain, w_gate=w_gate, w_branch=w_branch, w_out=w_out,
             ffn_norm=ffn_norm, moe_w_gate=moe_w_gate, moe_w_up=moe_w_up, moe_w_down=moe_w_down,
             shared_w_gate=shared_w_gate, shared_w_up=shared_w_up, shared_w_down=shared_w_down)
    B, S, D = x.shape
    T = B * S
    depth = attn_norm.shape[0]
    nkb = S // MOBA_BLOCK
    assert S % TK == 0 and S % TM_IN == 0 and T % TM_MERGE == 0 and nkb <= MOBA_HD
    consts = _constants(S)
    router_w_t = router_w.T
    router_b = router_bias[:, None]
    kc = jnp.broadcast_to(consts["kc"][None, :, None, :], (B, S, MOBA_HEADS, MOBA_HD))
    tm_moe = min(TM_MOE, T)
    tri = jnp.asarray(np.triu(np.ones((tm_moe, tm_moe), np.float32), 1), BF16)

    x2 = x.reshape(T, D)
    for l in range(depth):
        lw = _layer_weights(l, p)
        qa, ka, va, yb, yc, qd, kd, vd, km = _in_proj(x2, lw, consts, S)

        oa = _flash(_to_qT(qa, B, S, MLA_HEADS, LANES), _to_k(ka, B, S, MLA_HEADS, LANES),
                    _to_vT(va, B, S, MLA_HEADS, MLA_V), True, "mla_attn")
        ya = _from_oT(oa, B, S)

        qgT = qd.reshape(B, S, MOBA_HEADS, MOBA_HD).transpose(0, 2, 3, 1)
        kmean = km.reshape(B, nkb, MOBA_HEADS, MOBA_HD).transpose(0, 2, 1, 3)
        kmean = jnp.pad(kmean, ((0, 0), (0, 0), (0, MOBA_HD - nkb), (0, 0)))
        q_aug = _moba_select(qgT, kmean)
        q_aug = q_aug.reshape(B, MOBA_HEADS, 2 * MOBA_HD, S // TQ, TQ).transpose(0, 1, 3, 2, 4)
        k_aug = jnp.concatenate([kd.reshape(B, S, MOBA_HEADS, MOBA_HD), kc], axis=-1)
        k_aug = k_aug.reshape(B, S // TK, TK, MOBA_HEADS, 2 * MOBA_HD).transpose(0, 3, 1, 2, 4)
        od = _flash(q_aug, k_aug, _to_vT(vd, B, S, MOBA_HEADS, MOBA_HD), False, "moba_attn")
        yd = _from_oT(od, B, S)

        x1, h2, route, cnt = _merge(x2, ya, yb, yc, yd, lw, router_w_t, router_b)
        counts = cnt[:, :, 0].reshape(T // tm_moe, tm_moe // TM_MERGE, N_EXPERTS).sum(axis=1)
        x2 = _moe(x1, h2, route, counts.reshape(-1), tri, lw["ffn_g"], lw["ffn_u"], lw["ffn_d"])
    return x2.reshape(B, S, D)
```

```python
import functools
import math

import jax
import jax.numpy as jnp
import numpy as np
from jax import lax
from jax.experimental import pallas as pl
from jax.experimental.pallas import tpu as pltpu

F32 = jnp.float32
BF16 = jnp.bfloat16
EPS = 1e-6
NEG = -1e30

LANES = 128
VMEM_LIMIT = 56 * 1024 * 1024

MLA_HEADS = 4
MLA_NOPE = 64
MLA_ROPE = 32
MLA_V = 64
MLA_Q_LORA = 256
MLA_KV_LORA = 128
ROPE_THETA = 10000.0
MLA_QK = MLA_NOPE + MLA_ROPE

SGU_GROUPS = 4
SGU_CHUNK = 128
SGU_W = 256

POOL_WINDOWS = (2, 4, 8, 16)
POOL_MAXW = 16
POOL_W = 256

MOBA_HEADS = 4
MOBA_HD = 64
MOBA_BLOCK = 256
MOBA_TOPK = 3

N_EXPERTS = 16
N_GROUPS = 4
EPG = 4

C_MLA = 0
C_SGU = 512
C_POOL = 1024
C_MOBA = 1280
C_END = 2048

TQ = 512
TK = 512
TM_IN = TQ
TM_MERGE = 512
TM_MOE = 1024
SEL_L = 2048
MOE_CHUNK = 160
ONES_ROWS = 16


def _cparams(sem):
    return pltpu.CompilerParams(dimension_semantics=sem, vmem_limit_bytes=VMEM_LIMIT)


def _full(shape):
    n = len(shape)
    return pl.BlockSpec(shape, lambda *_: (0,) * n)


def _rms(x, g):
    return x * lax.rsqrt(jnp.mean(x * x, axis=-1, keepdims=True) + EPS) * g


def _dot(a, b):
    return jnp.dot(a, b, preferred_element_type=F32)


def _in_proj_kernel(x_ref, g_ref, w_ref, qn_ref, wq_ref, kvn_ref, wkv_ref, pk_ref,
                    qg_ref, kg_ref, g512_ref, r512_ref, cos_ref, sin_ref,
                    sn_ref, sw_ref, sb_ref, pw_ref, pb_ref, ps_ref,
                    mqg_ref, mkg_ref, g256_ref, pl_ref, kc_ref,
                    qa_ref, ka_ref, va_ref, yb_ref, yc_ref, qd_ref, kd_ref, vd_ref, km_ref,
                    halo_ref, buf_ref, *, tiles_per_seq, q_scale):
    tm = x_ref.shape[0]
    i = pl.program_id(0)
    x = x_ref[...]
    h = _rms(x, g_ref[...]).astype(BF16)

    cm = _dot(h, w_ref[:, C_MLA:C_SGU])
    c_q = _rms(cm[:, :MLA_Q_LORA], qn_ref[...]).astype(BF16)
    c_kv = _rms(cm[:, MLA_Q_LORA:MLA_Q_LORA + MLA_KV_LORA], kvn_ref[...]).astype(BF16)
    kpe_grp = cm[:, MLA_Q_LORA + MLA_KV_LORA:].astype(BF16)
    q = _dot(c_q, wq_ref[...])
    kv = _dot(c_kv, wkv_ref[...])
    k = kv[:, :512] + _dot(kpe_grp, pk_ref[...])
    va_ref[0, 0] = kv[:, 512:].T.astype(BF16)
    cos = jnp.concatenate([cos_ref[...]] * MLA_HEADS, axis=1)
    sin = jnp.concatenate([sin_ref[...]] * MLA_HEADS, axis=1)
    inv_d = 1.0 / MLA_QK

    def head_norm_rope(t, gain):
        ms = _dot((t * t).astype(BF16), g512_ref[...]) * inv_d
        t = t * lax.rsqrt(ms + EPS) * gain
        rot = _dot(t.astype(BF16), r512_ref[...])
        return t * cos + rot * sin

    qa_ref[0, 0] = (head_norm_rope(q, qg_ref[...]) * q_scale).T.astype(BF16)
    ka_ref[...] = head_norm_rope(k, kg_ref[...]).astype(BF16)

    z = jax.nn.gelu(_dot(h, w_ref[:, C_SGU:C_POOL]))
    u = z[:, :SGU_W]
    vv = _rms(z[:, SGU_W:], sn_ref[...]).astype(BF16)
    lane_grp = lax.broadcasted_iota(jnp.int32, (SGU_CHUNK, SGU_W), 1) // (SGU_W // SGU_GROUPS)
    for c in range(tm // SGU_CHUNK):
        vc = vv[c * SGU_CHUNK:(c + 1) * SGU_CHUNK, :]
        y = sb_ref[...]
        for gi in range(SGU_GROUPS):
            y = y + jnp.where(lane_grp == gi, _dot(sw_ref[gi], vc), 0.0)
        yb_ref[c * SGU_CHUNK:(c + 1) * SGU_CHUNK, :] = (
            u[c * SGU_CHUNK:(c + 1) * SGU_CHUNK, :] * y).astype(BF16)

    xp = _dot(h, w_ref[:, C_POOL:C_MOBA])

    @pl.when(i % tiles_per_seq == 0)
    def _():
        halo_ref[...] = jnp.zeros_like(halo_ref)

    buf_ref[0:POOL_MAXW, :] = halo_ref[...]
    buf_ref[POOL_MAXW:, :] = xp
    halo_ref[...] = xp[tm - POOL_MAXW:, :]
    pos = (i % tiles_per_seq) * tm + lax.broadcasted_iota(jnp.int32, (tm, POOL_W), 0)
    lane_w = lax.broadcasted_iota(jnp.int32, (tm, POOL_W), 1) // (POOL_W // len(POOL_WINDOWS))
    acc = xp
    pooled = jnp.zeros((tm, POOL_W), F32)
    shift = 1
    for gi, w in enumerate(POOL_WINDOWS):
        while shift < w:
            acc = acc + buf_ref[POOL_MAXW - shift:POOL_MAXW - shift + tm, :]
            shift += 1
        cnt = jnp.minimum(pos + 1, w).astype(F32)
        pooled = jnp.where(lane_w == gi, acc / cnt, pooled)
    p = (pooled - xp).astype(BF16)
    yc_ref[...] = ((_dot(p, pw_ref[...]) + pb_ref[...]) * ps_ref[...]).astype(BF16)

    cd = _dot(h, w_ref[:, C_MOBA:C_END])
    inv_hd = 1.0 / MOBA_HD

    def head_norm(t, gain):
        ms = _dot((t * t).astype(BF16), g256_ref[...]) * inv_hd
        return t * lax.rsqrt(ms + EPS) * gain

    qd_ref[0] = head_norm(cd[:, :256], mqg_ref[...]).T
    kd = head_norm(cd[:, 256:512], mkg_ref[...])
    kd_ref[...] = (_dot(kd.astype(BF16), pl_ref[...]) + kc_ref[...]).astype(BF16)
    vd_ref[0, 0] = cd[:, 512:].T.astype(BF16)
    for c in range(tm // MOBA_BLOCK):
        km_ref[c] = jnp.mean(kd[c * MOBA_BLOCK:(c + 1) * MOBA_BLOCK, :], axis=0, keepdims=True)


def _in_proj(x2, lw, consts, S):
    T, D = x2.shape
    tm = TM_IN
    nt = T // tm
    tok = lambda w: pl.BlockSpec((tm, w), lambda i: (i, 0))
    pos_spec = pl.BlockSpec((tm, LANES), lambda i: (i % (S // tm), 0))
    ins = [
        (x2, tok(D)), (lw["attn_norm"], None), (lw["w_in"], None),
        (lw["mla_q_norm"], None), (lw["wq"], None), (lw["mla_kv_norm"], None), (lw["wkv"], None),
        (consts["pk"], None), (lw["mla_qg"], None), (lw["mla_kg"], None),
        (consts["g512"], None), (consts["r512"], None),
        (consts["cos"], pos_spec), (consts["sin"], pos_spec),
        (lw["sgu_norm"], None), (lw["sgu_w"], None), (lw["sgu_b"], None),
        (lw["pool_w"], None), (lw["pool_b"], None), (lw["pool_scale"], None),
        (lw["moba_qg"], None), (lw["moba_kg"], None), (consts["g256"], None),
        (consts["place"], None),
        (consts["kc"], pl.BlockSpec((tm, 4 * LANES), lambda i: (i % (S // tm), 0))),
    ]
    arrays = [a for a, _ in ins]
    specs = [s if s is not None else _full(a.shape) for a, s in ins]
    nb = tm // MOBA_BLOCK
    B = T // S
    ns = S // tm
    tr = lambda f: pl.BlockSpec((1, 1, f, tm), lambda i: (i // ns, i % ns, 0, 0))
    out_shape = [
        jax.ShapeDtypeStruct((B, ns, 512, tm), BF16), jax.ShapeDtypeStruct((T, 512), BF16),
        jax.ShapeDtypeStruct((B, ns, 256, tm), BF16), jax.ShapeDtypeStruct((T, 256), BF16),
        jax.ShapeDtypeStruct((T, 256), BF16), jax.ShapeDtypeStruct((B, 256, S), F32),
        jax.ShapeDtypeStruct((T, 512), BF16), jax.ShapeDtypeStruct((B, ns, 256, tm), BF16),
        jax.ShapeDtypeStruct((T // MOBA_BLOCK, 1, 256), F32),
    ]
    out_specs = [tr(512), tok(512), tr(256), tok(256), tok(256),
                 pl.BlockSpec((1, 256, tm), lambda i: (i // ns, 0, i % ns)),
                 tok(512), tr(256), pl.BlockSpec((nb, 1, 256), lambda i: (i, 0, 0))]
    q_scale = (MLA_QK ** -0.5) * math.log2(math.e)
    return pl.pallas_call(
        functools.partial(_in_proj_kernel, tiles_per_seq=S // tm, q_scale=q_scale),
        grid=(nt,), in_specs=specs, out_specs=out_specs, out_shape=out_shape,
        scratch_shapes=[pltpu.VMEM((POOL_MAXW, POOL_W), F32),
                        pltpu.VMEM((tm + POOL_MAXW, POOL_W), F32)],
        compiler_params=_cparams(("arbitrary",)), name="in_proj",
    )(*arrays)


def _moba_select_kernel(q_ref, km_ref, o_ref, *, slope_log2_step):
    hh = pl.program_id(1)
    li = pl.program_id(2)
    q = q_ref[0]
    L = q.shape[1]
    nrow = km_ref.shape[2]
    g = jnp.dot(km_ref[0, 0], q, preferred_element_type=F32, precision=lax.Precision.HIGHEST)
    n_i = lax.broadcasted_iota(jnp.int32, (nrow, L), 0)
    own = (li * L + lax.broadcasted_iota(jnp.int32, (nrow, L), 1)) // MOBA_BLOCK
    g = jnp.where(n_i < own, g, NEG)
    sel = jnp.zeros((nrow, L), F32)
    for _ in range(MOBA_TOPK):
        mx = jnp.max(g, axis=0, keepdims=True)
        first = jnp.min(jnp.where(g == mx, n_i, nrow), axis=0, keepdims=True)
        hit = n_i == first
        sel = jnp.where(hit & (mx > 0.5 * NEG), 1.0, sel)
        g = jnp.where(hit, NEG, g)
    slope = jnp.exp2(-slope_log2_step * (hh + 1).astype(F32))
    rel = (n_i - own).astype(F32) * (slope * MOBA_BLOCK)
    bias = jnp.where((sel > 0.0) | (n_i == own), rel, NEG)
    bias = jnp.where(n_i == nrow - 1, slope, bias)
    qs = (q * (MOBA_HD ** -0.5)).astype(BF16)
    bias = bias.astype(BF16)
    tq = o_ref.shape[3]
    for t in range(L // tq):
        o_ref[0, t, 0:MOBA_HD, :] = qs[:, t * tq:(t + 1) * tq]
        o_ref[0, t, MOBA_HD:, :] = bias[:, t * tq:(t + 1) * tq]


def _moba_select(qgT, kmean):
    B, _, S = qgT.shape
    H, hd = MOBA_HEADS, MOBA_HD
    L = min(SEL_L, S)
    return pl.pallas_call(
        functools.partial(_moba_select_kernel, slope_log2_step=8.0 / MOBA_HEADS),
        grid=(B, H, S // L),
        in_specs=[pl.BlockSpec((1, hd, L), lambda b, h, l: (b, h, l)),
                  pl.BlockSpec((1, 1, 64, hd), lambda b, h, l: (b, h, 0, 0))],
        out_specs=pl.BlockSpec((1, L // TQ, 2 * hd, TQ), lambda b, h, l: (b, l, h, 0)),
        out_shape=jax.ShapeDtypeStruct((B, S // TQ, H * 2 * hd, TQ), BF16),
        compiler_params=_cparams(("arbitrary",) * 3), name="moba_select",
    )(qgT, kmean)


def _flash_kernel(q_ref, k_ref, v_ref, o_ref, s_scr, acc_scr, *, heads, use_exp2):
    H = heads
    tq = q_ref.shape[3]
    tk = k_ref.shape[2]
    dk = q_ref.shape[2] // H
    dv = v_ref.shape[2] // H
    ratio = tk // tq
    ex = jnp.exp2 if use_exp2 else jnp.exp
    i = pl.program_id(1)
    jd = i // ratio
    kpos = jd * tk + lax.broadcasted_iota(jnp.int32, (tk, tq), 0)
    qpos = i * tq + lax.broadcasted_iota(jnp.int32, (tk, tq), 1)
    causal = kpos <= qpos
    ones = jnp.ones((ONES_ROWS, tk), BF16)

    def scores(jt, h, masked):
        s = _dot(k_ref[0, jt, :, h * dk:(h + 1) * dk], q_ref[0, 0, h * dk:(h + 1) * dk, :])
        if masked:
            s = jnp.where(causal, s, NEG)
        s_scr[h] = s
        return jnp.max(s, axis=0, keepdims=True)

    def update(jt, h, cmax, m):
        m_new = jnp.maximum(m, cmax)
        alpha = ex(m - m_new)
        p = ex(s_scr[h] - m_new).astype(BF16)
        vt = jnp.concatenate([v_ref[0, jt, h * dv:(h + 1) * dv, :], ones], axis=0)
        acc_scr[h] = acc_scr[h] * alpha + _dot(vt, p)
        return m_new

    acc_scr[...] = jnp.zeros_like(acc_scr)
    ms = [jnp.full((1, tq), NEG, F32) for _ in range(H)]
    cmax = scores(jd, 0, True)
    for h in range(H):
        nxt = scores(jd, h + 1, True) if h + 1 < H else scores(0, 0, False)
        ms[h] = update(jd, h, cmax, ms[h])
        cmax = nxt

    def k_body(j, carry):
        ms, cmax = carry
        ms = list(ms)
        jn = jnp.minimum(j + 1, jd - 1)
        for h in range(H):
            nxt = scores(j, h + 1, False) if h + 1 < H else scores(jn, 0, False)
            ms[h] = update(j, h, cmax, ms[h])
            cmax = nxt
        return tuple(ms), cmax

    lax.fori_loop(0, jd, k_body, (tuple(ms), cmax))
    outs = []
    for h in range(H):
        acc = acc_scr[h]
        outs.append(acc[:dv] / acc[dv:dv + 1])
    o_ref[...] = jnp.concatenate(outs, axis=0).T.astype(BF16)


def _flash(qT, k, vT, heads, use_exp2, name):
    B, nq, hdk, tq = qT.shape
    nk, tk = k.shape[1:3]
    hdv = vT.shape[2]
    dva = hdv // heads + ONES_ROWS
    whole = lambda shp: pl.BlockSpec((1,) + shp, lambda b, i: (b, 0, 0, 0),
                                     pipeline_mode=pl.Buffered(1))
    return pl.pallas_call(
        functools.partial(_flash_kernel, heads=heads, use_exp2=use_exp2),
        grid=(B, nq),
        in_specs=[pl.BlockSpec((1, 1, hdk, tq), lambda b, i: (b, i, 0, 0)),
                  whole((nk, tk, hdk)), whole((nk, hdv, tk))],
        out_specs=pl.BlockSpec((tq, hdv), lambda b, i: (b * nq + i, 0)),
        out_shape=jax.ShapeDtypeStruct((B * nq * tq, hdv), BF16),
        scratch_shapes=[pltpu.VMEM((heads, tk, tq), F32), pltpu.VMEM((heads, dva, tq), F32)],
        compiler_params=_cparams(("arbitrary", "arbitrary")), name=name,
    )(qT, k, vT)


def _merge_kernel(x_ref, ya_ref, yb_ref, yc_ref, yd_ref, g_ref, wg_ref, wb_ref, wo_ref,
                  fg_ref, rw_ref, rb_ref, x1_ref, h2_ref, rt_ref, cnt_ref):
    x = x_ref[...]
    h = _rms(x, g_ref[...]).astype(BF16)
    merged = None
    for bi, y_ref in enumerate((ya_ref, yb_ref, yc_ref, yd_ref)):
        t = jax.nn.sigmoid(_dot(h, wg_ref[bi])) * _dot(y_ref[...], wb_ref[bi])
        merged = t if merged is None else merged + t
    x1 = x + _dot(merged.astype(BF16), wo_ref[...])
    x1_ref[...] = x1
    h2 = _rms(x1, fg_ref[...])
    h2_ref[...] = h2.astype(BF16)

    logits = lax.dot_general(rw_ref[...], h2, (((1,), (1,)), ((), ())),
                             preferred_element_type=F32, precision=lax.Precision.HIGHEST)
    scores = jax.nn.sigmoid(logits)
    biased = scores + rb_ref[...]
    s_rows = [scores[e:e + 1, :] for e in range(N_EXPERTS)]
    b_rows = [biased[e:e + 1, :] for e in range(N_EXPERTS)]

    def top2(vals):
        m1 = vals[0]
        i1 = jnp.zeros_like(m1, dtype=jnp.int32)
        for j in range(1, len(vals)):
            better = vals[j] > m1
            i1 = jnp.where(better, j, i1)
            m1 = jnp.where(better, vals[j], m1)
        m2 = None
        i2 = None
        for j in range(len(vals)):
            cand = jnp.where(i1 == j, NEG, vals[j])
            if m2 is None:
                m2, i2 = cand, jnp.zeros_like(i1)
            else:
                better = cand > m2
                i2 = jnp.where(better, j, i2)
                m2 = jnp.where(better, cand, m2)
        return m1, i1, m2, i2

    best = None
    for gi in range(N_GROUPS):
        bm1, bi1, bm2, bi2 = top2(b_rows[gi * EPG:(gi + 1) * EPG])
        gs = bm1 + bm2
        e1 = bi1 + gi * EPG
        e2 = bi2 + gi * EPG
        if best is None:
            best = (gs, e1, e2)
        else:
            better = gs > best[0]
            best = (jnp.where(better, gs, best[0]), jnp.where(better, e1, best[1]),
                    jnp.where(better, e2, best[2]))
    _, e1, e2 = best
    w1 = jnp.zeros_like(s_rows[0])
    w2 = jnp.zeros_like(s_rows[0])
    for e in range(N_EXPERTS):
        w1 = jnp.where(e1 == e, s_rows[e], w1)
        w2 = jnp.where(e2 == e, s_rows[e], w2)
    tot = w1 + w2
    w1 = w1 / tot
    w2 = w2 / tot
    r8 = lax.broadcasted_iota(jnp.int32, (8, scores.shape[1]), 0)
    rt_ref[...] = jnp.where(r8 == 0, e1.astype(F32), jnp.where(r8 == 1, e2.astype(F32),
                            jnp.where(r8 == 2, w1, jnp.where(r8 == 3, w2, 0.0))))
    row = lax.broadcasted_iota(jnp.int32, scores.shape, 0)
    assigned = jnp.where(row == e1, 1.0, jnp.where(row == e2, 1.0, 0.0))
    cnt = jnp.sum(assigned, axis=1, keepdims=True)
    cnt_ref[0] = jnp.broadcast_to(cnt, (N_EXPERTS, LANES)).astype(jnp.int32)


def _merge(x2, ya, yb, yc, yd, lw, router_w_t, router_b):
    T, D = x2.shape
    tm = TM_MERGE
    tok = lambda w: pl.BlockSpec((tm, w), lambda i: (i, 0))
    consts = [lw["attn_norm"], lw["w_gate"], lw["w_branch"], lw["w_out"], lw["ffn_norm"],
              router_w_t, router_b]
    return pl.pallas_call(
        _merge_kernel, grid=(T // tm,),
        in_specs=[tok(D), tok(256), tok(256), tok(256), tok(256)] + [_full(a.shape) for a in consts],
        out_specs=[tok(D), tok(D), pl.BlockSpec((8, tm), lambda i: (0, i)),
                   pl.BlockSpec((1, N_EXPERTS, LANES), lambda i: (i, 0, 0))],
        out_shape=[jax.ShapeDtypeStruct((T, D), F32), jax.ShapeDtypeStruct((T, D), BF16),
                   jax.ShapeDtypeStruct((8, T), F32),
                   jax.ShapeDtypeStruct((T // tm, N_EXPERTS, LANES), jnp.int32)],
        compiler_params=_cparams(("arbitrary",)), name="merge",
    )(x2, ya, yb, yc, yd, *consts)


def _moe_kernel(cnt_ref, x1_ref, h2_ref, rt_ref, tri_ref, sg_ref, su_ref, sd_ref,
                wg_ref, wu_ref, wd_ref, o_ref, rank_scr, gate_scr):
    i = pl.program_id(0)
    e = pl.program_id(1)
    tm = h2_ref.shape[0]

    def ffn(rows, scale, wg, wu, wd):
        a = jax.nn.silu(_dot(rows, wg)) * _dot(rows, wu)
        if scale is not None:
            a = a * scale
        return _dot(a.astype(BF16), wd)

    @pl.when(e == 0)
    def _():
        o_ref[...] = x1_ref[...] + ffn(h2_ref[...], None, sg_ref[...], su_ref[...], sd_ref[...])
        row = lax.broadcasted_iota(jnp.int32, (N_EXPERTS, tm), 0).astype(F32)
        e1, e2 = rt_ref[0:1, :], rt_ref[1:2, :]
        w1, w2 = rt_ref[2:3, :], rt_ref[3:4, :]
        assigned = jnp.where(row == e1, 1.0, jnp.where(row == e2, 1.0, 0.0))
        before = _dot(assigned.astype(BF16), tri_ref[...])
        rank_scr[...] = jnp.where(assigned > 0.0, before, -1.0)
        gate_scr[...] = jnp.where(row == e1, w1, jnp.where(row == e2, w2, 0.0))

    @pl.when(e > 0)
    def _():
        ex = e - 1
        cnt = cnt_ref[i * N_EXPERTS + ex]
        rank_row = rank_scr[pl.ds(ex, 1), :]
        gate_row = gate_scr[pl.ds(ex, 1), :]
        slot = lax.broadcasted_iota(jnp.int32, (MOE_CHUNK, tm), 0).astype(F32)

        def chunk(c, carry):
            hit = rank_row == slot + (c * MOE_CHUNK).astype(F32)
            onehot = jnp.where(hit, 1.0, 0.0).astype(BF16)
            rows = _dot(onehot, h2_ref[...]).astype(BF16)
            scale = jnp.sum(jnp.where(hit, gate_row, 0.0), axis=1, keepdims=True)
            y = ffn(rows, scale, wg_ref[0], wu_ref[0], wd_ref[0]).astype(BF16)
            o_ref[...] += lax.dot_general(onehot, y, (((0,), (0,)), ((), ())),
                                          preferred_element_type=F32)
            return carry

        lax.fori_loop(0, (cnt + MOE_CHUNK - 1) // MOE_CHUNK, chunk, 0)


def _moe(x1, h2, route, counts, tri, shared, wg, wu, wd):
    T, D = x1.shape
    tm = tri.shape[0]
    ne, _, ff = wg.shape
    const = lambda shp: pl.BlockSpec(shp, lambda i, e, c: (0, 0), pipeline_mode=pl.Buffered(1))
    expert = lambda shp: pl.BlockSpec((1,) + shp, lambda i, e, c: (jnp.maximum(e - 1, 0), 0, 0))
    grid_spec = pltpu.PrefetchScalarGridSpec(
        num_scalar_prefetch=1, grid=(T // tm, ne + 1),
        in_specs=[pl.BlockSpec((tm, D), lambda i, e, c: (i, 0)),
                  pl.BlockSpec((tm, D), lambda i, e, c: (i, 0)),
                  pl.BlockSpec((8, tm), lambda i, e, c: (0, i)),
                  const((tm, tm)), const((D, ff)), const((D, ff)), const((ff, D)),
                  expert((D, ff)), expert((D, ff)), expert((ff, D))],
        out_specs=pl.BlockSpec((tm, D), lambda i, e, c: (i, 0)),
        scratch_shapes=[pltpu.VMEM((N_EXPERTS, tm), F32), pltpu.VMEM((N_EXPERTS, tm), F32)])
    return pl.pallas_call(
        _moe_kernel, grid_spec=grid_spec,
        out_shape=jax.ShapeDtypeStruct((T, D), F32),
        compiler_params=_cparams(("arbitrary", "arbitrary")), name="moe",
    )(counts, x1, h2, route, tri, *shared, wg, wu, wd)


def _constants(S):
    half = MLA_ROPE // 2
    inv = ROPE_THETA ** (-jnp.arange(half, dtype=F32) / half)
    ang = jnp.arange(S, dtype=F32)[:, None] * inv[None, :]
    cos = jnp.ones((S, LANES), F32)
    cos = cos.at[:, MLA_NOPE:MLA_NOPE + half].set(jnp.cos(ang))
    cos = cos.at[:, MLA_NOPE + half:MLA_QK].set(jnp.cos(ang))
    sin = jnp.zeros((S, LANES), F32)
    sin = sin.at[:, MLA_NOPE:MLA_NOPE + half].set(jnp.sin(ang))
    sin = sin.at[:, MLA_NOPE + half:MLA_QK].set(jnp.sin(ang))

    r = np.zeros((LANES, LANES), np.float32)
    for j in range(half):
        r[MLA_NOPE + half + j, MLA_NOPE + j] = -1.0
        r[MLA_NOPE + j, MLA_NOPE + half + j] = 1.0
    g = np.zeros((LANES, LANES), np.float32)
    g[:MLA_QK, :] = 1.0
    eye4 = np.eye(MLA_HEADS, dtype=np.float32)
    pk = np.zeros((LANES, 4 * LANES), np.float32)
    for hh in range(MLA_HEADS):
        for j in range(MLA_ROPE):
            pk[j, hh * LANES + MLA_NOPE + j] = 1.0
    g256 = np.kron(np.eye(MOBA_HEADS, dtype=np.float32), np.ones((MOBA_HD, MOBA_HD), np.float32))

    nkb = S // MOBA_BLOCK
    kpos = np.arange(S)
    kc = np.zeros((S, MOBA_HD), np.float32)
    blk = kpos // MOBA_BLOCK
    m = blk < min(nkb, MOBA_HD) - 1
    kc[kpos[m], blk[m]] = 1.0
    kc[:, MOBA_HD - 1] = kpos % MOBA_BLOCK
    kc4 = np.zeros((S, MOBA_HEADS, 2 * MOBA_HD), np.float32)
    kc4[:, :, MOBA_HD:] = kc[:, None, :]
    place = np.zeros((MOBA_HEADS * MOBA_HD, MOBA_HEADS * 2 * MOBA_HD), np.float32)
    for hh in range(MOBA_HEADS):
        for j in range(MOBA_HD):
            place[hh * MOBA_HD + j, hh * 2 * MOBA_HD + j] = 1.0
    return {
        "cos": cos, "sin": sin,
        "r512": jnp.asarray(np.kron(eye4, r), BF16), "g512": jnp.asarray(np.kron(eye4, g), BF16),
        "pk": jnp.asarray(pk, BF16), "g256": jnp.asarray(g256, BF16),
        "kc": jnp.asarray(kc4.reshape(S, -1), BF16), "place": jnp.asarray(place, BF16),
    }


def _head_pad(v):
    one = jnp.concatenate([v, jnp.zeros((LANES - MLA_QK,), F32)])
    return jnp.tile(one, MLA_HEADS)[None, :]


def _layer_weights(l, p):
    D = p["w_in"].shape[1]
    w_in = p["w_in"][l]
    o1 = MLA_Q_LORA + MLA_KV_LORA + MLA_ROPE
    o2 = o1 + 2 * SGU_W
    o3 = o2 + POOL_W
    w = jnp.concatenate([w_in[:, :o1], jnp.zeros((D, C_SGU - o1), F32), w_in[:, o1:]], axis=1)

    wq = p["mla_w_q_up"][l].reshape(MLA_Q_LORA, MLA_HEADS, MLA_QK)
    wq = jnp.pad(wq, ((0, 0), (0, 0), (0, LANES - MLA_QK))).reshape(MLA_Q_LORA, MLA_HEADS * LANES)
    wkv = p["mla_w_kv_up"][l].reshape(MLA_KV_LORA, MLA_HEADS, MLA_NOPE + MLA_V)
    wk = jnp.pad(wkv[:, :, :MLA_NOPE], ((0, 0), (0, 0), (0, LANES - MLA_NOPE)))
    wk = wk.reshape(MLA_KV_LORA, MLA_HEADS * LANES)
    wv = wkv[:, :, MLA_NOPE:].reshape(MLA_KV_LORA, MLA_HEADS * MLA_V)

    tri = jnp.tril(jnp.ones((SGU_CHUNK, SGU_CHUNK), dtype=bool))
    sgu_w = jnp.where(tri[None], p["sgu_w"][l], 0.0)
    sgu_b = jnp.repeat(p["sgu_b"][l].T, SGU_W // SGU_GROUPS, axis=1)
    pool_w = jax.scipy.linalg.block_diag(*[p["pool_w"][l, gi] for gi in range(len(POOL_WINDOWS))])

    ne = p["moe_w_gate"].shape[1]
    return {
        "attn_norm": p["attn_norm"][l][None, :], "w_in": w.astype(BF16),
        "mla_q_norm": p["mla_q_norm"][l][None, :], "wq": wq.astype(BF16),
        "mla_kv_norm": p["mla_kv_norm"][l][None, :],
        "wkv": jnp.concatenate([wk, wv], axis=1).astype(BF16),
        "mla_qg": _head_pad(p["mla_q_gain"][l]), "mla_kg": _head_pad(p["mla_k_gain"][l]),
        "sgu_norm": p["sgu_norm"][l][None, :], "sgu_w": sgu_w.astype(BF16), "sgu_b": sgu_b,
        "pool_w": pool_w.astype(BF16), "pool_b": p["pool_b"][l][None, :],
        "pool_scale": p["pool_scale"][l][None, :],
        "moba_qg": jnp.tile(p["moba_q_gain"][l], MOBA_HEADS)[None, :],
        "moba_kg": jnp.tile(p["moba_k_gain"][l], MOBA_HEADS)[None, :],
        "w_gate": p["w_gate"][l].astype(BF16), "w_branch": p["w_branch"][l].astype(BF16),
        "w_out": p["w_out"][l].astype(BF16), "ffn_norm": p["ffn_norm"][l][None, :],
        "shared": (p["shared_w_gate"][l].astype(BF16), p["shared_w_up"][l].astype(BF16),
                   p["shared_w_down"][l].astype(BF16)),
        "ffn_g": p["moe_w_gate"][l].astype(BF16), "ffn_u": p["moe_w_up"][l].astype(BF16),
        "ffn_d": p["moe_w_down"][l].astype(BF16),
    }


def kernel(x, attn_norm, w_in, mla_q_norm, mla_w_q_up, mla_kv_norm, mla_w_kv_up, mla_q_gain, mla_k_gain, sgu_norm, sgu_w, sgu_b, pool_w, pool_b, pool_scale, moba_q_gain, moba_k_gain, w_gate, w_branch, w_out, ffn_norm, router_w, router_bias, moe_w_gate, moe_w_up, moe_w_down, shared_w_gate, shared_w_up, shared_w_down):
    p = dict(attn_norm=attn_norm, w_in=w_in, mla_q_norm=mla_q_norm, mla_w_q_up=mla_w_q_up,
             mla_kv_norm=mla_kv_norm, mla_w_kv_up=mla_w_kv_up, mla_q_gain=mla_q_gain,
             mla_k_gain=mla_k_gain, sgu_norm=sgu_norm, sgu_w=sgu_w, sgu_b=sgu_b, pool_w=pool_w,
             pool_b=pool_b, pool_scale=pool_scale, moba_q_gain=moba_q_gain,
             moba_k_gain=moba_k_gain, w_gate=w_gate, w_branch=w_branch, w_out=w_out,
             ffn_norm=ffn_norm, moe_w_gate=moe_w_gate, moe_w_up=moe_w_up, moe_w_down=moe_w_down,
             shared_w_gate=shared_w_gate, shared_w_up=shared_w_up, shared_w_down=shared_w_down)
    B, S, D = x.shape
    T = B * S
    depth = attn_norm.shape[0]
    nkb = S // MOBA_BLOCK
    assert TQ == TK and S % TK == 0 and T % TM_MERGE == 0 and nkb <= MOBA_HD
    consts = _constants(S)
    router_w_t = router_w.T
    router_b = router_bias[:, None]
    tm_moe = min(TM_MOE, T)
    tri = jnp.asarray(np.triu(np.ones((tm_moe, tm_moe), np.float32), 1), BF16)

    x2 = x.reshape(T, D)
    for l in range(depth):
        lw = _layer_weights(l, p)
        qa, ka, va, yb, yc, qd, kd, vd, km = _in_proj(x2, lw, consts, S)
        ya = _flash(qa, ka.reshape(B, S // TK, TK, -1), va, MLA_HEADS, True, "mla_attn")

        kmean = km.reshape(B, nkb, MOBA_HEADS, MOBA_HD).transpose(0, 2, 1, 3)
        kmean = jnp.pad(kmean, ((0, 0), (0, 0), (0, MOBA_HD - nkb), (0, 0)))
        q_aug = _moba_select(qd, kmean)
        yd = _flash(q_aug, kd.reshape(B, S // TK, TK, -1), vd, MOBA_HEADS, False, "moba_attn")

        x1, h2, route, cnt = _merge(x2, ya, yb, yc, yd, lw, router_w_t, router_b)
        counts = cnt[:, :, 0].reshape(T // tm_moe, tm_moe // TM_MERGE, N_EXPERTS).sum(axis=1)
        x2 = _moe(x1, h2, route, counts.reshape(-1), tri, lw["shared"],
                  lw["ffn_g"], lw["ffn_u"], lw["ffn_d"])
    return x2.reshape(B, S, D)
```

```python
import functools
import math

import jax
import jax.numpy as jnp
import numpy as np
from jax import lax
from jax.experimental import pallas as pl
from jax.experimental.pallas import tpu as pltpu

F32 = jnp.float32
BF16 = jnp.bfloat16
EPS = 1e-6
NEG = -1e30

LANES = 128
VMEM_LIMIT = 56 * 1024 * 1024

MLA_HEADS = 4
MLA_NOPE = 64
MLA_ROPE = 32
MLA_V = 64
MLA_Q_LORA = 256
MLA_KV_LORA = 128
ROPE_THETA = 10000.0
MLA_QK = MLA_NOPE + MLA_ROPE

SGU_GROUPS = 4
SGU_CHUNK = 128
SGU_W = 256

POOL_WINDOWS = (2, 4, 8, 16)
POOL_MAXW = 16
POOL_W = 256

MOBA_HEADS = 4
MOBA_HD = 64
MOBA_BLOCK = 256
MOBA_TOPK = 3

N_EXPERTS = 16
N_GROUPS = 4
EPG = 4

C_MLA = 0
C_SGU = 512
C_POOL = 1024
C_MOBA = 1280
C_END = 2048

TQ = 512
TK = 512
TM_IN = TQ
TM_MERGE = 512
TM_MOE = 1024
SEL_L = 2048
FLASH_UNROLL = 4
MOE_CHUNK = 160
ONES_ROWS = 16


def _cparams(sem):
    return pltpu.CompilerParams(dimension_semantics=sem, vmem_limit_bytes=VMEM_LIMIT)


def _full(shape):
    n = len(shape)
    return pl.BlockSpec(shape, lambda *_: (0,) * n)


def _rms(x, g):
    return x * lax.rsqrt(jnp.mean(x * x, axis=-1, keepdims=True) + EPS) * g


def _dot(a, b):
    return jnp.dot(a, b, preferred_element_type=F32)


def _in_proj_kernel(x_ref, g_ref, w_ref, qn_ref, wq_ref, kvn_ref, wkv_ref, pk_ref,
                    qg_ref, kg_ref, g512_ref, r512_ref, cos_ref, sin_ref,
                    sn_ref, sw_ref, sb_ref, pw_ref, pb_ref, ps_ref,
                    mqg_ref, mkg_ref, g256_ref, pl_ref, kc_ref,
                    qa_ref, ka_ref, va_ref, yb_ref, yc_ref, qd_ref, kd_ref, vd_ref, km_ref,
                    halo_ref, buf_ref, *, tiles_per_seq, q_scale):
    tm = x_ref.shape[0]
    i = pl.program_id(0)
    x = x_ref[...]
    h = _rms(x, g_ref[...]).astype(BF16)

    cm = _dot(h, w_ref[:, C_MLA:C_SGU])
    c_q = _rms(cm[:, :MLA_Q_LORA], qn_ref[...]).astype(BF16)
    c_kv = _rms(cm[:, MLA_Q_LORA:MLA_Q_LORA + MLA_KV_LORA], kvn_ref[...]).astype(BF16)
    kpe_grp = cm[:, MLA_Q_LORA + MLA_KV_LORA:].astype(BF16)
    q = _dot(c_q, wq_ref[...])
    kv = _dot(c_kv, wkv_ref[...])
    k = kv[:, :512] + _dot(kpe_grp, pk_ref[...])
    va_ref[0, 0] = kv[:, 512:].T.astype(BF16)
    cos = jnp.concatenate([cos_ref[...]] * MLA_HEADS, axis=1)
    sin = jnp.concatenate([sin_ref[...]] * MLA_HEADS, axis=1)
    inv_d = 1.0 / MLA_QK

    def head_norm_rope(t, gain):
        ms = _dot((t * t).astype(BF16), g512_ref[...]) * inv_d
        t = t * lax.rsqrt(ms + EPS) * gain
        rot = _dot(t.astype(BF16), r512_ref[...])
        return t * cos + rot * sin

    qa_ref[0, 0] = (head_norm_rope(q, qg_ref[...]) * q_scale).T.astype(BF16)
    ka_ref[...] = head_norm_rope(k, kg_ref[...]).astype(BF16)

    z = jax.nn.gelu(_dot(h, w_ref[:, C_SGU:C_POOL]))
    u = z[:, :SGU_W]
    vv = _rms(z[:, SGU_W:], sn_ref[...]).astype(BF16)
    lane_grp = lax.broadcasted_iota(jnp.int32, (SGU_CHUNK, SGU_W), 1) // (SGU_W // SGU_GROUPS)
    for c in range(tm // SGU_CHUNK):
        vc = vv[c * SGU_CHUNK:(c + 1) * SGU_CHUNK, :]
        y = sb_ref[...]
        for gi in range(SGU_GROUPS):
            y = y + jnp.where(lane_grp == gi, _dot(sw_ref[gi], vc), 0.0)
        yb_ref[c * SGU_CHUNK:(c + 1) * SGU_CHUNK, :] = (
            u[c * SGU_CHUNK:(c + 1) * SGU_CHUNK, :] * y).astype(BF16)

    xp = _dot(h, w_ref[:, C_POOL:C_MOBA])
    cd = _dot(h, w_ref[:, C_MOBA:C_END])

    @pl.when(i % tiles_per_seq == 0)
    def _():
        halo_ref[...] = jnp.zeros_like(halo_ref)

    buf_ref[0:POOL_MAXW, :] = halo_ref[...]
    buf_ref[POOL_MAXW:, :] = xp
    halo_ref[...] = xp[tm - POOL_MAXW:, :]
    pos = (i % tiles_per_seq) * tm + lax.broadcasted_iota(jnp.int32, (tm, POOL_W), 0)
    lane_w = lax.broadcasted_iota(jnp.int32, (tm, POOL_W), 1) // (POOL_W // len(POOL_WINDOWS))
    acc = xp
    pooled = jnp.zeros((tm, POOL_W), F32)
    shift = 1
    for gi, w in enumerate(POOL_WINDOWS):
        while shift < w:
            acc = acc + buf_ref[POOL_MAXW - shift:POOL_MAXW - shift + tm, :]
            shift += 1
        cnt = jnp.minimum(pos + 1, w).astype(F32)
        pooled = jnp.where(lane_w == gi, acc / cnt, pooled)
    p = (pooled - xp).astype(BF16)
    yc_ref[...] = ((_dot(p, pw_ref[...]) + pb_ref[...]) * ps_ref[...]).astype(BF16)

    inv_hd = 1.0 / MOBA_HD

    def head_norm(t, gain):
        ms = _dot((t * t).astype(BF16), g256_ref[...]) * inv_hd
        return t * lax.rsqrt(ms + EPS) * gain

    qd_ref[0] = head_norm(cd[:, :256], mqg_ref[...]).T
    kd = head_norm(cd[:, 256:512], mkg_ref[...])
    kd_ref[...] = (_dot(kd.astype(BF16), pl_ref[...]) + kc_ref[...]).astype(BF16)
    vd_ref[0, 0] = cd[:, 512:].T.astype(BF16)
    for c in range(tm // MOBA_BLOCK):
        km_ref[c] = jnp.mean(kd[c * MOBA_BLOCK:(c + 1) * MOBA_BLOCK, :], axis=0, keepdims=True)


def _in_proj(x2, lw, consts, S):
    T, D = x2.shape
    tm = TM_IN
    nt = T // tm
    tok = lambda w: pl.BlockSpec((tm, w), lambda i: (i, 0))
    pos_spec = pl.BlockSpec((tm, LANES), lambda i: (i % (S // tm), 0))
    ins = [
        (x2, tok(D)), (lw["attn_norm"], None), (lw["w_in"], None),
        (lw["mla_q_norm"], None), (lw["wq"], None), (lw["mla_kv_norm"], None), (lw["wkv"], None),
        (consts["pk"], None), (lw["mla_qg"], None), (lw["mla_kg"], None),
        (consts["g512"], None), (consts["r512"], None),
        (consts["cos"], pos_spec), (consts["sin"], pos_spec),
        (lw["sgu_norm"], None), (lw["sgu_w"], None), (lw["sgu_b"], None),
        (lw["pool_w"], None), (lw["pool_b"], None), (lw["pool_scale"], None),
        (lw["moba_qg"], None), (lw["moba_kg"], None), (consts["g256"], None),
        (consts["place"], None),
        (consts["kc"], pl.BlockSpec((tm, 4 * LANES), lambda i: (i % (S // tm), 0))),
    ]
    arrays = [a for a, _ in ins]
    specs = [s if s is not None else _full(a.shape) for a, s in ins]
    nb = tm // MOBA_BLOCK
    B = T // S
    ns = S // tm
    tr = lambda f: pl.BlockSpec((1, 1, f, tm), lambda i: (i // ns, i % ns, 0, 0))
    out_shape = [
        jax.ShapeDtypeStruct((B, ns, 512, tm), BF16), jax.ShapeDtypeStruct((T, 512), BF16),
        jax.ShapeDtypeStruct((B, ns, 256, tm), BF16), jax.ShapeDtypeStruct((T, 256), BF16),
        jax.ShapeDtypeStruct((T, 256), BF16), jax.ShapeDtypeStruct((B, 256, S), F32),
        jax.ShapeDtypeStruct((T, 512), BF16), jax.ShapeDtypeStruct((B, ns, 256, tm), BF16),
        jax.ShapeDtypeStruct((T // MOBA_BLOCK, 1, 256), F32),
    ]
    out_specs = [tr(512), tok(512), tr(256), tok(256), tok(256),
                 pl.BlockSpec((1, 256, tm), lambda i: (i // ns, 0, i % ns)),
                 tok(512), tr(256), pl.BlockSpec((nb, 1, 256), lambda i: (i, 0, 0))]
    q_scale = (MLA_QK ** -0.5) * math.log2(math.e)
    return pl.pallas_call(
        functools.partial(_in_proj_kernel, tiles_per_seq=S // tm, q_scale=q_scale),
        grid=(nt,), in_specs=specs, out_specs=out_specs, out_shape=out_shape,
        scratch_shapes=[pltpu.VMEM((POOL_MAXW, POOL_W), F32),
                        pltpu.VMEM((tm + POOL_MAXW, POOL_W), F32)],
        compiler_params=_cparams(("arbitrary",)), name="in_proj",
    )(*arrays)


def _moba_select_kernel(q_ref, km_ref, o_ref, *, slope_log2_step):
    hh = pl.program_id(1)
    li = pl.program_id(2)
    q = q_ref[0]
    L = q.shape[1]
    nrow = km_ref.shape[2]
    g = jnp.dot(km_ref[0, 0], q, preferred_element_type=F32, precision=lax.Precision.HIGHEST)
    n_i = lax.broadcasted_iota(jnp.int32, (nrow, L), 0)
    own = (li * L + lax.broadcasted_iota(jnp.int32, (nrow, L), 1)) // MOBA_BLOCK
    g = jnp.where(n_i < own, g, NEG)
    sel = jnp.zeros((nrow, L), F32)
    for _ in range(MOBA_TOPK):
        mx = jnp.max(g, axis=0, keepdims=True)
        first = jnp.min(jnp.where(g == mx, n_i, nrow), axis=0, keepdims=True)
        hit = n_i == first
        sel = jnp.where(hit & (mx > 0.5 * NEG), 1.0, sel)
        g = jnp.where(hit, NEG, g)
    slope = jnp.exp2(-slope_log2_step * (hh + 1).astype(F32))
    rel = (n_i - own).astype(F32) * (slope * MOBA_BLOCK)
    bias = jnp.where((sel > 0.0) | (n_i == own), rel, NEG)
    bias = jnp.where(n_i == nrow - 1, slope, bias)
    qs = (q * (MOBA_HD ** -0.5)).astype(BF16)
    bias = bias.astype(BF16)
    tq = o_ref.shape[3]
    for t in range(L // tq):
        o_ref[0, t, 0:MOBA_HD, :] = qs[:, t * tq:(t + 1) * tq]
        o_ref[0, t, MOBA_HD:, :] = bias[:, t * tq:(t + 1) * tq]


def _moba_select(qgT, kmean):
    B, _, S = qgT.shape
    H, hd = MOBA_HEADS, MOBA_HD
    L = min(SEL_L, S)
    return pl.pallas_call(
        functools.partial(_moba_select_kernel, slope_log2_step=8.0 / MOBA_HEADS),
        grid=(B, H, S // L),
        in_specs=[pl.BlockSpec((1, hd, L), lambda b, h, l: (b, h, l)),
                  pl.BlockSpec((1, 1, 64, hd), lambda b, h, l: (b, h, 0, 0))],
        out_specs=pl.BlockSpec((1, L // TQ, 2 * hd, TQ), lambda b, h, l: (b, l, h, 0)),
        out_shape=jax.ShapeDtypeStruct((B, S // TQ, H * 2 * hd, TQ), BF16),
        compiler_params=_cparams(("arbitrary",) * 3), name="moba_select",
    )(qgT, kmean)


def _flash_kernel(q_ref, k_ref, v_ref, o_ref, s_scr, acc_scr, *, heads, use_exp2):
    H = heads
    tq = q_ref.shape[3]
    tk = k_ref.shape[2]
    dk = q_ref.shape[2] // H
    dv = v_ref.shape[2] // H
    ratio = tk // tq
    ex = jnp.exp2 if use_exp2 else jnp.exp
    i = pl.program_id(1)
    jd = i // ratio
    kpos = jd * tk + lax.broadcasted_iota(jnp.int32, (tk, tq), 0)
    qpos = i * tq + lax.broadcasted_iota(jnp.int32, (tk, tq), 1)
    causal = kpos <= qpos
    ones = jnp.ones((ONES_ROWS, tk), BF16)

    def scores(jt, h, masked):
        s = _dot(k_ref[0, jt, :, h * dk:(h + 1) * dk], q_ref[0, 0, h * dk:(h + 1) * dk, :])
        if masked:
            s = jnp.where(causal, s, NEG)
        s_scr[h] = s
        return jnp.max(s, axis=0, keepdims=True)

    def update(jt, h, cmax, m):
        m_new = jnp.maximum(m, cmax)
        alpha = ex(m - m_new)
        p = ex(s_scr[h] - m_new).astype(BF16)
        vt = jnp.concatenate([v_ref[0, jt, h * dv:(h + 1) * dv, :], ones], axis=0)
        acc_scr[h] = acc_scr[h] * alpha + _dot(vt, p)
        return m_new

    acc_scr[...] = jnp.zeros_like(acc_scr)
    ms = [jnp.full((1, tq), NEG, F32) for _ in range(H)]
    cmax = scores(jd, 0, True)
    for h in range(H):
        nxt = scores(jd, h + 1, True) if h + 1 < H else scores(0, 0, False)
        ms[h] = update(jd, h, cmax, ms[h])
        cmax = nxt

    def tile_step(j, carry):
        ms, cmax = carry
        ms = list(ms)
        jn = jnp.minimum(j + 1, jd - 1)
        for h in range(H):
            nxt = scores(j, h + 1, False) if h + 1 < H else scores(jn, 0, False)
            ms[h] = update(j, h, cmax, ms[h])
            cmax = nxt
        return tuple(ms), cmax

    def unrolled(t, c):
        for u in range(FLASH_UNROLL):
            c = tile_step(FLASH_UNROLL * t + u, c)
        return c

    carry = lax.fori_loop(0, jd // FLASH_UNROLL, unrolled, (tuple(ms), cmax))
    lax.fori_loop(jd - jd % FLASH_UNROLL, jd, tile_step, carry)
    outs = []
    for h in range(H):
        acc = acc_scr[h]
        outs.append(acc[:dv] / acc[dv:dv + 1])
    o_ref[...] = jnp.concatenate(outs, axis=0).T.astype(BF16)


def _flash(qT, k, vT, heads, use_exp2, name):
    B, nq, hdk, tq = qT.shape
    nk, tk = k.shape[1:3]
    hdv = vT.shape[2]
    dva = hdv // heads + ONES_ROWS
    whole = lambda shp: pl.BlockSpec((1,) + shp, lambda b, i: (b, 0, 0, 0),
                                     pipeline_mode=pl.Buffered(1))
    return pl.pallas_call(
        functools.partial(_flash_kernel, heads=heads, use_exp2=use_exp2),
        grid=(B, nq),
        in_specs=[pl.BlockSpec((1, 1, hdk, tq), lambda b, i: (b, i, 0, 0)),
                  whole((nk, tk, hdk)), whole((nk, hdv, tk))],
        out_specs=pl.BlockSpec((tq, hdv), lambda b, i: (b * nq + i, 0)),
        out_shape=jax.ShapeDtypeStruct((B * nq * tq, hdv), BF16),
        scratch_shapes=[pltpu.VMEM((heads, tk, tq), F32), pltpu.VMEM((heads, dva, tq), F32)],
        compiler_params=_cparams(("arbitrary", "arbitrary")), name=name,
    )(qT, k, vT)


def _merge_kernel(x_ref, ya_ref, yb_ref, yc_ref, yd_ref, g_ref, wg_ref, wb_ref, wo_ref,
                  fg_ref, rw_ref, rb_ref, x1_ref, h2_ref, rt_ref, cnt_ref, h2_scr):
    @pl.when(pl.program_id(0) == 0)
    def _():
        h2_scr[...] = jnp.zeros_like(h2_scr)

    router = _router_stages(h2_scr, rw_ref, rb_ref, rt_ref, cnt_ref)
    x = x_ref[...]
    h = _rms(x, g_ref[...]).astype(BF16)
    merged = None
    for bi, y_ref in enumerate((ya_ref, yb_ref, yc_ref, yd_ref)):
        t = jax.nn.sigmoid(_dot(h, wg_ref[bi])) * _dot(y_ref[...], wb_ref[bi])
        merged = t if merged is None else merged + t
        next(router)
    x1 = x + _dot(merged.astype(BF16), wo_ref[...])
    x1_ref[...] = x1
    h2_new = _rms(x1, fg_ref[...])
    h2_ref[...] = h2_new.astype(BF16)
    for _ in router:
        pass
    h2_scr[...] = h2_new


def _router_stages(h2_scr, rw_ref, rb_ref, rt_ref, cnt_ref):
    h2 = h2_scr[...]
    nt = (((1,), (1,)), ((), ()))
    h2_hi = h2.astype(BF16)
    h2_lo = (h2 - h2_hi.astype(F32)).astype(BF16)
    rw = rw_ref[...]
    rw_hi = rw.astype(BF16)
    rw_lo = (rw - rw_hi.astype(F32)).astype(BF16)
    logits = (lax.dot_general(rw_hi, h2_hi, nt, preferred_element_type=F32)
              + lax.dot_general(rw_hi, h2_lo, nt, preferred_element_type=F32)
              + lax.dot_general(rw_lo, h2_hi, nt, preferred_element_type=F32))
    yield
    scores = jax.nn.sigmoid(logits)
    biased = scores + rb_ref[...]
    s_rows = [scores[e:e + 1, :] for e in range(N_EXPERTS)]
    b_rows = [biased[e:e + 1, :] for e in range(N_EXPERTS)]

    def top2(vals):
        m1 = vals[0]
        i1 = jnp.zeros_like(m1, dtype=jnp.int32)
        for j in range(1, len(vals)):
            better = vals[j] > m1
            i1 = jnp.where(better, j, i1)
            m1 = jnp.where(better, vals[j], m1)
        m2 = None
        i2 = None
        for j in range(len(vals)):
            cand = jnp.where(i1 == j, NEG, vals[j])
            if m2 is None:
                m2, i2 = cand, jnp.zeros_like(i1)
            else:
                better = cand > m2
                i2 = jnp.where(better, j, i2)
                m2 = jnp.where(better, cand, m2)
        return m1, i1, m2, i2

    best = None
    for gi in range(N_GROUPS):
        bm1, bi1, bm2, bi2 = top2(b_rows[gi * EPG:(gi + 1) * EPG])
        gs = bm1 + bm2
        e1 = bi1 + gi * EPG
        e2 = bi2 + gi * EPG
        if best is None:
            best = (gs, e1, e2)
        else:
            better = gs > best[0]
            best = (jnp.where(better, gs, best[0]), jnp.where(better, e1, best[1]),
                    jnp.where(better, e2, best[2]))
        if gi % 2 == 1:
            yield
    _, e1, e2 = best
    w1 = jnp.zeros_like(s_rows[0])
    w2 = jnp.zeros_like(s_rows[0])
    for e in range(N_EXPERTS):
        w1 = jnp.where(e1 == e, s_rows[e], w1)
        w2 = jnp.where(e2 == e, s_rows[e], w2)
    tot = w1 + w2
    w1 = w1 / tot
    w2 = w2 / tot
    r8 = lax.broadcasted_iota(jnp.int32, (8, scores.shape[1]), 0)
    rt_ref[...] = jnp.where(r8 == 0, e1.astype(F32), jnp.where(r8 == 1, e2.astype(F32),
                            jnp.where(r8 == 2, w1, jnp.where(r8 == 3, w2, 0.0))))
    row = lax.broadcasted_iota(jnp.int32, scores.shape, 0)
    assigned = jnp.where(row == e1, 1.0, jnp.where(row == e2, 1.0, 0.0))
    cnt = jnp.sum(assigned, axis=1, keepdims=True)
    cnt_ref[0] = jnp.broadcast_to(cnt, (N_EXPERTS, LANES)).astype(jnp.int32)
    yield


def _merge(x2, ya, yb, yc, yd, lw, router_w_t, router_b):
    T, D = x2.shape
    tm = TM_MERGE
    n = T // tm
    tok = lambda w: pl.BlockSpec((tm, w), lambda i: (jnp.minimum(i, n - 1), 0))
    consts = [lw["attn_norm"], lw["w_gate"], lw["w_branch"], lw["w_out"], lw["ffn_norm"],
              router_w_t, router_b]
    return pl.pallas_call(
        _merge_kernel, grid=(n + 1,),
        in_specs=[tok(D), tok(256), tok(256), tok(256), tok(256)] + [_full(a.shape) for a in consts],
        out_specs=[tok(D), tok(D), pl.BlockSpec((8, tm), lambda i: (0, jnp.maximum(i - 1, 0))),
                   pl.BlockSpec((1, N_EXPERTS, LANES), lambda i: (jnp.maximum(i - 1, 0), 0, 0))],
        out_shape=[jax.ShapeDtypeStruct((T, D), F32), jax.ShapeDtypeStruct((T, D), BF16),
                   jax.ShapeDtypeStruct((8, T), F32),
                   jax.ShapeDtypeStruct((n, N_EXPERTS, LANES), jnp.int32)],
        scratch_shapes=[pltpu.VMEM((tm, D), F32)],
        compiler_params=_cparams(("arbitrary",)), name="merge",
    )(x2, ya, yb, yc, yd, *consts)


def _moe_kernel(cnt_ref, x1_ref, h2_ref, rt_ref, tri_ref, sg_ref, su_ref, sd_ref,
                wg_ref, wu_ref, wd_ref, o_ref, rank_scr, gate_scr):
    i = pl.program_id(0)
    e = pl.program_id(1)
    tm = h2_ref.shape[0]

    def ffn(rows, scale, wg, wu, wd):
        a = jax.nn.silu(_dot(rows, wg)) * _dot(rows, wu)
        if scale is not None:
            a = a * scale
        return _dot(a.astype(BF16), wd)

    @pl.when(e == 0)
    def _():
        o_ref[...] = x1_ref[...] + ffn(h2_ref[...], None, sg_ref[...], su_ref[...], sd_ref[...])
        row = lax.broadcasted_iota(jnp.int32, (N_EXPERTS, tm), 0).astype(F32)
        e1, e2 = rt_ref[0:1, :], rt_ref[1:2, :]
        w1, w2 = rt_ref[2:3, :], rt_ref[3:4, :]
        assigned = jnp.where(row == e1, 1.0, jnp.where(row == e2, 1.0, 0.0))
        before = _dot(assigned.astype(BF16), tri_ref[...])
        rank_scr[...] = jnp.where(assigned > 0.0, before, -1.0)
        gate_scr[...] = jnp.where(row == e1, w1, jnp.where(row == e2, w2, 0.0))

    @pl.when(e > 0)
    def _():
        ex = e - 1
        cnt = cnt_ref[i * N_EXPERTS + ex]
        rank_row = rank_scr[pl.ds(ex, 1), :]
        gate_row = gate_scr[pl.ds(ex, 1), :]
        slot = lax.broadcasted_iota(jnp.int32, (MOE_CHUNK, tm), 0).astype(F32)

        def chunk(c, carry):
            hit = rank_row == slot + (c * MOE_CHUNK).astype(F32)
            onehot = jnp.where(hit, 1.0, 0.0).astype(BF16)
            rows = _dot(onehot, h2_ref[...]).astype(BF16)
            scale = jnp.sum(jnp.where(hit, gate_row, 0.0), axis=1, keepdims=True)
            y = ffn(rows, scale, wg_ref[0], wu_ref[0], wd_ref[0]).astype(BF16)
            o_ref[...] += lax.dot_general(onehot, y, (((0,), (0,)), ((), ())),
                                          preferred_element_type=F32)
            return carry

        lax.fori_loop(0, (cnt + MOE_CHUNK - 1) // MOE_CHUNK, chunk, 0)


def _moe(x1, h2, route, counts, tri, shared, wg, wu, wd):
    T, D = x1.shape
    tm = tri.shape[0]
    ne, _, ff = wg.shape
    const = lambda shp: pl.BlockSpec(shp, lambda i, e, c: (0, 0), pipeline_mode=pl.Buffered(1))
    expert = lambda shp: pl.BlockSpec((1,) + shp, lambda i, e, c: (jnp.maximum(e - 1, 0), 0, 0))
    grid_spec = pltpu.PrefetchScalarGridSpec(
        num_scalar_prefetch=1, grid=(T // tm, ne + 1),
        in_specs=[pl.BlockSpec((tm, D), lambda i, e, c: (i, 0)),
                  pl.BlockSpec((tm, D), lambda i, e, c: (i, 0)),
                  pl.BlockSpec((8, tm), lambda i, e, c: (0, i)),
                  const((tm, tm)), const((D, ff)), const((D, ff)), const((ff, D)),
                  expert((D, ff)), expert((D, ff)), expert((ff, D))],
        out_specs=pl.BlockSpec((tm, D), lambda i, e, c: (i, 0)),
        scratch_shapes=[pltpu.VMEM((N_EXPERTS, tm), F32), pltpu.VMEM((N_EXPERTS, tm), F32)])
    return pl.pallas_call(
        _moe_kernel, grid_spec=grid_spec,
        out_shape=jax.ShapeDtypeStruct((T, D), F32),
        compiler_params=_cparams(("arbitrary", "arbitrary")), name="moe",
    )(counts, x1, h2, route, tri, *shared, wg, wu, wd)


def _constants(S):
    half = MLA_ROPE // 2
    inv = ROPE_THETA ** (-jnp.arange(half, dtype=F32) / half)
    ang = jnp.arange(S, dtype=F32)[:, None] * inv[None, :]
    cos = jnp.ones((S, LANES), F32)
    cos = cos.at[:, MLA_NOPE:MLA_NOPE + half].set(jnp.cos(ang))
    cos = cos.at[:, MLA_NOPE + half:MLA_QK].set(jnp.cos(ang))
    sin = jnp.zeros((S, LANES), F32)
    sin = sin.at[:, MLA_NOPE:MLA_NOPE + half].set(jnp.sin(ang))
    sin = sin.at[:, MLA_NOPE + half:MLA_QK].set(jnp.sin(ang))

    r = np.zeros((LANES, LANES), np.float32)
    for j in range(half):
        r[MLA_NOPE + half + j, MLA_NOPE + j] = -1.0
        r[MLA_NOPE + j, MLA_NOPE + half + j] = 1.0
    g = np.zeros((LANES, LANES), np.float32)
    g[:MLA_QK, :] = 1.0
    eye4 = np.eye(MLA_HEADS, dtype=np.float32)
    pk = np.zeros((LANES, 4 * LANES), np.float32)
    for hh in range(MLA_HEADS):
        for j in range(MLA_ROPE):
            pk[j, hh * LANES + MLA_NOPE + j] = 1.0
    g256 = np.kron(np.eye(MOBA_HEADS, dtype=np.float32), np.ones((MOBA_HD, MOBA_HD), np.float32))

    nkb = S // MOBA_BLOCK
    kpos = np.arange(S)
    kc = np.zeros((S, MOBA_HD), np.float32)
    blk = kpos // MOBA_BLOCK
    m = blk < min(nkb, MOBA_HD) - 1
    kc[kpos[m], blk[m]] = 1.0
    kc[:, MOBA_HD - 1] = kpos % MOBA_BLOCK
    kc4 = np.zeros((S, MOBA_HEADS, 2 * MOBA_HD), np.float32)
    kc4[:, :, MOBA_HD:] = kc[:, None, :]
    place = np.zeros((MOBA_HEADS * MOBA_HD, MOBA_HEADS * 2 * MOBA_HD), np.float32)
    for hh in range(MOBA_HEADS):
        for j in range(MOBA_HD):
            place[hh * MOBA_HD + j, hh * 2 * MOBA_HD + j] = 1.0
    return {
        "cos": cos, "sin": sin,
        "r512": jnp.asarray(np.kron(eye4, r), BF16), "g512": jnp.asarray(np.kron(eye4, g), BF16),
        "pk": jnp.asarray(pk, BF16), "g256": jnp.asarray(g256, BF16),
        "kc": jnp.asarray(kc4.reshape(S, -1), BF16), "place": jnp.asarray(place, BF16),
    }


def _head_pad(v):
    one = jnp.concatenate([v, jnp.zeros((LANES - MLA_QK,), F32)])
    return jnp.tile(one, MLA_HEADS)[None, :]


def _layer_weights(l, p):
    D = p["w_in"].shape[1]
    w_in = p["w_in"][l]
    o1 = MLA_Q_LORA + MLA_KV_LORA + MLA_ROPE
    o2 = o1 + 2 * SGU_W
    o3 = o2 + POOL_W
    w = jnp.concatenate([w_in[:, :o1], jnp.zeros((D, C_SGU - o1), F32), w_in[:, o1:]], axis=1)

    wq = p["mla_w_q_up"][l].reshape(MLA_Q_LORA, MLA_HEADS, MLA_QK)
    wq = jnp.pad(wq, ((0, 0), (0, 0), (0, LANES - MLA_QK))).reshape(MLA_Q_LORA, MLA_HEADS * LANES)
    wkv = p["mla_w_kv_up"][l].reshape(MLA_KV_LORA, MLA_HEADS, MLA_NOPE + MLA_V)
    wk = jnp.pad(wkv[:, :, :MLA_NOPE], ((0, 0), (0, 0), (0, LANES - MLA_NOPE)))
    wk = wk.reshape(MLA_KV_LORA, MLA_HEADS * LANES)
    wv = wkv[:, :, MLA_NOPE:].reshape(MLA_KV_LORA, MLA_HEADS * MLA_V)

    tri = jnp.tril(jnp.ones((SGU_CHUNK, SGU_CHUNK), dtype=bool))
    sgu_w = jnp.where(tri[None], p["sgu_w"][l], 0.0)
    sgu_b = jnp.repeat(p["sgu_b"][l].T, SGU_W // SGU_GROUPS, axis=1)
    pool_w = jax.scipy.linalg.block_diag(*[p["pool_w"][l, gi] for gi in range(len(POOL_WINDOWS))])

    ne = p["moe_w_gate"].shape[1]
    return {
        "attn_norm": p["attn_norm"][l][None, :], "w_in": w.astype(BF16),
        "mla_q_norm": p["mla_q_norm"][l][None, :], "wq": wq.astype(BF16),
        "mla_kv_norm": p["mla_kv_norm"][l][None, :],
        "wkv": jnp.concatenate([wk, wv], axis=1).astype(BF16),
        "mla_qg": _head_pad(p["mla_q_gain"][l]), "mla_kg": _head_pad(p["mla_k_gain"][l]),
        "sgu_norm": p["sgu_norm"][l][None, :], "sgu_w": sgu_w.astype(BF16), "sgu_b": sgu_b,
        "pool_w": pool_w.astype(BF16), "pool_b": p["pool_b"][l][None, :],
        "pool_scale": p["pool_scale"][l][None, :],
        "moba_qg": jnp.tile(p["moba_q_gain"][l], MOBA_HEADS)[None, :],
        "moba_kg": jnp.tile(p["moba_k_gain"][l], MOBA_HEADS)[None, :],
        "w_gate": p["w_gate"][l].astype(BF16), "w_branch": p["w_branch"][l].astype(BF16),
        "w_out": p["w_out"][l].astype(BF16), "ffn_norm": p["ffn_norm"][l][None, :],
        "shared": (p["shared_w_gate"][l].astype(BF16), p["shared_w_up"][l].astype(BF16),
                   p["shared_w_down"][l].astype(BF16)),
        "ffn_g": p["moe_w_gate"][l].astype(BF16), "ffn_u": p["moe_w_up"][l].astype(BF16),
        "ffn_d": p["moe_w_down"][l].astype(BF16),
    }


def kernel(x, attn_norm, w_in, mla_q_norm, mla_w_q_up, mla_kv_norm, mla_w_kv_up, mla_q_gain, mla_k_gain, sgu_norm, sgu_w, sgu_b, pool_w, pool_b, pool_scale, moba_q_gain, moba_k_gain, w_gate, w_branch, w_out, ffn_norm, router_w, router_bias, moe_w_gate, moe_w_up, moe_w_down, shared_w_gate, shared_w_up, shared_w_down):
    p = dict(attn_norm=attn_norm, w_in=w_in, mla_q_norm=mla_q_norm, mla_w_q_up=mla_w_q_up,
             mla_kv_norm=mla_kv_norm, mla_w_kv_up=mla_w_kv_up, mla_q_gain=mla_q_gain,
             mla_k_gain=mla_k_gain, sgu_norm=sgu_norm, sgu_w=sgu_w, sgu_b=sgu_b, pool_w=pool_w,
             pool_b=pool_b, pool_scale=pool_scale, moba_q_gain=moba_q_gain,
             moba_k_gain=moba_k_gain, w_gate=w_gate, w_branch=w_branch, w_out=w_out,
             ffn_norm=ffn_norm, moe_w_gate=moe_w_gate, moe_w_up=moe_w_up, moe_w_down=moe_w_down,
             shared_w_gate=shared_w_gate, shared_w_up=shared_w_up, shared_w_down=shared_w_down)
    B, S, D = x.shape
    T = B * S
    depth = attn_norm.shape[0]
    nkb = S // MOBA_BLOCK
    assert TQ == TK and S % TK == 0 and T % TM_MERGE == 0 and nkb <= MOBA_HD
    consts = _constants(S)
    router_w_t = router_w.T
    router_b = router_bias[:, None]
    tm_moe = min(TM_MOE, T)
    tri = jnp.asarray(np.triu(np.ones((tm_moe, tm_moe), np.float32), 1), BF16)

    x2 = x.reshape(T, D)
    for l in range(depth):
        lw = _layer_weights(l, p)
        qa, ka, va, yb, yc, qd, kd, vd, km = _in_proj(x2, lw, consts, S)
        ya = _flash(qa, ka.reshape(B, S // TK, TK, -1), va, MLA_HEADS, True, "mla_attn")

        kmean = km.reshape(B, nkb, MOBA_HEADS, MOBA_HD).transpose(0, 2, 1, 3)
        kmean = jnp.pad(kmean, ((0, 0), (0, 0), (0, MOBA_HD - nkb), (0, 0)))
        q_aug = _moba_select(qd, kmean)
        yd = _flash(q_aug, kd.reshape(B, S // TK, TK, -1), vd, MOBA_HEADS, False, "moba_attn")

        x1, h2, route, cnt = _merge(x2, ya, yb, yc, yd, lw, router_w_t, router_b)
        counts = cnt[:, :, 0].reshape(T // tm_moe, tm_moe // TM_MERGE, N_EXPERTS).sum(axis=1)
        x2 = _moe(x1, h2, route, counts.reshape(-1), tri, lw["shared"],
                  lw["ffn_g"], lw["ffn_u"], lw["ffn_d"])
    return x2.reshape(B, S, D)
```

```python
import functools
import math

import jax
import jax.numpy as jnp
import numpy as np
from jax import lax
from jax.experimental import pallas as pl
from jax.experimental.pallas import tpu as pltpu

F32 = jnp.float32
BF16 = jnp.bfloat16
EPS = 1e-6
NEG = -1e30

LANES = 128
VMEM_LIMIT = 56 * 1024 * 1024

MLA_HEADS = 4
MLA_NOPE = 64
MLA_ROPE = 32
MLA_V = 64
MLA_Q_LORA = 256
MLA_KV_LORA = 128
ROPE_THETA = 10000.0
MLA_QK = MLA_NOPE + MLA_ROPE

SGU_GROUPS = 4
SGU_CHUNK = 128
SGU_W = 256

POOL_WINDOWS = (2, 4, 8, 16)
POOL_MAXW = 16
POOL_W = 256

MOBA_HEADS = 4
MOBA_HD = 64
MOBA_BLOCK = 256
MOBA_TOPK = 3

N_EXPERTS = 16
N_GROUPS = 4
EPG = 4

C_MLA = 0
C_SGU = 512
C_POOL = 1024
C_MOBA = 1280
C_END = 2048

TQ = 512
TK = 512
TM_IN = TQ
TM_MERGE = 512
TM_MOE = 1024
SEL_L = 2048
FLASH_UNROLL = 4
MOE_EXPERTS_PER_STEP = 2
MOE_CHUNK = 160
ONES_ROWS = 16


def _cparams(sem):
    return pltpu.CompilerParams(dimension_semantics=sem, vmem_limit_bytes=VMEM_LIMIT)


def _full(shape):
    n = len(shape)
    return pl.BlockSpec(shape, lambda *_: (0,) * n)


def _rms(x, g):
    return x * lax.rsqrt(jnp.mean(x * x, axis=-1, keepdims=True) + EPS) * g


def _dot(a, b):
    return jnp.dot(a, b, preferred_element_type=F32)


def _in_proj_kernel(x_ref, g_ref, w_ref, qn_ref, wq_ref, kvn_ref, wkv_ref, pk_ref,
                    qg_ref, kg_ref, g512_ref, r512_ref, cos_ref, sin_ref,
                    sn_ref, sw_ref, sb_ref, pw_ref, pb_ref, ps_ref,
                    mqg_ref, mkg_ref, g256_ref, pl_ref, kc_ref,
                    qa_ref, ka_ref, va_ref, yb_ref, yc_ref, qd_ref, kd_ref, vd_ref, km_ref,
                    halo_ref, buf_ref, *, tiles_per_seq, q_scale):
    tm = x_ref.shape[0]
    i = pl.program_id(0)
    x = x_ref[...]
    h = _rms(x, g_ref[...]).astype(BF16)

    cm = _dot(h, w_ref[:, C_MLA:C_SGU])
    c_q = _rms(cm[:, :MLA_Q_LORA], qn_ref[...]).astype(BF16)
    c_kv = _rms(cm[:, MLA_Q_LORA:MLA_Q_LORA + MLA_KV_LORA], kvn_ref[...]).astype(BF16)
    kpe_grp = cm[:, MLA_Q_LORA + MLA_KV_LORA:].astype(BF16)
    q = _dot(c_q, wq_ref[...])
    kv = _dot(c_kv, wkv_ref[...])
    k = kv[:, :512] + _dot(kpe_grp, pk_ref[...])
    va_ref[0, 0] = kv[:, 512:].T.astype(BF16)
    cos = jnp.concatenate([cos_ref[...]] * MLA_HEADS, axis=1)
    sin = jnp.concatenate([sin_ref[...]] * MLA_HEADS, axis=1)
    inv_d = 1.0 / MLA_QK

    def head_norm_rope(t, gain):
        ms = _dot((t * t).astype(BF16), g512_ref[...]) * inv_d
        t = t * lax.rsqrt(ms + EPS) * gain
        rot = _dot(t.astype(BF16), r512_ref[...])
        return t * cos + rot * sin

    qa_ref[0, 0] = (head_norm_rope(q, qg_ref[...]) * q_scale).T.astype(BF16)
    ka_ref[...] = head_norm_rope(k, kg_ref[...]).astype(BF16)

    z = jax.nn.gelu(_dot(h, w_ref[:, C_SGU:C_POOL]))
    u = z[:, :SGU_W]
    vv = _rms(z[:, SGU_W:], sn_ref[...]).astype(BF16)
    lane_grp = lax.broadcasted_iota(jnp.int32, (SGU_CHUNK, SGU_W), 1) // (SGU_W // SGU_GROUPS)
    for c in range(tm // SGU_CHUNK):
        vc = vv[c * SGU_CHUNK:(c + 1) * SGU_CHUNK, :]
        y = sb_ref[...]
        for gi in range(SGU_GROUPS):
            y = y + jnp.where(lane_grp == gi, _dot(sw_ref[gi], vc), 0.0)
        yb_ref[c * SGU_CHUNK:(c + 1) * SGU_CHUNK, :] = (
            u[c * SGU_CHUNK:(c + 1) * SGU_CHUNK, :] * y).astype(BF16)

    xp = _dot(h, w_ref[:, C_POOL:C_MOBA])
    cd = _dot(h, w_ref[:, C_MOBA:C_END])

    @pl.when(i % tiles_per_seq == 0)
    def _():
        halo_ref[...] = jnp.zeros_like(halo_ref)

    buf_ref[0:POOL_MAXW, :] = halo_ref[...]
    buf_ref[POOL_MAXW:, :] = xp
    halo_ref[...] = xp[tm - POOL_MAXW:, :]
    pos = (i % tiles_per_seq) * tm + lax.broadcasted_iota(jnp.int32, (tm, POOL_W), 0)
    lane_w = lax.broadcasted_iota(jnp.int32, (tm, POOL_W), 1) // (POOL_W // len(POOL_WINDOWS))
    acc = xp
    pooled = jnp.zeros((tm, POOL_W), F32)
    shift = 1
    for gi, w in enumerate(POOL_WINDOWS):
        while shift < w:
            acc = acc + buf_ref[POOL_MAXW - shift:POOL_MAXW - shift + tm, :]
            shift += 1
        cnt = jnp.minimum(pos + 1, w).astype(F32)
        pooled = jnp.where(lane_w == gi, acc / cnt, pooled)
    p = (pooled - xp).astype(BF16)
    yc_ref[...] = ((_dot(p, pw_ref[...]) + pb_ref[...]) * ps_ref[...]).astype(BF16)

    inv_hd = 1.0 / MOBA_HD

    def head_norm(t, gain):
        ms = _dot((t * t).astype(BF16), g256_ref[...]) * inv_hd
        return t * lax.rsqrt(ms + EPS) * gain

    qd_ref[0] = head_norm(cd[:, :256], mqg_ref[...]).T
    kd = head_norm(cd[:, 256:512], mkg_ref[...])
    kd_ref[...] = (_dot(kd.astype(BF16), pl_ref[...]) + kc_ref[...]).astype(BF16)
    vd_ref[0, 0] = cd[:, 512:].T.astype(BF16)
    for c in range(tm // MOBA_BLOCK):
        km_ref[c] = jnp.mean(kd[c * MOBA_BLOCK:(c + 1) * MOBA_BLOCK, :], axis=0, keepdims=True)


def _in_proj(x2, lw, consts, S):
    T, D = x2.shape
    tm = TM_IN
    nt = T // tm
    tok = lambda w: pl.BlockSpec((tm, w), lambda i: (i, 0))
    pos_spec = pl.BlockSpec((tm, LANES), lambda i: (i % (S // tm), 0))
    ins = [
        (x2, tok(D)), (lw["attn_norm"], None), (lw["w_in"], None),
        (lw["mla_q_norm"], None), (lw["wq"], None), (lw["mla_kv_norm"], None), (lw["wkv"], None),
        (consts["pk"], None), (lw["mla_qg"], None), (lw["mla_kg"], None),
        (consts["g512"], None), (consts["r512"], None),
        (consts["cos"], pos_spec), (consts["sin"], pos_spec),
        (lw["sgu_norm"], None), (lw["sgu_w"], None), (lw["sgu_b"], None),
        (lw["pool_w"], None), (lw["pool_b"], None), (lw["pool_scale"], None),
        (lw["moba_qg"], None), (lw["moba_kg"], None), (consts["g256"], None),
        (consts["place"], None),
        (consts["kc"], pl.BlockSpec((tm, 4 * LANES), lambda i: (i % (S // tm), 0))),
    ]
    arrays = [a for a, _ in ins]
    specs = [s if s is not None else _full(a.shape) for a, s in ins]
    nb = tm // MOBA_BLOCK
    B = T // S
    ns = S // tm
    tr = lambda f: pl.BlockSpec((1, 1, f, tm), lambda i: (i // ns, i % ns, 0, 0))
    out_shape = [
        jax.ShapeDtypeStruct((B, ns, 512, tm), BF16), jax.ShapeDtypeStruct((T, 512), BF16),
        jax.ShapeDtypeStruct((B, ns, 256, tm), BF16), jax.ShapeDtypeStruct((T, 256), BF16),
        jax.ShapeDtypeStruct((T, 256), BF16), jax.ShapeDtypeStruct((B, 256, S), F32),
        jax.ShapeDtypeStruct((T, 512), BF16), jax.ShapeDtypeStruct((B, ns, 256, tm), BF16),
        jax.ShapeDtypeStruct((T // MOBA_BLOCK, 1, 256), F32),
    ]
    out_specs = [tr(512), tok(512), tr(256), tok(256), tok(256),
                 pl.BlockSpec((1, 256, tm), lambda i: (i // ns, 0, i % ns)),
                 tok(512), tr(256), pl.BlockSpec((nb, 1, 256), lambda i: (i, 0, 0))]
    q_scale = (MLA_QK ** -0.5) * math.log2(math.e)
    return pl.pallas_call(
        functools.partial(_in_proj_kernel, tiles_per_seq=S // tm, q_scale=q_scale),
        grid=(nt,), in_specs=specs, out_specs=out_specs, out_shape=out_shape,
        scratch_shapes=[pltpu.VMEM((POOL_MAXW, POOL_W), F32),
                        pltpu.VMEM((tm + POOL_MAXW, POOL_W), F32)],
        compiler_params=_cparams(("arbitrary",)), name="in_proj",
    )(*arrays)


def _moba_select_kernel(q_ref, km_ref, o_ref, *, slope_log2_step):
    hh = pl.program_id(1)
    li = pl.program_id(2)
    q = q_ref[0]
    L = q.shape[1]
    nrow = km_ref.shape[2]
    g = jnp.dot(km_ref[0, 0], q, preferred_element_type=F32, precision=lax.Precision.HIGHEST)
    n_i = lax.broadcasted_iota(jnp.int32, (nrow, L), 0)
    own = (li * L + lax.broadcasted_iota(jnp.int32, (nrow, L), 1)) // MOBA_BLOCK
    g = jnp.where(n_i < own, g, NEG)
    sel = jnp.zeros((nrow, L), F32)
    for _ in range(MOBA_TOPK):
        mx = jnp.max(g, axis=0, keepdims=True)
        first = jnp.min(jnp.where(g == mx, n_i, nrow), axis=0, keepdims=True)
        hit = n_i == first
        sel = jnp.where(hit & (mx > 0.5 * NEG), 1.0, sel)
        g = jnp.where(hit, NEG, g)
    slope = jnp.exp2(-slope_log2_step * (hh + 1).astype(F32))
    rel = (n_i - own).astype(F32) * (slope * MOBA_BLOCK)
    bias = jnp.where((sel > 0.0) | (n_i == own), rel, NEG)
    bias = jnp.where(n_i == nrow - 1, slope, bias)
    qs = (q * (MOBA_HD ** -0.5)).astype(BF16)
    bias = bias.astype(BF16)
    tq = o_ref.shape[3]
    for t in range(L // tq):
        o_ref[0, t, 0:MOBA_HD, :] = qs[:, t * tq:(t + 1) * tq]
        o_ref[0, t, MOBA_HD:, :] = bias[:, t * tq:(t + 1) * tq]


def _moba_select(qgT, kmean):
    B, _, S = qgT.shape
    H, hd = MOBA_HEADS, MOBA_HD
    L = min(SEL_L, S)
    return pl.pallas_call(
        functools.partial(_moba_select_kernel, slope_log2_step=8.0 / MOBA_HEADS),
        grid=(B, H, S // L),
        in_specs=[pl.BlockSpec((1, hd, L), lambda b, h, l: (b, h, l)),
                  pl.BlockSpec((1, 1, 64, hd), lambda b, h, l: (b, h, 0, 0))],
        out_specs=pl.BlockSpec((1, L // TQ, 2 * hd, TQ), lambda b, h, l: (b, l, h, 0)),
        out_shape=jax.ShapeDtypeStruct((B, S // TQ, H * 2 * hd, TQ), BF16),
        compiler_params=_cparams(("arbitrary",) * 3), name="moba_select",
    )(qgT, kmean)


def _flash_kernel(q_ref, k_ref, v_ref, o_ref, s_scr, acc_scr, *, heads, use_exp2):
    H = heads
    tq = q_ref.shape[3]
    tk = k_ref.shape[2]
    dk = q_ref.shape[2] // H
    dv = v_ref.shape[2] // H
    ratio = tk // tq
    ex = jnp.exp2 if use_exp2 else jnp.exp
    i = pl.program_id(1)
    jd = i // ratio
    kpos = jd * tk + lax.broadcasted_iota(jnp.int32, (tk, tq), 0)
    qpos = i * tq + lax.broadcasted_iota(jnp.int32, (tk, tq), 1)
    causal = kpos <= qpos
    ones = jnp.ones((ONES_ROWS, tk), BF16)

    def scores(jt, h, masked):
        s = _dot(k_ref[0, jt, :, h * dk:(h + 1) * dk], q_ref[0, 0, h * dk:(h + 1) * dk, :])
        if masked:
            s = jnp.where(causal, s, NEG)
        s_scr[h] = s
        return jnp.max(s, axis=0, keepdims=True)

    def update(jt, h, cmax, m):
        m_new = jnp.maximum(m, cmax)
        alpha = ex(m - m_new)
        p = ex(s_scr[h] - m_new).astype(BF16)
        vt = jnp.concatenate([v_ref[0, jt, h * dv:(h + 1) * dv, :], ones], axis=0)
        acc_scr[h] = acc_scr[h] * alpha + _dot(vt, p)
        return m_new

    acc_scr[...] = jnp.zeros_like(acc_scr)
    ms = [jnp.full((1, tq), NEG, F32) for _ in range(H)]
    cmax = scores(jd, 0, True)
    for h in range(H):
        nxt = scores(jd, h + 1, True) if h + 1 < H else scores(0, 0, False)
        ms[h] = update(jd, h, cmax, ms[h])
        cmax = nxt

    def tile_step(j, carry):
        ms, cmax = carry
        ms = list(ms)
        jn = jnp.minimum(j + 1, jd - 1)
        for h in range(H):
            nxt = scores(j, h + 1, False) if h + 1 < H else scores(jn, 0, False)
            ms[h] = update(j, h, cmax, ms[h])
            cmax = nxt
        return tuple(ms), cmax

    def unrolled(t, c):
        for u in range(FLASH_UNROLL):
            c = tile_step(FLASH_UNROLL * t + u, c)
        return c

    carry = lax.fori_loop(0, jd // FLASH_UNROLL, unrolled, (tuple(ms), cmax))
    lax.fori_loop(jd - jd % FLASH_UNROLL, jd, tile_step, carry)
    outs = []
    for h in range(H):
        acc = acc_scr[h]
        outs.append(acc[:dv] / acc[dv:dv + 1])
    o_ref[...] = jnp.concatenate(outs, axis=0).T.astype(BF16)


def _flash(qT, k, vT, heads, use_exp2, name):
    B, nq, hdk, tq = qT.shape
    nk, tk = k.shape[1:3]
    hdv = vT.shape[2]
    dva = hdv // heads + ONES_ROWS
    whole = lambda shp: pl.BlockSpec((1,) + shp, lambda b, i: (b, 0, 0, 0),
                                     pipeline_mode=pl.Buffered(1))
    return pl.pallas_call(
        functools.partial(_flash_kernel, heads=heads, use_exp2=use_exp2),
        grid=(B, nq),
        in_specs=[pl.BlockSpec((1, 1, hdk, tq), lambda b, i: (b, i, 0, 0)),
                  whole((nk, tk, hdk)), whole((nk, hdv, tk))],
        out_specs=pl.BlockSpec((tq, hdv), lambda b, i: (b * nq + i, 0)),
        out_shape=jax.ShapeDtypeStruct((B * nq * tq, hdv), BF16),
        scratch_shapes=[pltpu.VMEM((heads, tk, tq), F32), pltpu.VMEM((heads, dva, tq), F32)],
        compiler_params=_cparams(("arbitrary", "arbitrary")), name=name,
    )(qT, k, vT)


def _merge_kernel(x_ref, ya_ref, yb_ref, yc_ref, yd_ref, g_ref, wg_ref, wb_ref, wo_ref,
                  fg_ref, rw_ref, rb_ref, x1_ref, h2_ref, rt_ref, cnt_ref, h2_scr):
    @pl.when(pl.program_id(0) == 0)
    def _():
        h2_scr[...] = jnp.zeros_like(h2_scr)

    router = _router_stages(h2_scr, rw_ref, rb_ref, rt_ref, cnt_ref)
    x = x_ref[...]
    h = _rms(x, g_ref[...]).astype(BF16)
    merged = None
    for bi, y_ref in enumerate((ya_ref, yb_ref, yc_ref, yd_ref)):
        t = jax.nn.sigmoid(_dot(h, wg_ref[bi])) * _dot(y_ref[...], wb_ref[bi])
        merged = t if merged is None else merged + t
        next(router)
    x1 = x + _dot(merged.astype(BF16), wo_ref[...])
    x1_ref[...] = x1
    h2_new = _rms(x1, fg_ref[...])
    h2_ref[...] = h2_new.astype(BF16)
    for _ in router:
        pass
    h2_scr[...] = h2_new


def _router_stages(h2_scr, rw_ref, rb_ref, rt_ref, cnt_ref):
    h2 = h2_scr[...]
    nt = (((1,), (1,)), ((), ()))
    h2_hi = h2.astype(BF16)
    h2_lo = (h2 - h2_hi.astype(F32)).astype(BF16)
    rw = rw_ref[...]
    rw_hi = rw.astype(BF16)
    rw_lo = (rw - rw_hi.astype(F32)).astype(BF16)
    logits = (lax.dot_general(rw_hi, h2_hi, nt, preferred_element_type=F32)
              + lax.dot_general(rw_hi, h2_lo, nt, preferred_element_type=F32)
              + lax.dot_general(rw_lo, h2_hi, nt, preferred_element_type=F32))
    yield
    scores = jax.nn.sigmoid(logits)
    biased = scores + rb_ref[...]
    s_rows = [scores[e:e + 1, :] for e in range(N_EXPERTS)]
    b_rows = [biased[e:e + 1, :] for e in range(N_EXPERTS)]

    def top2(vals):
        m1 = vals[0]
        i1 = jnp.zeros_like(m1, dtype=jnp.int32)
        for j in range(1, len(vals)):
            better = vals[j] > m1
            i1 = jnp.where(better, j, i1)
            m1 = jnp.where(better, vals[j], m1)
        m2 = None
        i2 = None
        for j in range(len(vals)):
            cand = jnp.where(i1 == j, NEG, vals[j])
            if m2 is None:
                m2, i2 = cand, jnp.zeros_like(i1)
            else:
                better = cand > m2
                i2 = jnp.where(better, j, i2)
                m2 = jnp.where(better, cand, m2)
        return m1, i1, m2, i2

    best = None
    for gi in range(N_GROUPS):
        bm1, bi1, bm2, bi2 = top2(b_rows[gi * EPG:(gi + 1) * EPG])
        gs = bm1 + bm2
        e1 = bi1 + gi * EPG
        e2 = bi2 + gi * EPG
        if best is None:
            best = (gs, e1, e2)
        else:
            better = gs > best[0]
            best = (jnp.where(better, gs, best[0]), jnp.where(better, e1, best[1]),
                    jnp.where(better, e2, best[2]))
        if gi % 2 == 1:
            yield
    _, e1, e2 = best
    w1 = jnp.zeros_like(s_rows[0])
    w2 = jnp.zeros_like(s_rows[0])
    for e in range(N_EXPERTS):
        w1 = jnp.where(e1 == e, s_rows[e], w1)
        w2 = jnp.where(e2 == e, s_rows[e], w2)
    tot = w1 + w2
    w1 = w1 / tot
    w2 = w2 / tot
    r8 = lax.broadcasted_iota(jnp.int32, (8, scores.shape[1]), 0)
    rt_ref[...] = jnp.where(r8 == 0, e1.astype(F32), jnp.where(r8 == 1, e2.astype(F32),
                            jnp.where(r8 == 2, w1, jnp.where(r8 == 3, w2, 0.0))))
    row = lax.broadcasted_iota(jnp.int32, scores.shape, 0)
    assigned = jnp.where(row == e1, 1.0, jnp.where(row == e2, 1.0, 0.0))
    cnt = jnp.sum(assigned, axis=1, keepdims=True)
    cnt_ref[0] = jnp.broadcast_to(cnt, (N_EXPERTS, LANES)).astype(jnp.int32)
    yield


def _merge(x2, ya, yb, yc, yd, lw, router_w_t, router_b):
    T, D = x2.shape
    tm = TM_MERGE
    n = T // tm
    tok = lambda w: pl.BlockSpec((tm, w), lambda i: (jnp.minimum(i, n - 1), 0))
    consts = [lw["attn_norm"], lw["w_gate"], lw["w_branch"], lw["w_out"], lw["ffn_norm"],
              router_w_t, router_b]
    return pl.pallas_call(
        _merge_kernel, grid=(n + 1,),
        in_specs=[tok(D), tok(256), tok(256), tok(256), tok(256)] + [_full(a.shape) for a in consts],
        out_specs=[tok(D), tok(D), pl.BlockSpec((8, tm), lambda i: (0, jnp.maximum(i - 1, 0))),
                   pl.BlockSpec((1, N_EXPERTS, LANES), lambda i: (jnp.maximum(i - 1, 0), 0, 0))],
        out_shape=[jax.ShapeDtypeStruct((T, D), F32), jax.ShapeDtypeStruct((T, D), BF16),
                   jax.ShapeDtypeStruct((8, T), F32),
                   jax.ShapeDtypeStruct((n, N_EXPERTS, LANES), jnp.int32)],
        scratch_shapes=[pltpu.VMEM((tm, D), F32)],
        compiler_params=_cparams(("arbitrary",)), name="merge",
    )(x2, ya, yb, yc, yd, *consts)


def _moe_kernel(cnt_ref, x1_ref, h2_ref, rt_ref, tri_ref, sg_ref, su_ref, sd_ref,
                wg_ref, wu_ref, wd_ref, o_ref, rank_scr, gate_scr):
    i = pl.program_id(0)
    e = pl.program_id(1)
    tm = h2_ref.shape[0]

    def ffn(rows, scale, wg, wu, wd):
        a = jax.nn.silu(_dot(rows, wg)) * _dot(rows, wu)
        if scale is not None:
            a = a * scale
        return _dot(a.astype(BF16), wd)

    @pl.when(e == 0)
    def _():
        o_ref[...] = x1_ref[...] + ffn(h2_ref[...], None, sg_ref[...], su_ref[...], sd_ref[...])
        row = lax.broadcasted_iota(jnp.int32, (N_EXPERTS, tm), 0).astype(F32)
        e1, e2 = rt_ref[0:1, :], rt_ref[1:2, :]
        w1, w2 = rt_ref[2:3, :], rt_ref[3:4, :]
        assigned = jnp.where(row == e1, 1.0, jnp.where(row == e2, 1.0, 0.0))
        before = _dot(assigned.astype(BF16), tri_ref[...])
        rank_scr[...] = jnp.where(assigned > 0.0, before, -1.0)
        gate_scr[...] = jnp.where(row == e1, w1, jnp.where(row == e2, w2, 0.0))

    def expert(sub):
        ex = (e - 1) * MOE_EXPERTS_PER_STEP + sub
        cnt = cnt_ref[i * N_EXPERTS + ex]
        rank_row = rank_scr[pl.ds(ex, 1), :]
        gate_row = gate_scr[pl.ds(ex, 1), :]
        slot = lax.broadcasted_iota(jnp.int32, (MOE_CHUNK, tm), 0).astype(F32)

        def chunk(c, carry):
            hit = rank_row == slot + (c * MOE_CHUNK).astype(F32)
            onehot = jnp.where(hit, 1.0, 0.0).astype(BF16)
            rows = _dot(onehot, h2_ref[...]).astype(BF16)
            scale = jnp.sum(jnp.where(hit, gate_row, 0.0), axis=1, keepdims=True)
            y = ffn(rows, scale, wg_ref[0, sub], wu_ref[0, sub], wd_ref[0, sub]).astype(BF16)
            o_ref[...] += lax.dot_general(onehot, y, (((0,), (0,)), ((), ())),
                                          preferred_element_type=F32)
            return carry

        lax.fori_loop(0, (cnt + MOE_CHUNK - 1) // MOE_CHUNK, chunk, 0)

    @pl.when(e > 0)
    def _():
        for sub in range(MOE_EXPERTS_PER_STEP):
            expert(sub)


def _moe(x1, h2, route, counts, tri, shared, layer, wg, wu, wd):
    T, D = x1.shape
    tm = tri.shape[0]
    _, ne, _, ff = wg.shape
    const = lambda shp: pl.BlockSpec(shp, lambda i, e, c: (0, 0), pipeline_mode=pl.Buffered(1))
    eps = MOE_EXPERTS_PER_STEP
    expert = lambda shp: pl.BlockSpec((1, eps) + shp,
                                      lambda i, e, c: (layer, jnp.maximum(e - 1, 0), 0, 0))
    grid_spec = pltpu.PrefetchScalarGridSpec(
        num_scalar_prefetch=1, grid=(T // tm, ne // eps + 1),
        in_specs=[pl.BlockSpec((tm, D), lambda i, e, c: (i, 0)),
                  pl.BlockSpec((tm, D), lambda i, e, c: (i, 0)),
                  pl.BlockSpec((8, tm), lambda i, e, c: (0, i)),
                  const((tm, tm)), const((D, ff)), const((D, ff)), const((ff, D)),
                  expert((D, ff)), expert((D, ff)), expert((ff, D))],
        out_specs=pl.BlockSpec((tm, D), lambda i, e, c: (i, 0)),
        scratch_shapes=[pltpu.VMEM((N_EXPERTS, tm), F32), pltpu.VMEM((N_EXPERTS, tm), F32)])
    return pl.pallas_call(
        _moe_kernel, grid_spec=grid_spec,
        out_shape=jax.ShapeDtypeStruct((T, D), F32),
        compiler_params=_cparams(("arbitrary", "arbitrary")), name="moe",
    )(counts, x1, h2, route, tri, *shared, wg, wu, wd)


def _constants(S):
    half = MLA_ROPE // 2
    inv = ROPE_THETA ** (-np.arange(half, dtype=np.float64) / half)
    ang = np.arange(S, dtype=np.float64)[:, None] * inv[None, :]
    cos = np.ones((S, LANES), np.float32)
    cos[:, MLA_NOPE:MLA_NOPE + half] = np.cos(ang)
    cos[:, MLA_NOPE + half:MLA_QK] = np.cos(ang)
    sin = np.zeros((S, LANES), np.float32)
    sin[:, MLA_NOPE:MLA_NOPE + half] = np.sin(ang)
    sin[:, MLA_NOPE + half:MLA_QK] = np.sin(ang)

    r = np.zeros((LANES, LANES), np.float32)
    for j in range(half):
        r[MLA_NOPE + half + j, MLA_NOPE + j] = -1.0
        r[MLA_NOPE + j, MLA_NOPE + half + j] = 1.0
    g = np.zeros((LANES, LANES), np.float32)
    g[:MLA_QK, :] = 1.0
    eye4 = np.eye(MLA_HEADS, dtype=np.float32)
    pk = np.zeros((LANES, 4 * LANES), np.float32)
    for hh in range(MLA_HEADS):
        for j in range(MLA_ROPE):
            pk[j, hh * LANES + MLA_NOPE + j] = 1.0
    g256 = np.kron(np.eye(MOBA_HEADS, dtype=np.float32), np.ones((MOBA_HD, MOBA_HD), np.float32))

    nkb = S // MOBA_BLOCK
    kpos = np.arange(S)
    kc = np.zeros((S, MOBA_HD), np.float32)
    blk = kpos // MOBA_BLOCK
    m = blk < min(nkb, MOBA_HD) - 1
    kc[kpos[m], blk[m]] = 1.0
    kc[:, MOBA_HD - 1] = kpos % MOBA_BLOCK
    kc4 = np.zeros((S, MOBA_HEADS, 2 * MOBA_HD), np.float32)
    kc4[:, :, MOBA_HD:] = kc[:, None, :]
    place = np.zeros((MOBA_HEADS * MOBA_HD, MOBA_HEADS * 2 * MOBA_HD), np.float32)
    for hh in range(MOBA_HEADS):
        for j in range(MOBA_HD):
            place[hh * MOBA_HD + j, hh * 2 * MOBA_HD + j] = 1.0
    return {
        "cos": jnp.asarray(cos), "sin": jnp.asarray(sin),
        "r512": jnp.asarray(np.kron(eye4, r), BF16), "g512": jnp.asarray(np.kron(eye4, g), BF16),
        "pk": jnp.asarray(pk, BF16), "g256": jnp.asarray(g256, BF16),
        "kc": jnp.asarray(kc4.reshape(S, -1), BF16), "place": jnp.asarray(place, BF16),
    }


def _head_pad(v):
    one = jnp.concatenate([v, jnp.zeros((LANES - MLA_QK,), F32)])
    return jnp.tile(one, MLA_HEADS)[None, :]


def _layer_weights(l, p):
    D = p["w_in"].shape[1]
    w_in = p["w_in"][l]
    o1 = MLA_Q_LORA + MLA_KV_LORA + MLA_ROPE
    o2 = o1 + 2 * SGU_W
    o3 = o2 + POOL_W
    w = jnp.concatenate([w_in[:, :o1], jnp.zeros((D, C_SGU - o1), F32), w_in[:, o1:]], axis=1)

    wq = p["mla_w_q_up"][l].reshape(MLA_Q_LORA, MLA_HEADS, MLA_QK)
    wq = jnp.pad(wq, ((0, 0), (0, 0), (0, LANES - MLA_QK))).reshape(MLA_Q_LORA, MLA_HEADS * LANES)
    wkv = p["mla_w_kv_up"][l].reshape(MLA_KV_LORA, MLA_HEADS, MLA_NOPE + MLA_V)
    wk = jnp.pad(wkv[:, :, :MLA_NOPE], ((0, 0), (0, 0), (0, LANES - MLA_NOPE)))
    wk = wk.reshape(MLA_KV_LORA, MLA_HEADS * LANES)
    wv = wkv[:, :, MLA_NOPE:].reshape(MLA_KV_LORA, MLA_HEADS * MLA_V)

    tri = jnp.tril(jnp.ones((SGU_CHUNK, SGU_CHUNK), dtype=bool))
    sgu_w = jnp.where(tri[None], p["sgu_w"][l], 0.0)
    sgu_b = jnp.repeat(p["sgu_b"][l].T, SGU_W // SGU_GROUPS, axis=1)
    pool_w = jax.scipy.linalg.block_diag(*[p["pool_w"][l, gi] for gi in range(len(POOL_WINDOWS))])

    ne = p["moe_w_gate"].shape[1]
    return {
        "attn_norm": p["attn_norm"][l][None, :], "w_in": w.astype(BF16),
        "mla_q_norm": p["mla_q_norm"][l][None, :], "wq": wq.astype(BF16),
        "mla_kv_norm": p["mla_kv_norm"][l][None, :],
        "wkv": jnp.concatenate([wk, wv], axis=1).astype(BF16),
        "mla_qg": _head_pad(p["mla_q_gain"][l]), "mla_kg": _head_pad(p["mla_k_gain"][l]),
        "sgu_norm": p["sgu_norm"][l][None, :], "sgu_w": sgu_w.astype(BF16), "sgu_b": sgu_b,
        "pool_w": pool_w.astype(BF16), "pool_b": p["pool_b"][l][None, :],
        "pool_scale": p["pool_scale"][l][None, :],
        "moba_qg": jnp.tile(p["moba_q_gain"][l], MOBA_HEADS)[None, :],
        "moba_kg": jnp.tile(p["moba_k_gain"][l], MOBA_HEADS)[None, :],
        "w_gate": p["w_gate"][l].astype(BF16), "w_branch": p["w_branch"][l].astype(BF16),
        "w_out": p["w_out"][l].astype(BF16), "ffn_norm": p["ffn_norm"][l][None, :],
        "shared": (p["shared_w_gate"][l].astype(BF16), p["shared_w_up"][l].astype(BF16),
                   p["shared_w_down"][l].astype(BF16)),
    }


def kernel(x, attn_norm, w_in, mla_q_norm, mla_w_q_up, mla_kv_norm, mla_w_kv_up, mla_q_gain, mla_k_gain, sgu_norm, sgu_w, sgu_b, pool_w, pool_b, pool_scale, moba_q_gain, moba_k_gain, w_gate, w_branch, w_out, ffn_norm, router_w, router_bias, moe_w_gate, moe_w_up, moe_w_down, shared_w_gate, shared_w_up, shared_w_down):
    p = dict(attn_norm=attn_norm, w_in=w_in, mla_q_norm=mla_q_norm, mla_w_q_up=mla_w_q_up,
             mla_kv_norm=mla_kv_norm, mla_w_kv_up=mla_w_kv_up, mla_q_gain=mla_q_gain,
             mla_k_gain=mla_k_gain, sgu_norm=sgu_norm, sgu_w=sgu_w, sgu_b=sgu_b, pool_w=pool_w,
             pool_b=pool_b, pool_scale=pool_scale, moba_q_gain=moba_q_gain,
             moba_k_gain=moba_k_gain, w_gate=w_gate, w_branch=w_branch, w_out=w_out,
             ffn_norm=ffn_norm, moe_w_gate=moe_w_gate, moe_w_up=moe_w_up, moe_w_down=moe_w_down,
             shared_w_gate=shared_w_gate, shared_w_up=shared_w_up, shared_w_down=shared_w_down)
    B, S, D = x.shape
    T = B * S
    depth = attn_norm.shape[0]
    nkb = S // MOBA_BLOCK
    assert TQ == TK and S % TK == 0 and T % TM_MERGE == 0 and nkb <= MOBA_HD
    consts = _constants(S)
    router_w_t = router_w.T
    router_b = router_bias[:, None]
    tm_moe = min(TM_MOE, T)
    tri = jnp.asarray(np.triu(np.ones((tm_moe, tm_moe), np.float32), 1), BF16)

    ffn_w = (moe_w_gate.astype(BF16), moe_w_up.astype(BF16), moe_w_down.astype(BF16))
    x2 = x.reshape(T, D)
    for l in range(depth):
        lw = _layer_weights(l, p)
        qa, ka, va, yb, yc, qd, kd, vd, km = _in_proj(x2, lw, consts, S)
        ya = _flash(qa, ka.reshape(B, S // TK, TK, -1), va, MLA_HEADS, True, "mla_attn")

        kmean = km.reshape(B, nkb, MOBA_HEADS, MOBA_HD).transpose(0, 2, 1, 3)
        kmean = jnp.pad(kmean, ((0, 0), (0, 0), (0, MOBA_HD - nkb), (0, 0)))
        q_aug = _moba_select(qd, kmean)
        yd = _flash(q_aug, kd.reshape(B, S // TK, TK, -1), vd, MOBA_HEADS, False, "moba_attn")

        x1, h2, route, cnt = _merge(x2, ya, yb, yc, yd, lw, router_w_t, router_b)
        counts = cnt[:, :, 0].reshape(T // tm_moe, tm_moe // TM_MERGE, N_EXPERTS).sum(axis=1)
        x2 = _moe(x1, h2, route, counts.reshape(-1), tri, lw["shared"], l, *ffn_w)
    return x2.reshape(B, S, D)
```

```python
import functools
import math

import jax
import jax.numpy as jnp
import numpy as np
from jax import lax
from jax.experimental import pallas as pl
from jax.experimental.pallas import tpu as pltpu

F32 = jnp.float32
BF16 = jnp.bfloat16
EPS = 1e-6
NEG = -1e30

LANES = 128
VMEM_LIMIT = 56 * 1024 * 1024

MLA_HEADS = 4
MLA_NOPE = 64
MLA_ROPE = 32
MLA_V = 64
MLA_Q_LORA = 256
MLA_KV_LORA = 128
ROPE_THETA = 10000.0
MLA_QK = MLA_NOPE + MLA_ROPE

SGU_GROUPS = 4
SGU_CHUNK = 128
SGU_W = 256

POOL_WINDOWS = (2, 4, 8, 16)
POOL_MAXW = 16
POOL_W = 256

MOBA_HEADS = 4
MOBA_HD = 64
MOBA_BLOCK = 256
MOBA_TOPK = 3

N_EXPERTS = 16
N_GROUPS = 4
EPG = 4

C_MLA = 0
C_SGU = 512
C_POOL = 1024
C_MOBA = 1280
C_END = 2048

TQ = 512
TK = 512
TM_IN = TQ
TM_MERGE = 512
TM_MOE = 1024
SEL_L = 2048
FLASH_UNROLL = 4
MOE_EXPERTS_PER_STEP = 4
MOE_CHUNK = 160
ONES_ROWS = 16


def _cparams(sem):
    return pltpu.CompilerParams(dimension_semantics=sem, vmem_limit_bytes=VMEM_LIMIT)


def _full(shape):
    n = len(shape)
    return pl.BlockSpec(shape, lambda *_: (0,) * n)


def _rms(x, g):
    return x * lax.rsqrt(jnp.mean(x * x, axis=-1, keepdims=True) + EPS) * g


def _dot(a, b):
    return jnp.dot(a, b, preferred_element_type=F32)


def _in_proj_kernel(x_ref, g_ref, w_ref, qn_ref, wq_ref, kvn_ref, wkv_ref, pk_ref,
                    qg_ref, kg_ref, g512_ref, r512_ref, cos_ref, sin_ref,
                    sn_ref, sw_ref, sb_ref, pw_ref, pb_ref, ps_ref,
                    mqg_ref, mkg_ref, g256_ref, pl_ref, kc_ref,
                    qa_ref, ka_ref, va_ref, yb_ref, yc_ref, qd_ref, kd_ref, vd_ref, km_ref,
                    halo_ref, buf_ref, *, tiles_per_seq, q_scale):
    tm = x_ref.shape[0]
    i = pl.program_id(0)
    x = x_ref[...]
    h = _rms(x, g_ref[...]).astype(BF16)

    cm = _dot(h, w_ref[:, C_MLA:C_SGU])
    c_q = _rms(cm[:, :MLA_Q_LORA], qn_ref[...]).astype(BF16)
    c_kv = _rms(cm[:, MLA_Q_LORA:MLA_Q_LORA + MLA_KV_LORA], kvn_ref[...]).astype(BF16)
    kpe_grp = cm[:, MLA_Q_LORA + MLA_KV_LORA:].astype(BF16)
    q = _dot(c_q, wq_ref[...])
    kv = _dot(c_kv, wkv_ref[...])
    k = kv[:, :512] + _dot(kpe_grp, pk_ref[...])
    va_ref[0, 0] = kv[:, 512:].T.astype(BF16)
    cos = jnp.concatenate([cos_ref[...]] * MLA_HEADS, axis=1)
    sin = jnp.concatenate([sin_ref[...]] * MLA_HEADS, axis=1)
    inv_d = 1.0 / MLA_QK

    def head_norm_rope(t, gain):
        ms = _dot((t * t).astype(BF16), g512_ref[...]) * inv_d
        t = t * lax.rsqrt(ms + EPS) * gain
        rot = _dot(t.astype(BF16), r512_ref[...])
        return t * cos + rot * sin

    qa_ref[0, 0] = (head_norm_rope(q, qg_ref[...]) * q_scale).T.astype(BF16)
    ka_ref[...] = head_norm_rope(k, kg_ref[...]).astype(BF16)

    z = jax.nn.gelu(_dot(h, w_ref[:, C_SGU:C_POOL]))
    u = z[:, :SGU_W]
    vv = _rms(z[:, SGU_W:], sn_ref[...]).astype(BF16)
    lane_grp = lax.broadcasted_iota(jnp.int32, (SGU_CHUNK, SGU_W), 1) // (SGU_W // SGU_GROUPS)
    for c in range(tm // SGU_CHUNK):
        vc = vv[c * SGU_CHUNK:(c + 1) * SGU_CHUNK, :]
        y = sb_ref[...]
        for gi in range(SGU_GROUPS):
            y = y + jnp.where(lane_grp == gi, _dot(sw_ref[gi], vc), 0.0)
        yb_ref[c * SGU_CHUNK:(c + 1) * SGU_CHUNK, :] = (
            u[c * SGU_CHUNK:(c + 1) * SGU_CHUNK, :] * y).astype(BF16)

    xp = _dot(h, w_ref[:, C_POOL:C_MOBA])
    cd = _dot(h, w_ref[:, C_MOBA:C_END])

    @pl.when(i % tiles_per_seq == 0)
    def _():
        halo_ref[...] = jnp.zeros_like(halo_ref)

    buf_ref[0:POOL_MAXW, :] = halo_ref[...]
    buf_ref[POOL_MAXW:, :] = xp
    halo_ref[...] = xp[tm - POOL_MAXW:, :]
    pos = (i % tiles_per_seq) * tm + lax.broadcasted_iota(jnp.int32, (tm, POOL_W), 0)
    lane_w = lax.broadcasted_iota(jnp.int32, (tm, POOL_W), 1) // (POOL_W // len(POOL_WINDOWS))
    acc = xp
    pooled = jnp.zeros((tm, POOL_W), F32)
    shift = 1
    for gi, w in enumerate(POOL_WINDOWS):
        while shift < w:
            acc = acc + buf_ref[POOL_MAXW - shift:POOL_MAXW - shift + tm, :]
            shift += 1
        cnt = jnp.minimum(pos + 1, w).astype(F32)
        pooled = jnp.where(lane_w == gi, acc / cnt, pooled)
    p = (pooled - xp).astype(BF16)
    yc_ref[...] = ((_dot(p, pw_ref[...]) + pb_ref[...]) * ps_ref[...]).astype(BF16)

    inv_hd = 1.0 / MOBA_HD

    def head_norm(t, gain):
        ms = _dot((t * t).astype(BF16), g256_ref[...]) * inv_hd
        return t * lax.rsqrt(ms + EPS) * gain

    qd_ref[0] = head_norm(cd[:, :256], mqg_ref[...]).T
    kd = head_norm(cd[:, 256:512], mkg_ref[...])
    kd_ref[...] = (_dot(kd.astype(BF16), pl_ref[...]) + kc_ref[...]).astype(BF16)
    vd_ref[0, 0] = cd[:, 512:].T.astype(BF16)
    for c in range(tm // MOBA_BLOCK):
        km_ref[c] = jnp.mean(kd[c * MOBA_BLOCK:(c + 1) * MOBA_BLOCK, :], axis=0, keepdims=True)


def _in_proj(x2, lw, consts, S):
    T, D = x2.shape
    tm = TM_IN
    nt = T // tm
    tok = lambda w: pl.BlockSpec((tm, w), lambda i: (i, 0))
    pos_spec = pl.BlockSpec((tm, LANES), lambda i: (i % (S // tm), 0))
    ins = [
        (x2, tok(D)), (lw["attn_norm"], None), (lw["w_in"], None),
        (lw["mla_q_norm"], None), (lw["wq"], None), (lw["mla_kv_norm"], None), (lw["wkv"], None),
        (consts["pk"], None), (lw["mla_qg"], None), (lw["mla_kg"], None),
        (consts["g512"], None), (consts["r512"], None),
        (consts["cos"], pos_spec), (consts["sin"], pos_spec),
        (lw["sgu_norm"], None), (lw["sgu_w"], None), (lw["sgu_b"], None),
        (lw["pool_w"], None), (lw["pool_b"], None), (lw["pool_scale"], None),
        (lw["moba_qg"], None), (lw["moba_kg"], None), (consts["g256"], None),
        (consts["place"], None),
        (consts["kc"], pl.BlockSpec((tm, 4 * LANES), lambda i: (i % (S // tm), 0))),
    ]
    arrays = [a for a, _ in ins]
    specs = [s if s is not None else _full(a.shape) for a, s in ins]
    nb = tm // MOBA_BLOCK
    B = T // S
    ns = S // tm
    tr = lambda f: pl.BlockSpec((1, 1, f, tm), lambda i: (i // ns, i % ns, 0, 0))
    out_shape = [
        jax.ShapeDtypeStruct((B, ns, 512, tm), BF16), jax.ShapeDtypeStruct((T, 512), BF16),
        jax.ShapeDtypeStruct((B, ns, 256, tm), BF16), jax.ShapeDtypeStruct((T, 256), BF16),
        jax.ShapeDtypeStruct((T, 256), BF16), jax.ShapeDtypeStruct((B, 256, S), F32),
        jax.ShapeDtypeStruct((T, 512), BF16), jax.ShapeDtypeStruct((B, ns, 256, tm), BF16),
        jax.ShapeDtypeStruct((T // MOBA_BLOCK, 1, 256), F32),
    ]
    out_specs = [tr(512), tok(512), tr(256), tok(256), tok(256),
                 pl.BlockSpec((1, 256, tm), lambda i: (i // ns, 0, i % ns)),
                 tok(512), tr(256), pl.BlockSpec((nb, 1, 256), lambda i: (i, 0, 0))]
    q_scale = (MLA_QK ** -0.5) * math.log2(math.e)
    return pl.pallas_call(
        functools.partial(_in_proj_kernel, tiles_per_seq=S // tm, q_scale=q_scale),
        grid=(nt,), in_specs=specs, out_specs=out_specs, out_shape=out_shape,
        scratch_shapes=[pltpu.VMEM((POOL_MAXW, POOL_W), F32),
                        pltpu.VMEM((tm + POOL_MAXW, POOL_W), F32)],
        compiler_params=_cparams(("arbitrary",)), name="in_proj",
    )(*arrays)


def _moba_select_kernel(q_ref, km_ref, o_ref, *, slope_log2_step):
    hh = pl.program_id(1)
    li = pl.program_id(2)
    q = q_ref[0]
    L = q.shape[1]
    nrow = km_ref.shape[2]
    g = jnp.dot(km_ref[0, 0], q, preferred_element_type=F32, precision=lax.Precision.HIGHEST)
    n_i = lax.broadcasted_iota(jnp.int32, (nrow, L), 0)
    own = (li * L + lax.broadcasted_iota(jnp.int32, (nrow, L), 1)) // MOBA_BLOCK
    g = jnp.where(n_i < own, g, NEG)
    sel = jnp.zeros((nrow, L), F32)
    for _ in range(MOBA_TOPK):
        mx = jnp.max(g, axis=0, keepdims=True)
        first = jnp.min(jnp.where(g == mx, n_i, nrow), axis=0, keepdims=True)
        hit = n_i == first
        sel = jnp.where(hit & (mx > 0.5 * NEG), 1.0, sel)
        g = jnp.where(hit, NEG, g)
    slope = jnp.exp2(-slope_log2_step * (hh + 1).astype(F32))
    rel = (n_i - own).astype(F32) * (slope * MOBA_BLOCK)
    bias = jnp.where((sel > 0.0) | (n_i == own), rel, NEG)
    bias = jnp.where(n_i == nrow - 1, slope, bias)
    qs = (q * (MOBA_HD ** -0.5)).astype(BF16)
    bias = bias.astype(BF16)
    tq = o_ref.shape[3]
    for t in range(L // tq):
        o_ref[0, t, 0:MOBA_HD, :] = qs[:, t * tq:(t + 1) * tq]
        o_ref[0, t, MOBA_HD:, :] = bias[:, t * tq:(t + 1) * tq]


def _moba_select(qgT, kmean):
    B, _, S = qgT.shape
    H, hd = MOBA_HEADS, MOBA_HD
    L = min(SEL_L, S)
    return pl.pallas_call(
        functools.partial(_moba_select_kernel, slope_log2_step=8.0 / MOBA_HEADS),
        grid=(B, H, S // L),
        in_specs=[pl.BlockSpec((1, hd, L), lambda b, h, l: (b, h, l)),
                  pl.BlockSpec((1, 1, 64, hd), lambda b, h, l: (b, h, 0, 0))],
        out_specs=pl.BlockSpec((1, L // TQ, 2 * hd, TQ), lambda b, h, l: (b, l, h, 0)),
        out_shape=jax.ShapeDtypeStruct((B, S // TQ, H * 2 * hd, TQ), BF16),
        compiler_params=_cparams(("arbitrary",) * 3), name="moba_select",
    )(qgT, kmean)


def _flash_kernel(q_ref, k_ref, v_ref, o_ref, s_scr, acc_scr, *, heads, use_exp2):
    H = heads
    tq = q_ref.shape[3]
    tk = k_ref.shape[2]
    dk = q_ref.shape[2] // H
    dv = v_ref.shape[2] // H
    ratio = tk // tq
    ex = jnp.exp2 if use_exp2 else jnp.exp
    i = pl.program_id(1)
    jd = i // ratio
    kpos = jd * tk + lax.broadcasted_iota(jnp.int32, (tk, tq), 0)
    qpos = i * tq + lax.broadcasted_iota(jnp.int32, (tk, tq), 1)
    causal = kpos <= qpos
    ones = jnp.ones((ONES_ROWS, tk), BF16)

    def scores(jt, h, masked):
        s = _dot(k_ref[0, jt, :, h * dk:(h + 1) * dk], q_ref[0, 0, h * dk:(h + 1) * dk, :])
        if masked:
            s = jnp.where(causal, s, NEG)
        s_scr[h] = s
        return jnp.max(s, axis=0, keepdims=True)

    def update(jt, h, cmax, m):
        m_new = jnp.maximum(m, cmax)
        alpha = ex(m - m_new)
        p = ex(s_scr[h] - m_new).astype(BF16)
        vt = jnp.concatenate([v_ref[0, jt, h * dv:(h + 1) * dv, :], ones], axis=0)
        acc_scr[h] = acc_scr[h] * alpha + _dot(vt, p)
        return m_new

    acc_scr[...] = jnp.zeros_like(acc_scr)
    ms = [jnp.full((1, tq), NEG, F32) for _ in range(H)]
    cmax = scores(jd, 0, True)
    for h in range(H):
        nxt = scores(jd, h + 1, True) if h + 1 < H else scores(0, 0, False)
        ms[h] = update(jd, h, cmax, ms[h])
        cmax = nxt

    def tile_step(j, carry):
        ms, cmax = carry
        ms = list(ms)
        jn = jnp.minimum(j + 1, jd - 1)
        for h in range(H):
            nxt = scores(j, h + 1, False) if h + 1 < H else scores(jn, 0, False)
            ms[h] = update(j, h, cmax, ms[h])
            cmax = nxt
        return tuple(ms), cmax

    def unrolled(t, c):
        for u in range(FLASH_UNROLL):
            c = tile_step(FLASH_UNROLL * t + u, c)
        return c

    carry = lax.fori_loop(0, jd // FLASH_UNROLL, unrolled, (tuple(ms), cmax))
    lax.fori_loop(jd - jd % FLASH_UNROLL, jd, tile_step, carry)
    outs = []
    for h in range(H):
        acc = acc_scr[h]
        outs.append(acc[:dv] / acc[dv:dv + 1])
    o_ref[...] = jnp.concatenate(outs, axis=0).T.astype(BF16)


def _flash(qT, k, vT, heads, use_exp2, name):
    B, nq, hdk, tq = qT.shape
    nk, tk = k.shape[1:3]
    hdv = vT.shape[2]
    dva = hdv // heads + ONES_ROWS
    whole = lambda shp: pl.BlockSpec((1,) + shp, lambda b, i: (b, 0, 0, 0),
                                     pipeline_mode=pl.Buffered(1))
    return pl.pallas_call(
        functools.partial(_flash_kernel, heads=heads, use_exp2=use_exp2),
        grid=(B, nq),
        in_specs=[pl.BlockSpec((1, 1, hdk, tq), lambda b, i: (b, i, 0, 0)),
                  whole((nk, tk, hdk)), whole((nk, hdv, tk))],
        out_specs=pl.BlockSpec((tq, hdv), lambda b, i: (b * nq + i, 0)),
        out_shape=jax.ShapeDtypeStruct((B * nq * tq, hdv), BF16),
        scratch_shapes=[pltpu.VMEM((heads, tk, tq), F32), pltpu.VMEM((heads, dva, tq), F32)],
        compiler_params=_cparams(("arbitrary", "arbitrary")), name=name,
    )(qT, k, vT)


def _merge_kernel(x_ref, ya_ref, yb_ref, yc_ref, yd_ref, g_ref, wg_ref, wb_ref, wo_ref,
                  fg_ref, rw_ref, rb_ref, x1_ref, h2_ref, rt_ref, cnt_ref, h2_scr):
    @pl.when(pl.program_id(0) == 0)
    def _():
        h2_scr[...] = jnp.zeros_like(h2_scr)

    router = _router_stages(h2_scr, rw_ref, rb_ref, rt_ref, cnt_ref)
    x = x_ref[...]
    h = _rms(x, g_ref[...]).astype(BF16)
    merged = None
    for bi, y_ref in enumerate((ya_ref, yb_ref, yc_ref, yd_ref)):
        t = jax.nn.sigmoid(_dot(h, wg_ref[bi])) * _dot(y_ref[...], wb_ref[bi])
        merged = t if merged is None else merged + t
        next(router)
    x1 = x + _dot(merged.astype(BF16), wo_ref[...])
    x1_ref[...] = x1
    h2_new = _rms(x1, fg_ref[...])
    h2_ref[...] = h2_new.astype(BF16)
    for _ in router:
        pass
    h2_scr[...] = h2_new


def _router_stages(h2_scr, rw_ref, rb_ref, rt_ref, cnt_ref):
    h2 = h2_scr[...]
    nt = (((1,), (1,)), ((), ()))
    h2_hi = h2.astype(BF16)
    h2_lo = (h2 - h2_hi.astype(F32)).astype(BF16)
    rw = rw_ref[...]
    rw_hi = rw.astype(BF16)
    rw_lo = (rw - rw_hi.astype(F32)).astype(BF16)
    logits = (lax.dot_general(rw_hi, h2_hi, nt, preferred_element_type=F32)
              + lax.dot_general(rw_hi, h2_lo, nt, preferred_element_type=F32)
              + lax.dot_general(rw_lo, h2_hi, nt, preferred_element_type=F32))
    yield
    scores = jax.nn.sigmoid(logits)
    biased = scores + rb_ref[...]
    s_rows = [scores[e:e + 1, :] for e in range(N_EXPERTS)]
    b_rows = [biased[e:e + 1, :] for e in range(N_EXPERTS)]

    def top2(vals):
        m1 = vals[0]
        i1 = jnp.zeros_like(m1, dtype=jnp.int32)
        for j in range(1, len(vals)):
            better = vals[j] > m1
            i1 = jnp.where(better, j, i1)
            m1 = jnp.where(better, vals[j], m1)
        m2 = None
        i2 = None
        for j in range(len(vals)):
            cand = jnp.where(i1 == j, NEG, vals[j])
            if m2 is None:
                m2, i2 = cand, jnp.zeros_like(i1)
            else:
                better = cand > m2
                i2 = jnp.where(better, j, i2)
                m2 = jnp.where(better, cand, m2)
        return m1, i1, m2, i2

    best = None
    for gi in range(N_GROUPS):
        bm1, bi1, bm2, bi2 = top2(b_rows[gi * EPG:(gi + 1) * EPG])
        gs = bm1 + bm2
        e1 = bi1 + gi * EPG
        e2 = bi2 + gi * EPG
        if best is None:
            best = (gs, e1, e2)
        else:
            better = gs > best[0]
            best = (jnp.where(better, gs, best[0]), jnp.where(better, e1, best[1]),
                    jnp.where(better, e2, best[2]))
        if gi % 2 == 1:
            yield
    _, e1, e2 = best
    w1 = jnp.zeros_like(s_rows[0])
    w2 = jnp.zeros_like(s_rows[0])
    for e in range(N_EXPERTS):
        w1 = jnp.where(e1 == e, s_rows[e], w1)
        w2 = jnp.where(e2 == e, s_rows[e], w2)
    tot = w1 + w2
    w1 = w1 / tot
    w2 = w2 / tot
    r8 = lax.broadcasted_iota(jnp.int32, (8, scores.shape[1]), 0)
    rt_ref[...] = jnp.where(r8 == 0, e1.astype(F32), jnp.where(r8 == 1, e2.astype(F32),
                            jnp.where(r8 == 2, w1, jnp.where(r8 == 3, w2, 0.0))))
    row = lax.broadcasted_iota(jnp.int32, scores.shape, 0)
    assigned = jnp.where(row == e1, 1.0, jnp.where(row == e2, 1.0, 0.0))
    cnt = jnp.sum(assigned, axis=1, keepdims=True)
    cnt_ref[0] = jnp.broadcast_to(cnt, (N_EXPERTS, LANES)).astype(jnp.int32)
    yield


def _merge(x2, ya, yb, yc, yd, lw, router_w_t, router_b):
    T, D = x2.shape
    tm = TM_MERGE
    n = T // tm
    tok = lambda w: pl.BlockSpec((tm, w), lambda i: (jnp.minimum(i, n - 1), 0))
    consts = [lw["attn_norm"], lw["w_gate"], lw["w_branch"], lw["w_out"], lw["ffn_norm"],
              router_w_t, router_b]
    return pl.pallas_call(
        _merge_kernel, grid=(n + 1,),
        in_specs=[tok(D), tok(256), tok(256), tok(256), tok(256)] + [_full(a.shape) for a in consts],
        out_specs=[tok(D), tok(D), pl.BlockSpec((8, tm), lambda i: (0, jnp.maximum(i - 1, 0))),
                   pl.BlockSpec((1, N_EXPERTS, LANES), lambda i: (jnp.maximum(i - 1, 0), 0, 0))],
        out_shape=[jax.ShapeDtypeStruct((T, D), F32), jax.ShapeDtypeStruct((T, D), BF16),
                   jax.ShapeDtypeStruct((8, T), F32),
                   jax.ShapeDtypeStruct((n, N_EXPERTS, LANES), jnp.int32)],
        scratch_shapes=[pltpu.VMEM((tm, D), F32)],
        compiler_params=_cparams(("arbitrary",)), name="merge",
    )(x2, ya, yb, yc, yd, *consts)


def _moe_kernel(cnt_ref, x1_ref, h2_ref, rt_ref, tri_ref, sg_ref, su_ref, sd_ref,
                wg_ref, wu_ref, wd_ref, o_ref, rank_scr, gate_scr):
    i = pl.program_id(0)
    e = pl.program_id(1)
    tm = h2_ref.shape[0]

    def ffn(rows, scale, wg, wu, wd):
        a = jax.nn.silu(_dot(rows, wg)) * _dot(rows, wu)
        if scale is not None:
            a = a * scale
        return _dot(a.astype(BF16), wd)

    @pl.when(e == 0)
    def _():
        o_ref[...] = x1_ref[...] + ffn(h2_ref[...], None, sg_ref[...], su_ref[...], sd_ref[...])
        row = lax.broadcasted_iota(jnp.int32, (N_EXPERTS, tm), 0).astype(F32)
        e1, e2 = rt_ref[0:1, :], rt_ref[1:2, :]
        w1, w2 = rt_ref[2:3, :], rt_ref[3:4, :]
        assigned = jnp.where(row == e1, 1.0, jnp.where(row == e2, 1.0, 0.0))
        before = _dot(assigned.astype(BF16), tri_ref[...])
        rank_scr[...] = jnp.where(assigned > 0.0, before, -1.0)
        gate_scr[...] = jnp.where(row == e1, w1, jnp.where(row == e2, w2, 0.0))

    @pl.when(e > 0)
    def _():
        subs = range(MOE_EXPERTS_PER_STEP)
        ex = [(e - 1) * MOE_EXPERTS_PER_STEP + sub for sub in subs]
        cnt = [cnt_ref[i * N_EXPERTS + x] for x in ex]
        rank_row = [rank_scr[pl.ds(x, 1), :] for x in ex]
        gate_row = [gate_scr[pl.ds(x, 1), :] for x in ex]
        slot = lax.broadcasted_iota(jnp.int32, (MOE_CHUNK, tm), 0).astype(F32)

        def chunk(c, carry):
            base = slot + (c * MOE_CHUNK).astype(F32)
            hit = [rank_row[s] == base for s in subs]
            onehot = [jnp.where(hit[s], 1.0, 0.0).astype(BF16) for s in subs]
            rows = [_dot(onehot[s], h2_ref[...]).astype(BF16) for s in subs]
            scale = [jnp.sum(jnp.where(hit[s], gate_row[s], 0.0), axis=1, keepdims=True)
                     for s in subs]
            y = [ffn(rows[s], scale[s], wg_ref[0, s], wu_ref[0, s], wd_ref[0, s]).astype(BF16)
                 for s in subs]
            o_ref[...] += lax.dot_general(jnp.concatenate(onehot, axis=0),
                                          jnp.concatenate(y, axis=0),
                                          (((0,), (0,)), ((), ())), preferred_element_type=F32)
            return carry

        most = functools.reduce(jnp.maximum, cnt)
        lax.fori_loop(0, (most + MOE_CHUNK - 1) // MOE_CHUNK, chunk, 0)


def _moe(x1, h2, route, counts, tri, shared, layer, wg, wu, wd):
    T, D = x1.shape
    tm = tri.shape[0]
    _, ne, _, ff = wg.shape
    const = lambda shp: pl.BlockSpec(shp, lambda i, e, c: (0, 0), pipeline_mode=pl.Buffered(1))
    eps = MOE_EXPERTS_PER_STEP
    expert = lambda shp: pl.BlockSpec((1, eps) + shp,
                                      lambda i, e, c: (layer, jnp.maximum(e - 1, 0), 0, 0))
    grid_spec = pltpu.PrefetchScalarGridSpec(
        num_scalar_prefetch=1, grid=(T // tm, ne // eps + 1),
        in_specs=[pl.BlockSpec((tm, D), lambda i, e, c: (i, 0)),
                  pl.BlockSpec((tm, D), lambda i, e, c: (i, 0)),
                  pl.BlockSpec((8, tm), lambda i, e, c: (0, i)),
                  const((tm, tm)), const((D, ff)), const((D, ff)), const((ff, D)),
                  expert((D, ff)), expert((D, ff)), expert((ff, D))],
        out_specs=pl.BlockSpec((tm, D), lambda i, e, c: (i, 0)),
        scratch_shapes=[pltpu.VMEM((N_EXPERTS, tm), F32), pltpu.VMEM((N_EXPERTS, tm), F32)])
    return pl.pallas_call(
        _moe_kernel, grid_spec=grid_spec,
        out_shape=jax.ShapeDtypeStruct((T, D), F32),
        compiler_params=_cparams(("arbitrary", "arbitrary")), name="moe",
    )(counts, x1, h2, route, tri, *shared, wg, wu, wd)


def _constants(S):
    half = MLA_ROPE // 2
    inv = ROPE_THETA ** (-np.arange(half, dtype=np.float64) / half)
    ang = np.arange(S, dtype=np.float64)[:, None] * inv[None, :]
    cos = np.ones((S, LANES), np.float32)
    cos[:, MLA_NOPE:MLA_NOPE + half] = np.cos(ang)
    cos[:, MLA_NOPE + half:MLA_QK] = np.cos(ang)
    sin = np.zeros((S, LANES), np.float32)
    sin[:, MLA_NOPE:MLA_NOPE + half] = np.sin(ang)
    sin[:, MLA_NOPE + half:MLA_QK] = np.sin(ang)

    r = np.zeros((LANES, LANES), np.float32)
    for j in range(half):
        r[MLA_NOPE + half + j, MLA_NOPE + j] = -1.0
        r[MLA_NOPE + j, MLA_NOPE + half + j] = 1.0
    g = np.zeros((LANES, LANES), np.float32)
    g[:MLA_QK, :] = 1.0
    eye4 = np.eye(MLA_HEADS, dtype=np.float32)
    pk = np.zeros((LANES, 4 * LANES), np.float32)
    for hh in range(MLA_HEADS):
        for j in range(MLA_ROPE):
            pk[j, hh * LANES + MLA_NOPE + j] = 1.0
    g256 = np.kron(np.eye(MOBA_HEADS, dtype=np.float32), np.ones((MOBA_HD, MOBA_HD), np.float32))

    nkb = S // MOBA_BLOCK
    kpos = np.arange(S)
    kc = np.zeros((S, MOBA_HD), np.float32)
    blk = kpos // MOBA_BLOCK
    m = blk < min(nkb, MOBA_HD) - 1
    kc[kpos[m], blk[m]] = 1.0
    kc[:, MOBA_HD - 1] = kpos % MOBA_BLOCK
    kc4 = np.zeros((S, MOBA_HEADS, 2 * MOBA_HD), np.float32)
    kc4[:, :, MOBA_HD:] = kc[:, None, :]
    place = np.zeros((MOBA_HEADS * MOBA_HD, MOBA_HEADS * 2 * MOBA_HD), np.float32)
    for hh in range(MOBA_HEADS):
        for j in range(MOBA_HD):
            place[hh * MOBA_HD + j, hh * 2 * MOBA_HD + j] = 1.0
    return {
        "cos": jnp.asarray(cos), "sin": jnp.asarray(sin),
        "r512": jnp.asarray(np.kron(eye4, r), BF16), "g512": jnp.asarray(np.kron(eye4, g), BF16),
        "pk": jnp.asarray(pk, BF16), "g256": jnp.asarray(g256, BF16),
        "kc": jnp.asarray(kc4.reshape(S, -1), BF16), "place": jnp.asarray(place, BF16),
    }


def _head_pad(v):
    one = jnp.concatenate([v, jnp.zeros((LANES - MLA_QK,), F32)])
    return jnp.tile(one, MLA_HEADS)[None, :]


def _layer_weights(l, p):
    D = p["w_in"].shape[1]
    w_in = p["w_in"][l]
    o1 = MLA_Q_LORA + MLA_KV_LORA + MLA_ROPE
    o2 = o1 + 2 * SGU_W
    o3 = o2 + POOL_W
    w = jnp.concatenate([w_in[:, :o1], jnp.zeros((D, C_SGU - o1), F32), w_in[:, o1:]], axis=1)

    wq = p["mla_w_q_up"][l].reshape(MLA_Q_LORA, MLA_HEADS, MLA_QK)
    wq = jnp.pad(wq, ((0, 0), (0, 0), (0, LANES - MLA_QK))).reshape(MLA_Q_LORA, MLA_HEADS * LANES)
    wkv = p["mla_w_kv_up"][l].reshape(MLA_KV_LORA, MLA_HEADS, MLA_NOPE + MLA_V)
    wk = jnp.pad(wkv[:, :, :MLA_NOPE], ((0, 0), (0, 0), (0, LANES - MLA_NOPE)))
    wk = wk.reshape(MLA_KV_LORA, MLA_HEADS * LANES)
    wv = wkv[:, :, MLA_NOPE:].reshape(MLA_KV_LORA, MLA_HEADS * MLA_V)

    tri = jnp.tril(jnp.ones((SGU_CHUNK, SGU_CHUNK), dtype=bool))
    sgu_w = jnp.where(tri[None], p["sgu_w"][l], 0.0)
    sgu_b = jnp.repeat(p["sgu_b"][l].T, SGU_W // SGU_GROUPS, axis=1)
    pool_w = jax.scipy.linalg.block_diag(*[p["pool_w"][l, gi] for gi in range(len(POOL_WINDOWS))])

    ne = p["moe_w_gate"].shape[1]
    return {
        "attn_norm": p["attn_norm"][l][None, :], "w_in": w.astype(BF16),
        "mla_q_norm": p["mla_q_norm"][l][None, :], "wq": wq.astype(BF16),
        "mla_kv_norm": p["mla_kv_norm"][l][None, :],
        "wkv": jnp.concatenate([wk, wv], axis=1).astype(BF16),
        "mla_qg": _head_pad(p["mla_q_gain"][l]), "mla_kg": _head_pad(p["mla_k_gain"][l]),
        "sgu_norm": p["sgu_norm"][l][None, :], "sgu_w": sgu_w.astype(BF16), "sgu_b": sgu_b,
        "pool_w": pool_w.astype(BF16), "pool_b": p["pool_b"][l][None, :],
        "pool_scale": p["pool_scale"][l][None, :],
        "moba_qg": jnp.tile(p["moba_q_gain"][l], MOBA_HEADS)[None, :],
        "moba_kg": jnp.tile(p["moba_k_gain"][l], MOBA_HEADS)[None, :],
        "w_gate": p["w_gate"][l].astype(BF16), "w_branch": p["w_branch"][l].astype(BF16),
        "w_out": p["w_out"][l].astype(BF16), "ffn_norm": p["ffn_norm"][l][None, :],
        "shared": (p["shared_w_gate"][l].astype(BF16), p["shared_w_up"][l].astype(BF16),
                   p["shared_w_down"][l].astype(BF16)),
    }


def kernel(x, attn_norm, w_in, mla_q_norm, mla_w_q_up, mla_kv_norm, mla_w_kv_up, mla_q_gain, mla_k_gain, sgu_norm, sgu_w, sgu_b, pool_w, pool_b, pool_scale, moba_q_gain, moba_k_gain, w_gate, w_branch, w_out, ffn_norm, router_w, router_bias, moe_w_gate, moe_w_up, moe_w_down, shared_w_gate, shared_w_up, shared_w_down):
    p = dict(attn_norm=attn_norm, w_in=w_in, mla_q_norm=mla_q_norm, mla_w_q_up=mla_w_q_up,
             mla_kv_norm=mla_kv_norm, mla_w_kv_up=mla_w_kv_up, mla_q_gain=mla_q_gain,
             mla_k_gain=mla_k_gain, sgu_norm=sgu_norm, sgu_w=sgu_w, sgu_b=sgu_b, pool_w=pool_w,
             pool_b=pool_b, pool_scale=pool_scale, moba_q_gain=moba_q_gain,
             moba_k_gain=moba_k_gain, w_gate=w_gate, w_branch=w_branch, w_out=w_out,
             ffn_norm=ffn_norm, moe_w_gate=moe_w_gate, moe_w_up=moe_w_up, moe_w_down=moe_w_down,
             shared_w_gate=shared_w_gate, shared_w_up=shared_w_up, shared_w_down=shared_w_down)
    B, S, D = x.shape
    T = B * S
    depth = attn_norm.shape[0]
    nkb = S // MOBA_BLOCK
    assert TQ == TK and S % TK == 0 and T % TM_MERGE == 0 and nkb <= MOBA_HD
    consts = _constants(S)
    router_w_t = router_w.T
    router_b = router_bias[:, None]
    tm_moe = min(TM_MOE, T)
    tri = jnp.asarray(np.triu(np.ones((tm_moe, tm_moe), np.float32), 1), BF16)

    ffn_w = (moe_w_gate.astype(BF16), moe_w_up.astype(BF16), moe_w_down.astype(BF16))
    x2 = x.reshape(T, D)
    for l in range(depth):
        lw = _layer_weights(l, p)
        qa, ka, va, yb, yc, qd, kd, vd, km = _in_proj(x2, lw, consts, S)
        ya = _flash(qa, ka.reshape(B, S // TK, TK, -1), va, MLA_HEADS, True, "mla_attn")

        kmean = km.reshape(B, nkb, MOBA_HEADS, MOBA_HD).transpose(0, 2, 1, 3)
        kmean = jnp.pad(kmean, ((0, 0), (0, 0), (0, MOBA_HD - nkb), (0, 0)))
        q_aug = _moba_select(qd, kmean)
        yd = _flash(q_aug, kd.reshape(B, S // TK, TK, -1), vd, MOBA_HEADS, False, "moba_attn")

        x1, h2, route, cnt = _merge(x2, ya, yb, yc, yd, lw, router_w_t, router_b)
        counts = cnt[:, :, 0].reshape(T // tm_moe, tm_moe // TM_MERGE, N_EXPERTS).sum(axis=1)
        x2 = _moe(x1, h2, route, counts.reshape(-1), tri, lw["shared"], l, *ffn_w)
    return x2.reshape(B, S, D)
```

```python
import functools
import math

import jax
import jax.numpy as jnp
import numpy as np
from jax import lax
from jax.experimental import pallas as pl
from jax.experimental.pallas import tpu as pltpu

F32 = jnp.float32
BF16 = jnp.bfloat16
EPS = 1e-6
NEG = -1e30

LANES = 128
VMEM_LIMIT = 56 * 1024 * 1024

MLA_HEADS = 4
MLA_NOPE = 64
MLA_ROPE = 32
MLA_V = 64
MLA_Q_LORA = 256
MLA_KV_LORA = 128
ROPE_THETA = 10000.0
MLA_QK = MLA_NOPE + MLA_ROPE

SGU_GROUPS = 4
SGU_CHUNK = 128
SGU_W = 256

POOL_WINDOWS = (2, 4, 8, 16)
POOL_MAXW = 16
POOL_W = 256

MOBA_HEADS = 4
MOBA_HD = 64
MOBA_BLOCK = 256
MOBA_TOPK = 3

N_EXPERTS = 16
N_GROUPS = 4
EPG = 4

C_MLA = 0
C_SGU = 512
C_POOL = 1024
C_MOBA = 1280
C_END = 2048

TQ = 512
TK = 512
TM_IN = TQ
TM_MERGE = 512
TM_MOE = 1024
SEL_L = 2048
FLASH_UNROLLS = (4, 2, 1)
MOE_EXPERTS_PER_STEP = 4
MOE_CHUNK = 96
ONES_ROWS = 16


def _cparams(sem):
    return pltpu.CompilerParams(dimension_semantics=sem, vmem_limit_bytes=VMEM_LIMIT)


def _full(shape):
    n = len(shape)
    return pl.BlockSpec(shape, lambda *_: (0,) * n, pipeline_mode=pl.Buffered(1))


def _rms(x, g):
    return x * lax.rsqrt(jnp.mean(x * x, axis=-1, keepdims=True) + EPS) * g


def _dot(a, b):
    return jnp.dot(a, b, preferred_element_type=F32)


def _in_proj_kernel(x_ref, g_ref, w_ref, qn_ref, wq_ref, kvn_ref, wkv_ref, pk_ref,
                    qg_ref, kg_ref, g512_ref, r512_ref, cos_ref, sin_ref,
                    sn_ref, sw_ref, sb_ref, pw_ref, pb_ref, ps_ref,
                    mqg_ref, mkg_ref, g256_ref, pl_ref, kc_ref,
                    qa_ref, ka_ref, va_ref, yb_ref, yc_ref, qd_ref, kd_ref, vd_ref, km_ref,
                    halo_ref, buf_ref, *, tiles_per_seq, q_scale):
    tm = x_ref.shape[0]
    i = pl.program_id(0)
    x = x_ref[...]
    h = _rms(x, g_ref[...]).astype(BF16)

    cm = _dot(h, w_ref[:, C_MLA:C_SGU])
    c_q = _rms(cm[:, :MLA_Q_LORA], qn_ref[...]).astype(BF16)
    c_kv = _rms(cm[:, MLA_Q_LORA:MLA_Q_LORA + MLA_KV_LORA], kvn_ref[...]).astype(BF16)
    kpe_grp = cm[:, MLA_Q_LORA + MLA_KV_LORA:].astype(BF16)
    q = _dot(c_q, wq_ref[...])
    kv = _dot(c_kv, wkv_ref[...])
    k = kv[:, :512] + _dot(kpe_grp, pk_ref[...])
    va_ref[0, 0] = kv[:, 512:].T.astype(BF16)
    cos = jnp.concatenate([cos_ref[...]] * MLA_HEADS, axis=1)
    sin = jnp.concatenate([sin_ref[...]] * MLA_HEADS, axis=1)
    inv_d = 1.0 / MLA_QK

    def head_norm_rope(t, gain):
        ms = _dot((t * t).astype(BF16), g512_ref[...]) * inv_d
        t = t * lax.rsqrt(ms + EPS) * gain
        rot = _dot(t.astype(BF16), r512_ref[...])
        return t * cos + rot * sin

    qa_ref[0, 0] = (head_norm_rope(q, qg_ref[...]) * q_scale).T.astype(BF16)
    ka_ref[...] = head_norm_rope(k, kg_ref[...]).astype(BF16)

    z = jax.nn.gelu(_dot(h, w_ref[:, C_SGU:C_POOL]))
    u = z[:, :SGU_W]
    vv = _rms(z[:, SGU_W:], sn_ref[...]).astype(BF16)
    lane_grp = lax.broadcasted_iota(jnp.int32, (SGU_CHUNK, SGU_W), 1) // (SGU_W // SGU_GROUPS)
    for c in range(tm // SGU_CHUNK):
        vc = vv[c * SGU_CHUNK:(c + 1) * SGU_CHUNK, :]
        y = sb_ref[...]
        for gi in range(SGU_GROUPS):
            y = y + jnp.where(lane_grp == gi, _dot(sw_ref[gi], vc), 0.0)
        yb_ref[c * SGU_CHUNK:(c + 1) * SGU_CHUNK, :] = (
            u[c * SGU_CHUNK:(c + 1) * SGU_CHUNK, :] * y).astype(BF16)

    xp = _dot(h, w_ref[:, C_POOL:C_MOBA])
    cd = _dot(h, w_ref[:, C_MOBA:C_END])

    @pl.when(i % tiles_per_seq == 0)
    def _():
        halo_ref[...] = jnp.zeros_like(halo_ref)

    buf_ref[0:POOL_MAXW, :] = halo_ref[...]
    buf_ref[POOL_MAXW:, :] = xp
    halo_ref[...] = xp[tm - POOL_MAXW:, :]
    pos = (i % tiles_per_seq) * tm + lax.broadcasted_iota(jnp.int32, (tm, POOL_W), 0)
    lane_w = lax.broadcasted_iota(jnp.int32, (tm, POOL_W), 1) // (POOL_W // len(POOL_WINDOWS))
    acc = xp
    pooled = jnp.zeros((tm, POOL_W), F32)
    shift = 1
    for gi, w in enumerate(POOL_WINDOWS):
        while shift < w:
            acc = acc + buf_ref[POOL_MAXW - shift:POOL_MAXW - shift + tm, :]
            shift += 1
        cnt = jnp.minimum(pos + 1, w).astype(F32)
        pooled = jnp.where(lane_w == gi, acc / cnt, pooled)
    p = (pooled - xp).astype(BF16)
    yc_ref[...] = ((_dot(p, pw_ref[...]) + pb_ref[...]) * ps_ref[...]).astype(BF16)

    inv_hd = 1.0 / MOBA_HD

    def head_norm(t, gain):
        ms = _dot((t * t).astype(BF16), g256_ref[...]) * inv_hd
        return t * lax.rsqrt(ms + EPS) * gain

    qd_ref[0] = head_norm(cd[:, :256], mqg_ref[...]).T
    kd = head_norm(cd[:, 256:512], mkg_ref[...])
    kd_ref[...] = (_dot(kd.astype(BF16), pl_ref[...]) + kc_ref[...]).astype(BF16)
    vd_ref[0, 0] = cd[:, 512:].T.astype(BF16)
    for c in range(tm // MOBA_BLOCK):
        km_ref[c] = jnp.mean(kd[c * MOBA_BLOCK:(c + 1) * MOBA_BLOCK, :], axis=0, keepdims=True)


def _in_proj(x2, lw, consts, S):
    T, D = x2.shape
    tm = TM_IN
    nt = T // tm
    tok = lambda w: pl.BlockSpec((tm, w), lambda i: (i, 0))
    pos_spec = pl.BlockSpec((tm, LANES), lambda i: (i % (S // tm), 0))
    ins = [
        (x2, tok(D)), (lw["attn_norm"], None), (lw["w_in"], None),
        (lw["mla_q_norm"], None), (lw["wq"], None), (lw["mla_kv_norm"], None), (lw["wkv"], None),
        (consts["pk"], None), (lw["mla_qg"], None), (lw["mla_kg"], None),
        (consts["g512"], None), (consts["r512"], None),
        (consts["cos"], pos_spec), (consts["sin"], pos_spec),
        (lw["sgu_norm"], None), (lw["sgu_w"], None), (lw["sgu_b"], None),
        (lw["pool_w"], None), (lw["pool_b"], None), (lw["pool_scale"], None),
        (lw["moba_qg"], None), (lw["moba_kg"], None), (consts["g256"], None),
        (consts["place"], None),
        (consts["kc"], pl.BlockSpec((tm, 4 * LANES), lambda i: (i % (S // tm), 0))),
    ]
    arrays = [a for a, _ in ins]
    specs = [s if s is not None else _full(a.shape) for a, s in ins]
    nb = tm // MOBA_BLOCK
    B = T // S
    ns = S // tm
    tr = lambda f: pl.BlockSpec((1, 1, f, tm), lambda i: (i // ns, i % ns, 0, 0))
    out_shape = [
        jax.ShapeDtypeStruct((B, ns, 512, tm), BF16), jax.ShapeDtypeStruct((T, 512), BF16),
        jax.ShapeDtypeStruct((B, ns, 256, tm), BF16), jax.ShapeDtypeStruct((T, 256), BF16),
        jax.ShapeDtypeStruct((T, 256), BF16), jax.ShapeDtypeStruct((B, 256, S), F32),
        jax.ShapeDtypeStruct((T, 512), BF16), jax.ShapeDtypeStruct((B, ns, 256, tm), BF16),
        jax.ShapeDtypeStruct((T // MOBA_BLOCK, 1, 256), F32),
    ]
    out_specs = [tr(512), tok(512), tr(256), tok(256), tok(256),
                 pl.BlockSpec((1, 256, tm), lambda i: (i // ns, 0, i % ns)),
                 tok(512), tr(256), pl.BlockSpec((nb, 1, 256), lambda i: (i, 0, 0))]
    q_scale = (MLA_QK ** -0.5) * math.log2(math.e)
    return pl.pallas_call(
        functools.partial(_in_proj_kernel, tiles_per_seq=S // tm, q_scale=q_scale),
        grid=(nt,), in_specs=specs, out_specs=out_specs, out_shape=out_shape,
        scratch_shapes=[pltpu.VMEM((POOL_MAXW, POOL_W), F32),
                        pltpu.VMEM((tm + POOL_MAXW, POOL_W), F32)],
        compiler_params=_cparams(("arbitrary",)), name="in_proj",
    )(*arrays)


def _moba_select_kernel(q_ref, km_ref, o_ref, *, slope_log2_step):
    hh = pl.program_id(1)
    li = pl.program_id(2)
    q = q_ref[0]
    L = q.shape[1]
    nrow = km_ref.shape[2]
    g = jnp.dot(km_ref[0, 0], q, preferred_element_type=F32, precision=lax.Precision.HIGHEST)
    n_i = lax.broadcasted_iota(jnp.int32, (nrow, L), 0)
    own = (li * L + lax.broadcasted_iota(jnp.int32, (nrow, L), 1)) // MOBA_BLOCK
    g = jnp.where(n_i < own, g, NEG)
    sel = jnp.zeros((nrow, L), F32)
    for _ in range(MOBA_TOPK):
        mx = jnp.max(g, axis=0, keepdims=True)
        first = jnp.min(jnp.where(g == mx, n_i, nrow), axis=0, keepdims=True)
        hit = n_i == first
        sel = jnp.where(hit & (mx > 0.5 * NEG), 1.0, sel)
        g = jnp.where(hit, NEG, g)
    slope = jnp.exp2(-slope_log2_step * (hh + 1).astype(F32))
    rel = (n_i - own).astype(F32) * (slope * MOBA_BLOCK)
    bias = jnp.where((sel > 0.0) | (n_i == own), rel, NEG)
    bias = jnp.where(n_i == nrow - 1, slope, bias)
    qs = (q * (MOBA_HD ** -0.5)).astype(BF16)
    bias = bias.astype(BF16)
    tq = o_ref.shape[3]
    for t in range(L // tq):
        o_ref[0, t, 0:MOBA_HD, :] = qs[:, t * tq:(t + 1) * tq]
        o_ref[0, t, MOBA_HD:, :] = bias[:, t * tq:(t + 1) * tq]


def _moba_select(qgT, kmean):
    B, _, S = qgT.shape
    H, hd = MOBA_HEADS, MOBA_HD
    L = min(SEL_L, S)
    return pl.pallas_call(
        functools.partial(_moba_select_kernel, slope_log2_step=8.0 / MOBA_HEADS),
        grid=(B, H, S // L),
        in_specs=[pl.BlockSpec((1, hd, L), lambda b, h, l: (b, h, l)),
                  pl.BlockSpec((1, 1, 64, hd), lambda b, h, l: (b, h, 0, 0))],
        out_specs=pl.BlockSpec((1, L // TQ, 2 * hd, TQ), lambda b, h, l: (b, l, h, 0)),
        out_shape=jax.ShapeDtypeStruct((B, S // TQ, H * 2 * hd, TQ), BF16),
        compiler_params=_cparams(("arbitrary",) * 3), name="moba_select",
    )(qgT, kmean)


def _flash_kernel(q_ref, k_ref, v_ref, o_ref, s_scr, acc_scr, *, heads, use_exp2):
    H = heads
    tq = q_ref.shape[3]
    tk = k_ref.shape[2]
    dk = q_ref.shape[2] // H
    dv = v_ref.shape[2] // H
    ratio = tk // tq
    ex = jnp.exp2 if use_exp2 else jnp.exp
    i = pl.program_id(1)
    jd = i // ratio
    kpos = jd * tk + lax.broadcasted_iota(jnp.int32, (tk, tq), 0)
    qpos = i * tq + lax.broadcasted_iota(jnp.int32, (tk, tq), 1)
    causal = kpos <= qpos
    ones = jnp.ones((ONES_ROWS, tk), BF16)

    def scores(jt, h, masked):
        s = _dot(k_ref[0, jt, :, h * dk:(h + 1) * dk], q_ref[0, 0, h * dk:(h + 1) * dk, :])
        if masked:
            s = jnp.where(causal, s, NEG)
        s_scr[h] = s
        return jnp.max(s, axis=0, keepdims=True)

    def update(jt, h, cmax, m):
        m_new = jnp.maximum(m, cmax)
        alpha = ex(m - m_new)
        p = ex(s_scr[h] - m_new).astype(BF16)
        vt = jnp.concatenate([v_ref[0, jt, h * dv:(h + 1) * dv, :], ones], axis=0)
        acc_scr[h] = acc_scr[h] * alpha + _dot(vt, p)
        return m_new

    acc_scr[...] = jnp.zeros_like(acc_scr)
    ms = [jnp.full((1, tq), NEG, F32) for _ in range(H)]
    cmax = scores(jd, 0, True)
    for h in range(H):
        nxt = scores(jd, h + 1, True) if h + 1 < H else scores(0, 0, False)
        ms[h] = update(jd, h, cmax, ms[h])
        cmax = nxt

    def tile_step(j, carry):
        ms, cmax = carry
        ms = list(ms)
        jn = jnp.minimum(j + 1, jd - 1)
        for h in range(H):
            nxt = scores(j, h + 1, False) if h + 1 < H else scores(jn, 0, False)
            ms[h] = update(j, h, cmax, ms[h])
            cmax = nxt
        return tuple(ms), cmax

    carry = (tuple(ms), cmax)
    done = 0
    for unroll in FLASH_UNROLLS:
        def body(t, c, unroll=unroll, done=done):
            for u in range(unroll):
                c = tile_step(done + unroll * t + u, c)
            return c

        trips = (jd - done) // unroll
        carry = lax.fori_loop(0, trips, body, carry)
        done = done + trips * unroll
    outs = []
    for h in range(H):
        acc = acc_scr[h]
        outs.append(acc[:dv] / acc[dv:dv + 1])
    o_ref[...] = jnp.concatenate(outs, axis=0).T.astype(BF16)


def _flash(qT, k, vT, heads, use_exp2, name):
    B, nq, hdk, tq = qT.shape
    nk, tk = k.shape[1:3]
    hdv = vT.shape[2]
    dva = hdv // heads + ONES_ROWS
    whole = lambda shp: pl.BlockSpec((1,) + shp, lambda b, i: (b, 0, 0, 0),
                                     pipeline_mode=pl.Buffered(1))
    return pl.pallas_call(
        functools.partial(_flash_kernel, heads=heads, use_exp2=use_exp2),
        grid=(B, nq),
        in_specs=[pl.BlockSpec((1, 1, hdk, tq), lambda b, i: (b, i, 0, 0)),
                  whole((nk, tk, hdk)), whole((nk, hdv, tk))],
        out_specs=pl.BlockSpec((tq, hdv), lambda b, i: (b * nq + i, 0)),
        out_shape=jax.ShapeDtypeStruct((B * nq * tq, hdv), BF16),
        scratch_shapes=[pltpu.VMEM((heads, tk, tq), F32), pltpu.VMEM((heads, dva, tq), F32)],
        compiler_params=_cparams(("arbitrary", "arbitrary")), name=name,
    )(qT, k, vT)


def _merge_kernel(x_ref, ya_ref, yb_ref, yc_ref, yd_ref, g_ref, wg_ref, wb_ref, wo_ref,
                  fg_ref, rw_ref, rb_ref, x1_ref, h2_ref, rt_ref, cnt_ref, h2_scr):
    @pl.when(pl.program_id(0) == 0)
    def _():
        h2_scr[...] = jnp.zeros_like(h2_scr)

    router = _router_stages(h2_scr, rw_ref, rb_ref, rt_ref, cnt_ref)
    x = x_ref[...]
    h = _rms(x, g_ref[...]).astype(BF16)
    merged = None
    for bi, y_ref in enumerate((ya_ref, yb_ref, yc_ref, yd_ref)):
        t = jax.nn.sigmoid(_dot(h, wg_ref[bi])) * _dot(y_ref[...], wb_ref[bi])
        merged = t if merged is None else merged + t
        next(router)
    x1 = x + _dot(merged.astype(BF16), wo_ref[...])
    x1_ref[...] = x1
    h2_new = _rms(x1, fg_ref[...])
    h2_ref[...] = h2_new.astype(BF16)
    for _ in router:
        pass
    h2_scr[...] = h2_new


def _router_stages(h2_scr, rw_ref, rb_ref, rt_ref, cnt_ref):
    h2 = h2_scr[...]
    nt = (((1,), (1,)), ((), ()))
    h2_hi = h2.astype(BF16)
    h2_lo = (h2 - h2_hi.astype(F32)).astype(BF16)
    rw = rw_ref[...]
    rw_hi = rw.astype(BF16)
    rw_lo = (rw - rw_hi.astype(F32)).astype(BF16)
    logits = (lax.dot_general(rw_hi, h2_hi, nt, preferred_element_type=F32)
              + lax.dot_general(rw_hi, h2_lo, nt, preferred_element_type=F32)
              + lax.dot_general(rw_lo, h2_hi, nt, preferred_element_type=F32))
    yield
    scores = jax.nn.sigmoid(logits)
    biased = scores + rb_ref[...]
    s_rows = [scores[e:e + 1, :] for e in range(N_EXPERTS)]
    b_rows = [biased[e:e + 1, :] for e in range(N_EXPERTS)]

    def top2(vals):
        m1 = vals[0]
        i1 = jnp.zeros_like(m1, dtype=jnp.int32)
        for j in range(1, len(vals)):
            better = vals[j] > m1
            i1 = jnp.where(better, j, i1)
            m1 = jnp.where(better, vals[j], m1)
        m2 = None
        i2 = None
        for j in range(len(vals)):
            cand = jnp.where(i1 == j, NEG, vals[j])
            if m2 is None:
                m2, i2 = cand, jnp.zeros_like(i1)
            else:
                better = cand > m2
                i2 = jnp.where(better, j, i2)
                m2 = jnp.where(better, cand, m2)
        return m1, i1, m2, i2

    best = None
    for gi in range(N_GROUPS):
        bm1, bi1, bm2, bi2 = top2(b_rows[gi * EPG:(gi + 1) * EPG])
        gs = bm1 + bm2
        e1 = bi1 + gi * EPG
        e2 = bi2 + gi * EPG
        if best is None:
            best = (gs, e1, e2)
        else:
            better = gs > best[0]
            best = (jnp.where(better, gs, best[0]), jnp.where(better, e1, best[1]),
                    jnp.where(better, e2, best[2]))
        if gi % 2 == 1:
            yield
    _, e1, e2 = best
    w1 = jnp.zeros_like(s_rows[0])
    w2 = jnp.zeros_like(s_rows[0])
    for e in range(N_EXPERTS):
        w1 = jnp.where(e1 == e, s_rows[e], w1)
        w2 = jnp.where(e2 == e, s_rows[e], w2)
    tot = w1 + w2
    w1 = w1 / tot
    w2 = w2 / tot
    r8 = lax.broadcasted_iota(jnp.int32, (8, scores.shape[1]), 0)
    rt_ref[...] = jnp.where(r8 == 0, e1.astype(F32), jnp.where(r8 == 1, e2.astype(F32),
                            jnp.where(r8 == 2, w1, jnp.where(r8 == 3, w2, 0.0))))
    row = lax.broadcasted_iota(jnp.int32, scores.shape, 0)
    assigned = jnp.where(row == e1, 1.0, jnp.where(row == e2, 1.0, 0.0))
    cnt = jnp.sum(assigned, axis=1, keepdims=True)
    cnt_ref[0] = jnp.broadcast_to(cnt, (N_EXPERTS, LANES)).astype(jnp.int32)
    yield


def _merge(x2, ya, yb, yc, yd, lw, router_w_t, router_b):
    T, D = x2.shape
    tm = TM_MERGE
    n = T // tm
    tok = lambda w: pl.BlockSpec((tm, w), lambda i: (jnp.minimum(i, n - 1), 0))
    consts = [lw["attn_norm"], lw["w_gate"], lw["w_branch"], lw["w_out"], lw["ffn_norm"],
              router_w_t, router_b]
    return pl.pallas_call(
        _merge_kernel, grid=(n + 1,),
        in_specs=[tok(D), tok(256), tok(256), tok(256), tok(256)] + [_full(a.shape) for a in consts],
        out_specs=[tok(D), tok(D), pl.BlockSpec((8, tm), lambda i: (0, jnp.maximum(i - 1, 0))),
                   pl.BlockSpec((1, N_EXPERTS, LANES), lambda i: (jnp.maximum(i - 1, 0), 0, 0))],
        out_shape=[jax.ShapeDtypeStruct((T, D), F32), jax.ShapeDtypeStruct((T, D), BF16),
                   jax.ShapeDtypeStruct((8, T), F32),
                   jax.ShapeDtypeStruct((n, N_EXPERTS, LANES), jnp.int32)],
        scratch_shapes=[pltpu.VMEM((tm, D), F32)],
        compiler_params=_cparams(("arbitrary",)), name="merge",
    )(x2, ya, yb, yc, yd, *consts)


def _moe_kernel(cnt_ref, x1_ref, h2_ref, rt_ref, tri_ref, sg_ref, su_ref, sd_ref,
                wg_ref, wu_ref, wd_ref, o_ref, rank_scr, gate_scr):
    i = pl.program_id(0)
    e = pl.program_id(1)
    tm = h2_ref.shape[0]

    def ffn(rows, scale, wg, wu, wd):
        a = jax.nn.silu(_dot(rows, wg)) * _dot(rows, wu)
        if scale is not None:
            a = a * scale
        return _dot(a.astype(BF16), wd)

    sub = tri_ref.shape[0]
    nsub = tm // sub
    spans = [slice(k * sub, (k + 1) * sub) for k in range(nsub)]

    @pl.when(e == 0)
    def _():
        o_ref[...] = x1_ref[...] + ffn(h2_ref[...], None, sg_ref[...], su_ref[...], sd_ref[...])
        row = lax.broadcasted_iota(jnp.int32, (N_EXPERTS, tm), 0).astype(F32)
        e1, e2 = rt_ref[0:1, :], rt_ref[1:2, :]
        w1, w2 = rt_ref[2:3, :], rt_ref[3:4, :]
        assigned = jnp.where(row == e1, 1.0, jnp.where(row == e2, 1.0, 0.0))
        for sp in spans:
            before = _dot(assigned[:, sp].astype(BF16), tri_ref[...])
            rank_scr[:, sp] = jnp.where(assigned[:, sp] > 0.0, before, -1.0)
        gate_scr[...] = jnp.where(row == e1, w1, jnp.where(row == e2, w2, 0.0))

    @pl.when(e > 0)
    def _():
        subs = range(MOE_EXPERTS_PER_STEP)
        ex = [(e - 1) * MOE_EXPERTS_PER_STEP + s for s in subs]
        cnt = [cnt_ref[(i * nsub + k) * N_EXPERTS + x] for x in ex for k in range(nsub)]
        rank_row = [rank_scr[pl.ds(x, 1), :] for x in ex]
        gate_row = [gate_scr[pl.ds(x, 1), :] for x in ex]
        slot = lax.broadcasted_iota(jnp.int32, (MOE_CHUNK, sub), 0).astype(F32)

        def chunk(c, carry):
            base = slot + (c * MOE_CHUNK).astype(F32)
            onehot = [[None] * nsub for _ in subs]
            y = []
            for s in subs:
                rows, scale = [], []
                for k, sp in enumerate(spans):
                    hit = rank_row[s][:, sp] == base
                    onehot[s][k] = jnp.where(hit, 1.0, 0.0).astype(BF16)
                    rows.append(_dot(onehot[s][k], h2_ref[sp, :]).astype(BF16))
                    scale.append(jnp.sum(jnp.where(hit, gate_row[s][:, sp], 0.0),
                                         axis=1, keepdims=True))
                y.append(ffn(jnp.concatenate(rows, axis=0), jnp.concatenate(scale, axis=0),
                             wg_ref[0, s], wu_ref[0, s], wd_ref[0, s]).astype(BF16))
            for k, sp in enumerate(spans):
                rows_k = slice(k * MOE_CHUNK, (k + 1) * MOE_CHUNK)
                o_ref[sp, :] += lax.dot_general(
                    jnp.concatenate([onehot[s][k] for s in subs], axis=0),
                    jnp.concatenate([y[s][rows_k] for s in subs], axis=0),
                    (((0,), (0,)), ((), ())), preferred_element_type=F32)
            return carry

        most = functools.reduce(jnp.maximum, cnt)
        lax.fori_loop(0, (most + MOE_CHUNK - 1) // MOE_CHUNK, chunk, 0)


def _moe(x1, h2, route, counts, tri, shared, layer, wg, wu, wd):
    T, D = x1.shape
    tm = min(TM_MOE, T)
    sub = tri.shape[0]
    _, ne, _, ff = wg.shape
    const = lambda shp: pl.BlockSpec(shp, lambda i, e, c: (0, 0), pipeline_mode=pl.Buffered(1))
    eps = MOE_EXPERTS_PER_STEP
    expert = lambda shp: pl.BlockSpec((1, eps) + shp,
                                      lambda i, e, c: (layer, jnp.maximum(e - 1, 0), 0, 0))
    grid_spec = pltpu.PrefetchScalarGridSpec(
        num_scalar_prefetch=1, grid=(T // tm, ne // eps + 1),
        in_specs=[pl.BlockSpec((tm, D), lambda i, e, c: (i, 0)),
                  pl.BlockSpec((tm, D), lambda i, e, c: (i, 0)),
                  pl.BlockSpec((8, tm), lambda i, e, c: (0, i)),
                  const((sub, sub)), const((D, ff)), const((D, ff)), const((ff, D)),
                  expert((D, ff)), expert((D, ff)), expert((ff, D))],
        out_specs=pl.BlockSpec((tm, D), lambda i, e, c: (i, 0)),
        scratch_shapes=[pltpu.VMEM((N_EXPERTS, tm), F32), pltpu.VMEM((N_EXPERTS, tm), F32)])
    return pl.pallas_call(
        _moe_kernel, grid_spec=grid_spec,
        out_shape=jax.ShapeDtypeStruct((T, D), F32),
        compiler_params=_cparams(("arbitrary", "arbitrary")), name="moe",
    )(counts, x1, h2, route, tri, *shared, wg, wu, wd)


def _constants(S):
    half = MLA_ROPE // 2
    inv = ROPE_THETA ** (-np.arange(half, dtype=np.float64) / half)
    ang = np.arange(S, dtype=np.float64)[:, None] * inv[None, :]
    cos = np.ones((S, LANES), np.float32)
    cos[:, MLA_NOPE:MLA_NOPE + half] = np.cos(ang)
    cos[:, MLA_NOPE + half:MLA_QK] = np.cos(ang)
    sin = np.zeros((S, LANES), np.float32)
    sin[:, MLA_NOPE:MLA_NOPE + half] = np.sin(ang)
    sin[:, MLA_NOPE + half:MLA_QK] = np.sin(ang)

    r = np.zeros((LANES, LANES), np.float32)
    for j in range(half):
        r[MLA_NOPE + half + j, MLA_NOPE + j] = -1.0
        r[MLA_NOPE + j, MLA_NOPE + half + j] = 1.0
    g = np.zeros((LANES, LANES), np.float32)
    g[:MLA_QK, :] = 1.0
    eye4 = np.eye(MLA_HEADS, dtype=np.float32)
    pk = np.zeros((LANES, 4 * LANES), np.float32)
    for hh in range(MLA_HEADS):
        for j in range(MLA_ROPE):
            pk[j, hh * LANES + MLA_NOPE + j] = 1.0
    g256 = np.kron(np.eye(MOBA_HEADS, dtype=np.float32), np.ones((MOBA_HD, MOBA_HD), np.float32))

    nkb = S // MOBA_BLOCK
    kpos = np.arange(S)
    kc = np.zeros((S, MOBA_HD), np.float32)
    blk = kpos // MOBA_BLOCK
    m = blk < min(nkb, MOBA_HD) - 1
    kc[kpos[m], blk[m]] = 1.0
    kc[:, MOBA_HD - 1] = kpos % MOBA_BLOCK
    kc4 = np.zeros((S, MOBA_HEADS, 2 * MOBA_HD), np.float32)
    kc4[:, :, MOBA_HD:] = kc[:, None, :]
    place = np.zeros((MOBA_HEADS * MOBA_HD, MOBA_HEADS * 2 * MOBA_HD), np.float32)
    for hh in range(MOBA_HEADS):
        for j in range(MOBA_HD):
            place[hh * MOBA_HD + j, hh * 2 * MOBA_HD + j] = 1.0
    return {
        "cos": jnp.asarray(cos), "sin": jnp.asarray(sin),
        "r512": jnp.asarray(np.kron(eye4, r), BF16), "g512": jnp.asarray(np.kron(eye4, g), BF16),
        "pk": jnp.asarray(pk, BF16), "g256": jnp.asarray(g256, BF16),
        "kc": jnp.asarray(kc4.reshape(S, -1), BF16), "place": jnp.asarray(place, BF16),
    }


def _head_pad(v):
    one = jnp.concatenate([v, jnp.zeros((LANES - MLA_QK,), F32)])
    return jnp.tile(one, MLA_HEADS)[None, :]


def _layer_weights(l, p):
    D = p["w_in"].shape[1]
    w_in = p["w_in"][l]
    o1 = MLA_Q_LORA + MLA_KV_LORA + MLA_ROPE
    o2 = o1 + 2 * SGU_W
    o3 = o2 + POOL_W
    w = jnp.concatenate([w_in[:, :o1], jnp.zeros((D, C_SGU - o1), F32), w_in[:, o1:]], axis=1)

    wq = p["mla_w_q_up"][l].reshape(MLA_Q_LORA, MLA_HEADS, MLA_QK)
    wq = jnp.pad(wq, ((0, 0), (0, 0), (0, LANES - MLA_QK))).reshape(MLA_Q_LORA, MLA_HEADS * LANES)
    wkv = p["mla_w_kv_up"][l].reshape(MLA_KV_LORA, MLA_HEADS, MLA_NOPE + MLA_V)
    wk = jnp.pad(wkv[:, :, :MLA_NOPE], ((0, 0), (0, 0), (0, LANES - MLA_NOPE)))
    wk = wk.reshape(MLA_KV_LORA, MLA_HEADS * LANES)
    wv = wkv[:, :, MLA_NOPE:].reshape(MLA_KV_LORA, MLA_HEADS * MLA_V)

    tri = jnp.tril(jnp.ones((SGU_CHUNK, SGU_CHUNK), dtype=bool))
    sgu_w = jnp.where(tri[None], p["sgu_w"][l], 0.0)
    sgu_b = jnp.repeat(p["sgu_b"][l].T, SGU_W // SGU_GROUPS, axis=1)
    pool_w = jax.scipy.linalg.block_diag(*[p["pool_w"][l, gi] for gi in range(len(POOL_WINDOWS))])

    ne = p["moe_w_gate"].shape[1]
    return {
        "attn_norm": p["attn_norm"][l][None, :], "w_in": w.astype(BF16),
        "mla_q_norm": p["mla_q_norm"][l][None, :], "wq": wq.astype(BF16),
        "mla_kv_norm": p["mla_kv_norm"][l][None, :],
        "wkv": jnp.concatenate([wk, wv], axis=1).astype(BF16),
        "mla_qg": _head_pad(p["mla_q_gain"][l]), "mla_kg": _head_pad(p["mla_k_gain"][l]),
        "sgu_norm": p["sgu_norm"][l][None, :], "sgu_w": sgu_w.astype(BF16), "sgu_b": sgu_b,
        "pool_w": pool_w.astype(BF16), "pool_b": p["pool_b"][l][None, :],
        "pool_scale": p["pool_scale"][l][None, :],
        "moba_qg": jnp.tile(p["moba_q_gain"][l], MOBA_HEADS)[None, :],
        "moba_kg": jnp.tile(p["moba_k_gain"][l], MOBA_HEADS)[None, :],
        "w_gate": p["w_gate"][l].astype(BF16), "w_branch": p["w_branch"][l].astype(BF16),
        "w_out": p["w_out"][l].astype(BF16), "ffn_norm": p["ffn_norm"][l][None, :],
        "shared": (p["shared_w_gate"][l].astype(BF16), p["shared_w_up"][l].astype(BF16),
                   p["shared_w_down"][l].astype(BF16)),
    }


def kernel(x, attn_norm, w_in, mla_q_norm, mla_w_q_up, mla_kv_norm, mla_w_kv_up, mla_q_gain, mla_k_gain, sgu_norm, sgu_w, sgu_b, pool_w, pool_b, pool_scale, moba_q_gain, moba_k_gain, w_gate, w_branch, w_out, ffn_norm, router_w, router_bias, moe_w_gate, moe_w_up, moe_w_down, shared_w_gate, shared_w_up, shared_w_down):
    p = dict(attn_norm=attn_norm, w_in=w_in, mla_q_norm=mla_q_norm, mla_w_q_up=mla_w_q_up,
             mla_kv_norm=mla_kv_norm, mla_w_kv_up=mla_w_kv_up, mla_q_gain=mla_q_gain,
             mla_k_gain=mla_k_gain, sgu_norm=sgu_norm, sgu_w=sgu_w, sgu_b=sgu_b, pool_w=pool_w,
             pool_b=pool_b, pool_scale=pool_scale, moba_q_gain=moba_q_gain,
             moba_k_gain=moba_k_gain, w_gate=w_gate, w_branch=w_branch, w_out=w_out,
             ffn_norm=ffn_norm, moe_w_gate=moe_w_gate, moe_w_up=moe_w_up, moe_w_down=moe_w_down,
             shared_w_gate=shared_w_gate, shared_w_up=shared_w_up, shared_w_down=shared_w_down)
    B, S, D = x.shape
    T = B * S
    depth = attn_norm.shape[0]
    nkb = S // MOBA_BLOCK
    assert TQ == TK and S % TK == 0 and T % TM_MERGE == 0 and nkb <= MOBA_HD
    consts = _constants(S)
    router_w_t = router_w.T
    router_b = router_bias[:, None]
    assert min(TM_MOE, T) % TM_MERGE == 0 and N_EXPERTS % MOE_EXPERTS_PER_STEP == 0
    tri = jnp.asarray(np.triu(np.ones((TM_MERGE, TM_MERGE), np.float32), 1), BF16)

    ffn_w = (moe_w_gate.astype(BF16), moe_w_up.astype(BF16), moe_w_down.astype(BF16))
    x2 = x.reshape(T, D)
    for l in range(depth):
        lw = _layer_weights(l, p)
        qa, ka, va, yb, yc, qd, kd, vd, km = _in_proj(x2, lw, consts, S)
        ya = _flash(qa, ka.reshape(B, S // TK, TK, -1), va, MLA_HEADS, True, "mla_attn")

        kmean = km.reshape(B, nkb, MOBA_HEADS, MOBA_HD).transpose(0, 2, 1, 3)
        kmean = jnp.pad(kmean, ((0, 0), (0, 0), (0, MOBA_HD - nkb), (0, 0)))
        q_aug = _moba_select(qd, kmean)
        yd = _flash(q_aug, kd.reshape(B, S // TK, TK, -1), vd, MOBA_HEADS, False, "moba_attn")

        x1, h2, route, cnt = _merge(x2, ya, yb, yc, yd, lw, router_w_t, router_b)
        x2 = _moe(x1, h2, route, cnt[:, :, 0].reshape(-1), tri, lw["shared"], l, *ffn_w)
    return x2.reshape(B, S, D)
```

```python
import functools
import math

import jax
import jax.numpy as jnp
import numpy as np
from jax import lax
from jax.experimental import pallas as pl
from jax.experimental.pallas import tpu as pltpu

F32 = jnp.float32
BF16 = jnp.bfloat16
EPS = 1e-6
NEG = -1e30

LANES = 128
VMEM_LIMIT = 56 * 1024 * 1024

MLA_HEADS = 4
MLA_NOPE = 64
MLA_ROPE = 32
MLA_V = 64
MLA_Q_LORA = 256
MLA_KV_LORA = 128
ROPE_THETA = 10000.0
MLA_QK = MLA_NOPE + MLA_ROPE

SGU_GROUPS = 4
SGU_CHUNK = 128
SGU_W = 256

POOL_WINDOWS = (2, 4, 8, 16)
POOL_MAXW = 16
POOL_W = 256

MOBA_HEADS = 4
MOBA_HD = 64
MOBA_BLOCK = 256
MOBA_TOPK = 3

N_EXPERTS = 16
N_GROUPS = 4
EPG = 4

C_MLA = 0
C_SGU = 512
C_POOL = 1024
C_MOBA = 1280
C_END = 2048

TQ = 512
TK = 512
TM_IN = TQ
TM_MERGE = 512
TM_MOE = 1024
SEL_L = 2048
FLASH_UNROLLS = (4, 2, 1)
MOE_EXPERTS_PER_STEP = 4
MOE_SUB = 256
MOE_CHUNK = 48
ONES_ROWS = 16


def _cparams(sem):
    return pltpu.CompilerParams(dimension_semantics=sem, vmem_limit_bytes=VMEM_LIMIT)


def _full(shape):
    n = len(shape)
    return pl.BlockSpec(shape, lambda *_: (0,) * n, pipeline_mode=pl.Buffered(1))


def _rms(x, g):
    return x * lax.rsqrt(jnp.mean(x * x, axis=-1, keepdims=True) + EPS) * g


def _dot(a, b):
    return jnp.dot(a, b, preferred_element_type=F32)


def _in_proj_kernel(x_ref, g_ref, w_ref, qn_ref, wq_ref, kvn_ref, wkv_ref, pk_ref,
                    qg_ref, kg_ref, g512_ref, r512_ref, cos_ref, sin_ref,
                    sn_ref, sw_ref, sb_ref, pw_ref, pb_ref, ps_ref,
                    mqg_ref, mkg_ref, g256_ref, pl_ref, kc_ref,
                    qa_ref, ka_ref, va_ref, yb_ref, yc_ref, qd_ref, kd_ref, vd_ref, km_ref,
                    halo_ref, buf_ref, *, tiles_per_seq, q_scale):
    tm = x_ref.shape[0]
    i = pl.program_id(0)
    x = x_ref[...]
    h = _rms(x, g_ref[...]).astype(BF16)

    cm = _dot(h, w_ref[:, C_MLA:C_SGU])
    c_q = _rms(cm[:, :MLA_Q_LORA], qn_ref[...]).astype(BF16)
    c_kv = _rms(cm[:, MLA_Q_LORA:MLA_Q_LORA + MLA_KV_LORA], kvn_ref[...]).astype(BF16)
    kpe_grp = cm[:, MLA_Q_LORA + MLA_KV_LORA:].astype(BF16)
    q = _dot(c_q, wq_ref[...])
    kv = _dot(c_kv, wkv_ref[...])
    k = kv[:, :512] + _dot(kpe_grp, pk_ref[...])
    va_ref[0, 0] = kv[:, 512:].T.astype(BF16)
    cos = jnp.concatenate([cos_ref[...]] * MLA_HEADS, axis=1)
    sin = jnp.concatenate([sin_ref[...]] * MLA_HEADS, axis=1)
    inv_d = 1.0 / MLA_QK

    def head_norm_rope(t, gain):
        ms = _dot((t * t).astype(BF16), g512_ref[...]) * inv_d
        t = t * lax.rsqrt(ms + EPS) * gain
        rot = _dot(t.astype(BF16), r512_ref[...])
        return t * cos + rot * sin

    qa_ref[0, 0] = (head_norm_rope(q, qg_ref[...]) * q_scale).T.astype(BF16)
    ka_ref[...] = head_norm_rope(k, kg_ref[...]).astype(BF16)

    z = jax.nn.gelu(_dot(h, w_ref[:, C_SGU:C_POOL]))
    u = z[:, :SGU_W]
    vv = _rms(z[:, SGU_W:], sn_ref[...]).astype(BF16)
    lane_grp = lax.broadcasted_iota(jnp.int32, (SGU_CHUNK, SGU_W), 1) // (SGU_W // SGU_GROUPS)
    for c in range(tm // SGU_CHUNK):
        vc = vv[c * SGU_CHUNK:(c + 1) * SGU_CHUNK, :]
        y = sb_ref[...]
        for gi in range(SGU_GROUPS):
            y = y + jnp.where(lane_grp == gi, _dot(sw_ref[gi], vc), 0.0)
        yb_ref[c * SGU_CHUNK:(c + 1) * SGU_CHUNK, :] = (
            u[c * SGU_CHUNK:(c + 1) * SGU_CHUNK, :] * y).astype(BF16)

    xp = _dot(h, w_ref[:, C_POOL:C_MOBA])
    cd = _dot(h, w_ref[:, C_MOBA:C_END])

    @pl.when(i % tiles_per_seq == 0)
    def _():
        halo_ref[...] = jnp.zeros_like(halo_ref)

    buf_ref[0:POOL_MAXW, :] = halo_ref[...]
    buf_ref[POOL_MAXW:, :] = xp
    halo_ref[...] = xp[tm - POOL_MAXW:, :]
    pos = (i % tiles_per_seq) * tm + lax.broadcasted_iota(jnp.int32, (tm, POOL_W), 0)
    lane_w = lax.broadcasted_iota(jnp.int32, (tm, POOL_W), 1) // (POOL_W // len(POOL_WINDOWS))
    acc = xp
    pooled = jnp.zeros((tm, POOL_W), F32)
    shift = 1
    for gi, w in enumerate(POOL_WINDOWS):
        while shift < w:
            acc = acc + buf_ref[POOL_MAXW - shift:POOL_MAXW - shift + tm, :]
            shift += 1
        cnt = jnp.minimum(pos + 1, w).astype(F32)
        pooled = jnp.where(lane_w == gi, acc / cnt, pooled)
    p = (pooled - xp).astype(BF16)
    yc_ref[...] = ((_dot(p, pw_ref[...]) + pb_ref[...]) * ps_ref[...]).astype(BF16)

    inv_hd = 1.0 / MOBA_HD

    def head_norm(t, gain):
        ms = _dot((t * t).astype(BF16), g256_ref[...]) * inv_hd
        return t * lax.rsqrt(ms + EPS) * gain

    qd_ref[0] = head_norm(cd[:, :256], mqg_ref[...]).T
    kd = head_norm(cd[:, 256:512], mkg_ref[...])
    kd_ref[...] = (_dot(kd.astype(BF16), pl_ref[...]) + kc_ref[...]).astype(BF16)
    vd_ref[0, 0] = cd[:, 512:].T.astype(BF16)
    for c in range(tm // MOBA_BLOCK):
        km_ref[c] = jnp.mean(kd[c * MOBA_BLOCK:(c + 1) * MOBA_BLOCK, :], axis=0, keepdims=True)


def _in_proj(x2, lw, consts, S):
    T, D = x2.shape
    tm = TM_IN
    nt = T // tm
    tok = lambda w: pl.BlockSpec((tm, w), lambda i: (i, 0))
    pos_spec = pl.BlockSpec((tm, LANES), lambda i: (i % (S // tm), 0))
    ins = [
        (x2, tok(D)), (lw["attn_norm"], None), (lw["w_in"], None),
        (lw["mla_q_norm"], None), (lw["wq"], None), (lw["mla_kv_norm"], None), (lw["wkv"], None),
        (consts["pk"], None), (lw["mla_qg"], None), (lw["mla_kg"], None),
        (consts["g512"], None), (consts["r512"], None),
        (consts["cos"], pos_spec), (consts["sin"], pos_spec),
        (lw["sgu_norm"], None), (lw["sgu_w"], None), (lw["sgu_b"], None),
        (lw["pool_w"], None), (lw["pool_b"], None), (lw["pool_scale"], None),
        (lw["moba_qg"], None), (lw["moba_kg"], None), (consts["g256"], None),
        (consts["place"], None),
        (consts["kc"], pl.BlockSpec((tm, 4 * LANES), lambda i: (i % (S // tm), 0))),
    ]
    arrays = [a for a, _ in ins]
    specs = [s if s is not None else _full(a.shape) for a, s in ins]
    nb = tm // MOBA_BLOCK
    B = T // S
    ns = S // tm
    tr = lambda f: pl.BlockSpec((1, 1, f, tm), lambda i: (i // ns, i % ns, 0, 0))
    out_shape = [
        jax.ShapeDtypeStruct((B, ns, 512, tm), BF16), jax.ShapeDtypeStruct((T, 512), BF16),
        jax.ShapeDtypeStruct((B, ns, 256, tm), BF16), jax.ShapeDtypeStruct((T, 256), BF16),
        jax.ShapeDtypeStruct((T, 256), BF16), jax.ShapeDtypeStruct((B, 256, S), F32),
        jax.ShapeDtypeStruct((T, 512), BF16), jax.ShapeDtypeStruct((B, ns, 256, tm), BF16),
        jax.ShapeDtypeStruct((T // MOBA_BLOCK, 1, 256), F32),
    ]
    out_specs = [tr(512), tok(512), tr(256), tok(256), tok(256),
                 pl.BlockSpec((1, 256, tm), lambda i: (i // ns, 0, i % ns)),
                 tok(512), tr(256), pl.BlockSpec((nb, 1, 256), lambda i: (i, 0, 0))]
    q_scale = (MLA_QK ** -0.5) * math.log2(math.e)
    return pl.pallas_call(
        functools.partial(_in_proj_kernel, tiles_per_seq=S // tm, q_scale=q_scale),
        grid=(nt,), in_specs=specs, out_specs=out_specs, out_shape=out_shape,
        scratch_shapes=[pltpu.VMEM((POOL_MAXW, POOL_W), F32),
                        pltpu.VMEM((tm + POOL_MAXW, POOL_W), F32)],
        compiler_params=_cparams(("arbitrary",)), name="in_proj",
    )(*arrays)


def _moba_select_kernel(q_ref, km_ref, o_ref, *, slope_log2_step):
    hh = pl.program_id(1)
    li = pl.program_id(2)
    q = q_ref[0]
    L = q.shape[1]
    nrow = km_ref.shape[2]
    g = jnp.dot(km_ref[0, 0], q, preferred_element_type=F32, precision=lax.Precision.HIGHEST)
    n_i = lax.broadcasted_iota(jnp.int32, (nrow, L), 0)
    own = (li * L + lax.broadcasted_iota(jnp.int32, (nrow, L), 1)) // MOBA_BLOCK
    g = jnp.where(n_i < own, g, NEG)
    sel = jnp.zeros((nrow, L), F32)
    for _ in range(MOBA_TOPK):
        mx = jnp.max(g, axis=0, keepdims=True)
        first = jnp.min(jnp.where(g == mx, n_i, nrow), axis=0, keepdims=True)
        hit = n_i == first
        sel = jnp.where(hit & (mx > 0.5 * NEG), 1.0, sel)
        g = jnp.where(hit, NEG, g)
    slope = jnp.exp2(-slope_log2_step * (hh + 1).astype(F32))
    rel = (n_i - own).astype(F32) * (slope * MOBA_BLOCK)
    bias = jnp.where((sel > 0.0) | (n_i == own), rel, NEG)
    bias = jnp.where(n_i == nrow - 1, slope, bias)
    qs = (q * (MOBA_HD ** -0.5)).astype(BF16)
    bias = bias.astype(BF16)
    tq = o_ref.shape[3]
    for t in range(L // tq):
        o_ref[0, t, 0:MOBA_HD, :] = qs[:, t * tq:(t + 1) * tq]
        o_ref[0, t, MOBA_HD:, :] = bias[:, t * tq:(t + 1) * tq]


def _moba_select(qgT, kmean):
    B, _, S = qgT.shape
    H, hd = MOBA_HEADS, MOBA_HD
    L = min(SEL_L, S)
    return pl.pallas_call(
        functools.partial(_moba_select_kernel, slope_log2_step=8.0 / MOBA_HEADS),
        grid=(B, H, S // L),
        in_specs=[pl.BlockSpec((1, hd, L), lambda b, h, l: (b, h, l)),
                  pl.BlockSpec((1, 1, 64, hd), lambda b, h, l: (b, h, 0, 0))],
        out_specs=pl.BlockSpec((1, L // TQ, 2 * hd, TQ), lambda b, h, l: (b, l, h, 0)),
        out_shape=jax.ShapeDtypeStruct((B, S // TQ, H * 2 * hd, TQ), BF16),
        compiler_params=_cparams(("arbitrary",) * 3), name="moba_select",
    )(qgT, kmean)


def _flash_kernel(q_ref, k_ref, v_ref, o_ref, s_scr, acc_scr, *, heads, use_exp2):
    H = heads
    tq = q_ref.shape[3]
    tk = k_ref.shape[2]
    dk = q_ref.shape[2] // H
    dv = v_ref.shape[2] // H
    ratio = tk // tq
    ex = jnp.exp2 if use_exp2 else jnp.exp
    i = pl.program_id(1)
    jd = i // ratio
    kpos = jd * tk + lax.broadcasted_iota(jnp.int32, (tk, tq), 0)
    qpos = i * tq + lax.broadcasted_iota(jnp.int32, (tk, tq), 1)
    causal = kpos <= qpos
    ones = jnp.ones((ONES_ROWS, tk), BF16)

    def scores(jt, h, masked):
        s = _dot(k_ref[0, jt, :, h * dk:(h + 1) * dk], q_ref[0, 0, h * dk:(h + 1) * dk, :])
        if masked:
            s = jnp.where(causal, s, NEG)
        s_scr[h] = s
        return jnp.max(s, axis=0, keepdims=True)

    def update(jt, h, cmax, m):
        m_new = jnp.maximum(m, cmax)
        alpha = ex(m - m_new)
        p = ex(s_scr[h] - m_new).astype(BF16)
        vt = jnp.concatenate([v_ref[0, jt, h * dv:(h + 1) * dv, :], ones], axis=0)
        acc_scr[h] = acc_scr[h] * alpha + _dot(vt, p)
        return m_new

    acc_scr[...] = jnp.zeros_like(acc_scr)
    ms = [jnp.full((1, tq), NEG, F32) for _ in range(H)]
    cmax = scores(jd, 0, True)
    for h in range(H):
        nxt = scores(jd, h + 1, True) if h + 1 < H else scores(0, 0, False)
        ms[h] = update(jd, h, cmax, ms[h])
        cmax = nxt

    def tile_step(j, carry):
        ms, cmax = carry
        ms = list(ms)
        jn = jnp.minimum(j + 1, jd - 1)
        for h in range(H):
            nxt = scores(j, h + 1, False) if h + 1 < H else scores(jn, 0, False)
            ms[h] = update(j, h, cmax, ms[h])
            cmax = nxt
        return tuple(ms), cmax

    carry = (tuple(ms), cmax)
    done = 0
    for unroll in FLASH_UNROLLS:
        def body(t, c, unroll=unroll, done=done):
            for u in range(unroll):
                c = tile_step(done + unroll * t + u, c)
            return c

        trips = (jd - done) // unroll
        carry = lax.fori_loop(0, trips, body, carry)
        done = done + trips * unroll
    outs = []
    for h in range(H):
        acc = acc_scr[h]
        outs.append(acc[:dv] / acc[dv:dv + 1])
    o_ref[...] = jnp.concatenate(outs, axis=0).T.astype(BF16)


def _flash(qT, k, vT, heads, use_exp2, name):
    B, nq, hdk, tq = qT.shape
    nk, tk = k.shape[1:3]
    hdv = vT.shape[2]
    dva = hdv // heads + ONES_ROWS
    whole = lambda shp: pl.BlockSpec((1,) + shp, lambda b, i: (b, 0, 0, 0),
                                     pipeline_mode=pl.Buffered(1))
    return pl.pallas_call(
        functools.partial(_flash_kernel, heads=heads, use_exp2=use_exp2),
        grid=(B, nq),
        in_specs=[pl.BlockSpec((1, 1, hdk, tq), lambda b, i: (b, i, 0, 0)),
                  whole((nk, tk, hdk)), whole((nk, hdv, tk))],
        out_specs=pl.BlockSpec((tq, hdv), lambda b, i: (b * nq + i, 0)),
        out_shape=jax.ShapeDtypeStruct((B * nq * tq, hdv), BF16),
        scratch_shapes=[pltpu.VMEM((heads, tk, tq), F32), pltpu.VMEM((heads, dva, tq), F32)],
        compiler_params=_cparams(("arbitrary", "arbitrary")), name=name,
    )(qT, k, vT)


def _merge_kernel(x_ref, ya_ref, yb_ref, yc_ref, yd_ref, g_ref, wg_ref, wb_ref, wo_ref,
                  fg_ref, rw_ref, rb_ref, x1_ref, h2_ref, rt_ref, cnt_ref, h2_scr):
    @pl.when(pl.program_id(0) == 0)
    def _():
        h2_scr[...] = jnp.zeros_like(h2_scr)

    router = _router_stages(h2_scr, rw_ref, rb_ref, rt_ref, cnt_ref)
    x = x_ref[...]
    h = _rms(x, g_ref[...]).astype(BF16)
    merged = None
    for bi, y_ref in enumerate((ya_ref, yb_ref, yc_ref, yd_ref)):
        t = jax.nn.sigmoid(_dot(h, wg_ref[bi])) * _dot(y_ref[...], wb_ref[bi])
        merged = t if merged is None else merged + t
        next(router)
    x1 = x + _dot(merged.astype(BF16), wo_ref[...])
    x1_ref[...] = x1
    h2_new = _rms(x1, fg_ref[...])
    h2_ref[...] = h2_new.astype(BF16)
    for _ in router:
        pass
    h2_scr[...] = h2_new


def _router_stages(h2_scr, rw_ref, rb_ref, rt_ref, cnt_ref):
    h2 = h2_scr[...]
    nt = (((1,), (1,)), ((), ()))
    h2_hi = h2.astype(BF16)
    h2_lo = (h2 - h2_hi.astype(F32)).astype(BF16)
    rw = rw_ref[...]
    rw_hi = rw.astype(BF16)
    rw_lo = (rw - rw_hi.astype(F32)).astype(BF16)
    logits = (lax.dot_general(rw_hi, h2_hi, nt, preferred_element_type=F32)
              + lax.dot_general(rw_hi, h2_lo, nt, preferred_element_type=F32)
              + lax.dot_general(rw_lo, h2_hi, nt, preferred_element_type=F32))
    yield
    scores = jax.nn.sigmoid(logits)
    biased = scores + rb_ref[...]
    s_rows = [scores[e:e + 1, :] for e in range(N_EXPERTS)]
    b_rows = [biased[e:e + 1, :] for e in range(N_EXPERTS)]

    def top2(vals):
        m1 = vals[0]
        i1 = jnp.zeros_like(m1, dtype=jnp.int32)
        for j in range(1, len(vals)):
            better = vals[j] > m1
            i1 = jnp.where(better, j, i1)
            m1 = jnp.where(better, vals[j], m1)
        m2 = None
        i2 = None
        for j in range(len(vals)):
            cand = jnp.where(i1 == j, NEG, vals[j])
            if m2 is None:
                m2, i2 = cand, jnp.zeros_like(i1)
            else:
                better = cand > m2
                i2 = jnp.where(better, j, i2)
                m2 = jnp.where(better, cand, m2)
        return m1, i1, m2, i2

    best = None
    for gi in range(N_GROUPS):
        bm1, bi1, bm2, bi2 = top2(b_rows[gi * EPG:(gi + 1) * EPG])
        gs = bm1 + bm2
        e1 = bi1 + gi * EPG
        e2 = bi2 + gi * EPG
        if best is None:
            best = (gs, e1, e2)
        else:
            better = gs > best[0]
            best = (jnp.where(better, gs, best[0]), jnp.where(better, e1, best[1]),
                    jnp.where(better, e2, best[2]))
        if gi % 2 == 1:
            yield
    _, e1, e2 = best
    w1 = jnp.zeros_like(s_rows[0])
    w2 = jnp.zeros_like(s_rows[0])
    for e in range(N_EXPERTS):
        w1 = jnp.where(e1 == e, s_rows[e], w1)
        w2 = jnp.where(e2 == e, s_rows[e], w2)
    tot = w1 + w2
    w1 = w1 / tot
    w2 = w2 / tot
    r8 = lax.broadcasted_iota(jnp.int32, (8, scores.shape[1]), 0)
    rt_ref[...] = jnp.where(r8 == 0, e1.astype(F32), jnp.where(r8 == 1, e2.astype(F32),
                            jnp.where(r8 == 2, w1, jnp.where(r8 == 3, w2, 0.0))))
    row = lax.broadcasted_iota(jnp.int32, scores.shape, 0)
    assigned = jnp.where(row == e1, 1.0, jnp.where(row == e2, 1.0, 0.0))
    lane = lax.broadcasted_iota(jnp.int32, (N_EXPERTS, LANES), 1)
    cnt = jnp.zeros((N_EXPERTS, LANES), F32)
    for k in range(scores.shape[1] // MOE_SUB):
        part = jnp.sum(assigned[:, k * MOE_SUB:(k + 1) * MOE_SUB], axis=1, keepdims=True)
        cnt = jnp.where(lane == k, part, cnt)
    cnt_ref[0] = cnt.astype(jnp.int32)
    yield


def _merge(x2, ya, yb, yc, yd, lw, router_w_t, router_b):
    T, D = x2.shape
    tm = TM_MERGE
    n = T // tm
    tok = lambda w: pl.BlockSpec((tm, w), lambda i: (jnp.minimum(i, n - 1), 0))
    consts = [lw["attn_norm"], lw["w_gate"], lw["w_branch"], lw["w_out"], lw["ffn_norm"],
              router_w_t, router_b]
    return pl.pallas_call(
        _merge_kernel, grid=(n + 1,),
        in_specs=[tok(D), tok(256), tok(256), tok(256), tok(256)] + [_full(a.shape) for a in consts],
        out_specs=[tok(D), tok(D), pl.BlockSpec((8, tm), lambda i: (0, jnp.maximum(i - 1, 0))),
                   pl.BlockSpec((1, N_EXPERTS, LANES), lambda i: (jnp.maximum(i - 1, 0), 0, 0))],
        out_shape=[jax.ShapeDtypeStruct((T, D), F32), jax.ShapeDtypeStruct((T, D), BF16),
                   jax.ShapeDtypeStruct((8, T), F32),
                   jax.ShapeDtypeStruct((n, N_EXPERTS, LANES), jnp.int32)],
        scratch_shapes=[pltpu.VMEM((tm, D), F32)],
        compiler_params=_cparams(("arbitrary",)), name="merge",
    )(x2, ya, yb, yc, yd, *consts)


def _moe_kernel(cnt_ref, x1_ref, h2_ref, rt_ref, tri_ref, sg_ref, su_ref, sd_ref,
                wg_ref, wu_ref, wd_ref, o_ref, rank_scr, gate_scr):
    i = pl.program_id(0)
    e = pl.program_id(1)
    tm = h2_ref.shape[0]

    def ffn(rows, scale, wg, wu, wd):
        a = jax.nn.silu(_dot(rows, wg)) * _dot(rows, wu)
        if scale is not None:
            a = a * scale
        return _dot(a.astype(BF16), wd)

    sub = tri_ref.shape[0]
    nsub = tm // sub
    spans = [slice(k * sub, (k + 1) * sub) for k in range(nsub)]

    @pl.when(e == 0)
    def _():
        o_ref[...] = x1_ref[...] + ffn(h2_ref[...], None, sg_ref[...], su_ref[...], sd_ref[...])
        row = lax.broadcasted_iota(jnp.int32, (N_EXPERTS, tm), 0).astype(F32)
        e1, e2 = rt_ref[0:1, :], rt_ref[1:2, :]
        w1, w2 = rt_ref[2:3, :], rt_ref[3:4, :]
        assigned = jnp.where(row == e1, 1.0, jnp.where(row == e2, 1.0, 0.0))
        for sp in spans:
            before = _dot(assigned[:, sp].astype(BF16), tri_ref[...])
            rank_scr[:, sp] = jnp.where(assigned[:, sp] > 0.0, before, -1.0)
        gate_scr[...] = jnp.where(row == e1, w1, jnp.where(row == e2, w2, 0.0))

    @pl.when(e > 0)
    def _():
        subs = range(MOE_EXPERTS_PER_STEP)
        ex = [(e - 1) * MOE_EXPERTS_PER_STEP + s for s in subs]
        cnt = [cnt_ref[(i * nsub + k) * N_EXPERTS + x] for x in ex for k in range(nsub)]
        rank_row = [rank_scr[pl.ds(x, 1), :] for x in ex]
        gate_row = [gate_scr[pl.ds(x, 1), :] for x in ex]
        slot = lax.broadcasted_iota(jnp.int32, (MOE_CHUNK, sub), 0).astype(F32)

        def chunk(c, carry):
            base = slot + (c * MOE_CHUNK).astype(F32)
            onehot, rows, scale = [], [], []
            for sp in spans:
                hit = [rank_row[s][:, sp] == base for s in subs]
                onehot.append(jnp.concatenate(
                    [jnp.where(h, 1.0, 0.0).astype(BF16) for h in hit], axis=0))
                rows.append(_dot(onehot[-1], h2_ref[sp, :]).astype(BF16))
                scale.append([jnp.sum(jnp.where(hit[s], gate_row[s][:, sp], 0.0),
                                      axis=1, keepdims=True) for s in subs])
            y = []
            for s in subs:
                mine = slice(s * MOE_CHUNK, (s + 1) * MOE_CHUNK)
                y.append(ffn(jnp.concatenate([r[mine] for r in rows], axis=0),
                             jnp.concatenate([sc[s] for sc in scale], axis=0),
                             wg_ref[0, s], wu_ref[0, s], wd_ref[0, s]).astype(BF16))
            for k, sp in enumerate(spans):
                rows_k = slice(k * MOE_CHUNK, (k + 1) * MOE_CHUNK)
                o_ref[sp, :] += lax.dot_general(
                    onehot[k], jnp.concatenate([y[s][rows_k] for s in subs], axis=0),
                    (((0,), (0,)), ((), ())), preferred_element_type=F32)
            return carry

        most = functools.reduce(jnp.maximum, cnt)
        lax.fori_loop(0, (most + MOE_CHUNK - 1) // MOE_CHUNK, chunk, 0)


def _moe(x1, h2, route, counts, tri, shared, layer, wg, wu, wd):
    T, D = x1.shape
    tm = min(TM_MOE, T)
    sub = tri.shape[0]
    _, ne, _, ff = wg.shape
    const = lambda shp: pl.BlockSpec(shp, lambda i, e, c: (0, 0), pipeline_mode=pl.Buffered(1))
    eps = MOE_EXPERTS_PER_STEP
    expert = lambda shp: pl.BlockSpec((1, eps) + shp,
                                      lambda i, e, c: (layer, jnp.maximum(e - 1, 0), 0, 0))
    grid_spec = pltpu.PrefetchScalarGridSpec(
        num_scalar_prefetch=1, grid=(T // tm, ne // eps + 1),
        in_specs=[pl.BlockSpec((tm, D), lambda i, e, c: (i, 0)),
                  pl.BlockSpec((tm, D), lambda i, e, c: (i, 0)),
                  pl.BlockSpec((8, tm), lambda i, e, c: (0, i)),
                  const((sub, sub)), const((D, ff)), const((D, ff)), const((ff, D)),
                  expert((D, ff)), expert((D, ff)), expert((ff, D))],
        out_specs=pl.BlockSpec((tm, D), lambda i, e, c: (i, 0)),
        scratch_shapes=[pltpu.VMEM((N_EXPERTS, tm), F32), pltpu.VMEM((N_EXPERTS, tm), F32)])
    return pl.pallas_call(
        _moe_kernel, grid_spec=grid_spec,
        out_shape=jax.ShapeDtypeStruct((T, D), F32),
        compiler_params=_cparams(("arbitrary", "arbitrary")), name="moe",
    )(counts, x1, h2, route, tri, *shared, wg, wu, wd)


def _constants(S):
    half = MLA_ROPE // 2
    inv = ROPE_THETA ** (-np.arange(half, dtype=np.float64) / half)
    ang = np.arange(S, dtype=np.float64)[:, None] * inv[None, :]
    cos = np.ones((S, LANES), np.float32)
    cos[:, MLA_NOPE:MLA_NOPE + half] = np.cos(ang)
    cos[:, MLA_NOPE + half:MLA_QK] = np.cos(ang)
    sin = np.zeros((S, LANES), np.float32)
    sin[:, MLA_NOPE:MLA_NOPE + half] = np.sin(ang)
    sin[:, MLA_NOPE + half:MLA_QK] = np.sin(ang)

    r = np.zeros((LANES, LANES), np.float32)
    for j in range(half):
        r[MLA_NOPE + half + j, MLA_NOPE + j] = -1.0
        r[MLA_NOPE + j, MLA_NOPE + half + j] = 1.0
    g = np.zeros((LANES, LANES), np.float32)
    g[:MLA_QK, :] = 1.0
    eye4 = np.eye(MLA_HEADS, dtype=np.float32)
    pk = np.zeros((LANES, 4 * LANES), np.float32)
    for hh in range(MLA_HEADS):
        for j in range(MLA_ROPE):
            pk[j, hh * LANES + MLA_NOPE + j] = 1.0
    g256 = np.kron(np.eye(MOBA_HEADS, dtype=np.float32), np.ones((MOBA_HD, MOBA_HD), np.float32))

    nkb = S // MOBA_BLOCK
    kpos = np.arange(S)
    kc = np.zeros((S, MOBA_HD), np.float32)
    blk = kpos // MOBA_BLOCK
    m = blk < min(nkb, MOBA_HD) - 1
    kc[kpos[m], blk[m]] = 1.0
    kc[:, MOBA_HD - 1] = kpos % MOBA_BLOCK
    kc4 = np.zeros((S, MOBA_HEADS, 2 * MOBA_HD), np.float32)
    kc4[:, :, MOBA_HD:] = kc[:, None, :]
    place = np.zeros((MOBA_HEADS * MOBA_HD, MOBA_HEADS * 2 * MOBA_HD), np.float32)
    for hh in range(MOBA_HEADS):
        for j in range(MOBA_HD):
            place[hh * MOBA_HD + j, hh * 2 * MOBA_HD + j] = 1.0
    return {
        "cos": jnp.asarray(cos), "sin": jnp.asarray(sin),
        "r512": jnp.asarray(np.kron(eye4, r), BF16), "g512": jnp.asarray(np.kron(eye4, g), BF16),
        "pk": jnp.asarray(pk, BF16), "g256": jnp.asarray(g256, BF16),
        "kc": jnp.asarray(kc4.reshape(S, -1), BF16), "place": jnp.asarray(place, BF16),
    }


def _head_pad(v):
    one = jnp.concatenate([v, jnp.zeros((LANES - MLA_QK,), F32)])
    return jnp.tile(one, MLA_HEADS)[None, :]


def _layer_weights(l, p):
    D = p["w_in"].shape[1]
    w_in = p["w_in"][l]
    o1 = MLA_Q_LORA + MLA_KV_LORA + MLA_ROPE
    o2 = o1 + 2 * SGU_W
    o3 = o2 + POOL_W
    w = jnp.concatenate([w_in[:, :o1], jnp.zeros((D, C_SGU - o1), F32), w_in[:, o1:]], axis=1)

    wq = p["mla_w_q_up"][l].reshape(MLA_Q_LORA, MLA_HEADS, MLA_QK)
    wq = jnp.pad(wq, ((0, 0), (0, 0), (0, LANES - MLA_QK))).reshape(MLA_Q_LORA, MLA_HEADS * LANES)
    wkv = p["mla_w_kv_up"][l].reshape(MLA_KV_LORA, MLA_HEADS, MLA_NOPE + MLA_V)
    wk = jnp.pad(wkv[:, :, :MLA_NOPE], ((0, 0), (0, 0), (0, LANES - MLA_NOPE)))
    wk = wk.reshape(MLA_KV_LORA, MLA_HEADS * LANES)
    wv = wkv[:, :, MLA_NOPE:].reshape(MLA_KV_LORA, MLA_HEADS * MLA_V)

    tri = jnp.tril(jnp.ones((SGU_CHUNK, SGU_CHUNK), dtype=bool))
    sgu_w = jnp.where(tri[None], p["sgu_w"][l], 0.0)
    sgu_b = jnp.repeat(p["sgu_b"][l].T, SGU_W // SGU_GROUPS, axis=1)
    pool_w = jax.scipy.linalg.block_diag(*[p["pool_w"][l, gi] for gi in range(len(POOL_WINDOWS))])

    ne = p["moe_w_gate"].shape[1]
    return {
        "attn_norm": p["attn_norm"][l][None, :], "w_in": w.astype(BF16),
        "mla_q_norm": p["mla_q_norm"][l][None, :], "wq": wq.astype(BF16),
        "mla_kv_norm": p["mla_kv_norm"][l][None, :],
        "wkv": jnp.concatenate([wk, wv], axis=1).astype(BF16),
        "mla_qg": _head_pad(p["mla_q_gain"][l]), "mla_kg": _head_pad(p["mla_k_gain"][l]),
        "sgu_norm": p["sgu_norm"][l][None, :], "sgu_w": sgu_w.astype(BF16), "sgu_b": sgu_b,
        "pool_w": pool_w.astype(BF16), "pool_b": p["pool_b"][l][None, :],
        "pool_scale": p["pool_scale"][l][None, :],
        "moba_qg": jnp.tile(p["moba_q_gain"][l], MOBA_HEADS)[None, :],
        "moba_kg": jnp.tile(p["moba_k_gain"][l], MOBA_HEADS)[None, :],
        "w_gate": p["w_gate"][l].astype(BF16), "w_branch": p["w_branch"][l].astype(BF16),
        "w_out": p["w_out"][l].astype(BF16), "ffn_norm": p["ffn_norm"][l][None, :],
        "shared": (p["shared_w_gate"][l].astype(BF16), p["shared_w_up"][l].astype(BF16),
                   p["shared_w_down"][l].astype(BF16)),
    }


def kernel(x, attn_norm, w_in, mla_q_norm, mla_w_q_up, mla_kv_norm, mla_w_kv_up, mla_q_gain, mla_k_gain, sgu_norm, sgu_w, sgu_b, pool_w, pool_b, pool_scale, moba_q_gain, moba_k_gain, w_gate, w_branch, w_out, ffn_norm, router_w, router_bias, moe_w_gate, moe_w_up, moe_w_down, shared_w_gate, shared_w_up, shared_w_down):
    p = dict(attn_norm=attn_norm, w_in=w_in, mla_q_norm=mla_q_norm, mla_w_q_up=mla_w_q_up,
             mla_kv_norm=mla_kv_norm, mla_w_kv_up=mla_w_kv_up, mla_q_gain=mla_q_gain,
             mla_k_gain=mla_k_gain, sgu_norm=sgu_norm, sgu_w=sgu_w, sgu_b=sgu_b, pool_w=pool_w,
             pool_b=pool_b, pool_scale=pool_scale, moba_q_gain=moba_q_gain,
             moba_k_gain=moba_k_gain, w_gate=w_gate, w_branch=w_branch, w_out=w_out,
             ffn_norm=ffn_norm, moe_w_gate=moe_w_gate, moe_w_up=moe_w_up, moe_w_down=moe_w_down,
             shared_w_gate=shared_w_gate, shared_w_up=shared_w_up, shared_w_down=shared_w_down)
    B, S, D = x.shape
    T = B * S
    depth = attn_norm.shape[0]
    nkb = S // MOBA_BLOCK
    assert TQ == TK and S % TK == 0 and T % TM_MERGE == 0 and nkb <= MOBA_HD
    consts = _constants(S)
    router_w_t = router_w.T
    router_b = router_bias[:, None]
    assert min(TM_MOE, T) % TM_MERGE == 0 and N_EXPERTS % MOE_EXPERTS_PER_STEP == 0
    tri = jnp.asarray(np.triu(np.ones((MOE_SUB, MOE_SUB), np.float32), 1), BF16)

    ffn_w = (moe_w_gate.astype(BF16), moe_w_up.astype(BF16), moe_w_down.astype(BF16))
    x2 = x.reshape(T, D)
    for l in range(depth):
        lw = _layer_weights(l, p)
        qa, ka, va, yb, yc, qd, kd, vd, km = _in_proj(x2, lw, consts, S)
        ya = _flash(qa, ka.reshape(B, S // TK, TK, -1), va, MLA_HEADS, True, "mla_attn")

        kmean = km.reshape(B, nkb, MOBA_HEADS, MOBA_HD).transpose(0, 2, 1, 3)
        kmean = jnp.pad(kmean, ((0, 0), (0, 0), (0, MOBA_HD - nkb), (0, 0)))
        q_aug = _moba_select(qd, kmean)
        yd = _flash(q_aug, kd.reshape(B, S // TK, TK, -1), vd, MOBA_HEADS, False, "moba_attn")

        x1, h2, route, cnt = _merge(x2, ya, yb, yc, yd, lw, router_w_t, router_b)
        counts = cnt[:, :, :TM_MERGE // MOE_SUB].transpose(0, 2, 1).reshape(-1)
        x2 = _moe(x1, h2, route, counts, tri, lw["shared"], l, *ffn_w)
    return x2.reshape(B, S, D)
```

```python
import functools
import math

import jax
import jax.numpy as jnp
import numpy as np
from jax import lax
from jax.experimental import pallas as pl
from jax.experimental.pallas import tpu as pltpu

F32 = jnp.float32
BF16 = jnp.bfloat16
EPS = 1e-6
NEG = -1e30

LANES = 128
VMEM_LIMIT = 56 * 1024 * 1024

MLA_HEADS = 4
MLA_NOPE = 64
MLA_ROPE = 32
MLA_V = 64
MLA_Q_LORA = 256
MLA_KV_LORA = 128
ROPE_THETA = 10000.0
MLA_QK = MLA_NOPE + MLA_ROPE

SGU_GROUPS = 4
SGU_CHUNK = 128
SGU_W = 256

POOL_WINDOWS = (2, 4, 8, 16)
POOL_MAXW = 16
POOL_W = 256

MOBA_HEADS = 4
MOBA_HD = 64
MOBA_BLOCK = 256
MOBA_TOPK = 3

N_EXPERTS = 16
N_GROUPS = 4
EPG = 4

C_MLA = 0
C_SGU = 512
C_POOL = 1024
C_MOBA = 1280
C_END = 2048

TQ = 512
TK = 512
TM_IN = TQ
TM_MERGE = 512
TM_MOE = 2048
SEL_L = 2048
FLASH_UNROLLS = (4, 2, 1)
MOE_EXPERTS_PER_STEP = 4
MOE_SHARED_ROWS = 512
MOE_SUB = 256
MOE_CHUNK = 48
ONES_ROWS = 16


def _cparams(sem):
    return pltpu.CompilerParams(dimension_semantics=sem, vmem_limit_bytes=VMEM_LIMIT)


def _full(shape):
    n = len(shape)
    return pl.BlockSpec(shape, lambda *_: (0,) * n, pipeline_mode=pl.Buffered(1))


def _rms(x, g):
    return x * lax.rsqrt(jnp.mean(x * x, axis=-1, keepdims=True) + EPS) * g


def _dot(a, b):
    return jnp.dot(a, b, preferred_element_type=F32)


def _in_proj_kernel(x_ref, g_ref, w_ref, qn_ref, wq_ref, kvn_ref, wkv_ref, pk_ref,
                    qg_ref, kg_ref, g512_ref, r512_ref, cos_ref, sin_ref,
                    sn_ref, sw_ref, sb_ref, pw_ref, pb_ref, ps_ref,
                    mqg_ref, mkg_ref, g256_ref, pl_ref, kc_ref,
                    qa_ref, ka_ref, va_ref, yb_ref, yc_ref, qd_ref, kd_ref, vd_ref, km_ref,
                    halo_ref, buf_ref, *, tiles_per_seq, q_scale):
    tm = x_ref.shape[0]
    i = pl.program_id(0)
    x = x_ref[...]
    h = _rms(x, g_ref[...]).astype(BF16)

    cm = _dot(h, w_ref[:, C_MLA:C_SGU])
    c_q = _rms(cm[:, :MLA_Q_LORA], qn_ref[...]).astype(BF16)
    c_kv = _rms(cm[:, MLA_Q_LORA:MLA_Q_LORA + MLA_KV_LORA], kvn_ref[...]).astype(BF16)
    kpe_grp = cm[:, MLA_Q_LORA + MLA_KV_LORA:].astype(BF16)
    q = _dot(c_q, wq_ref[...])
    kv = _dot(c_kv, wkv_ref[...])
    k = kv[:, :512] + _dot(kpe_grp, pk_ref[...])
    va_ref[0, 0] = kv[:, 512:].T.astype(BF16)
    cos = jnp.concatenate([cos_ref[...]] * MLA_HEADS, axis=1)
    sin = jnp.concatenate([sin_ref[...]] * MLA_HEADS, axis=1)
    inv_d = 1.0 / MLA_QK

    def head_norm_rope(t, gain):
        ms = _dot((t * t).astype(BF16), g512_ref[...]) * inv_d
        t = t * lax.rsqrt(ms + EPS) * gain
        rot = _dot(t.astype(BF16), r512_ref[...])
        return t * cos + rot * sin

    qa_ref[0, 0] = (head_norm_rope(q, qg_ref[...]) * q_scale).T.astype(BF16)
    ka_ref[...] = head_norm_rope(k, kg_ref[...]).astype(BF16)

    z = jax.nn.gelu(_dot(h, w_ref[:, C_SGU:C_POOL]))
    u = z[:, :SGU_W]
    vv = _rms(z[:, SGU_W:], sn_ref[...]).astype(BF16)
    lane_grp = lax.broadcasted_iota(jnp.int32, (SGU_CHUNK, SGU_W), 1) // (SGU_W // SGU_GROUPS)
    for c in range(tm // SGU_CHUNK):
        vc = vv[c * SGU_CHUNK:(c + 1) * SGU_CHUNK, :]
        y = sb_ref[...]
        for gi in range(SGU_GROUPS):
            y = y + jnp.where(lane_grp == gi, _dot(sw_ref[gi], vc), 0.0)
        yb_ref[c * SGU_CHUNK:(c + 1) * SGU_CHUNK, :] = (
            u[c * SGU_CHUNK:(c + 1) * SGU_CHUNK, :] * y).astype(BF16)

    xp = _dot(h, w_ref[:, C_POOL:C_MOBA])
    cd = _dot(h, w_ref[:, C_MOBA:C_END])

    @pl.when(i % tiles_per_seq == 0)
    def _():
        halo_ref[...] = jnp.zeros_like(halo_ref)

    buf_ref[0:POOL_MAXW, :] = halo_ref[...]
    buf_ref[POOL_MAXW:, :] = xp
    halo_ref[...] = xp[tm - POOL_MAXW:, :]
    pos = (i % tiles_per_seq) * tm + lax.broadcasted_iota(jnp.int32, (tm, POOL_W), 0)
    lane_w = lax.broadcasted_iota(jnp.int32, (tm, POOL_W), 1) // (POOL_W // len(POOL_WINDOWS))
    acc = xp
    pooled = jnp.zeros((tm, POOL_W), F32)
    shift = 1
    for gi, w in enumerate(POOL_WINDOWS):
        while shift < w:
            acc = acc + buf_ref[POOL_MAXW - shift:POOL_MAXW - shift + tm, :]
            shift += 1
        cnt = jnp.minimum(pos + 1, w).astype(F32)
        pooled = jnp.where(lane_w == gi, acc / cnt, pooled)
    p = (pooled - xp).astype(BF16)
    yc_ref[...] = ((_dot(p, pw_ref[...]) + pb_ref[...]) * ps_ref[...]).astype(BF16)

    inv_hd = 1.0 / MOBA_HD

    def head_norm(t, gain):
        ms = _dot((t * t).astype(BF16), g256_ref[...]) * inv_hd
        return t * lax.rsqrt(ms + EPS) * gain

    qd_ref[0] = head_norm(cd[:, :256], mqg_ref[...]).T
    kd = head_norm(cd[:, 256:512], mkg_ref[...])
    kd_ref[...] = (_dot(kd.astype(BF16), pl_ref[...]) + kc_ref[...]).astype(BF16)
    vd_ref[0, 0] = cd[:, 512:].T.astype(BF16)
    for c in range(tm // MOBA_BLOCK):
        km_ref[c] = jnp.mean(kd[c * MOBA_BLOCK:(c + 1) * MOBA_BLOCK, :], axis=0, keepdims=True)


def _in_proj(x2, lw, consts, S):
    T, D = x2.shape
    tm = TM_IN
    nt = T // tm
    tok = lambda w: pl.BlockSpec((tm, w), lambda i: (i, 0))
    pos_spec = pl.BlockSpec((tm, LANES), lambda i: (i % (S // tm), 0))
    ins = [
        (x2, tok(D)), (lw["attn_norm"], None), (lw["w_in"], None),
        (lw["mla_q_norm"], None), (lw["wq"], None), (lw["mla_kv_norm"], None), (lw["wkv"], None),
        (consts["pk"], None), (lw["mla_qg"], None), (lw["mla_kg"], None),
        (consts["g512"], None), (consts["r512"], None),
        (consts["cos"], pos_spec), (consts["sin"], pos_spec),
        (lw["sgu_norm"], None), (lw["sgu_w"], None), (lw["sgu_b"], None),
        (lw["pool_w"], None), (lw["pool_b"], None), (lw["pool_scale"], None),
        (lw["moba_qg"], None), (lw["moba_kg"], None), (consts["g256"], None),
        (consts["place"], None),
        (consts["kc"], pl.BlockSpec((tm, 4 * LANES), lambda i: (i % (S // tm), 0))),
    ]
    arrays = [a for a, _ in ins]
    specs = [s if s is not None else _full(a.shape) for a, s in ins]
    nb = tm // MOBA_BLOCK
    B = T // S
    ns = S // tm
    tr = lambda f: pl.BlockSpec((1, 1, f, tm), lambda i: (i // ns, i % ns, 0, 0))
    out_shape = [
        jax.ShapeDtypeStruct((B, ns, 512, tm), BF16), jax.ShapeDtypeStruct((T, 512), BF16),
        jax.ShapeDtypeStruct((B, ns, 256, tm), BF16), jax.ShapeDtypeStruct((T, 256), BF16),
        jax.ShapeDtypeStruct((T, 256), BF16), jax.ShapeDtypeStruct((B, 256, S), F32),
        jax.ShapeDtypeStruct((T, 512), BF16), jax.ShapeDtypeStruct((B, ns, 256, tm), BF16),
        jax.ShapeDtypeStruct((T // MOBA_BLOCK, 1, 256), F32),
    ]
    out_specs = [tr(512), tok(512), tr(256), tok(256), tok(256),
                 pl.BlockSpec((1, 256, tm), lambda i: (i // ns, 0, i % ns)),
                 tok(512), tr(256), pl.BlockSpec((nb, 1, 256), lambda i: (i, 0, 0))]
    q_scale = (MLA_QK ** -0.5) * math.log2(math.e)
    return pl.pallas_call(
        functools.partial(_in_proj_kernel, tiles_per_seq=S // tm, q_scale=q_scale),
        grid=(nt,), in_specs=specs, out_specs=out_specs, out_shape=out_shape,
        scratch_shapes=[pltpu.VMEM((POOL_MAXW, POOL_W), F32),
                        pltpu.VMEM((tm + POOL_MAXW, POOL_W), F32)],
        compiler_params=_cparams(("arbitrary",)), name="in_proj",
    )(*arrays)


def _moba_select_kernel(q_ref, km_ref, o_ref, *, slope_log2_step):
    hh = pl.program_id(1)
    li = pl.program_id(2)
    q = q_ref[0]
    L = q.shape[1]
    nrow = km_ref.shape[2]
    g = jnp.dot(km_ref[0, 0], q, preferred_element_type=F32, precision=lax.Precision.HIGHEST)
    n_i = lax.broadcasted_iota(jnp.int32, (nrow, L), 0)
    own = (li * L + lax.broadcasted_iota(jnp.int32, (nrow, L), 1)) // MOBA_BLOCK
    g = jnp.where(n_i < own, g, NEG)
    sel = jnp.zeros((nrow, L), F32)
    for _ in range(MOBA_TOPK):
        mx = jnp.max(g, axis=0, keepdims=True)
        first = jnp.min(jnp.where(g == mx, n_i, nrow), axis=0, keepdims=True)
        hit = n_i == first
        sel = jnp.where(hit & (mx > 0.5 * NEG), 1.0, sel)
        g = jnp.where(hit, NEG, g)
    slope = jnp.exp2(-slope_log2_step * (hh + 1).astype(F32))
    rel = (n_i - own).astype(F32) * (slope * MOBA_BLOCK)
    bias = jnp.where((sel > 0.0) | (n_i == own), rel, NEG)
    bias = jnp.where(n_i == nrow - 1, slope, bias)
    qs = (q * (MOBA_HD ** -0.5)).astype(BF16)
    bias = bias.astype(BF16)
    tq = o_ref.shape[3]
    for t in range(L // tq):
        o_ref[0, t, 0:MOBA_HD, :] = qs[:, t * tq:(t + 1) * tq]
        o_ref[0, t, MOBA_HD:, :] = bias[:, t * tq:(t + 1) * tq]


def _moba_select(qgT, kmean):
    B, _, S = qgT.shape
    H, hd = MOBA_HEADS, MOBA_HD
    L = min(SEL_L, S)
    return pl.pallas_call(
        functools.partial(_moba_select_kernel, slope_log2_step=8.0 / MOBA_HEADS),
        grid=(B, H, S // L),
        in_specs=[pl.BlockSpec((1, hd, L), lambda b, h, l: (b, h, l)),
                  pl.BlockSpec((1, 1, 64, hd), lambda b, h, l: (b, h, 0, 0))],
        out_specs=pl.BlockSpec((1, L // TQ, 2 * hd, TQ), lambda b, h, l: (b, l, h, 0)),
        out_shape=jax.ShapeDtypeStruct((B, S // TQ, H * 2 * hd, TQ), BF16),
        compiler_params=_cparams(("arbitrary",) * 3), name="moba_select",
    )(qgT, kmean)


def _flash_kernel(q_ref, k_ref, v_ref, o_ref, s_scr, acc_scr, *, heads, use_exp2):
    H = heads
    tq = q_ref.shape[3]
    tk = k_ref.shape[2]
    dk = q_ref.shape[2] // H
    dv = v_ref.shape[2] // H
    ratio = tk // tq
    ex = jnp.exp2 if use_exp2 else jnp.exp
    i = pl.program_id(1)
    jd = i // ratio
    kpos = jd * tk + lax.broadcasted_iota(jnp.int32, (tk, tq), 0)
    qpos = i * tq + lax.broadcasted_iota(jnp.int32, (tk, tq), 1)
    causal = kpos <= qpos
    ones = jnp.ones((ONES_ROWS, tk), BF16)

    def scores(jt, h, masked):
        s = _dot(k_ref[0, jt, :, h * dk:(h + 1) * dk], q_ref[0, 0, h * dk:(h + 1) * dk, :])
        if masked:
            s = jnp.where(causal, s, NEG)
        s_scr[h] = s
        return jnp.max(s, axis=0, keepdims=True)

    def update(jt, h, cmax, m):
        m_new = jnp.maximum(m, cmax)
        alpha = ex(m - m_new)
        p = ex(s_scr[h] - m_new).astype(BF16)
        vt = jnp.concatenate([v_ref[0, jt, h * dv:(h + 1) * dv, :], ones], axis=0)
        acc_scr[h] = acc_scr[h] * alpha + _dot(vt, p)
        return m_new

    acc_scr[...] = jnp.zeros_like(acc_scr)
    ms = [jnp.full((1, tq), NEG, F32) for _ in range(H)]
    cmax = scores(jd, 0, True)
    for h in range(H):
        nxt = scores(jd, h + 1, True) if h + 1 < H else scores(0, 0, False)
        ms[h] = update(jd, h, cmax, ms[h])
        cmax = nxt

    def tile_step(j, carry):
        ms, cmax = carry
        ms = list(ms)
        jn = jnp.minimum(j + 1, jd - 1)
        for h in range(H):
            nxt = scores(j, h + 1, False) if h + 1 < H else scores(jn, 0, False)
            ms[h] = update(j, h, cmax, ms[h])
            cmax = nxt
        return tuple(ms), cmax

    carry = (tuple(ms), cmax)
    done = 0
    for unroll in FLASH_UNROLLS:
        def body(t, c, unroll=unroll, done=done):
            for u in range(unroll):
                c = tile_step(done + unroll * t + u, c)
            return c

        trips = (jd - done) // unroll
        carry = lax.fori_loop(0, trips, body, carry)
        done = done + trips * unroll
    outs = []
    for h in range(H):
        acc = acc_scr[h]
        outs.append(acc[:dv] / acc[dv:dv + 1])
    o_ref[...] = jnp.concatenate(outs, axis=0).T.astype(BF16)


def _flash(qT, k, vT, heads, use_exp2, name):
    B, nq, hdk, tq = qT.shape
    nk, tk = k.shape[1:3]
    hdv = vT.shape[2]
    dva = hdv // heads + ONES_ROWS
    whole = lambda shp: pl.BlockSpec((1,) + shp, lambda b, i: (b, 0, 0, 0),
                                     pipeline_mode=pl.Buffered(1))
    return pl.pallas_call(
        functools.partial(_flash_kernel, heads=heads, use_exp2=use_exp2),
        grid=(B, nq),
        in_specs=[pl.BlockSpec((1, 1, hdk, tq), lambda b, i: (b, i, 0, 0)),
                  whole((nk, tk, hdk)), whole((nk, hdv, tk))],
        out_specs=pl.BlockSpec((tq, hdv), lambda b, i: (b * nq + i, 0)),
        out_shape=jax.ShapeDtypeStruct((B * nq * tq, hdv), BF16),
        scratch_shapes=[pltpu.VMEM((heads, tk, tq), F32), pltpu.VMEM((heads, dva, tq), F32)],
        compiler_params=_cparams(("arbitrary", "arbitrary")), name=name,
    )(qT, k, vT)


def _merge_kernel(x_ref, ya_ref, yb_ref, yc_ref, yd_ref, g_ref, wg_ref, wb_ref, wo_ref,
                  fg_ref, rw_ref, rb_ref, x1_ref, h2_ref, rt_ref, cnt_ref, h2_scr):
    @pl.when(pl.program_id(0) == 0)
    def _():
        h2_scr[...] = jnp.zeros_like(h2_scr)

    router = _router_stages(h2_scr, rw_ref, rb_ref, rt_ref, cnt_ref)
    x = x_ref[...]
    h = _rms(x, g_ref[...]).astype(BF16)
    merged = None
    for bi, y_ref in enumerate((ya_ref, yb_ref, yc_ref, yd_ref)):
        t = jax.nn.sigmoid(_dot(h, wg_ref[bi])) * _dot(y_ref[...], wb_ref[bi])
        merged = t if merged is None else merged + t
        next(router)
    x1 = x + _dot(merged.astype(BF16), wo_ref[...])
    x1_ref[...] = x1
    h2_new = _rms(x1, fg_ref[...])
    h2_ref[...] = h2_new.astype(BF16)
    for _ in router:
        pass
    h2_scr[...] = h2_new


def _router_stages(h2_scr, rw_ref, rb_ref, rt_ref, cnt_ref):
    h2 = h2_scr[...]
    nt = (((1,), (1,)), ((), ()))
    h2_hi = h2.astype(BF16)
    h2_lo = (h2 - h2_hi.astype(F32)).astype(BF16)
    rw = rw_ref[...]
    rw_hi = rw.astype(BF16)
    rw_lo = (rw - rw_hi.astype(F32)).astype(BF16)
    logits = (lax.dot_general(rw_hi, h2_hi, nt, preferred_element_type=F32)
              + lax.dot_general(rw_hi, h2_lo, nt, preferred_element_type=F32)
              + lax.dot_general(rw_lo, h2_hi, nt, preferred_element_type=F32))
    yield
    scores = jax.nn.sigmoid(logits)
    biased = scores + rb_ref[...]
    s_rows = [scores[e:e + 1, :] for e in range(N_EXPERTS)]
    b_rows = [biased[e:e + 1, :] for e in range(N_EXPERTS)]

    def top2(vals):
        m1 = vals[0]
        i1 = jnp.zeros_like(m1, dtype=jnp.int32)
        for j in range(1, len(vals)):
            better = vals[j] > m1
            i1 = jnp.where(better, j, i1)
            m1 = jnp.where(better, vals[j], m1)
        m2 = None
        i2 = None
        for j in range(len(vals)):
            cand = jnp.where(i1 == j, NEG, vals[j])
            if m2 is None:
                m2, i2 = cand, jnp.zeros_like(i1)
            else:
                better = cand > m2
                i2 = jnp.where(better, j, i2)
                m2 = jnp.where(better, cand, m2)
        return m1, i1, m2, i2

    best = None
    for gi in range(N_GROUPS):
        bm1, bi1, bm2, bi2 = top2(b_rows[gi * EPG:(gi + 1) * EPG])
        gs = bm1 + bm2
        e1 = bi1 + gi * EPG
        e2 = bi2 + gi * EPG
        if best is None:
            best = (gs, e1, e2)
        else:
            better = gs > best[0]
            best = (jnp.where(better, gs, best[0]), jnp.where(better, e1, best[1]),
                    jnp.where(better, e2, best[2]))
        if gi % 2 == 1:
            yield
    _, e1, e2 = best
    w1 = jnp.zeros_like(s_rows[0])
    w2 = jnp.zeros_like(s_rows[0])
    for e in range(N_EXPERTS):
        w1 = jnp.where(e1 == e, s_rows[e], w1)
        w2 = jnp.where(e2 == e, s_rows[e], w2)
    tot = w1 + w2
    w1 = w1 / tot
    w2 = w2 / tot
    r8 = lax.broadcasted_iota(jnp.int32, (8, scores.shape[1]), 0)
    rt_ref[...] = jnp.where(r8 == 0, e1.astype(F32), jnp.where(r8 == 1, e2.astype(F32),
                            jnp.where(r8 == 2, w1, jnp.where(r8 == 3, w2, 0.0))))
    row = lax.broadcasted_iota(jnp.int32, scores.shape, 0)
    assigned = jnp.where(row == e1, 1.0, jnp.where(row == e2, 1.0, 0.0))
    lane = lax.broadcasted_iota(jnp.int32, (N_EXPERTS, LANES), 1)
    cnt = jnp.zeros((N_EXPERTS, LANES), F32)
    for k in range(scores.shape[1] // MOE_SUB):
        part = jnp.sum(assigned[:, k * MOE_SUB:(k + 1) * MOE_SUB], axis=1, keepdims=True)
        cnt = jnp.where(lane == k, part, cnt)
    cnt_ref[0] = cnt.astype(jnp.int32)
    yield


def _merge(x2, ya, yb, yc, yd, lw, router_w_t, router_b):
    T, D = x2.shape
    tm = TM_MERGE
    n = T // tm
    tok = lambda w: pl.BlockSpec((tm, w), lambda i: (jnp.minimum(i, n - 1), 0))
    consts = [lw["attn_norm"], lw["w_gate"], lw["w_branch"], lw["w_out"], lw["ffn_norm"],
              router_w_t, router_b]
    return pl.pallas_call(
        _merge_kernel, grid=(n + 1,),
        in_specs=[tok(D), tok(256), tok(256), tok(256), tok(256)] + [_full(a.shape) for a in consts],
        out_specs=[tok(D), tok(D), pl.BlockSpec((8, tm), lambda i: (0, jnp.maximum(i - 1, 0))),
                   pl.BlockSpec((1, N_EXPERTS, LANES), lambda i: (jnp.maximum(i - 1, 0), 0, 0))],
        out_shape=[jax.ShapeDtypeStruct((T, D), F32), jax.ShapeDtypeStruct((T, D), BF16),
                   jax.ShapeDtypeStruct((8, T), F32),
                   jax.ShapeDtypeStruct((n, N_EXPERTS, LANES), jnp.int32)],
        scratch_shapes=[pltpu.VMEM((tm, D), F32)],
        compiler_params=_cparams(("arbitrary",)), name="merge",
    )(x2, ya, yb, yc, yd, *consts)


def _moe_kernel(cnt_ref, x1_ref, h2_ref, rt_ref, tri_ref, sg_ref, su_ref, sd_ref,
                wg_ref, wu_ref, wd_ref, o_ref, rank_scr, gate_scr):
    i = pl.program_id(0)
    e = pl.program_id(1)
    tm = h2_ref.shape[0]

    def ffn(rows, scale, wg, wu, wd):
        a = jax.nn.silu(_dot(rows, wg)) * _dot(rows, wu)
        if scale is not None:
            a = a * scale
        return _dot(a.astype(BF16), wd)

    sub = tri_ref.shape[0]
    nsub = tm // sub
    spans = [slice(k * sub, (k + 1) * sub) for k in range(nsub)]

    @pl.when(e == 0)
    def _():
        for r in range(0, tm, MOE_SHARED_ROWS):
            rs = slice(r, r + MOE_SHARED_ROWS)
            o_ref[rs, :] = x1_ref[rs, :] + ffn(h2_ref[rs, :], None, sg_ref[...], su_ref[...],
                                               sd_ref[...])
        row = lax.broadcasted_iota(jnp.int32, (N_EXPERTS, tm), 0).astype(F32)
        e1, e2 = rt_ref[0:1, :], rt_ref[1:2, :]
        w1, w2 = rt_ref[2:3, :], rt_ref[3:4, :]
        assigned = jnp.where(row == e1, 1.0, jnp.where(row == e2, 1.0, 0.0))
        for sp in spans:
            before = _dot(assigned[:, sp].astype(BF16), tri_ref[...])
            rank_scr[:, sp] = jnp.where(assigned[:, sp] > 0.0, before, -1.0)
        gate_scr[...] = jnp.where(row == e1, w1, jnp.where(row == e2, w2, 0.0))

    @pl.when(e > 0)
    def _():
        subs = range(MOE_EXPERTS_PER_STEP)
        ex = [(e - 1) * MOE_EXPERTS_PER_STEP + s for s in subs]
        cnt = [cnt_ref[(i * nsub + k) * N_EXPERTS + x] for x in ex for k in range(nsub)]
        rank_row = [rank_scr[pl.ds(x, 1), :] for x in ex]
        gate_row = [gate_scr[pl.ds(x, 1), :] for x in ex]
        slot = lax.broadcasted_iota(jnp.int32, (MOE_CHUNK, sub), 0).astype(F32)

        def chunk(c, carry):
            base = slot + (c * MOE_CHUNK).astype(F32)
            onehot, rows, scale = [], [], []
            for sp in spans:
                hit = [rank_row[s][:, sp] == base for s in subs]
                onehot.append(jnp.concatenate(
                    [jnp.where(h, 1.0, 0.0).astype(BF16) for h in hit], axis=0))
                rows.append(_dot(onehot[-1], h2_ref[sp, :]).astype(BF16))
                scale.append([jnp.sum(jnp.where(hit[s], gate_row[s][:, sp], 0.0),
                                      axis=1, keepdims=True) for s in subs])
            y = []
            for s in subs:
                mine = slice(s * MOE_CHUNK, (s + 1) * MOE_CHUNK)
                y.append(ffn(jnp.concatenate([r[mine] for r in rows], axis=0),
                             jnp.concatenate([sc[s] for sc in scale], axis=0),
                             wg_ref[0, s], wu_ref[0, s], wd_ref[0, s]).astype(BF16))
            for k, sp in enumerate(spans):
                rows_k = slice(k * MOE_CHUNK, (k + 1) * MOE_CHUNK)
                o_ref[sp, :] += lax.dot_general(
                    onehot[k], jnp.concatenate([y[s][rows_k] for s in subs], axis=0),
                    (((0,), (0,)), ((), ())), preferred_element_type=F32)
            return carry

        most = functools.reduce(jnp.maximum, cnt)
        lax.fori_loop(0, (most + MOE_CHUNK - 1) // MOE_CHUNK, chunk, 0)


def _moe(x1, h2, route, counts, tri, shared, layer, wg, wu, wd):
    T, D = x1.shape
    tm = min(TM_MOE, T)
    sub = tri.shape[0]
    _, ne, _, ff = wg.shape
    const = lambda shp: pl.BlockSpec(shp, lambda i, e, c: (0, 0), pipeline_mode=pl.Buffered(1))
    eps = MOE_EXPERTS_PER_STEP
    expert = lambda shp: pl.BlockSpec((1, eps) + shp,
                                      lambda i, e, c: (layer, jnp.maximum(e - 1, 0), 0, 0))
    grid_spec = pltpu.PrefetchScalarGridSpec(
        num_scalar_prefetch=1, grid=(T // tm, ne // eps + 1),
        in_specs=[pl.BlockSpec((tm, D), lambda i, e, c: (i, 0), pipeline_mode=pl.Buffered(1)),
                  pl.BlockSpec((tm, D), lambda i, e, c: (i, 0), pipeline_mode=pl.Buffered(1)),
                  pl.BlockSpec((8, tm), lambda i, e, c: (0, i)),
                  const((sub, sub)), const((D, ff)), const((D, ff)), const((ff, D)),
                  expert((D, ff)), expert((D, ff)), expert((ff, D))],
        out_specs=pl.BlockSpec((tm, D), lambda i, e, c: (i, 0), pipeline_mode=pl.Buffered(1)),
        scratch_shapes=[pltpu.VMEM((N_EXPERTS, tm), F32), pltpu.VMEM((N_EXPERTS, tm), F32)])
    return pl.pallas_call(
        _moe_kernel, grid_spec=grid_spec,
        out_shape=jax.ShapeDtypeStruct((T, D), F32),
        compiler_params=_cparams(("arbitrary", "arbitrary")), name="moe",
    )(counts, x1, h2, route, tri, *shared, wg, wu, wd)


def _constants(S):
    half = MLA_ROPE // 2
    inv = ROPE_THETA ** (-np.arange(half, dtype=np.float64) / half)
    ang = np.arange(S, dtype=np.float64)[:, None] * inv[None, :]
    cos = np.ones((S, LANES), np.float32)
    cos[:, MLA_NOPE:MLA_NOPE + half] = np.cos(ang)
    cos[:, MLA_NOPE + half:MLA_QK] = np.cos(ang)
    sin = np.zeros((S, LANES), np.float32)
    sin[:, MLA_NOPE:MLA_NOPE + half] = np.sin(ang)
    sin[:, MLA_NOPE + half:MLA_QK] = np.sin(ang)

    r = np.zeros((LANES, LANES), np.float32)
    for j in range(half):
        r[MLA_NOPE + half + j, MLA_NOPE + j] = -1.0
        r[MLA_NOPE + j, MLA_NOPE + half + j] = 1.0
    g = np.zeros((LANES, LANES), np.float32)
    g[:MLA_QK, :] = 1.0
    eye4 = np.eye(MLA_HEADS, dtype=np.float32)
    pk = np.zeros((LANES, 4 * LANES), np.float32)
    for hh in range(MLA_HEADS):
        for j in range(MLA_ROPE):
            pk[j, hh * LANES + MLA_NOPE + j] = 1.0
    g256 = np.kron(np.eye(MOBA_HEADS, dtype=np.float32), np.ones((MOBA_HD, MOBA_HD), np.float32))

    nkb = S // MOBA_BLOCK
    kpos = np.arange(S)
    kc = np.zeros((S, MOBA_HD), np.float32)
    blk = kpos // MOBA_BLOCK
    m = blk < min(nkb, MOBA_HD) - 1
    kc[kpos[m], blk[m]] = 1.0
    kc[:, MOBA_HD - 1] = kpos % MOBA_BLOCK
    kc4 = np.zeros((S, MOBA_HEADS, 2 * MOBA_HD), np.float32)
    kc4[:, :, MOBA_HD:] = kc[:, None, :]
    place = np.zeros((MOBA_HEADS * MOBA_HD, MOBA_HEADS * 2 * MOBA_HD), np.float32)
    for hh in range(MOBA_HEADS):
        for j in range(MOBA_HD):
            place[hh * MOBA_HD + j, hh * 2 * MOBA_HD + j] = 1.0
    return {
        "cos": jnp.asarray(cos), "sin": jnp.asarray(sin),
        "r512": jnp.asarray(np.kron(eye4, r), BF16), "g512": jnp.asarray(np.kron(eye4, g), BF16),
        "pk": jnp.asarray(pk, BF16), "g256": jnp.asarray(g256, BF16),
        "kc": jnp.asarray(kc4.reshape(S, -1), BF16), "place": jnp.asarray(place, BF16),
    }


def _head_pad(v):
    one = jnp.concatenate([v, jnp.zeros((LANES - MLA_QK,), F32)])
    return jnp.tile(one, MLA_HEADS)[None, :]


def _layer_weights(l, p):
    D = p["w_in"].shape[1]
    w_in = p["w_in"][l]
    o1 = MLA_Q_LORA + MLA_KV_LORA + MLA_ROPE
    o2 = o1 + 2 * SGU_W
    o3 = o2 + POOL_W
    w = jnp.concatenate([w_in[:, :o1], jnp.zeros((D, C_SGU - o1), F32), w_in[:, o1:]], axis=1)

    wq = p["mla_w_q_up"][l].reshape(MLA_Q_LORA, MLA_HEADS, MLA_QK)
    wq = jnp.pad(wq, ((0, 0), (0, 0), (0, LANES - MLA_QK))).reshape(MLA_Q_LORA, MLA_HEADS * LANES)
    wkv = p["mla_w_kv_up"][l].reshape(MLA_KV_LORA, MLA_HEADS, MLA_NOPE + MLA_V)
    wk = jnp.pad(wkv[:, :, :MLA_NOPE], ((0, 0), (0, 0), (0, LANES - MLA_NOPE)))
    wk = wk.reshape(MLA_KV_LORA, MLA_HEADS * LANES)
    wv = wkv[:, :, MLA_NOPE:].reshape(MLA_KV_LORA, MLA_HEADS * MLA_V)

    tri = jnp.tril(jnp.ones((SGU_CHUNK, SGU_CHUNK), dtype=bool))
    sgu_w = jnp.where(tri[None], p["sgu_w"][l], 0.0)
    sgu_b = jnp.repeat(p["sgu_b"][l].T, SGU_W // SGU_GROUPS, axis=1)
    pool_w = jax.scipy.linalg.block_diag(*[p["pool_w"][l, gi] for gi in range(len(POOL_WINDOWS))])

    ne = p["moe_w_gate"].shape[1]
    return {
        "attn_norm": p["attn_norm"][l][None, :], "w_in": w.astype(BF16),
        "mla_q_norm": p["mla_q_norm"][l][None, :], "wq": wq.astype(BF16),
        "mla_kv_norm": p["mla_kv_norm"][l][None, :],
        "wkv": jnp.concatenate([wk, wv], axis=1).astype(BF16),
        "mla_qg": _head_pad(p["mla_q_gain"][l]), "mla_kg": _head_pad(p["mla_k_gain"][l]),
        "sgu_norm": p["sgu_norm"][l][None, :], "sgu_w": sgu_w.astype(BF16), "sgu_b": sgu_b,
        "pool_w": pool_w.astype(BF16), "pool_b": p["pool_b"][l][None, :],
        "pool_scale": p["pool_scale"][l][None, :],
        "moba_qg": jnp.tile(p["moba_q_gain"][l], MOBA_HEADS)[None, :],
        "moba_kg": jnp.tile(p["moba_k_gain"][l], MOBA_HEADS)[None, :],
        "w_gate": p["w_gate"][l].astype(BF16), "w_branch": p["w_branch"][l].astype(BF16),
        "w_out": p["w_out"][l].astype(BF16), "ffn_norm": p["ffn_norm"][l][None, :],
        "shared": (p["shared_w_gate"][l].astype(BF16), p["shared_w_up"][l].astype(BF16),
                   p["shared_w_down"][l].astype(BF16)),
    }


def kernel(x, attn_norm, w_in, mla_q_norm, mla_w_q_up, mla_kv_norm, mla_w_kv_up, mla_q_gain, mla_k_gain, sgu_norm, sgu_w, sgu_b, pool_w, pool_b, pool_scale, moba_q_gain, moba_k_gain, w_gate, w_branch, w_out, ffn_norm, router_w, router_bias, moe_w_gate, moe_w_up, moe_w_down, shared_w_gate, shared_w_up, shared_w_down):
    p = dict(attn_norm=attn_norm, w_in=w_in, mla_q_norm=mla_q_norm, mla_w_q_up=mla_w_q_up,
             mla_kv_norm=mla_kv_norm, mla_w_kv_up=mla_w_kv_up, mla_q_gain=mla_q_gain,
             mla_k_gain=mla_k_gain, sgu_norm=sgu_norm, sgu_w=sgu_w, sgu_b=sgu_b, pool_w=pool_w,
             pool_b=pool_b, pool_scale=pool_scale, moba_q_gain=moba_q_gain,
             moba_k_gain=moba_k_gain, w_gate=w_gate, w_branch=w_branch, w_out=w_out,
             ffn_norm=ffn_norm, moe_w_gate=moe_w_gate, moe_w_up=moe_w_up, moe_w_down=moe_w_down,
             shared_w_gate=shared_w_gate, shared_w_up=shared_w_up, shared_w_down=shared_w_down)
    B, S, D = x.shape
    T = B * S
    depth = attn_norm.shape[0]
    nkb = S // MOBA_BLOCK
    assert TQ == TK and S % TK == 0 and T % TM_MERGE == 0 and nkb <= MOBA_HD
    consts = _constants(S)
    router_w_t = router_w.T
    router_b = router_bias[:, None]
    assert min(TM_MOE, T) % TM_MERGE == 0 and N_EXPERTS % MOE_EXPERTS_PER_STEP == 0
    tri = jnp.asarray(np.triu(np.ones((MOE_SUB, MOE_SUB), np.float32), 1), BF16)

    ffn_w = (moe_w_gate.astype(BF16), moe_w_up.astype(BF16), moe_w_down.astype(BF16))
    x2 = x.reshape(T, D)
    for l in range(depth):
        lw = _layer_weights(l, p)
        qa, ka, va, yb, yc, qd, kd, vd, km = _in_proj(x2, lw, consts, S)
        ya = _flash(qa, ka.reshape(B, S // TK, TK, -1), va, MLA_HEADS, True, "mla_attn")

        kmean = km.reshape(B, nkb, MOBA_HEADS, MOBA_HD).transpose(0, 2, 1, 3)
        kmean = jnp.pad(kmean, ((0, 0), (0, 0), (0, MOBA_HD - nkb), (0, 0)))
        q_aug = _moba_select(qd, kmean)
        yd = _flash(q_aug, kd.reshape(B, S // TK, TK, -1), vd, MOBA_HEADS, False, "moba_attn")

        x1, h2, route, cnt = _merge(x2, ya, yb, yc, yd, lw, router_w_t, router_b)
        counts = cnt[:, :, :TM_MERGE // MOE_SUB].transpose(0, 2, 1).reshape(-1)
        x2 = _moe(x1, h2, route, counts, tri, lw["shared"], l, *ffn_w)
    return x2.reshape(B, S, D)
```

```python
import functools
import math

import jax
import jax.numpy as jnp
import numpy as np
from jax import lax
from jax.experimental import pallas as pl
from jax.experimental.pallas import tpu as pltpu

F32 = jnp.float32
BF16 = jnp.bfloat16
EPS = 1e-6
NEG = -1e30

LANES = 128
VMEM_LIMIT = 56 * 1024 * 1024

MLA_HEADS = 4
MLA_NOPE = 64
MLA_ROPE = 32
MLA_V = 64
MLA_Q_LORA = 256
MLA_KV_LORA = 128
ROPE_THETA = 10000.0
MLA_QK = MLA_NOPE + MLA_ROPE

SGU_GROUPS = 4
SGU_CHUNK = 128
SGU_W = 256

POOL_WINDOWS = (2, 4, 8, 16)
POOL_MAXW = 16
POOL_W = 256

MOBA_HEADS = 4
MOBA_HD = 64
MOBA_BLOCK = 256
MOBA_TOPK = 3

N_EXPERTS = 16
N_GROUPS = 4
EPG = 4

C_MLA = 0
C_SGU = 512
C_POOL = 1024
C_MOBA = 1280
C_END = 2048

TQ = 512
TK = 512
TM_IN = TQ
TM_MERGE = 512
TM_MOE = 2048
SEL_L = 2048
FLASH_UNROLLS = (4, 2, 1)
MOE_EXPERTS_PER_STEP = 4
MOE_SHARED_ROWS = 512
MOE_SUB = 256
MOE_CHUNK = 48
ONES_ROWS = 16


def _cparams(sem):
    return pltpu.CompilerParams(dimension_semantics=sem, vmem_limit_bytes=VMEM_LIMIT)


def _full(shape):
    n = len(shape)
    return pl.BlockSpec(shape, lambda *_: (0,) * n, pipeline_mode=pl.Buffered(1))


def _rms(x, g):
    return x * lax.rsqrt(jnp.mean(x * x, axis=-1, keepdims=True) + EPS) * g


def _dot(a, b):
    return jnp.dot(a, b, preferred_element_type=F32)


def _in_proj_kernel(x_ref, g_ref, w_ref, qn_ref, wq_ref, kvn_ref, wkv_ref,
                    qg_ref, kg_ref, cos_ref, sin_ref,
                    sn_ref, sw_ref, sb_ref, pw_ref, pb_ref, ps_ref,
                    mqg_ref, mkg_ref, kc_ref,
                    qa_ref, ka_ref, va_ref, yb_ref, yc_ref, qd_ref, kd_ref, vd_ref, km_ref,
                    halo_ref, buf_ref, proj_scr, *, tiles_per_seq, q_scale):
    tm = x_ref.shape[0]
    i = pl.program_id(0)

    @pl.when(i % tiles_per_seq == 0)
    def _():
        halo_ref[...] = jnp.zeros_like(halo_ref)

    x = x_ref[...]
    h = _rms(x, g_ref[...]).astype(BF16)
    lane = lax.broadcasted_iota(jnp.int32, (tm, LANES), 1)

    cm = _dot(h, w_ref[:, C_MLA:C_SGU])
    c_q = _rms(cm[:, :MLA_Q_LORA], qn_ref[...]).astype(BF16)
    c_kv = _rms(cm[:, MLA_Q_LORA:MLA_Q_LORA + MLA_KV_LORA], kvn_ref[...]).astype(BF16)
    kpe_grp = cm[:, MLA_Q_LORA + MLA_KV_LORA:]
    q = _dot(c_q, wq_ref[...])
    kv = _dot(c_kv, wkv_ref[...])
    va_ref[0, 0] = kv[:, 512:].T.astype(BF16)
    proj_scr[...] = _dot(h, w_ref[:, C_SGU:C_END])
    cos = cos_ref[...]
    sin = sin_ref[...]
    inv_d = 1.0 / MLA_QK
    rope_lo = lane < MLA_NOPE + MLA_ROPE // 2

    def head_norm_rope(t, gain):
        ms = jnp.sum(t * t, axis=-1, keepdims=True) * inv_d
        t = t * lax.rsqrt(ms + EPS) * gain
        rot = jnp.where(rope_lo, pltpu.roll(t, LANES - MLA_ROPE // 2, 1),
                        pltpu.roll(t, MLA_ROPE // 2, 1))
        return t * cos + rot * sin

    heads = [slice(hh * LANES, (hh + 1) * LANES) for hh in range(MLA_HEADS)]
    qh = [head_norm_rope(q[:, sl], qg_ref[...]) * q_scale for sl in heads]
    qa_ref[0, 0] = jnp.concatenate(qh, axis=1).T.astype(BF16)
    kh = [head_norm_rope(kv[:, sl] + kpe_grp, kg_ref[...]) for sl in heads]
    ka_ref[...] = jnp.concatenate(kh, axis=1).astype(BF16)

    z = jax.nn.gelu(proj_scr[:, 0:C_POOL - C_SGU])
    u = z[:, :SGU_W]
    vv = _rms(z[:, SGU_W:], sn_ref[...]).astype(BF16)
    lane_grp = lax.broadcasted_iota(jnp.int32, (SGU_CHUNK, SGU_W), 1) // (SGU_W // SGU_GROUPS)
    for c in range(tm // SGU_CHUNK):
        vc = vv[c * SGU_CHUNK:(c + 1) * SGU_CHUNK, :]
        y = sb_ref[...]
        for gi in range(SGU_GROUPS):
            y = y + jnp.where(lane_grp == gi, _dot(sw_ref[gi], vc), 0.0)
        yb_ref[c * SGU_CHUNK:(c + 1) * SGU_CHUNK, :] = (
            u[c * SGU_CHUNK:(c + 1) * SGU_CHUNK, :] * y).astype(BF16)

    xp = proj_scr[:, C_POOL - C_SGU:C_MOBA - C_SGU]
    buf_ref[0:POOL_MAXW, :] = halo_ref[...]
    buf_ref[POOL_MAXW:, :] = xp
    halo_ref[...] = xp[tm - POOL_MAXW:, :]
    pos = (i % tiles_per_seq) * tm + lax.broadcasted_iota(jnp.int32, (tm, POOL_W), 0)
    lane_w = lax.broadcasted_iota(jnp.int32, (tm, POOL_W), 1) // (POOL_W // len(POOL_WINDOWS))
    acc = xp
    pooled = jnp.zeros((tm, POOL_W), F32)
    shift = 1
    for gi, w in enumerate(POOL_WINDOWS):
        while shift < w:
            acc = acc + buf_ref[POOL_MAXW - shift:POOL_MAXW - shift + tm, :]
            shift += 1
        cnt = jnp.minimum(pos + 1, w).astype(F32)
        pooled = jnp.where(lane_w == gi, acc / cnt, pooled)
    p = (pooled - xp).astype(BF16)
    yc_ref[...] = ((_dot(p, pw_ref[...]) + pb_ref[...]) * ps_ref[...]).astype(BF16)

    cd = proj_scr[:, C_MOBA - C_SGU:]
    inv_hd = 1.0 / MOBA_HD
    lo = lane < MOBA_HD

    def pair_norm(t, gain):
        sq = t * t
        s_all = jnp.sum(sq, axis=-1, keepdims=True)
        s_lo = jnp.sum(jnp.where(lo, sq, 0.0), axis=-1, keepdims=True)
        ms = jnp.where(lo, s_lo, s_all - s_lo) * inv_hd
        return t * lax.rsqrt(ms + EPS) * gain

    pairs = [slice(g * LANES, (g + 1) * LANES) for g in range(MOBA_HEADS // 2)]
    qd = jnp.concatenate([pair_norm(cd[:, sl], mqg_ref[...]) for sl in pairs], axis=1)
    qd_ref[0] = qd.T
    kp = [pair_norm(cd[:, 256 + g * LANES:256 + (g + 1) * LANES], mkg_ref[...])
          for g in range(MOBA_HEADS // 2)]
    kd = jnp.concatenate(kp, axis=1)
    kh = []
    for hh in range(MOBA_HEADS):
        src = kp[hh // 2] if hh % 2 == 0 else pltpu.roll(kp[hh // 2], MOBA_HD, 1)
        kh.append(jnp.where(lo, src, kc_ref[:, hh * LANES:(hh + 1) * LANES].astype(F32)))
    kd_ref[...] = jnp.concatenate(kh, axis=1).astype(BF16)
    vd_ref[0, 0] = cd[:, 512:].T.astype(BF16)
    for c in range(tm // MOBA_BLOCK):
        km_ref[c] = jnp.mean(kd[c * MOBA_BLOCK:(c + 1) * MOBA_BLOCK, :], axis=0, keepdims=True)


def _in_proj(x2, lw, consts, S):
    T, D = x2.shape
    tm = TM_IN
    nt = T // tm
    tok = lambda w: pl.BlockSpec((tm, w), lambda i: (i, 0))
    pos_spec = pl.BlockSpec((tm, LANES), lambda i: (i % (S // tm), 0))
    ins = [
        (x2, tok(D)), (lw["attn_norm"], None), (lw["w_in"], None),
        (lw["mla_q_norm"], None), (lw["wq"], None), (lw["mla_kv_norm"], None), (lw["wkv"], None),
        (lw["mla_qg"], None), (lw["mla_kg"], None),
        (consts["cos"], pos_spec), (consts["sin"], pos_spec),
        (lw["sgu_norm"], None), (lw["sgu_w"], None), (lw["sgu_b"], None),
        (lw["pool_w"], None), (lw["pool_b"], None), (lw["pool_scale"], None),
        (lw["moba_qg"], None), (lw["moba_kg"], None),
        (consts["kc"], pl.BlockSpec((tm, 4 * LANES), lambda i: (i % (S // tm), 0))),
    ]
    arrays = [a for a, _ in ins]
    specs = [s if s is not None else _full(a.shape) for a, s in ins]
    nb = tm // MOBA_BLOCK
    B = T // S
    ns = S // tm
    tr = lambda f: pl.BlockSpec((1, 1, f, tm), lambda i: (i // ns, i % ns, 0, 0))
    out_shape = [
        jax.ShapeDtypeStruct((B, ns, 512, tm), BF16), jax.ShapeDtypeStruct((T, 512), BF16),
        jax.ShapeDtypeStruct((B, ns, 256, tm), BF16), jax.ShapeDtypeStruct((T, 256), BF16),
        jax.ShapeDtypeStruct((T, 256), BF16), jax.ShapeDtypeStruct((B, 256, S), F32),
        jax.ShapeDtypeStruct((T, 512), BF16), jax.ShapeDtypeStruct((B, ns, 256, tm), BF16),
        jax.ShapeDtypeStruct((T // MOBA_BLOCK, 1, 256), F32),
    ]
    out_specs = [tr(512), tok(512), tr(256), tok(256), tok(256),
                 pl.BlockSpec((1, 256, tm), lambda i: (i // ns, 0, i % ns)),
                 tok(512), tr(256), pl.BlockSpec((nb, 1, 256), lambda i: (i, 0, 0))]
    q_scale = (MLA_QK ** -0.5) * math.log2(math.e)
    return pl.pallas_call(
        functools.partial(_in_proj_kernel, tiles_per_seq=S // tm, q_scale=q_scale),
        grid=(nt,), in_specs=specs, out_specs=out_specs, out_shape=out_shape,
        scratch_shapes=[pltpu.VMEM((POOL_MAXW, POOL_W), F32),
                        pltpu.VMEM((tm + POOL_MAXW, POOL_W), F32),
                        pltpu.VMEM((tm, C_END - C_SGU), F32)],
        compiler_params=_cparams(("arbitrary",)), name="in_proj",
    )(*arrays)


def _moba_select_kernel(q_ref, km_ref, o_ref, *, slope_log2_step):
    hh = pl.program_id(1)
    li = pl.program_id(2)
    q = q_ref[0]
    L = q.shape[1]
    nrow = km_ref.shape[2]
    g = jnp.dot(km_ref[0, 0], q, preferred_element_type=F32, precision=lax.Precision.HIGHEST)
    n_i = lax.broadcasted_iota(jnp.int32, (nrow, L), 0)
    own = (li * L + lax.broadcasted_iota(jnp.int32, (nrow, L), 1)) // MOBA_BLOCK
    g = jnp.where(n_i < own, g, NEG)
    sel = jnp.zeros((nrow, L), F32)
    for _ in range(MOBA_TOPK):
        mx = jnp.max(g, axis=0, keepdims=True)
        first = jnp.min(jnp.where(g == mx, n_i, nrow), axis=0, keepdims=True)
        hit = n_i == first
        sel = jnp.where(hit & (mx > 0.5 * NEG), 1.0, sel)
        g = jnp.where(hit, NEG, g)
    slope = jnp.exp2(-slope_log2_step * (hh + 1).astype(F32))
    rel = (n_i - own).astype(F32) * (slope * MOBA_BLOCK)
    bias = jnp.where((sel > 0.0) | (n_i == own), rel, NEG)
    bias = jnp.where(n_i == nrow - 1, slope, bias)
    qs = (q * (MOBA_HD ** -0.5)).astype(BF16)
    bias = bias.astype(BF16)
    tq = o_ref.shape[3]
    for t in range(L // tq):
        o_ref[0, t, 0:MOBA_HD, :] = qs[:, t * tq:(t + 1) * tq]
        o_ref[0, t, MOBA_HD:, :] = bias[:, t * tq:(t + 1) * tq]


def _moba_select(qgT, kmean):
    B, _, S = qgT.shape
    H, hd = MOBA_HEADS, MOBA_HD
    L = min(SEL_L, S)
    return pl.pallas_call(
        functools.partial(_moba_select_kernel, slope_log2_step=8.0 / MOBA_HEADS),
        grid=(B, H, S // L),
        in_specs=[pl.BlockSpec((1, hd, L), lambda b, h, l: (b, h, l)),
                  pl.BlockSpec((1, 1, 64, hd), lambda b, h, l: (b, h, 0, 0))],
        out_specs=pl.BlockSpec((1, L // TQ, 2 * hd, TQ), lambda b, h, l: (b, l, h, 0)),
        out_shape=jax.ShapeDtypeStruct((B, S // TQ, H * 2 * hd, TQ), BF16),
        compiler_params=_cparams(("arbitrary",) * 3), name="moba_select",
    )(qgT, kmean)


def _flash_kernel(q_ref, k_ref, v_ref, o_ref, s_scr, acc_scr, *, heads, use_exp2):
    H = heads
    tq = q_ref.shape[3]
    tk = k_ref.shape[2]
    dk = q_ref.shape[2] // H
    dv = v_ref.shape[2] // H
    ratio = tk // tq
    ex = jnp.exp2 if use_exp2 else jnp.exp
    i = pl.program_id(1)
    jd = i // ratio
    kpos = jd * tk + lax.broadcasted_iota(jnp.int32, (tk, tq), 0)
    qpos = i * tq + lax.broadcasted_iota(jnp.int32, (tk, tq), 1)
    causal = kpos <= qpos
    ones = jnp.ones((ONES_ROWS, tk), BF16)

    def scores(jt, h, masked):
        s = _dot(k_ref[0, jt, :, h * dk:(h + 1) * dk], q_ref[0, 0, h * dk:(h + 1) * dk, :])
        if masked:
            s = jnp.where(causal, s, NEG)
        s_scr[h] = s
        return jnp.max(s, axis=0, keepdims=True)

    def update(jt, h, cmax, m):
        m_new = jnp.maximum(m, cmax)
        alpha = ex(m - m_new)
        p = ex(s_scr[h] - m_new).astype(BF16)
        vt = jnp.concatenate([v_ref[0, jt, h * dv:(h + 1) * dv, :], ones], axis=0)
        acc_scr[h] = acc_scr[h] * alpha + _dot(vt, p)
        return m_new

    acc_scr[...] = jnp.zeros_like(acc_scr)
    ms = [jnp.full((1, tq), NEG, F32) for _ in range(H)]
    cmax = scores(jd, 0, True)
    for h in range(H):
        nxt = scores(jd, h + 1, True) if h + 1 < H else scores(0, 0, False)
        ms[h] = update(jd, h, cmax, ms[h])
        cmax = nxt

    def tile_step(j, carry):
        ms, cmax = carry
        ms = list(ms)
        jn = jnp.minimum(j + 1, jd - 1)
        for h in range(H):
            nxt = scores(j, h + 1, False) if h + 1 < H else scores(jn, 0, False)
            ms[h] = update(j, h, cmax, ms[h])
            cmax = nxt
        return tuple(ms), cmax

    carry = (tuple(ms), cmax)
    done = 0
    for unroll in FLASH_UNROLLS:
        def body(t, c, unroll=unroll, done=done):
            for u in range(unroll):
                c = tile_step(done + unroll * t + u, c)
            return c

        trips = (jd - done) // unroll
        carry = lax.fori_loop(0, trips, body, carry)
        done = done + trips * unroll
    outs = []
    for h in range(H):
        acc = acc_scr[h]
        outs.append(acc[:dv] / acc[dv:dv + 1])
    o_ref[...] = jnp.concatenate(outs, axis=0).T.astype(BF16)


def _flash(qT, k, vT, heads, use_exp2, name):
    B, nq, hdk, tq = qT.shape
    nk, tk = k.shape[1:3]
    hdv = vT.shape[2]
    dva = hdv // heads + ONES_ROWS
    whole = lambda shp: pl.BlockSpec((1,) + shp, lambda b, i: (b, 0, 0, 0),
                                     pipeline_mode=pl.Buffered(1))
    return pl.pallas_call(
        functools.partial(_flash_kernel, heads=heads, use_exp2=use_exp2),
        grid=(B, nq),
        in_specs=[pl.BlockSpec((1, 1, hdk, tq), lambda b, i: (b, i, 0, 0)),
                  whole((nk, tk, hdk)), whole((nk, hdv, tk))],
        out_specs=pl.BlockSpec((tq, hdv), lambda b, i: (b * nq + i, 0)),
        out_shape=jax.ShapeDtypeStruct((B * nq * tq, hdv), BF16),
        scratch_shapes=[pltpu.VMEM((heads, tk, tq), F32), pltpu.VMEM((heads, dva, tq), F32)],
        compiler_params=_cparams(("arbitrary", "arbitrary")), name=name,
    )(qT, k, vT)


def _merge_kernel(x_ref, ya_ref, yb_ref, yc_ref, yd_ref, g_ref, wg_ref, wb_ref, wo_ref,
                  fg_ref, rw_ref, rb_ref, x1_ref, h2_ref, rt_ref, cnt_ref, h2_scr):
    @pl.when(pl.program_id(0) == 0)
    def _():
        h2_scr[...] = jnp.zeros_like(h2_scr)

    router = _router_stages(h2_scr, rw_ref, rb_ref, rt_ref, cnt_ref)
    x = x_ref[...]
    h = _rms(x, g_ref[...]).astype(BF16)
    merged = None
    for bi, y_ref in enumerate((ya_ref, yb_ref, yc_ref, yd_ref)):
        t = jax.nn.sigmoid(_dot(h, wg_ref[bi])) * _dot(y_ref[...], wb_ref[bi])
        merged = t if merged is None else merged + t
        next(router)
    x1 = x + _dot(merged.astype(BF16), wo_ref[...])
    x1_ref[...] = x1
    h2_new = _rms(x1, fg_ref[...])
    h2_ref[...] = h2_new.astype(BF16)
    for _ in router:
        pass
    h2_scr[...] = h2_new


def _router_stages(h2_scr, rw_ref, rb_ref, rt_ref, cnt_ref):
    h2 = h2_scr[...]
    nt = (((1,), (1,)), ((), ()))
    h2_hi = h2.astype(BF16)
    h2_lo = (h2 - h2_hi.astype(F32)).astype(BF16)
    rw = rw_ref[...]
    rw_hi = rw.astype(BF16)
    rw_lo = (rw - rw_hi.astype(F32)).astype(BF16)
    logits = (lax.dot_general(rw_hi, h2_hi, nt, preferred_element_type=F32)
              + lax.dot_general(rw_hi, h2_lo, nt, preferred_element_type=F32)
              + lax.dot_general(rw_lo, h2_hi, nt, preferred_element_type=F32))
    yield
    scores = jax.nn.sigmoid(logits)
    biased = scores + rb_ref[...]
    s_rows = [scores[e:e + 1, :] for e in range(N_EXPERTS)]
    b_rows = [biased[e:e + 1, :] for e in range(N_EXPERTS)]

    def top2(vals):
        m1 = vals[0]
        i1 = jnp.zeros_like(m1, dtype=jnp.int32)
        for j in range(1, len(vals)):
            better = vals[j] > m1
            i1 = jnp.where(better, j, i1)
            m1 = jnp.where(better, vals[j], m1)
        m2 = None
        i2 = None
        for j in range(len(vals)):
            cand = jnp.where(i1 == j, NEG, vals[j])
            if m2 is None:
                m2, i2 = cand, jnp.zeros_like(i1)
            else:
                better = cand > m2
                i2 = jnp.where(better, j, i2)
                m2 = jnp.where(better, cand, m2)
        return m1, i1, m2, i2

    best = None
    for gi in range(N_GROUPS):
        bm1, bi1, bm2, bi2 = top2(b_rows[gi * EPG:(gi + 1) * EPG])
        gs = bm1 + bm2
        e1 = bi1 + gi * EPG
        e2 = bi2 + gi * EPG
        if best is None:
            best = (gs, e1, e2)
        else:
            better = gs > best[0]
            best = (jnp.where(better, gs, best[0]), jnp.where(better, e1, best[1]),
                    jnp.where(better, e2, best[2]))
        if gi % 2 == 1:
            yield
    _, e1, e2 = best
    w1 = jnp.zeros_like(s_rows[0])
    w2 = jnp.zeros_like(s_rows[0])
    for e in range(N_EXPERTS):
        w1 = jnp.where(e1 == e, s_rows[e], w1)
        w2 = jnp.where(e2 == e, s_rows[e], w2)
    tot = w1 + w2
    w1 = w1 / tot
    w2 = w2 / tot
    r8 = lax.broadcasted_iota(jnp.int32, (8, scores.shape[1]), 0)
    rt_ref[...] = jnp.where(r8 == 0, e1.astype(F32), jnp.where(r8 == 1, e2.astype(F32),
                            jnp.where(r8 == 2, w1, jnp.where(r8 == 3, w2, 0.0))))
    row = lax.broadcasted_iota(jnp.int32, scores.shape, 0)
    assigned = jnp.where(row == e1, 1.0, jnp.where(row == e2, 1.0, 0.0))
    lane = lax.broadcasted_iota(jnp.int32, (N_EXPERTS, LANES), 1)
    cnt = jnp.zeros((N_EXPERTS, LANES), F32)
    for k in range(scores.shape[1] // MOE_SUB):
        part = jnp.sum(assigned[:, k * MOE_SUB:(k + 1) * MOE_SUB], axis=1, keepdims=True)
        cnt = jnp.where(lane == k, part, cnt)
    cnt_ref[0] = cnt.astype(jnp.int32)
    yield


def _merge(x2, ya, yb, yc, yd, lw, router_w_t, router_b):
    T, D = x2.shape
    tm = TM_MERGE
    n = T // tm
    tok = lambda w: pl.BlockSpec((tm, w), lambda i: (jnp.minimum(i, n - 1), 0))
    consts = [lw["attn_norm"], lw["w_gate"], lw["w_branch"], lw["w_out"], lw["ffn_norm"],
              router_w_t, router_b]
    return pl.pallas_call(
        _merge_kernel, grid=(n + 1,),
        in_specs=[tok(D), tok(256), tok(256), tok(256), tok(256)] + [_full(a.shape) for a in consts],
        out_specs=[tok(D), tok(D), pl.BlockSpec((8, tm), lambda i: (0, jnp.maximum(i - 1, 0))),
                   pl.BlockSpec((1, N_EXPERTS, LANES), lambda i: (jnp.maximum(i - 1, 0), 0, 0))],
        out_shape=[jax.ShapeDtypeStruct((T, D), F32), jax.ShapeDtypeStruct((T, D), BF16),
                   jax.ShapeDtypeStruct((8, T), F32),
                   jax.ShapeDtypeStruct((n, N_EXPERTS, LANES), jnp.int32)],
        scratch_shapes=[pltpu.VMEM((tm, D), F32)],
        compiler_params=_cparams(("arbitrary",)), name="merge",
    )(x2, ya, yb, yc, yd, *consts)


def _moe_kernel(cnt_ref, x1_ref, h2_ref, rt_ref, tri_ref, sg_ref, su_ref, sd_ref,
                wg_ref, wu_ref, wd_ref, o_ref, rank_scr, gate_scr):
    i = pl.program_id(0)
    e = pl.program_id(1)
    tm = h2_ref.shape[0]

    def ffn(rows, scale, wg, wu, wd):
        a = jax.nn.silu(_dot(rows, wg)) * _dot(rows, wu)
        if scale is not None:
            a = a * scale
        return _dot(a.astype(BF16), wd)

    sub = tri_ref.shape[0]
    nsub = tm // sub
    spans = [slice(k * sub, (k + 1) * sub) for k in range(nsub)]

    @pl.when(e == 0)
    def _():
        for r in range(0, tm, MOE_SHARED_ROWS):
            rs = slice(r, r + MOE_SHARED_ROWS)
            o_ref[rs, :] = x1_ref[rs, :] + ffn(h2_ref[rs, :], None, sg_ref[...], su_ref[...],
                                               sd_ref[...])
        row = lax.broadcasted_iota(jnp.int32, (N_EXPERTS, tm), 0).astype(F32)
        e1, e2 = rt_ref[0:1, :], rt_ref[1:2, :]
        w1, w2 = rt_ref[2:3, :], rt_ref[3:4, :]
        assigned = jnp.where(row == e1, 1.0, jnp.where(row == e2, 1.0, 0.0))
        for sp in spans:
            before = _dot(assigned[:, sp].astype(BF16), tri_ref[...])
            rank_scr[:, sp] = jnp.where(assigned[:, sp] > 0.0, before, -1.0)
        gate_scr[...] = jnp.where(row == e1, w1, jnp.where(row == e2, w2, 0.0))

    @pl.when(e > 0)
    def _():
        subs = range(MOE_EXPERTS_PER_STEP)
        ex = [(e - 1) * MOE_EXPERTS_PER_STEP + s for s in subs]
        cnt = [cnt_ref[(i * nsub + k) * N_EXPERTS + x] for x in ex for k in range(nsub)]
        rank_row = [rank_scr[pl.ds(x, 1), :] for x in ex]
        gate_row = [gate_scr[pl.ds(x, 1), :] for x in ex]
        slot = lax.broadcasted_iota(jnp.int32, (MOE_CHUNK, sub), 0).astype(F32)

        def chunk(c, carry):
            base = slot + (c * MOE_CHUNK).astype(F32)
            onehot, rows, scale = [], [], []
            for sp in spans:
                hit = [rank_row[s][:, sp] == base for s in subs]
                onehot.append(jnp.concatenate(
                    [jnp.where(h, 1.0, 0.0).astype(BF16) for h in hit], axis=0))
                rows.append(_dot(onehot[-1], h2_ref[sp, :]).astype(BF16))
                scale.append([jnp.sum(jnp.where(hit[s], gate_row[s][:, sp], 0.0),
                                      axis=1, keepdims=True) for s in subs])
            y = []
            for s in subs:
                mine = slice(s * MOE_CHUNK, (s + 1) * MOE_CHUNK)
                y.append(ffn(jnp.concatenate([r[mine] for r in rows], axis=0),
                             jnp.concatenate([sc[s] for sc in scale], axis=0),
                             wg_ref[0, s], wu_ref[0, s], wd_ref[0, s]).astype(BF16))
            for k, sp in enumerate(spans):
                rows_k = slice(k * MOE_CHUNK, (k + 1) * MOE_CHUNK)
                o_ref[sp, :] += lax.dot_general(
                    onehot[k], jnp.concatenate([y[s][rows_k] for s in subs], axis=0),
                    (((0,), (0,)), ((), ())), preferred_element_type=F32)
            return carry

        most = functools.reduce(jnp.maximum, cnt)
        lax.fori_loop(0, (most + MOE_CHUNK - 1) // MOE_CHUNK, chunk, 0)


def _moe(x1, h2, route, counts, tri, shared, layer, wg, wu, wd):
    T, D = x1.shape
    tm = min(TM_MOE, T)
    sub = tri.shape[0]
    _, ne, _, ff = wg.shape
    const = lambda shp: pl.BlockSpec(shp, lambda i, e, c: (0, 0), pipeline_mode=pl.Buffered(1))
    eps = MOE_EXPERTS_PER_STEP
    expert = lambda shp: pl.BlockSpec((1, eps) + shp,
                                      lambda i, e, c: (layer, jnp.maximum(e - 1, 0), 0, 0))
    grid_spec = pltpu.PrefetchScalarGridSpec(
        num_scalar_prefetch=1, grid=(T // tm, ne // eps + 1),
        in_specs=[pl.BlockSpec((tm, D), lambda i, e, c: (i, 0), pipeline_mode=pl.Buffered(1)),
                  pl.BlockSpec((tm, D), lambda i, e, c: (i, 0), pipeline_mode=pl.Buffered(1)),
                  pl.BlockSpec((8, tm), lambda i, e, c: (0, i)),
                  const((sub, sub)), const((D, ff)), const((D, ff)), const((ff, D)),
                  expert((D, ff)), expert((D, ff)), expert((ff, D))],
        out_specs=pl.BlockSpec((tm, D), lambda i, e, c: (i, 0), pipeline_mode=pl.Buffered(1)),
        scratch_shapes=[pltpu.VMEM((N_EXPERTS, tm), F32), pltpu.VMEM((N_EXPERTS, tm), F32)])
    return pl.pallas_call(
        _moe_kernel, grid_spec=grid_spec,
        out_shape=jax.ShapeDtypeStruct((T, D), F32),
        compiler_params=_cparams(("arbitrary", "arbitrary")), name="moe",
    )(counts, x1, h2, route, tri, *shared, wg, wu, wd)


def _constants(S):
    half = MLA_ROPE // 2
    inv = ROPE_THETA ** (-np.arange(half, dtype=np.float64) / half)
    ang = np.arange(S, dtype=np.float64)[:, None] * inv[None, :]
    cos = np.ones((S, LANES), np.float32)
    cos[:, MLA_NOPE:MLA_NOPE + half] = np.cos(ang)
    cos[:, MLA_NOPE + half:MLA_QK] = np.cos(ang)
    sin = np.zeros((S, LANES), np.float32)
    sin[:, MLA_NOPE:MLA_NOPE + half] = -np.sin(ang)
    sin[:, MLA_NOPE + half:MLA_QK] = np.sin(ang)

    nkb = S // MOBA_BLOCK
    kpos = np.arange(S)
    kc = np.zeros((S, MOBA_HD), np.float32)
    blk = kpos // MOBA_BLOCK
    m = blk < min(nkb, MOBA_HD) - 1
    kc[kpos[m], blk[m]] = 1.0
    kc[:, MOBA_HD - 1] = kpos % MOBA_BLOCK
    kc4 = np.zeros((S, MOBA_HEADS, 2 * MOBA_HD), np.float32)
    kc4[:, :, MOBA_HD:] = kc[:, None, :]
    return {"cos": jnp.asarray(cos), "sin": jnp.asarray(sin),
            "kc": jnp.asarray(kc4.reshape(S, -1), BF16)}


def _head_pad(v):
    return jnp.concatenate([v, jnp.zeros((LANES - MLA_QK,), F32)])[None, :]


def _layer_weights(l, p):
    D = p["w_in"].shape[1]
    w_in = p["w_in"][l]
    o0 = MLA_Q_LORA + MLA_KV_LORA
    o1 = o0 + MLA_ROPE
    w = jnp.concatenate([w_in[:, :o0], jnp.zeros((D, MLA_NOPE), F32), w_in[:, o0:o1],
                         jnp.zeros((D, LANES - MLA_QK), F32), w_in[:, o1:]], axis=1)

    wq = p["mla_w_q_up"][l].reshape(MLA_Q_LORA, MLA_HEADS, MLA_QK)
    wq = jnp.pad(wq, ((0, 0), (0, 0), (0, LANES - MLA_QK))).reshape(MLA_Q_LORA, MLA_HEADS * LANES)
    wkv = p["mla_w_kv_up"][l].reshape(MLA_KV_LORA, MLA_HEADS, MLA_NOPE + MLA_V)
    wk = jnp.pad(wkv[:, :, :MLA_NOPE], ((0, 0), (0, 0), (0, LANES - MLA_NOPE)))
    wk = wk.reshape(MLA_KV_LORA, MLA_HEADS * LANES)
    wv = wkv[:, :, MLA_NOPE:].reshape(MLA_KV_LORA, MLA_HEADS * MLA_V)

    tri = jnp.tril(jnp.ones((SGU_CHUNK, SGU_CHUNK), dtype=bool))
    sgu_w = jnp.where(tri[None], p["sgu_w"][l], 0.0)
    sgu_b = jnp.repeat(p["sgu_b"][l].T, SGU_W // SGU_GROUPS, axis=1)
    pool_w = jax.scipy.linalg.block_diag(*[p["pool_w"][l, gi] for gi in range(len(POOL_WINDOWS))])

    return {
        "attn_norm": p["attn_norm"][l][None, :], "w_in": w.astype(BF16),
        "mla_q_norm": p["mla_q_norm"][l][None, :], "wq": wq.astype(BF16),
        "mla_kv_norm": p["mla_kv_norm"][l][None, :],
        "wkv": jnp.concatenate([wk, wv], axis=1).astype(BF16),
        "mla_qg": _head_pad(p["mla_q_gain"][l]), "mla_kg": _head_pad(p["mla_k_gain"][l]),
        "sgu_norm": p["sgu_norm"][l][None, :], "sgu_w": sgu_w.astype(BF16), "sgu_b": sgu_b,
        "pool_w": pool_w.astype(BF16), "pool_b": p["pool_b"][l][None, :],
        "pool_scale": p["pool_scale"][l][None, :],
        "moba_qg": jnp.tile(p["moba_q_gain"][l], 2)[None, :],
        "moba_kg": jnp.tile(p["moba_k_gain"][l], 2)[None, :],
        "w_gate": p["w_gate"][l].astype(BF16), "w_branch": p["w_branch"][l].astype(BF16),
        "w_out": p["w_out"][l].astype(BF16), "ffn_norm": p["ffn_norm"][l][None, :],
        "shared": (p["shared_w_gate"][l].astype(BF16), p["shared_w_up"][l].astype(BF16),
                   p["shared_w_down"][l].astype(BF16)),
    }


def kernel(x, attn_norm, w_in, mla_q_norm, mla_w_q_up, mla_kv_norm, mla_w_kv_up, mla_q_gain, mla_k_gain, sgu_norm, sgu_w, sgu_b, pool_w, pool_b, pool_scale, moba_q_gain, moba_k_gain, w_gate, w_branch, w_out, ffn_norm, router_w, router_bias, moe_w_gate, moe_w_up, moe_w_down, shared_w_gate, shared_w_up, shared_w_down):
    p = dict(attn_norm=attn_norm, w_in=w_in, mla_q_norm=mla_q_norm, mla_w_q_up=mla_w_q_up,
             mla_kv_norm=mla_kv_norm, mla_w_kv_up=mla_w_kv_up, mla_q_gain=mla_q_gain,
             mla_k_gain=mla_k_gain, sgu_norm=sgu_norm, sgu_w=sgu_w, sgu_b=sgu_b, pool_w=pool_w,
             pool_b=pool_b, pool_scale=pool_scale, moba_q_gain=moba_q_gain,
             moba_k_gain=moba_k_gain, w_gate=w_gate, w_branch=w_branch, w_out=w_out,
             ffn_norm=ffn_norm, moe_w_gate=moe_w_gate, moe_w_up=moe_w_up, moe_w_down=moe_w_down,
             shared_w_gate=shared_w_gate, shared_w_up=shared_w_up, shared_w_down=shared_w_down)
    B, S, D = x.shape
    T = B * S
    depth = attn_norm.shape[0]
    nkb = S // MOBA_BLOCK
    assert TQ == TK and S % TK == 0 and T % TM_MERGE == 0 and nkb <= MOBA_HD
    consts = _constants(S)
    router_w_t = router_w.T
    router_b = router_bias[:, None]
    assert min(TM_MOE, T) % TM_MERGE == 0 and N_EXPERTS % MOE_EXPERTS_PER_STEP == 0
    tri = jnp.asarray(np.triu(np.ones((MOE_SUB, MOE_SUB), np.float32), 1), BF16)

    ffn_w = (moe_w_gate.astype(BF16), moe_w_up.astype(BF16), moe_w_down.astype(BF16))
    x2 = x.reshape(T, D)
    for l in range(depth):
        lw = _layer_weights(l, p)
        qa, ka, va, yb, yc, qd, kd, vd, km = _in_proj(x2, lw, consts, S)
        ya = _flash(qa, ka.reshape(B, S // TK, TK, -1), va, MLA_HEADS, True, "mla_attn")

        kmean = km.reshape(B, nkb, MOBA_HEADS, MOBA_HD).transpose(0, 2, 1, 3)
        kmean = jnp.pad(kmean, ((0, 0), (0, 0), (0, MOBA_HD - nkb), (0, 0)))
        q_aug = _moba_select(qd, kmean)
        yd = _flash(q_aug, kd.reshape(B, S // TK, TK, -1), vd, MOBA_HEADS, False, "moba_attn")

        x1, h2, route, cnt = _merge(x2, ya, yb, yc, yd, lw, router_w_t, router_b)
        counts = cnt[:, :, :TM_MERGE // MOE_SUB].transpose(0, 2, 1).reshape(-1)
        x2 = _moe(x1, h2, route, counts, tri, lw["shared"], l, *ffn_w)
    return x2.reshape(B, S, D)
```

```python
import functools
import math

import jax
import jax.numpy as jnp
import numpy as np
from jax import lax
from jax.experimental import pallas as pl
from jax.experimental.pallas import tpu as pltpu

F32 = jnp.float32
BF16 = jnp.bfloat16
EPS = 1e-6
NEG = -1e30

LANES = 128
VMEM_LIMIT = 56 * 1024 * 1024

MLA_HEADS = 4
MLA_NOPE = 64
MLA_ROPE = 32
MLA_V = 64
MLA_Q_LORA = 256
MLA_KV_LORA = 128
ROPE_THETA = 10000.0
MLA_QK = MLA_NOPE + MLA_ROPE

SGU_GROUPS = 4
SGU_CHUNK = 128
SGU_W = 256

POOL_WINDOWS = (2, 4, 8, 16)
POOL_MAXW = 16
POOL_W = 256

MOBA_HEADS = 4
MOBA_HD = 64
MOBA_BLOCK = 256
MOBA_TOPK = 3

N_EXPERTS = 16
N_GROUPS = 4
EPG = 4

C_MLA = 0
C_SGU = 512
C_POOL = 1024
C_MOBA = 1280
C_END = 2048

TQ = 512
TK = 512
TM_IN = TQ
TM_MERGE = 512
TM_MOE = 2048
SEL_L = 2048
FLASH_UNROLLS = (4, 2, 1)
MOE_EXPERTS_PER_STEP = 4
MOE_SHARED_ROWS = 512
MOE_SUB = 256
MOE_CHUNK = 48


def _cparams(sem):
    return pltpu.CompilerParams(dimension_semantics=sem, vmem_limit_bytes=VMEM_LIMIT)


def _full(shape):
    n = len(shape)
    return pl.BlockSpec(shape, lambda *_: (0,) * n, pipeline_mode=pl.Buffered(1))


def _rms(x, g):
    return x * lax.rsqrt(jnp.mean(x * x, axis=-1, keepdims=True) + EPS) * g


def _dot(a, b):
    return jnp.dot(a, b, preferred_element_type=F32)


def _in_proj_kernel(x_ref, g_ref, w_ref, qn_ref, wq_ref, kvn_ref, wkv_ref,
                    qg_ref, kg_ref, cos_ref, sin_ref,
                    sn_ref, sw_ref, sb_ref, pw_ref, pb_ref, ps_ref,
                    mqg_ref, mkg_ref, kc_ref,
                    qa_ref, ka_ref, va_ref, yb_ref, yc_ref, qd_ref, kd_ref, vd_ref, km_ref,
                    halo_ref, buf_ref, proj_scr, *, tiles_per_seq, q_scale):
    tm = x_ref.shape[0]
    i = pl.program_id(0)

    @pl.when(i % tiles_per_seq == 0)
    def _():
        halo_ref[...] = jnp.zeros_like(halo_ref)

    x = x_ref[...]
    h = _rms(x, g_ref[...]).astype(BF16)
    lane = lax.broadcasted_iota(jnp.int32, (tm, LANES), 1)

    cm = _dot(h, w_ref[:, C_MLA:C_SGU])
    c_q = _rms(cm[:, :MLA_Q_LORA], qn_ref[...]).astype(BF16)
    c_kv = _rms(cm[:, MLA_Q_LORA:MLA_Q_LORA + MLA_KV_LORA], kvn_ref[...]).astype(BF16)
    kpe_grp = cm[:, MLA_Q_LORA + MLA_KV_LORA:]
    q = _dot(c_q, wq_ref[...])
    kv = _dot(c_kv, wkv_ref[...])
    va_ref[0, 0] = kv[:, 512:].T.astype(BF16)
    proj_scr[...] = _dot(h, w_ref[:, C_SGU:C_END])
    cos = cos_ref[...]
    sin = sin_ref[...]
    inv_d = 1.0 / MLA_QK
    rope_lo = lane < MLA_NOPE + MLA_ROPE // 2

    def head_norm_rope(t, gain):
        ms = jnp.sum(t * t, axis=-1, keepdims=True) * inv_d
        t = t * lax.rsqrt(ms + EPS) * gain
        rot = jnp.where(rope_lo, pltpu.roll(t, LANES - MLA_ROPE // 2, 1),
                        pltpu.roll(t, MLA_ROPE // 2, 1))
        return t * cos + rot * sin

    heads = [slice(hh * LANES, (hh + 1) * LANES) for hh in range(MLA_HEADS)]
    qh = [head_norm_rope(q[:, sl], qg_ref[...]) * q_scale for sl in heads]
    qa_ref[0, 0] = jnp.concatenate(qh, axis=1).T.astype(BF16)
    kh = [head_norm_rope(kv[:, sl] + kpe_grp, kg_ref[...]) for sl in heads]
    ka_ref[...] = jnp.concatenate(kh, axis=1).astype(BF16)

    z = jax.nn.gelu(proj_scr[:, 0:C_POOL - C_SGU])
    u = z[:, :SGU_W]
    vv = _rms(z[:, SGU_W:], sn_ref[...]).astype(BF16)
    lane_grp = lax.broadcasted_iota(jnp.int32, (SGU_CHUNK, SGU_W), 1) // (SGU_W // SGU_GROUPS)
    for c in range(tm // SGU_CHUNK):
        vc = vv[c * SGU_CHUNK:(c + 1) * SGU_CHUNK, :]
        y = sb_ref[...]
        for gi in range(SGU_GROUPS):
            y = y + jnp.where(lane_grp == gi, _dot(sw_ref[gi], vc), 0.0)
        yb_ref[c * SGU_CHUNK:(c + 1) * SGU_CHUNK, :] = (
            u[c * SGU_CHUNK:(c + 1) * SGU_CHUNK, :] * y).astype(BF16)

    xp = proj_scr[:, C_POOL - C_SGU:C_MOBA - C_SGU]
    buf_ref[0:POOL_MAXW, :] = halo_ref[...]
    buf_ref[POOL_MAXW:, :] = xp
    halo_ref[...] = xp[tm - POOL_MAXW:, :]
    pos = (i % tiles_per_seq) * tm + lax.broadcasted_iota(jnp.int32, (tm, POOL_W), 0)
    lane_w = lax.broadcasted_iota(jnp.int32, (tm, POOL_W), 1) // (POOL_W // len(POOL_WINDOWS))
    acc = xp
    pooled = jnp.zeros((tm, POOL_W), F32)
    shift = 1
    for gi, w in enumerate(POOL_WINDOWS):
        while shift < w:
            acc = acc + buf_ref[POOL_MAXW - shift:POOL_MAXW - shift + tm, :]
            shift += 1
        cnt = jnp.minimum(pos + 1, w).astype(F32)
        pooled = jnp.where(lane_w == gi, acc / cnt, pooled)
    p = (pooled - xp).astype(BF16)
    yc_ref[...] = ((_dot(p, pw_ref[...]) + pb_ref[...]) * ps_ref[...]).astype(BF16)

    cd = proj_scr[:, C_MOBA - C_SGU:]
    inv_hd = 1.0 / MOBA_HD
    lo = lane < MOBA_HD

    def pair_norm(t, gain):
        sq = t * t
        s_all = jnp.sum(sq, axis=-1, keepdims=True)
        s_lo = jnp.sum(jnp.where(lo, sq, 0.0), axis=-1, keepdims=True)
        ms = jnp.where(lo, s_lo, s_all - s_lo) * inv_hd
        return t * lax.rsqrt(ms + EPS) * gain

    pairs = [slice(g * LANES, (g + 1) * LANES) for g in range(MOBA_HEADS // 2)]
    qd = jnp.concatenate([pair_norm(cd[:, sl], mqg_ref[...]) for sl in pairs], axis=1)
    qd_ref[0] = qd.T
    kp = [pair_norm(cd[:, 256 + g * LANES:256 + (g + 1) * LANES], mkg_ref[...])
          for g in range(MOBA_HEADS // 2)]
    kd = jnp.concatenate(kp, axis=1)
    kh = []
    for hh in range(MOBA_HEADS):
        src = kp[hh // 2] if hh % 2 == 0 else pltpu.roll(kp[hh // 2], MOBA_HD, 1)
        kh.append(jnp.where(lo, src, kc_ref[:, hh * LANES:(hh + 1) * LANES].astype(F32)))
    kd_ref[...] = jnp.concatenate(kh, axis=1).astype(BF16)
    vd_ref[0, 0] = cd[:, 512:].T.astype(BF16)
    for c in range(tm // MOBA_BLOCK):
        km_ref[c] = jnp.mean(kd[c * MOBA_BLOCK:(c + 1) * MOBA_BLOCK, :], axis=0, keepdims=True)


def _in_proj(x2, lw, consts, S):
    T, D = x2.shape
    tm = TM_IN
    nt = T // tm
    tok = lambda w: pl.BlockSpec((tm, w), lambda i: (i, 0))
    pos_spec = pl.BlockSpec((tm, LANES), lambda i: (i % (S // tm), 0))
    ins = [
        (x2, tok(D)), (lw["attn_norm"], None), (lw["w_in"], None),
        (lw["mla_q_norm"], None), (lw["wq"], None), (lw["mla_kv_norm"], None), (lw["wkv"], None),
        (lw["mla_qg"], None), (lw["mla_kg"], None),
        (consts["cos"], pos_spec), (consts["sin"], pos_spec),
        (lw["sgu_norm"], None), (lw["sgu_w"], None), (lw["sgu_b"], None),
        (lw["pool_w"], None), (lw["pool_b"], None), (lw["pool_scale"], None),
        (lw["moba_qg"], None), (lw["moba_kg"], None),
        (consts["kc"], pl.BlockSpec((tm, 4 * LANES), lambda i: (i % (S // tm), 0))),
    ]
    arrays = [a for a, _ in ins]
    specs = [s if s is not None else _full(a.shape) for a, s in ins]
    nb = tm // MOBA_BLOCK
    B = T // S
    ns = S // tm
    tr = lambda f: pl.BlockSpec((1, 1, f, tm), lambda i: (i // ns, i % ns, 0, 0))
    out_shape = [
        jax.ShapeDtypeStruct((B, ns, 512, tm), BF16), jax.ShapeDtypeStruct((T, 512), BF16),
        jax.ShapeDtypeStruct((B, ns, 256, tm), BF16), jax.ShapeDtypeStruct((T, 256), BF16),
        jax.ShapeDtypeStruct((T, 256), BF16), jax.ShapeDtypeStruct((B, 256, S), F32),
        jax.ShapeDtypeStruct((T, 512), BF16), jax.ShapeDtypeStruct((B, ns, 256, tm), BF16),
        jax.ShapeDtypeStruct((T // MOBA_BLOCK, 1, 256), F32),
    ]
    out_specs = [tr(512), tok(512), tr(256), tok(256), tok(256),
                 pl.BlockSpec((1, 256, tm), lambda i: (i // ns, 0, i % ns)),
                 tok(512), tr(256), pl.BlockSpec((nb, 1, 256), lambda i: (i, 0, 0))]
    q_scale = (MLA_QK ** -0.5) * math.log2(math.e)
    return pl.pallas_call(
        functools.partial(_in_proj_kernel, tiles_per_seq=S // tm, q_scale=q_scale),
        grid=(nt,), in_specs=specs, out_specs=out_specs, out_shape=out_shape,
        scratch_shapes=[pltpu.VMEM((POOL_MAXW, POOL_W), F32),
                        pltpu.VMEM((tm + POOL_MAXW, POOL_W), F32),
                        pltpu.VMEM((tm, C_END - C_SGU), F32)],
        compiler_params=_cparams(("arbitrary",)), name="in_proj",
    )(*arrays)


def _moba_select_kernel(q_ref, km_ref, o_ref, *, slope_log2_step):
    hh = pl.program_id(1)
    li = pl.program_id(2)
    q = q_ref[0]
    L = q.shape[1]
    nrow = km_ref.shape[2]
    g = jnp.dot(km_ref[0, 0], q, preferred_element_type=F32, precision=lax.Precision.HIGHEST)
    n_i = lax.broadcasted_iota(jnp.int32, (nrow, L), 0)
    own = (li * L + lax.broadcasted_iota(jnp.int32, (nrow, L), 1)) // MOBA_BLOCK
    g = jnp.where(n_i < own, g, NEG)
    sel = jnp.zeros((nrow, L), F32)
    for _ in range(MOBA_TOPK):
        mx = jnp.max(g, axis=0, keepdims=True)
        first = jnp.min(jnp.where(g == mx, n_i, nrow), axis=0, keepdims=True)
        hit = n_i == first
        sel = jnp.where(hit & (mx > 0.5 * NEG), 1.0, sel)
        g = jnp.where(hit, NEG, g)
    slope = jnp.exp2(-slope_log2_step * (hh + 1).astype(F32))
    rel = (n_i - own).astype(F32) * (slope * MOBA_BLOCK)
    bias = jnp.where((sel > 0.0) | (n_i == own), rel, NEG)
    bias = jnp.where(n_i == nrow - 1, slope, bias)
    qs = (q * (MOBA_HD ** -0.5)).astype(BF16)
    bias = bias.astype(BF16)
    tq = o_ref.shape[3]
    for t in range(L // tq):
        o_ref[0, t, 0:MOBA_HD, :] = qs[:, t * tq:(t + 1) * tq]
        o_ref[0, t, MOBA_HD:, :] = bias[:, t * tq:(t + 1) * tq]


def _moba_select(qgT, kmean):
    B, _, S = qgT.shape
    H, hd = MOBA_HEADS, MOBA_HD
    L = min(SEL_L, S)
    return pl.pallas_call(
        functools.partial(_moba_select_kernel, slope_log2_step=8.0 / MOBA_HEADS),
        grid=(B, H, S // L),
        in_specs=[pl.BlockSpec((1, hd, L), lambda b, h, l: (b, h, l)),
                  pl.BlockSpec((1, 1, 64, hd), lambda b, h, l: (b, h, 0, 0))],
        out_specs=pl.BlockSpec((1, L // TQ, 2 * hd, TQ), lambda b, h, l: (b, l, h, 0)),
        out_shape=jax.ShapeDtypeStruct((B, S // TQ, H * 2 * hd, TQ), BF16),
        compiler_params=_cparams(("arbitrary",) * 3), name="moba_select",
    )(qgT, kmean)


def _flash_kernel(q_ref, k_ref, v_ref, o_ref, s_scr, acc_scr, *, heads, use_exp2):
    H = heads
    tq = q_ref.shape[3]
    tk = k_ref.shape[2]
    dk = q_ref.shape[2] // H
    dv = v_ref.shape[2] // H
    ratio = tk // tq
    ex = jnp.exp2 if use_exp2 else jnp.exp
    i = pl.program_id(1)
    jd = i // ratio
    kpos = jd * tk + lax.broadcasted_iota(jnp.int32, (tk, tq), 0)
    qpos = i * tq + lax.broadcasted_iota(jnp.int32, (tk, tq), 1)
    causal = kpos <= qpos

    def scores(jt, h, masked):
        s = _dot(k_ref[0, jt, :, h * dk:(h + 1) * dk], q_ref[0, 0, h * dk:(h + 1) * dk, :])
        if masked:
            s = jnp.where(causal, s, NEG)
        s_scr[h] = s
        return jnp.max(s, axis=0, keepdims=True)

    def update(jt, h, cmax, state):
        m, l = state
        m_new = jnp.maximum(m, cmax)
        alpha = ex(m - m_new)
        p = ex(s_scr[h] - m_new)
        l_new = l * alpha + jnp.sum(p, axis=0, keepdims=True)
        acc_scr[h] = acc_scr[h] * alpha + _dot(v_ref[0, jt, h * dv:(h + 1) * dv, :],
                                               p.astype(BF16))
        return m_new, l_new

    acc_scr[...] = jnp.zeros_like(acc_scr)
    ms = [(jnp.full((1, tq), NEG, F32), jnp.zeros((1, tq), F32)) for _ in range(H)]
    cmax = scores(jd, 0, True)
    for h in range(H):
        nxt = scores(jd, h + 1, True) if h + 1 < H else scores(0, 0, False)
        ms[h] = update(jd, h, cmax, ms[h])
        cmax = nxt

    def tile_step(j, carry):
        ms, cmax = carry
        ms = list(ms)
        jn = jnp.minimum(j + 1, jd - 1)
        for h in range(H):
            nxt = scores(j, h + 1, False) if h + 1 < H else scores(jn, 0, False)
            ms[h] = update(j, h, cmax, ms[h])
            cmax = nxt
        return tuple(ms), cmax

    carry = (tuple(ms), cmax)
    done = 0
    for unroll in FLASH_UNROLLS:
        def body(t, c, unroll=unroll, done=done):
            for u in range(unroll):
                c = tile_step(done + unroll * t + u, c)
            return c

        trips = (jd - done) // unroll
        carry = lax.fori_loop(0, trips, body, carry)
        done = done + trips * unroll
    outs = [acc_scr[h] / carry[0][h][1] for h in range(H)]
    o_ref[...] = jnp.concatenate(outs, axis=0).T.astype(BF16)


def _flash(qT, k, vT, heads, use_exp2, name):
    B, nq, hdk, tq = qT.shape
    nk, tk = k.shape[1:3]
    hdv = vT.shape[2]
    whole = lambda shp: pl.BlockSpec((1,) + shp, lambda b, i: (b, 0, 0, 0),
                                     pipeline_mode=pl.Buffered(1))
    return pl.pallas_call(
        functools.partial(_flash_kernel, heads=heads, use_exp2=use_exp2),
        grid=(B, nq),
        in_specs=[pl.BlockSpec((1, 1, hdk, tq), lambda b, i: (b, i, 0, 0)),
                  whole((nk, tk, hdk)), whole((nk, hdv, tk))],
        out_specs=pl.BlockSpec((tq, hdv), lambda b, i: (b * nq + i, 0)),
        out_shape=jax.ShapeDtypeStruct((B * nq * tq, hdv), BF16),
        scratch_shapes=[pltpu.VMEM((heads, tk, tq), F32),
                        pltpu.VMEM((heads, hdv // heads, tq), F32)],
        compiler_params=_cparams(("arbitrary", "arbitrary")), name=name,
    )(qT, k, vT)


def _merge_kernel(x_ref, ya_ref, yb_ref, yc_ref, yd_ref, g_ref, wg_ref, wb_ref, wo_ref,
                  fg_ref, rw_ref, rb_ref, x1_ref, h2_ref, rt_ref, cnt_ref, h2_scr):
    @pl.when(pl.program_id(0) == 0)
    def _():
        h2_scr[...] = jnp.zeros_like(h2_scr)

    router = _router_stages(h2_scr, rw_ref, rb_ref, rt_ref, cnt_ref)
    x = x_ref[...]
    h = _rms(x, g_ref[...]).astype(BF16)
    merged = None
    for bi, y_ref in enumerate((ya_ref, yb_ref, yc_ref, yd_ref)):
        t = jax.nn.sigmoid(_dot(h, wg_ref[bi])) * _dot(y_ref[...], wb_ref[bi])
        merged = t if merged is None else merged + t
        next(router)
    x1 = x + _dot(merged.astype(BF16), wo_ref[...])
    x1_ref[...] = x1
    h2_new = _rms(x1, fg_ref[...])
    h2_ref[...] = h2_new.astype(BF16)
    for _ in router:
        pass
    h2_scr[...] = h2_new


def _router_stages(h2_scr, rw_ref, rb_ref, rt_ref, cnt_ref):
    h2 = h2_scr[...]
    nt = (((1,), (1,)), ((), ()))
    h2_hi = h2.astype(BF16)
    h2_lo = (h2 - h2_hi.astype(F32)).astype(BF16)
    rw = rw_ref[...]
    rw_hi = rw.astype(BF16)
    rw_lo = (rw - rw_hi.astype(F32)).astype(BF16)
    logits = (lax.dot_general(rw_hi, h2_hi, nt, preferred_element_type=F32)
              + lax.dot_general(rw_hi, h2_lo, nt, preferred_element_type=F32)
              + lax.dot_general(rw_lo, h2_hi, nt, preferred_element_type=F32))
    yield
    scores = jax.nn.sigmoid(logits)
    biased = scores + rb_ref[...]
    s_rows = [scores[e:e + 1, :] for e in range(N_EXPERTS)]
    b_rows = [biased[e:e + 1, :] for e in range(N_EXPERTS)]

    def top2(vals):
        m1 = vals[0]
        i1 = jnp.zeros_like(m1, dtype=jnp.int32)
        for j in range(1, len(vals)):
            better = vals[j] > m1
            i1 = jnp.where(better, j, i1)
            m1 = jnp.where(better, vals[j], m1)
        m2 = None
        i2 = None
        for j in range(len(vals)):
            cand = jnp.where(i1 == j, NEG, vals[j])
            if m2 is None:
                m2, i2 = cand, jnp.zeros_like(i1)
            else:
                better = cand > m2
                i2 = jnp.where(better, j, i2)
                m2 = jnp.where(better, cand, m2)
        return m1, i1, m2, i2

    best = None
    for gi in range(N_GROUPS):
        bm1, bi1, bm2, bi2 = top2(b_rows[gi * EPG:(gi + 1) * EPG])
        gs = bm1 + bm2
        e1 = bi1 + gi * EPG
        e2 = bi2 + gi * EPG
        if best is None:
            best = (gs, e1, e2)
        else:
            better = gs > best[0]
            best = (jnp.where(better, gs, best[0]), jnp.where(better, e1, best[1]),
                    jnp.where(better, e2, best[2]))
        if gi % 2 == 1:
            yield
    _, e1, e2 = best
    w1 = jnp.zeros_like(s_rows[0])
    w2 = jnp.zeros_like(s_rows[0])
    for e in range(N_EXPERTS):
        w1 = jnp.where(e1 == e, s_rows[e], w1)
        w2 = jnp.where(e2 == e, s_rows[e], w2)
    tot = w1 + w2
    w1 = w1 / tot
    w2 = w2 / tot
    r8 = lax.broadcasted_iota(jnp.int32, (8, scores.shape[1]), 0)
    rt_ref[...] = jnp.where(r8 == 0, e1.astype(F32), jnp.where(r8 == 1, e2.astype(F32),
                            jnp.where(r8 == 2, w1, jnp.where(r8 == 3, w2, 0.0))))
    row = lax.broadcasted_iota(jnp.int32, scores.shape, 0)
    assigned = jnp.where(row == e1, 1.0, jnp.where(row == e2, 1.0, 0.0))
    lane = lax.broadcasted_iota(jnp.int32, (N_EXPERTS, LANES), 1)
    cnt = jnp.zeros((N_EXPERTS, LANES), F32)
    for k in range(scores.shape[1] // MOE_SUB):
        part = jnp.sum(assigned[:, k * MOE_SUB:(k + 1) * MOE_SUB], axis=1, keepdims=True)
        cnt = jnp.where(lane == k, part, cnt)
    cnt_ref[0] = cnt.astype(jnp.int32)
    yield


def _merge(x2, ya, yb, yc, yd, lw, router_w_t, router_b):
    T, D = x2.shape
    tm = TM_MERGE
    n = T // tm
    tok = lambda w: pl.BlockSpec((tm, w), lambda i: (jnp.minimum(i, n - 1), 0))
    consts = [lw["attn_norm"], lw["w_gate"], lw["w_branch"], lw["w_out"], lw["ffn_norm"],
              router_w_t, router_b]
    return pl.pallas_call(
        _merge_kernel, grid=(n + 1,),
        in_specs=[tok(D), tok(256), tok(256), tok(256), tok(256)] + [_full(a.shape) for a in consts],
        out_specs=[tok(D), tok(D), pl.BlockSpec((8, tm), lambda i: (0, jnp.maximum(i - 1, 0))),
                   pl.BlockSpec((1, N_EXPERTS, LANES), lambda i: (jnp.maximum(i - 1, 0), 0, 0))],
        out_shape=[jax.ShapeDtypeStruct((T, D), F32), jax.ShapeDtypeStruct((T, D), BF16),
                   jax.ShapeDtypeStruct((8, T), F32),
                   jax.ShapeDtypeStruct((n, N_EXPERTS, LANES), jnp.int32)],
        scratch_shapes=[pltpu.VMEM((tm, D), F32)],
        compiler_params=_cparams(("arbitrary",)), name="merge",
    )(x2, ya, yb, yc, yd, *consts)


def _moe_kernel(cnt_ref, x1_ref, h2_ref, rt_ref, tri_ref, sg_ref, su_ref, sd_ref,
                wg_ref, wu_ref, wd_ref, o_ref, rank_scr, gate_scr):
    i = pl.program_id(0)
    e = pl.program_id(1)
    tm = h2_ref.shape[0]

    def ffn(rows, scale, wg, wu, wd):
        a = jax.nn.silu(_dot(rows, wg)) * _dot(rows, wu)
        if scale is not None:
            a = a * scale
        return _dot(a.astype(BF16), wd)

    sub = tri_ref.shape[0]
    nsub = tm // sub
    spans = [slice(k * sub, (k + 1) * sub) for k in range(nsub)]

    @pl.when(e == 0)
    def _():
        for r in range(0, tm, MOE_SHARED_ROWS):
            rs = slice(r, r + MOE_SHARED_ROWS)
            o_ref[rs, :] = x1_ref[rs, :] + ffn(h2_ref[rs, :], None, sg_ref[...], su_ref[...],
                                               sd_ref[...])
        row = lax.broadcasted_iota(jnp.int32, (N_EXPERTS, tm), 0).astype(F32)
        e1, e2 = rt_ref[0:1, :], rt_ref[1:2, :]
        w1, w2 = rt_ref[2:3, :], rt_ref[3:4, :]
        assigned = jnp.where(row == e1, 1.0, jnp.where(row == e2, 1.0, 0.0))
        for sp in spans:
            before = _dot(assigned[:, sp].astype(BF16), tri_ref[...])
            rank_scr[:, sp] = jnp.where(assigned[:, sp] > 0.0, before, -1.0)
        gate_scr[...] = jnp.where(row == e1, w1, jnp.where(row == e2, w2, 0.0))

    @pl.when(e > 0)
    def _():
        subs = range(MOE_EXPERTS_PER_STEP)
        ex = [(e - 1) * MOE_EXPERTS_PER_STEP + s for s in subs]
        cnt = [cnt_ref[(i * nsub + k) * N_EXPERTS + x] for x in ex for k in range(nsub)]
        rank_row = [rank_scr[pl.ds(x, 1), :] for x in ex]
        gate_row = [gate_scr[pl.ds(x, 1), :] for x in ex]
        slot = lax.broadcasted_iota(jnp.int32, (MOE_CHUNK, sub), 0).astype(F32)

        def chunk(c, carry):
            base = slot + (c * MOE_CHUNK).astype(F32)
            onehot, rows, scale = [], [], []
            for sp in spans:
                hit = [rank_row[s][:, sp] == base for s in subs]
                onehot.append(jnp.concatenate(
                    [jnp.where(h, 1.0, 0.0).astype(BF16) for h in hit], axis=0))
                rows.append(_dot(onehot[-1], h2_ref[sp, :]).astype(BF16))
                scale.append([jnp.sum(jnp.where(hit[s], gate_row[s][:, sp], 0.0),
                                      axis=1, keepdims=True) for s in subs])
            y = []
            for s in subs:
                mine = slice(s * MOE_CHUNK, (s + 1) * MOE_CHUNK)
                y.append(ffn(jnp.concatenate([r[mine] for r in rows], axis=0),
                             jnp.concatenate([sc[s] for sc in scale], axis=0),
                             wg_ref[0, s], wu_ref[0, s], wd_ref[0, s]).astype(BF16))
            for k, sp in enumerate(spans):
                rows_k = slice(k * MOE_CHUNK, (k + 1) * MOE_CHUNK)
                o_ref[sp, :] += lax.dot_general(
                    onehot[k], jnp.concatenate([y[s][rows_k] for s in subs], axis=0),
                    (((0,), (0,)), ((), ())), preferred_element_type=F32)
            return carry

        most = functools.reduce(jnp.maximum, cnt)
        lax.fori_loop(0, (most + MOE_CHUNK - 1) // MOE_CHUNK, chunk, 0)


def _moe(x1, h2, route, counts, tri, shared, layer, wg, wu, wd):
    T, D = x1.shape
    tm = min(TM_MOE, T)
    sub = tri.shape[0]
    _, ne, _, ff = wg.shape
    const = lambda shp: pl.BlockSpec(shp, lambda i, e, c: (0, 0), pipeline_mode=pl.Buffered(1))
    eps = MOE_EXPERTS_PER_STEP
    expert = lambda shp: pl.BlockSpec((1, eps) + shp,
                                      lambda i, e, c: (layer, jnp.maximum(e - 1, 0), 0, 0))
    grid_spec = pltpu.PrefetchScalarGridSpec(
        num_scalar_prefetch=1, grid=(T // tm, ne // eps + 1),
        in_specs=[pl.BlockSpec((tm, D), lambda i, e, c: (i, 0), pipeline_mode=pl.Buffered(1)),
                  pl.BlockSpec((tm, D), lambda i, e, c: (i, 0), pipeline_mode=pl.Buffered(1)),
                  pl.BlockSpec((8, tm), lambda i, e, c: (0, i)),
                  const((sub, sub)), const((D, ff)), const((D, ff)), const((ff, D)),
                  expert((D, ff)), expert((D, ff)), expert((ff, D))],
        out_specs=pl.BlockSpec((tm, D), lambda i, e, c: (i, 0), pipeline_mode=pl.Buffered(1)),
        scratch_shapes=[pltpu.VMEM((N_EXPERTS, tm), F32), pltpu.VMEM((N_EXPERTS, tm), F32)])
    return pl.pallas_call(
        _moe_kernel, grid_spec=grid_spec,
        out_shape=jax.ShapeDtypeStruct((T, D), F32),
        compiler_params=_cparams(("arbitrary", "arbitrary")), name="moe",
    )(counts, x1, h2, route, tri, *shared, wg, wu, wd)


def _constants(S):
    half = MLA_ROPE // 2
    inv = ROPE_THETA ** (-np.arange(half, dtype=np.float64) / half)
    ang = np.arange(S, dtype=np.float64)[:, None] * inv[None, :]
    cos = np.ones((S, LANES), np.float32)
    cos[:, MLA_NOPE:MLA_NOPE + half] = np.cos(ang)
    cos[:, MLA_NOPE + half:MLA_QK] = np.cos(ang)
    sin = np.zeros((S, LANES), np.float32)
    sin[:, MLA_NOPE:MLA_NOPE + half] = -np.sin(ang)
    sin[:, MLA_NOPE + half:MLA_QK] = np.sin(ang)

    nkb = S // MOBA_BLOCK
    kpos = np.arange(S)
    kc = np.zeros((S, MOBA_HD), np.float32)
    blk = kpos // MOBA_BLOCK
    m = blk < min(nkb, MOBA_HD) - 1
    kc[kpos[m], blk[m]] = 1.0
    kc[:, MOBA_HD - 1] = kpos % MOBA_BLOCK
    kc4 = np.zeros((S, MOBA_HEADS, 2 * MOBA_HD), np.float32)
    kc4[:, :, MOBA_HD:] = kc[:, None, :]
    return {"cos": jnp.asarray(cos), "sin": jnp.asarray(sin),
            "kc": jnp.asarray(kc4.reshape(S, -1), BF16)}


def _head_pad(v):
    return jnp.concatenate([v, jnp.zeros((LANES - MLA_QK,), F32)])[None, :]


def _layer_weights(l, p):
    D = p["w_in"].shape[1]
    w_in = p["w_in"][l]
    o0 = MLA_Q_LORA + MLA_KV_LORA
    o1 = o0 + MLA_ROPE
    w = jnp.concatenate([w_in[:, :o0], jnp.zeros((D, MLA_NOPE), F32), w_in[:, o0:o1],
                         jnp.zeros((D, LANES - MLA_QK), F32), w_in[:, o1:]], axis=1)

    wq = p["mla_w_q_up"][l].reshape(MLA_Q_LORA, MLA_HEADS, MLA_QK)
    wq = jnp.pad(wq, ((0, 0), (0, 0), (0, LANES - MLA_QK))).reshape(MLA_Q_LORA, MLA_HEADS * LANES)
    wkv = p["mla_w_kv_up"][l].reshape(MLA_KV_LORA, MLA_HEADS, MLA_NOPE + MLA_V)
    wk = jnp.pad(wkv[:, :, :MLA_NOPE], ((0, 0), (0, 0), (0, LANES - MLA_NOPE)))
    wk = wk.reshape(MLA_KV_LORA, MLA_HEADS * LANES)
    wv = wkv[:, :, MLA_NOPE:].reshape(MLA_KV_LORA, MLA_HEADS * MLA_V)

    tri = jnp.tril(jnp.ones((SGU_CHUNK, SGU_CHUNK), dtype=bool))
    sgu_w = jnp.where(tri[None], p["sgu_w"][l], 0.0)
    sgu_b = jnp.repeat(p["sgu_b"][l].T, SGU_W // SGU_GROUPS, axis=1)
    pool_w = jax.scipy.linalg.block_diag(*[p["pool_w"][l, gi] for gi in range(len(POOL_WINDOWS))])

    return {
        "attn_norm": p["attn_norm"][l][None, :], "w_in": w.astype(BF16),
        "mla_q_norm": p["mla_q_norm"][l][None, :], "wq": wq.astype(BF16),
        "mla_kv_norm": p["mla_kv_norm"][l][None, :],
        "wkv": jnp.concatenate([wk, wv], axis=1).astype(BF16),
        "mla_qg": _head_pad(p["mla_q_gain"][l]), "mla_kg": _head_pad(p["mla_k_gain"][l]),
        "sgu_norm": p["sgu_norm"][l][None, :], "sgu_w": sgu_w.astype(BF16), "sgu_b": sgu_b,
        "pool_w": pool_w.astype(BF16), "pool_b": p["pool_b"][l][None, :],
        "pool_scale": p["pool_scale"][l][None, :],
        "moba_qg": jnp.tile(p["moba_q_gain"][l], 2)[None, :],
        "moba_kg": jnp.tile(p["moba_k_gain"][l], 2)[None, :],
        "w_gate": p["w_gate"][l].astype(BF16), "w_branch": p["w_branch"][l].astype(BF16),
        "w_out": p["w_out"][l].astype(BF16), "ffn_norm": p["ffn_norm"][l][None, :],
        "shared": (p["shared_w_gate"][l].astype(BF16), p["shared_w_up"][l].astype(BF16),
                   p["shared_w_down"][l].astype(BF16)),
    }


def kernel(x, attn_norm, w_in, mla_q_norm, mla_w_q_up, mla_kv_norm, mla_w_kv_up, mla_q_gain, mla_k_gain, sgu_norm, sgu_w, sgu_b, pool_w, pool_b, pool_scale, moba_q_gain, moba_k_gain, w_gate, w_branch, w_out, ffn_norm, router_w, router_bias, moe_w_gate, moe_w_up, moe_w_down, shared_w_gate, shared_w_up, shared_w_down):
    p = dict(attn_norm=attn_norm, w_in=w_in, mla_q_norm=mla_q_norm, mla_w_q_up=mla_w_q_up,
             mla_kv_norm=mla_kv_norm, mla_w_kv_up=mla_w_kv_up, mla_q_gain=mla_q_gain,
             mla_k_gain=mla_k_gain, sgu_norm=sgu_norm, sgu_w=sgu_w, sgu_b=sgu_b, pool_w=pool_w,
             pool_b=pool_b, pool_scale=pool_scale, moba_q_gain=moba_q_gain,
             moba_k_gain=moba_k_gain, w_gate=w_gate, w_branch=w_branch, w_out=w_out,
             ffn_norm=ffn_norm, moe_w_gate=moe_w_gate, moe_w_up=moe_w_up, moe_w_down=moe_w_down,
             shared_w_gate=shared_w_gate, shared_w_up=shared_w_up, shared_w_down=shared_w_down)
    B, S, D = x.shape
    T = B * S
    depth = attn_norm.shape[0]
    nkb = S // MOBA_BLOCK
    assert TQ == TK and S % TK == 0 and T % TM_MERGE == 0 and nkb <= MOBA_HD
    consts = _constants(S)
    router_w_t = router_w.T
    router_b = router_bias[:, None]
    assert min(TM_MOE, T) % TM_MERGE == 0 and N_EXPERTS % MOE_EXPERTS_PER_STEP == 0
    tri = jnp.asarray(np.triu(np.ones((MOE_SUB, MOE_SUB), np.float32), 1), BF16)

    ffn_w = (moe_w_gate.astype(BF16), moe_w_up.astype(BF16), moe_w_down.astype(BF16))
    x2 = x.reshape(T, D)
    for l in range(depth):
        lw = _layer_weights(l, p)
        qa, ka, va, yb, yc, qd, kd, vd, km = _in_proj(x2, lw, consts, S)
        ya = _flash(qa, ka.reshape(B, S // TK, TK, -1), va, MLA_HEADS, True, "mla_attn")

        kmean = km.reshape(B, nkb, MOBA_HEADS, MOBA_HD).transpose(0, 2, 1, 3)
        kmean = jnp.pad(kmean, ((0, 0), (0, 0), (0, MOBA_HD - nkb), (0, 0)))
        q_aug = _moba_select(qd, kmean)
        yd = _flash(q_aug, kd.reshape(B, S // TK, TK, -1), vd, MOBA_HEADS, False, "moba_attn")

        x1, h2, route, cnt = _merge(x2, ya, yb, yc, yd, lw, router_w_t, router_b)
        counts = cnt[:, :, :TM_MERGE // MOE_SUB].transpose(0, 2, 1).reshape(-1)
        x2 = _moe(x1, h2, route, counts, tri, lw["shared"], l, *ffn_w)
    return x2.reshape(B, S, D)
```

```python
import functools
import math

import jax
import jax.numpy as jnp
import numpy as np
from jax import lax
from jax.experimental import pallas as pl
from jax.experimental.pallas import tpu as pltpu

F32 = jnp.float32
BF16 = jnp.bfloat16
EPS = 1e-6
NEG = -1e30

LANES = 128
VMEM_LIMIT = 56 * 1024 * 1024

MLA_HEADS = 4
MLA_NOPE = 64
MLA_ROPE = 32
MLA_V = 64
MLA_Q_LORA = 256
MLA_KV_LORA = 128
ROPE_THETA = 10000.0
MLA_QK = MLA_NOPE + MLA_ROPE

SGU_GROUPS = 4
SGU_CHUNK = 128
SGU_W = 256

POOL_WINDOWS = (2, 4, 8, 16)
POOL_MAXW = 16
POOL_W = 256

MOBA_HEADS = 4
MOBA_HD = 64
MOBA_BLOCK = 256
MOBA_TOPK = 3

N_EXPERTS = 16
N_GROUPS = 4
EPG = 4

C_MLA = 0
C_SGU = 512
C_POOL = 1024
C_MOBA = 1280
C_END = 2048

TQ = 512
TK = 512
TM_IN = TQ
TM_MERGE = 512
TM_MOE = 2048
SEL_L = 2048
FLASH_UNROLLS = (4, 2, 1)
FLASH_AHEAD = 2
MOE_EXPERTS_PER_STEP = 4
MOE_SHARED_ROWS = 512
MOE_SUB = 256
MOE_CHUNK = 48
ONES_ROWS = 16


def _cparams(sem):
    return pltpu.CompilerParams(dimension_semantics=sem, vmem_limit_bytes=VMEM_LIMIT)


def _full(shape):
    n = len(shape)
    return pl.BlockSpec(shape, lambda *_: (0,) * n, pipeline_mode=pl.Buffered(1))


def _rms(x, g):
    return x * lax.rsqrt(jnp.mean(x * x, axis=-1, keepdims=True) + EPS) * g


def _dot(a, b):
    return jnp.dot(a, b, preferred_element_type=F32)


def _in_proj_kernel(x_ref, g_ref, w_ref, qn_ref, wq_ref, kvn_ref, wkv_ref,
                    qg_ref, kg_ref, cos_ref, sin_ref,
                    sn_ref, sw_ref, sb_ref, pw_ref, pb_ref, ps_ref,
                    mqg_ref, mkg_ref, kc_ref,
                    qa_ref, ka_ref, va_ref, yb_ref, yc_ref, qd_ref, kd_ref, vd_ref, km_ref,
                    halo_ref, buf_ref, proj_scr, *, tiles_per_seq, q_scale):
    tm = x_ref.shape[0]
    i = pl.program_id(0)

    @pl.when(i % tiles_per_seq == 0)
    def _():
        halo_ref[...] = jnp.zeros_like(halo_ref)

    x = x_ref[...]
    h = _rms(x, g_ref[...]).astype(BF16)
    lane = lax.broadcasted_iota(jnp.int32, (tm, LANES), 1)

    cm = _dot(h, w_ref[:, C_MLA:C_SGU])
    c_q = _rms(cm[:, :MLA_Q_LORA], qn_ref[...]).astype(BF16)
    c_kv = _rms(cm[:, MLA_Q_LORA:MLA_Q_LORA + MLA_KV_LORA], kvn_ref[...]).astype(BF16)
    kpe_grp = cm[:, MLA_Q_LORA + MLA_KV_LORA:]
    q = _dot(c_q, wq_ref[...])
    kv = _dot(c_kv, wkv_ref[...])
    va_ref[0, 0] = kv[:, 512:].T.astype(BF16)
    proj_scr[...] = _dot(h, w_ref[:, C_SGU:C_END])
    cos = cos_ref[...]
    sin = sin_ref[...]
    inv_d = 1.0 / MLA_QK
    rope_lo = lane < MLA_NOPE + MLA_ROPE // 2

    def head_norm_rope(t, gain):
        ms = jnp.sum(t * t, axis=-1, keepdims=True) * inv_d
        t = t * lax.rsqrt(ms + EPS) * gain
        rot = jnp.where(rope_lo, pltpu.roll(t, LANES - MLA_ROPE // 2, 1),
                        pltpu.roll(t, MLA_ROPE // 2, 1))
        return t * cos + rot * sin

    heads = [slice(hh * LANES, (hh + 1) * LANES) for hh in range(MLA_HEADS)]
    qh = [head_norm_rope(q[:, sl], qg_ref[...]) * q_scale for sl in heads]
    qa_ref[0, 0] = jnp.concatenate(qh, axis=1).T.astype(BF16)
    kh = [head_norm_rope(kv[:, sl] + kpe_grp, kg_ref[...]) for sl in heads]
    ka_ref[...] = jnp.concatenate(kh, axis=1).astype(BF16)

    z = jax.nn.gelu(proj_scr[:, 0:C_POOL - C_SGU])
    u = z[:, :SGU_W]
    vv = _rms(z[:, SGU_W:], sn_ref[...]).astype(BF16)
    lane_grp = lax.broadcasted_iota(jnp.int32, (SGU_CHUNK, SGU_W), 1) // (SGU_W // SGU_GROUPS)
    for c in range(tm // SGU_CHUNK):
        vc = vv[c * SGU_CHUNK:(c + 1) * SGU_CHUNK, :]
        y = sb_ref[...]
        for gi in range(SGU_GROUPS):
            y = y + jnp.where(lane_grp == gi, _dot(sw_ref[gi], vc), 0.0)
        yb_ref[c * SGU_CHUNK:(c + 1) * SGU_CHUNK, :] = (
            u[c * SGU_CHUNK:(c + 1) * SGU_CHUNK, :] * y).astype(BF16)

    xp = proj_scr[:, C_POOL - C_SGU:C_MOBA - C_SGU]
    buf_ref[0:POOL_MAXW, :] = halo_ref[...]
    buf_ref[POOL_MAXW:, :] = xp
    halo_ref[...] = xp[tm - POOL_MAXW:, :]
    pos = (i % tiles_per_seq) * tm + lax.broadcasted_iota(jnp.int32, (tm, POOL_W), 0)
    lane_w = lax.broadcasted_iota(jnp.int32, (tm, POOL_W), 1) // (POOL_W // len(POOL_WINDOWS))
    acc = xp
    pooled = jnp.zeros((tm, POOL_W), F32)
    shift = 1
    for gi, w in enumerate(POOL_WINDOWS):
        while shift < w:
            acc = acc + buf_ref[POOL_MAXW - shift:POOL_MAXW - shift + tm, :]
            shift += 1
        cnt = jnp.minimum(pos + 1, w).astype(F32)
        pooled = jnp.where(lane_w == gi, acc / cnt, pooled)
    p = (pooled - xp).astype(BF16)
    yc_ref[...] = ((_dot(p, pw_ref[...]) + pb_ref[...]) * ps_ref[...]).astype(BF16)

    cd = proj_scr[:, C_MOBA - C_SGU:]
    inv_hd = 1.0 / MOBA_HD
    lo = lane < MOBA_HD

    def pair_norm(t, gain):
        sq = t * t
        s_all = jnp.sum(sq, axis=-1, keepdims=True)
        s_lo = jnp.sum(jnp.where(lo, sq, 0.0), axis=-1, keepdims=True)
        ms = jnp.where(lo, s_lo, s_all - s_lo) * inv_hd
        return t * lax.rsqrt(ms + EPS) * gain

    pairs = [slice(g * LANES, (g + 1) * LANES) for g in range(MOBA_HEADS // 2)]
    qd = jnp.concatenate([pair_norm(cd[:, sl], mqg_ref[...]) for sl in pairs], axis=1)
    qd_ref[0] = qd.T
    kp = [pair_norm(cd[:, 256 + g * LANES:256 + (g + 1) * LANES], mkg_ref[...])
          for g in range(MOBA_HEADS // 2)]
    kd = jnp.concatenate(kp, axis=1)
    kh = []
    for hh in range(MOBA_HEADS):
        src = kp[hh // 2] if hh % 2 == 0 else pltpu.roll(kp[hh // 2], MOBA_HD, 1)
        kh.append(jnp.where(lo, src, kc_ref[:, hh * LANES:(hh + 1) * LANES].astype(F32)))
    kd_ref[...] = jnp.concatenate(kh, axis=1).astype(BF16)
    vd_ref[0, 0] = cd[:, 512:].T.astype(BF16)
    for c in range(tm // MOBA_BLOCK):
        km_ref[c] = jnp.mean(kd[c * MOBA_BLOCK:(c + 1) * MOBA_BLOCK, :], axis=0, keepdims=True)


def _in_proj(x2, lw, consts, S):
    T, D = x2.shape
    tm = TM_IN
    nt = T // tm
    tok = lambda w: pl.BlockSpec((tm, w), lambda i: (i, 0))
    pos_spec = pl.BlockSpec((tm, LANES), lambda i: (i % (S // tm), 0))
    ins = [
        (x2, tok(D)), (lw["attn_norm"], None), (lw["w_in"], None),
        (lw["mla_q_norm"], None), (lw["wq"], None), (lw["mla_kv_norm"], None), (lw["wkv"], None),
        (lw["mla_qg"], None), (lw["mla_kg"], None),
        (consts["cos"], pos_spec), (consts["sin"], pos_spec),
        (lw["sgu_norm"], None), (lw["sgu_w"], None), (lw["sgu_b"], None),
        (lw["pool_w"], None), (lw["pool_b"], None), (lw["pool_scale"], None),
        (lw["moba_qg"], None), (lw["moba_kg"], None),
        (consts["kc"], pl.BlockSpec((tm, 4 * LANES), lambda i: (i % (S // tm), 0))),
    ]
    arrays = [a for a, _ in ins]
    specs = [s if s is not None else _full(a.shape) for a, s in ins]
    nb = tm // MOBA_BLOCK
    B = T // S
    ns = S // tm
    tr = lambda f: pl.BlockSpec((1, 1, f, tm), lambda i: (i // ns, i % ns, 0, 0))
    out_shape = [
        jax.ShapeDtypeStruct((B, ns, 512, tm), BF16), jax.ShapeDtypeStruct((T, 512), BF16),
        jax.ShapeDtypeStruct((B, ns, 256, tm), BF16), jax.ShapeDtypeStruct((T, 256), BF16),
        jax.ShapeDtypeStruct((T, 256), BF16), jax.ShapeDtypeStruct((B, 256, S), F32),
        jax.ShapeDtypeStruct((T, 512), BF16), jax.ShapeDtypeStruct((B, ns, 256, tm), BF16),
        jax.ShapeDtypeStruct((T // MOBA_BLOCK, 1, 256), F32),
    ]
    out_specs = [tr(512), tok(512), tr(256), tok(256), tok(256),
                 pl.BlockSpec((1, 256, tm), lambda i: (i // ns, 0, i % ns)),
                 tok(512), tr(256), pl.BlockSpec((nb, 1, 256), lambda i: (i, 0, 0))]
    q_scale = (MLA_QK ** -0.5) * math.log2(math.e)
    return pl.pallas_call(
        functools.partial(_in_proj_kernel, tiles_per_seq=S // tm, q_scale=q_scale),
        grid=(nt,), in_specs=specs, out_specs=out_specs, out_shape=out_shape,
        scratch_shapes=[pltpu.VMEM((POOL_MAXW, POOL_W), F32),
                        pltpu.VMEM((tm + POOL_MAXW, POOL_W), F32),
                        pltpu.VMEM((tm, C_END - C_SGU), F32)],
        compiler_params=_cparams(("arbitrary",)), name="in_proj",
    )(*arrays)


def _moba_select_kernel(q_ref, km_ref, o_ref, *, slope_log2_step):
    hh = pl.program_id(1)
    li = pl.program_id(2)
    q = q_ref[0]
    L = q.shape[1]
    nrow = km_ref.shape[2]
    g = jnp.dot(km_ref[0, 0], q, preferred_element_type=F32, precision=lax.Precision.HIGHEST)
    n_i = lax.broadcasted_iota(jnp.int32, (nrow, L), 0)
    own = (li * L + lax.broadcasted_iota(jnp.int32, (nrow, L), 1)) // MOBA_BLOCK
    g = jnp.where(n_i < own, g, NEG)
    sel = jnp.zeros((nrow, L), F32)
    for _ in range(MOBA_TOPK):
        mx = jnp.max(g, axis=0, keepdims=True)
        first = jnp.min(jnp.where(g == mx, n_i, nrow), axis=0, keepdims=True)
        hit = n_i == first
        sel = jnp.where(hit & (mx > 0.5 * NEG), 1.0, sel)
        g = jnp.where(hit, NEG, g)
    slope = jnp.exp2(-slope_log2_step * (hh + 1).astype(F32))
    rel = (n_i - own).astype(F32) * (slope * MOBA_BLOCK)
    bias = jnp.where((sel > 0.0) | (n_i == own), rel, NEG)
    bias = jnp.where(n_i == nrow - 1, slope, bias)
    qs = (q * (MOBA_HD ** -0.5)).astype(BF16)
    bias = bias.astype(BF16)
    tq = o_ref.shape[3]
    for t in range(L // tq):
        o_ref[0, t, 0:MOBA_HD, :] = qs[:, t * tq:(t + 1) * tq]
        o_ref[0, t, MOBA_HD:, :] = bias[:, t * tq:(t + 1) * tq]


def _moba_select(qgT, kmean):
    B, _, S = qgT.shape
    H, hd = MOBA_HEADS, MOBA_HD
    L = min(SEL_L, S)
    return pl.pallas_call(
        functools.partial(_moba_select_kernel, slope_log2_step=8.0 / MOBA_HEADS),
        grid=(B, H, S // L),
        in_specs=[pl.BlockSpec((1, hd, L), lambda b, h, l: (b, h, l)),
                  pl.BlockSpec((1, 1, 64, hd), lambda b, h, l: (b, h, 0, 0))],
        out_specs=pl.BlockSpec((1, L // TQ, 2 * hd, TQ), lambda b, h, l: (b, l, h, 0)),
        out_shape=jax.ShapeDtypeStruct((B, S // TQ, H * 2 * hd, TQ), BF16),
        compiler_params=_cparams(("arbitrary",) * 3), name="moba_select",
    )(qgT, kmean)


def _flash_kernel(q_ref, k_ref, v_ref, o_ref, s_scr, acc_scr, *, heads, use_exp2):
    H = heads
    tq = q_ref.shape[3]
    tk = k_ref.shape[2]
    dk = q_ref.shape[2] // H
    dv = v_ref.shape[2] // H
    ratio = tk // tq
    ex = jnp.exp2 if use_exp2 else jnp.exp
    i = pl.program_id(1)
    jd = i // ratio
    kpos = jd * tk + lax.broadcasted_iota(jnp.int32, (tk, tq), 0)
    qpos = i * tq + lax.broadcasted_iota(jnp.int32, (tk, tq), 1)
    causal = kpos <= qpos
    ones = jnp.ones((ONES_ROWS, tk), BF16)

    def scores(jt, h, masked):
        s = _dot(k_ref[0, jt, :, h * dk:(h + 1) * dk], q_ref[0, 0, h * dk:(h + 1) * dk, :])
        if masked:
            s = jnp.where(causal, s, NEG)
        s_scr[h] = s
        return jnp.max(s, axis=0, keepdims=True)

    def update(jt, h, cmax, m):
        m_new = jnp.maximum(m, cmax)
        alpha = ex(m - m_new)
        p = ex(s_scr[h] - m_new).astype(BF16)
        vt = jnp.concatenate([v_ref[0, jt, h * dv:(h + 1) * dv, :], ones], axis=0)
        acc_scr[h] = acc_scr[h] * alpha + _dot(vt, p)
        return m_new

    acc_scr[...] = jnp.zeros_like(acc_scr)
    ms = [jnp.full((1, tq), NEG, F32) for _ in range(H)]
    ahead = FLASH_AHEAD
    cmax = [scores(jd, h, True) for h in range(ahead)]
    for h in range(H):
        nh = h + ahead
        cmax.append(scores(jd, nh, True) if nh < H else scores(0, nh - H, False))
        ms[h] = update(jd, h, cmax.pop(0), ms[h])

    def tile_step(j, carry):
        ms, cmax = carry
        ms, cmax = list(ms), list(cmax)
        jn = jnp.minimum(j + 1, jd - 1)
        for h in range(H):
            nh = h + ahead
            cmax.append(scores(j, nh, False) if nh < H else scores(jn, nh - H, False))
            ms[h] = update(j, h, cmax.pop(0), ms[h])
        return tuple(ms), tuple(cmax)

    carry = (tuple(ms), tuple(cmax))
    done = 0
    for unroll in FLASH_UNROLLS:
        def body(t, c, unroll=unroll, done=done):
            for u in range(unroll):
                c = tile_step(done + unroll * t + u, c)
            return c

        trips = (jd - done) // unroll
        carry = lax.fori_loop(0, trips, body, carry)
        done = done + trips * unroll
    outs = []
    for h in range(H):
        acc = acc_scr[h]
        outs.append(acc[:dv] / acc[dv:dv + 1])
    o_ref[...] = jnp.concatenate(outs, axis=0).T.astype(BF16)


def _flash(qT, k, vT, heads, use_exp2, name):
    B, nq, hdk, tq = qT.shape
    nk, tk = k.shape[1:3]
    hdv = vT.shape[2]
    dva = hdv // heads + ONES_ROWS
    whole = lambda shp: pl.BlockSpec((1,) + shp, lambda b, i: (b, 0, 0, 0),
                                     pipeline_mode=pl.Buffered(1))
    return pl.pallas_call(
        functools.partial(_flash_kernel, heads=heads, use_exp2=use_exp2),
        grid=(B, nq),
        in_specs=[pl.BlockSpec((1, 1, hdk, tq), lambda b, i: (b, i, 0, 0)),
                  whole((nk, tk, hdk)), whole((nk, hdv, tk))],
        out_specs=pl.BlockSpec((tq, hdv), lambda b, i: (b * nq + i, 0)),
        out_shape=jax.ShapeDtypeStruct((B * nq * tq, hdv), BF16),
        scratch_shapes=[pltpu.VMEM((heads, tk, tq), F32), pltpu.VMEM((heads, dva, tq), F32)],
        compiler_params=_cparams(("arbitrary", "arbitrary")), name=name,
    )(qT, k, vT)


def _merge_kernel(x_ref, ya_ref, yb_ref, yc_ref, yd_ref, g_ref, wg_ref, wb_ref, wo_ref,
                  fg_ref, rw_ref, rb_ref, x1_ref, h2_ref, rt_ref, cnt_ref, h2_scr):
    @pl.when(pl.program_id(0) == 0)
    def _():
        h2_scr[...] = jnp.zeros_like(h2_scr)

    router = _router_stages(h2_scr, rw_ref, rb_ref, rt_ref, cnt_ref)
    x = x_ref[...]
    h = _rms(x, g_ref[...]).astype(BF16)
    merged = None
    for bi, y_ref in enumerate((ya_ref, yb_ref, yc_ref, yd_ref)):
        t = jax.nn.sigmoid(_dot(h, wg_ref[bi])) * _dot(y_ref[...], wb_ref[bi])
        merged = t if merged is None else merged + t
        next(router)
    x1 = x + _dot(merged.astype(BF16), wo_ref[...])
    x1_ref[...] = x1
    h2_new = _rms(x1, fg_ref[...])
    h2_ref[...] = h2_new.astype(BF16)
    for _ in router:
        pass
    h2_scr[...] = h2_new


def _router_stages(h2_scr, rw_ref, rb_ref, rt_ref, cnt_ref):
    h2 = h2_scr[...]
    nt = (((1,), (1,)), ((), ()))
    h2_hi = h2.astype(BF16)
    h2_lo = (h2 - h2_hi.astype(F32)).astype(BF16)
    rw = rw_ref[...]
    rw_hi = rw.astype(BF16)
    rw_lo = (rw - rw_hi.astype(F32)).astype(BF16)
    logits = (lax.dot_general(rw_hi, h2_hi, nt, preferred_element_type=F32)
              + lax.dot_general(rw_hi, h2_lo, nt, preferred_element_type=F32)
              + lax.dot_general(rw_lo, h2_hi, nt, preferred_element_type=F32))
    yield
    scores = jax.nn.sigmoid(logits)
    biased = scores + rb_ref[...]
    s_rows = [scores[e:e + 1, :] for e in range(N_EXPERTS)]
    b_rows = [biased[e:e + 1, :] for e in range(N_EXPERTS)]

    def top2(vals):
        m1 = vals[0]
        i1 = jnp.zeros_like(m1, dtype=jnp.int32)
        for j in range(1, len(vals)):
            better = vals[j] > m1
            i1 = jnp.where(better, j, i1)
            m1 = jnp.where(better, vals[j], m1)
        m2 = None
        i2 = None
        for j in range(len(vals)):
            cand = jnp.where(i1 == j, NEG, vals[j])
            if m2 is None:
                m2, i2 = cand, jnp.zeros_like(i1)
            else:
                better = cand > m2
                i2 = jnp.where(better, j, i2)
                m2 = jnp.where(better, cand, m2)
        return m1, i1, m2, i2

    best = None
    for gi in range(N_GROUPS):
        bm1, bi1, bm2, bi2 = top2(b_rows[gi * EPG:(gi + 1) * EPG])
        gs = bm1 + bm2
        e1 = bi1 + gi * EPG
        e2 = bi2 + gi * EPG
        if best is None:
            best = (gs, e1, e2)
        else:
            better = gs > best[0]
            best = (jnp.where(better, gs, best[0]), jnp.where(better, e1, best[1]),
                    jnp.where(better, e2, best[2]))
        if gi % 2 == 1:
            yield
    _, e1, e2 = best
    w1 = jnp.zeros_like(s_rows[0])
    w2 = jnp.zeros_like(s_rows[0])
    for e in range(N_EXPERTS):
        w1 = jnp.where(e1 == e, s_rows[e], w1)
        w2 = jnp.where(e2 == e, s_rows[e], w2)
    tot = w1 + w2
    w1 = w1 / tot
    w2 = w2 / tot
    r8 = lax.broadcasted_iota(jnp.int32, (8, scores.shape[1]), 0)
    rt_ref[...] = jnp.where(r8 == 0, e1.astype(F32), jnp.where(r8 == 1, e2.astype(F32),
                            jnp.where(r8 == 2, w1, jnp.where(r8 == 3, w2, 0.0))))
    row = lax.broadcasted_iota(jnp.int32, scores.shape, 0)
    assigned = jnp.where(row == e1, 1.0, jnp.where(row == e2, 1.0, 0.0))
    lane = lax.broadcasted_iota(jnp.int32, (N_EXPERTS, LANES), 1)
    cnt = jnp.zeros((N_EXPERTS, LANES), F32)
    for k in range(scores.shape[1] // MOE_SUB):
        part = jnp.sum(assigned[:, k * MOE_SUB:(k + 1) * MOE_SUB], axis=1, keepdims=True)
        cnt = jnp.where(lane == k, part, cnt)
    cnt_ref[0] = cnt.astype(jnp.int32)
    yield


def _merge(x2, ya, yb, yc, yd, lw, router_w_t, router_b):
    T, D = x2.shape
    tm = TM_MERGE
    n = T // tm
    tok = lambda w: pl.BlockSpec((tm, w), lambda i: (jnp.minimum(i, n - 1), 0))
    consts = [lw["attn_norm"], lw["w_gate"], lw["w_branch"], lw["w_out"], lw["ffn_norm"],
              router_w_t, router_b]
    return pl.pallas_call(
        _merge_kernel, grid=(n + 1,),
        in_specs=[tok(D), tok(256), tok(256), tok(256), tok(256)] + [_full(a.shape) for a in consts],
        out_specs=[tok(D), tok(D), pl.BlockSpec((8, tm), lambda i: (0, jnp.maximum(i - 1, 0))),
                   pl.BlockSpec((1, N_EXPERTS, LANES), lambda i: (jnp.maximum(i - 1, 0), 0, 0))],
        out_shape=[jax.ShapeDtypeStruct((T, D), F32), jax.ShapeDtypeStruct((T, D), BF16),
                   jax.ShapeDtypeStruct((8, T), F32),
                   jax.ShapeDtypeStruct((n, N_EXPERTS, LANES), jnp.int32)],
        scratch_shapes=[pltpu.VMEM((tm, D), F32)],
        compiler_params=_cparams(("arbitrary",)), name="merge",
    )(x2, ya, yb, yc, yd, *consts)


def _moe_kernel(cnt_ref, x1_ref, h2_ref, rt_ref, tri_ref, sg_ref, su_ref, sd_ref,
                wg_ref, wu_ref, wd_ref, o_ref, rank_scr, gate_scr):
    i = pl.program_id(0)
    e = pl.program_id(1)
    tm = h2_ref.shape[0]

    def ffn(rows, scale, wg, wu, wd):
        a = jax.nn.silu(_dot(rows, wg)) * _dot(rows, wu)
        if scale is not None:
            a = a * scale
        return _dot(a.astype(BF16), wd)

    sub = tri_ref.shape[0]
    nsub = tm // sub
    spans = [slice(k * sub, (k + 1) * sub) for k in range(nsub)]

    @pl.when(e == 0)
    def _():
        for r in range(0, tm, MOE_SHARED_ROWS):
            rs = slice(r, r + MOE_SHARED_ROWS)
            o_ref[rs, :] = x1_ref[rs, :] + ffn(h2_ref[rs, :], None, sg_ref[...], su_ref[...],
                                               sd_ref[...])
        row = lax.broadcasted_iota(jnp.int32, (N_EXPERTS, tm), 0).astype(F32)
        e1, e2 = rt_ref[0:1, :], rt_ref[1:2, :]
        w1, w2 = rt_ref[2:3, :], rt_ref[3:4, :]
        assigned = jnp.where(row == e1, 1.0, jnp.where(row == e2, 1.0, 0.0))
        for sp in spans:
            before = _dot(assigned[:, sp].astype(BF16), tri_ref[...])
            rank_scr[:, sp] = jnp.where(assigned[:, sp] > 0.0, before, -1.0)
        gate_scr[...] = jnp.where(row == e1, w1, jnp.where(row == e2, w2, 0.0))

    @pl.when(e > 0)
    def _():
        subs = range(MOE_EXPERTS_PER_STEP)
        ex = [(e - 1) * MOE_EXPERTS_PER_STEP + s for s in subs]
        cnt = [cnt_ref[(i * nsub + k) * N_EXPERTS + x] for x in ex for k in range(nsub)]
        rank_row = [rank_scr[pl.ds(x, 1), :] for x in ex]
        gate_row = [gate_scr[pl.ds(x, 1), :] for x in ex]
        slot = lax.broadcasted_iota(jnp.int32, (MOE_CHUNK, sub), 0).astype(F32)

        def chunk(c, carry):
            base = slot + (c * MOE_CHUNK).astype(F32)
            onehot, rows, scale = [], [], []
            for sp in spans:
                hit = [rank_row[s][:, sp] == base for s in subs]
                onehot.append(jnp.concatenate(
                    [jnp.where(h, 1.0, 0.0).astype(BF16) for h in hit], axis=0))
                rows.append(_dot(onehot[-1], h2_ref[sp, :]).astype(BF16))
                scale.append([jnp.sum(jnp.where(hit[s], gate_row[s][:, sp], 0.0),
                                      axis=1, keepdims=True) for s in subs])
            y = []
            for s in subs:
                mine = slice(s * MOE_CHUNK, (s + 1) * MOE_CHUNK)
                y.append(ffn(jnp.concatenate([r[mine] for r in rows], axis=0),
                             jnp.concatenate([sc[s] for sc in scale], axis=0),
                             wg_ref[0, s], wu_ref[0, s], wd_ref[0, s]).astype(BF16))
            for k, sp in enumerate(spans):
                rows_k = slice(k * MOE_CHUNK, (k + 1) * MOE_CHUNK)
                o_ref[sp, :] += lax.dot_general(
                    onehot[k], jnp.concatenate([y[s][rows_k] for s in subs], axis=0),
                    (((0,), (0,)), ((), ())), preferred_element_type=F32)
            return carry

        most = functools.reduce(jnp.maximum, cnt)
        lax.fori_loop(0, (most + MOE_CHUNK - 1) // MOE_CHUNK, chunk, 0)


def _moe(x1, h2, route, counts, tri, shared, layer, wg, wu, wd):
    T, D = x1.shape
    tm = min(TM_MOE, T)
    sub = tri.shape[0]
    _, ne, _, ff = wg.shape
    const = lambda shp: pl.BlockSpec(shp, lambda i, e, c: (0, 0), pipeline_mode=pl.Buffered(1))
    eps = MOE_EXPERTS_PER_STEP
    expert = lambda shp: pl.BlockSpec((1, eps) + shp,
                                      lambda i, e, c: (layer, jnp.maximum(e - 1, 0), 0, 0))
    grid_spec = pltpu.PrefetchScalarGridSpec(
        num_scalar_prefetch=1, grid=(T // tm, ne // eps + 1),
        in_specs=[pl.BlockSpec((tm, D), lambda i, e, c: (i, 0), pipeline_mode=pl.Buffered(1)),
                  pl.BlockSpec((tm, D), lambda i, e, c: (i, 0), pipeline_mode=pl.Buffered(1)),
                  pl.BlockSpec((8, tm), lambda i, e, c: (0, i)),
                  const((sub, sub)), const((D, ff)), const((D, ff)), const((ff, D)),
                  expert((D, ff)), expert((D, ff)), expert((ff, D))],
        out_specs=pl.BlockSpec((tm, D), lambda i, e, c: (i, 0), pipeline_mode=pl.Buffered(1)),
        scratch_shapes=[pltpu.VMEM((N_EXPERTS, tm), F32), pltpu.VMEM((N_EXPERTS, tm), F32)])
    return pl.pallas_call(
        _moe_kernel, grid_spec=grid_spec,
        out_shape=jax.ShapeDtypeStruct((T, D), F32),
        compiler_params=_cparams(("arbitrary", "arbitrary")), name="moe",
    )(counts, x1, h2, route, tri, *shared, wg, wu, wd)


def _constants(S):
    half = MLA_ROPE // 2
    inv = ROPE_THETA ** (-np.arange(half, dtype=np.float64) / half)
    ang = np.arange(S, dtype=np.float64)[:, None] * inv[None, :]
    cos = np.ones((S, LANES), np.float32)
    cos[:, MLA_NOPE:MLA_NOPE + half] = np.cos(ang)
    cos[:, MLA_NOPE + half:MLA_QK] = np.cos(ang)
    sin = np.zeros((S, LANES), np.float32)
    sin[:, MLA_NOPE:MLA_NOPE + half] = -np.sin(ang)
    sin[:, MLA_NOPE + half:MLA_QK] = np.sin(ang)

    nkb = S // MOBA_BLOCK
    kpos = np.arange(S)
    kc = np.zeros((S, MOBA_HD), np.float32)
    blk = kpos // MOBA_BLOCK
    m = blk < min(nkb, MOBA_HD) - 1
    kc[kpos[m], blk[m]] = 1.0
    kc[:, MOBA_HD - 1] = kpos % MOBA_BLOCK
    kc4 = np.zeros((S, MOBA_HEADS, 2 * MOBA_HD), np.float32)
    kc4[:, :, MOBA_HD:] = kc[:, None, :]
    return {"cos": jnp.asarray(cos), "sin": jnp.asarray(sin),
            "kc": jnp.asarray(kc4.reshape(S, -1), BF16)}


def _head_pad(v):
    return jnp.concatenate([v, jnp.zeros((LANES - MLA_QK,), F32)])[None, :]


def _layer_weights(l, p):
    D = p["w_in"].shape[1]
    w_in = p["w_in"][l]
    o0 = MLA_Q_LORA + MLA_KV_LORA
    o1 = o0 + MLA_ROPE
    w = jnp.concatenate([w_in[:, :o0], jnp.zeros((D, MLA_NOPE), F32), w_in[:, o0:o1],
                         jnp.zeros((D, LANES - MLA_QK), F32), w_in[:, o1:]], axis=1)

    wq = p["mla_w_q_up"][l].reshape(MLA_Q_LORA, MLA_HEADS, MLA_QK)
    wq = jnp.pad(wq, ((0, 0), (0, 0), (0, LANES - MLA_QK))).reshape(MLA_Q_LORA, MLA_HEADS * LANES)
    wkv = p["mla_w_kv_up"][l].reshape(MLA_KV_LORA, MLA_HEADS, MLA_NOPE + MLA_V)
    wk = jnp.pad(wkv[:, :, :MLA_NOPE], ((0, 0), (0, 0), (0, LANES - MLA_NOPE)))
    wk = wk.reshape(MLA_KV_LORA, MLA_HEADS * LANES)
    wv = wkv[:, :, MLA_NOPE:].reshape(MLA_KV_LORA, MLA_HEADS * MLA_V)

    tri = jnp.tril(jnp.ones((SGU_CHUNK, SGU_CHUNK), dtype=bool))
    sgu_w = jnp.where(tri[None], p["sgu_w"][l], 0.0)
    sgu_b = jnp.repeat(p["sgu_b"][l].T, SGU_W // SGU_GROUPS, axis=1)
    pool_w = jax.scipy.linalg.block_diag(*[p["pool_w"][l, gi] for gi in range(len(POOL_WINDOWS))])

    return {
        "attn_norm": p["attn_norm"][l][None, :], "w_in": w.astype(BF16),
        "mla_q_norm": p["mla_q_norm"][l][None, :], "wq": wq.astype(BF16),
        "mla_kv_norm": p["mla_kv_norm"][l][None, :],
        "wkv": jnp.concatenate([wk, wv], axis=1).astype(BF16),
        "mla_qg": _head_pad(p["mla_q_gain"][l]), "mla_kg": _head_pad(p["mla_k_gain"][l]),
        "sgu_norm": p["sgu_norm"][l][None, :], "sgu_w": sgu_w.astype(BF16), "sgu_b": sgu_b,
        "pool_w": pool_w.astype(BF16), "pool_b": p["pool_b"][l][None, :],
        "pool_scale": p["pool_scale"][l][None, :],
        "moba_qg": jnp.tile(p["moba_q_gain"][l], 2)[None, :],
        "moba_kg": jnp.tile(p["moba_k_gain"][l], 2)[None, :],
        "w_gate": p["w_gate"][l].astype(BF16), "w_branch": p["w_branch"][l].astype(BF16),
        "w_out": p["w_out"][l].astype(BF16), "ffn_norm": p["ffn_norm"][l][None, :],
        "shared": (p["shared_w_gate"][l].astype(BF16), p["shared_w_up"][l].astype(BF16),
                   p["shared_w_down"][l].astype(BF16)),
    }


def kernel(x, attn_norm, w_in, mla_q_norm, mla_w_q_up, mla_kv_norm, mla_w_kv_up, mla_q_gain, mla_k_gain, sgu_norm, sgu_w, sgu_b, pool_w, pool_b, pool_scale, moba_q_gain, moba_k_gain, w_gate, w_branch, w_out, ffn_norm, router_w, router_bias, moe_w_gate, moe_w_up, moe_w_down, shared_w_gate, shared_w_up, shared_w_down):
    p = dict(attn_norm=attn_norm, w_in=w_in, mla_q_norm=mla_q_norm, mla_w_q_up=mla_w_q_up,
             mla_kv_norm=mla_kv_norm, mla_w_kv_up=mla_w_kv_up, mla_q_gain=mla_q_gain,
             mla_k_gain=mla_k_gain, sgu_norm=sgu_norm, sgu_w=sgu_w, sgu_b=sgu_b, pool_w=pool_w,
             pool_b=pool_b, pool_scale=pool_scale, moba_q_gain=moba_q_gain,
             moba_k_gain=moba_k_gain, w_gate=w_gate, w_branch=w_branch, w_out=w_out,
             ffn_norm=ffn_norm, moe_w_gate=moe_w_gate, moe_w_up=moe_w_up, moe_w_down=moe_w_down,
             shared_w_gate=shared_w_gate, shared_w_up=shared_w_up, shared_w_down=shared_w_down)
    B, S, D = x.shape
    T = B * S
    depth = attn_norm.shape[0]
    nkb = S // MOBA_BLOCK
    assert TQ == TK and S % TK == 0 and T % TM_MERGE == 0 and nkb <= MOBA_HD
    consts = _constants(S)
    router_w_t = router_w.T
    router_b = router_bias[:, None]
    assert min(TM_MOE, T) % TM_MERGE == 0 and N_EXPERTS % MOE_EXPERTS_PER_STEP == 0
    tri = jnp.asarray(np.triu(np.ones((MOE_SUB, MOE_SUB), np.float32), 1), BF16)

    ffn_w = (moe_w_gate.astype(BF16), moe_w_up.astype(BF16), moe_w_down.astype(BF16))
    x2 = x.reshape(T, D)
    for l in range(depth):
        lw = _layer_weights(l, p)
        qa, ka, va, yb, yc, qd, kd, vd, km = _in_proj(x2, lw, consts, S)
        ya = _flash(qa, ka.reshape(B, S // TK, TK, -1), va, MLA_HEADS, True, "mla_attn")

        kmean = km.reshape(B, nkb, MOBA_HEADS, MOBA_HD).transpose(0, 2, 1, 3)
        kmean = jnp.pad(kmean, ((0, 0), (0, 0), (0, MOBA_HD - nkb), (0, 0)))
        q_aug = _moba_select(qd, kmean)
        yd = _flash(q_aug, kd.reshape(B, S // TK, TK, -1), vd, MOBA_HEADS, False, "moba_attn")

        x1, h2, route, cnt = _merge(x2, ya, yb, yc, yd, lw, router_w_t, router_b)
        counts = cnt[:, :, :TM_MERGE // MOE_SUB].transpose(0, 2, 1).reshape(-1)
        x2 = _moe(x1, h2, route, counts, tri, lw["shared"], l, *ffn_w)
    return x2.reshape(B, S, D)
```

```python
import functools
import math

import jax
import jax.numpy as jnp
import numpy as np
from jax import lax
from jax.experimental import pallas as pl
from jax.experimental.pallas import tpu as pltpu

F32 = jnp.float32
BF16 = jnp.bfloat16
EPS = 1e-6
NEG = -1e30

LANES = 128
VMEM_LIMIT = 56 * 1024 * 1024

MLA_HEADS = 4
MLA_NOPE = 64
MLA_ROPE = 32
MLA_V = 64
MLA_Q_LORA = 256
MLA_KV_LORA = 128
ROPE_THETA = 10000.0
MLA_QK = MLA_NOPE + MLA_ROPE

SGU_GROUPS = 4
SGU_CHUNK = 128
SGU_W = 256

POOL_WINDOWS = (2, 4, 8, 16)
POOL_MAXW = 16
POOL_W = 256

MOBA_HEADS = 4
MOBA_HD = 64
MOBA_BLOCK = 256
MOBA_TOPK = 3

N_EXPERTS = 16
N_GROUPS = 4
EPG = 4

C_MLA = 0
C_SGU = 512
C_POOL = 1024
C_MOBA = 1280
C_END = 2048

TQ = 512
TK = 512
TM_IN = TQ
TM_MERGE = 512
TM_MOE = 2048
SEL_L = 2048
FLASH_UNROLLS = (4, 2, 1)
FLASH_AHEAD = 2
MOE_EXPERTS_PER_STEP = 4
MOE_SHARED_ROWS = 512
MOE_SUB = 256
MOE_CHUNK = 48
ONES_ROWS = 16


def _cparams(sem):
    return pltpu.CompilerParams(dimension_semantics=sem, vmem_limit_bytes=VMEM_LIMIT)


def _full(shape):
    n = len(shape)
    return pl.BlockSpec(shape, lambda *_: (0,) * n, pipeline_mode=pl.Buffered(1))


def _rms(x, g):
    return x * lax.rsqrt(jnp.mean(x * x, axis=-1, keepdims=True) + EPS) * g


def _dot(a, b):
    return jnp.dot(a, b, preferred_element_type=F32)


def _in_proj_kernel(x_ref, g_ref, w_ref, qn_ref, wq_ref, kvn_ref, wkv_ref,
                    qg_ref, kg_ref, swap_ref, cos_ref, sin_ref,
                    sn_ref, sw_ref, sb_ref, pw_ref, pb_ref, ps_ref,
                    mqg_ref, mkg_ref, kc_ref,
                    qa_ref, ka_ref, va_ref, yb_ref, yc_ref, qd_ref, kd_ref, vd_ref, km_ref,
                    halo_ref, buf_ref, proj_scr, *, tiles_per_seq, q_scale):
    tm = x_ref.shape[0]
    i = pl.program_id(0)

    @pl.when(i % tiles_per_seq == 0)
    def _():
        halo_ref[...] = jnp.zeros_like(halo_ref)

    x = x_ref[...]
    h = _rms(x, g_ref[...]).astype(BF16)
    lane = lax.broadcasted_iota(jnp.int32, (tm, LANES), 1)

    cm = _dot(h, w_ref[:, C_MLA:C_SGU])
    c_q = _rms(cm[:, :MLA_Q_LORA], qn_ref[...]).astype(BF16)
    c_kv = _rms(cm[:, MLA_Q_LORA:MLA_Q_LORA + MLA_KV_LORA], kvn_ref[...]).astype(BF16)
    kpe_grp = cm[:, MLA_Q_LORA + MLA_KV_LORA:]
    q = _dot(c_q, wq_ref[...])
    kv = _dot(c_kv, wkv_ref[...])
    va_ref[0, 0] = kv[:, 512:].T.astype(BF16)
    proj_scr[...] = _dot(h, w_ref[:, C_SGU:C_END])
    cos = cos_ref[...]
    sin = sin_ref[...]
    inv_d = 1.0 / MLA_QK
    rope_lo = lane < MLA_NOPE + MLA_ROPE // 2

    def head_norm_rope(t, gain):
        ms = jnp.sum(t * t, axis=-1, keepdims=True) * inv_d
        t = t * lax.rsqrt(ms + EPS) * gain
        rot = _dot(t.astype(BF16), swap_ref[...])
        return t * cos + rot * sin

    heads = [slice(hh * LANES, (hh + 1) * LANES) for hh in range(MLA_HEADS)]
    qh = [head_norm_rope(q[:, sl], qg_ref[...]) * q_scale for sl in heads]
    qa_ref[0, 0] = jnp.concatenate(qh, axis=1).T.astype(BF16)
    kh = [head_norm_rope(kv[:, sl] + kpe_grp, kg_ref[...]) for sl in heads]
    ka_ref[...] = jnp.concatenate(kh, axis=1).astype(BF16)

    z = jax.nn.gelu(proj_scr[:, 0:C_POOL - C_SGU])
    u = z[:, :SGU_W]
    vv = _rms(z[:, SGU_W:], sn_ref[...]).astype(BF16)
    lane_grp = lax.broadcasted_iota(jnp.int32, (SGU_CHUNK, SGU_W), 1) // (SGU_W // SGU_GROUPS)
    for c in range(tm // SGU_CHUNK):
        vc = vv[c * SGU_CHUNK:(c + 1) * SGU_CHUNK, :]
        y = sb_ref[...]
        for gi in range(SGU_GROUPS):
            y = y + jnp.where(lane_grp == gi, _dot(sw_ref[gi], vc), 0.0)
        yb_ref[c * SGU_CHUNK:(c + 1) * SGU_CHUNK, :] = (
            u[c * SGU_CHUNK:(c + 1) * SGU_CHUNK, :] * y).astype(BF16)

    xp = proj_scr[:, C_POOL - C_SGU:C_MOBA - C_SGU]
    buf_ref[0:POOL_MAXW, :] = halo_ref[...]
    buf_ref[POOL_MAXW:, :] = xp
    halo_ref[...] = xp[tm - POOL_MAXW:, :]
    pos = (i % tiles_per_seq) * tm + lax.broadcasted_iota(jnp.int32, (tm, POOL_W), 0)
    lane_w = lax.broadcasted_iota(jnp.int32, (tm, POOL_W), 1) // (POOL_W // len(POOL_WINDOWS))
    acc = xp
    pooled = jnp.zeros((tm, POOL_W), F32)
    shift = 1
    for gi, w in enumerate(POOL_WINDOWS):
        while shift < w:
            acc = acc + buf_ref[POOL_MAXW - shift:POOL_MAXW - shift + tm, :]
            shift += 1
        cnt = jnp.minimum(pos + 1, w).astype(F32)
        pooled = jnp.where(lane_w == gi, acc / cnt, pooled)
    p = (pooled - xp).astype(BF16)
    yc_ref[...] = ((_dot(p, pw_ref[...]) + pb_ref[...]) * ps_ref[...]).astype(BF16)

    cd = proj_scr[:, C_MOBA - C_SGU:]
    inv_hd = 1.0 / MOBA_HD
    lo = lane < MOBA_HD

    def pair_norm(t, gain):
        sq = t * t
        s_all = jnp.sum(sq, axis=-1, keepdims=True)
        s_lo = jnp.sum(jnp.where(lo, sq, 0.0), axis=-1, keepdims=True)
        ms = jnp.where(lo, s_lo, s_all - s_lo) * inv_hd
        return t * lax.rsqrt(ms + EPS) * gain

    pairs = [slice(g * LANES, (g + 1) * LANES) for g in range(MOBA_HEADS // 2)]
    qd = jnp.concatenate([pair_norm(cd[:, sl], mqg_ref[...]) for sl in pairs], axis=1)
    qd_ref[0] = qd.T
    kp = [pair_norm(cd[:, 256 + g * LANES:256 + (g + 1) * LANES], mkg_ref[...])
          for g in range(MOBA_HEADS // 2)]
    kd = jnp.concatenate(kp, axis=1)
    kh = []
    for hh in range(MOBA_HEADS):
        src = kp[hh // 2] if hh % 2 == 0 else pltpu.roll(kp[hh // 2], MOBA_HD, 1)
        kh.append(jnp.where(lo, src, kc_ref[:, hh * LANES:(hh + 1) * LANES].astype(F32)))
    kd_ref[...] = jnp.concatenate(kh, axis=1).astype(BF16)
    vd_ref[0, 0] = cd[:, 512:].T.astype(BF16)
    for c in range(tm // MOBA_BLOCK):
        km_ref[c] = jnp.mean(kd[c * MOBA_BLOCK:(c + 1) * MOBA_BLOCK, :], axis=0, keepdims=True)


def _in_proj(x2, lw, consts, S):
    T, D = x2.shape
    tm = TM_IN
    nt = T // tm
    tok = lambda w: pl.BlockSpec((tm, w), lambda i: (i, 0))
    pos_spec = pl.BlockSpec((tm, LANES), lambda i: (i % (S // tm), 0))
    ins = [
        (x2, tok(D)), (lw["attn_norm"], None), (lw["w_in"], None),
        (lw["mla_q_norm"], None), (lw["wq"], None), (lw["mla_kv_norm"], None), (lw["wkv"], None),
        (lw["mla_qg"], None), (lw["mla_kg"], None), (consts["swap"], None),
        (consts["cos"], pos_spec), (consts["sin"], pos_spec),
        (lw["sgu_norm"], None), (lw["sgu_w"], None), (lw["sgu_b"], None),
        (lw["pool_w"], None), (lw["pool_b"], None), (lw["pool_scale"], None),
        (lw["moba_qg"], None), (lw["moba_kg"], None),
        (consts["kc"], pl.BlockSpec((tm, 4 * LANES), lambda i: (i % (S // tm), 0))),
    ]
    arrays = [a for a, _ in ins]
    specs = [s if s is not None else _full(a.shape) for a, s in ins]
    nb = tm // MOBA_BLOCK
    B = T // S
    ns = S // tm
    tr = lambda f: pl.BlockSpec((1, 1, f, tm), lambda i: (i // ns, i % ns, 0, 0))
    out_shape = [
        jax.ShapeDtypeStruct((B, ns, 512, tm), BF16), jax.ShapeDtypeStruct((T, 512), BF16),
        jax.ShapeDtypeStruct((B, ns, 256, tm), BF16), jax.ShapeDtypeStruct((T, 256), BF16),
        jax.ShapeDtypeStruct((T, 256), BF16), jax.ShapeDtypeStruct((B, 256, S), F32),
        jax.ShapeDtypeStruct((T, 512), BF16), jax.ShapeDtypeStruct((B, ns, 256, tm), BF16),
        jax.ShapeDtypeStruct((T // MOBA_BLOCK, 1, 256), F32),
    ]
    out_specs = [tr(512), tok(512), tr(256), tok(256), tok(256),
                 pl.BlockSpec((1, 256, tm), lambda i: (i // ns, 0, i % ns)),
                 tok(512), tr(256), pl.BlockSpec((nb, 1, 256), lambda i: (i, 0, 0))]
    q_scale = (MLA_QK ** -0.5) * math.log2(math.e)
    return pl.pallas_call(
        functools.partial(_in_proj_kernel, tiles_per_seq=S // tm, q_scale=q_scale),
        grid=(nt,), in_specs=specs, out_specs=out_specs, out_shape=out_shape,
        scratch_shapes=[pltpu.VMEM((POOL_MAXW, POOL_W), F32),
                        pltpu.VMEM((tm + POOL_MAXW, POOL_W), F32),
                        pltpu.VMEM((tm, C_END - C_SGU), F32)],
        compiler_params=_cparams(("arbitrary",)), name="in_proj",
    )(*arrays)


def _moba_select_kernel(q_ref, km_ref, o_ref, *, slope_log2_step):
    hh = pl.program_id(1)
    li = pl.program_id(2)
    q = q_ref[0]
    L = q.shape[1]
    nrow = km_ref.shape[2]
    g = jnp.dot(km_ref[0, 0], q, preferred_element_type=F32, precision=lax.Precision.HIGHEST)
    n_i = lax.broadcasted_iota(jnp.int32, (nrow, L), 0)
    own = (li * L + lax.broadcasted_iota(jnp.int32, (nrow, L), 1)) // MOBA_BLOCK
    g = jnp.where(n_i < own, g, NEG)
    sel = jnp.zeros((nrow, L), F32)
    for _ in range(MOBA_TOPK):
        mx = jnp.max(g, axis=0, keepdims=True)
        first = jnp.min(jnp.where(g == mx, n_i, nrow), axis=0, keepdims=True)
        hit = n_i == first
        sel = jnp.where(hit & (mx > 0.5 * NEG), 1.0, sel)
        g = jnp.where(hit, NEG, g)
    slope = jnp.exp2(-slope_log2_step * (hh + 1).astype(F32))
    rel = (n_i - own).astype(F32) * (slope * MOBA_BLOCK)
    bias = jnp.where((sel > 0.0) | (n_i == own), rel, NEG)
    bias = jnp.where(n_i == nrow - 1, slope, bias)
    qs = (q * (MOBA_HD ** -0.5)).astype(BF16)
    bias = bias.astype(BF16)
    tq = o_ref.shape[3]
    for t in range(L // tq):
        o_ref[0, t, 0:MOBA_HD, :] = qs[:, t * tq:(t + 1) * tq]
        o_ref[0, t, MOBA_HD:, :] = bias[:, t * tq:(t + 1) * tq]


def _moba_select(qgT, kmean):
    B, _, S = qgT.shape
    H, hd = MOBA_HEADS, MOBA_HD
    L = min(SEL_L, S)
    return pl.pallas_call(
        functools.partial(_moba_select_kernel, slope_log2_step=8.0 / MOBA_HEADS),
        grid=(B, H, S // L),
        in_specs=[pl.BlockSpec((1, hd, L), lambda b, h, l: (b, h, l)),
                  pl.BlockSpec((1, 1, 64, hd), lambda b, h, l: (b, h, 0, 0))],
        out_specs=pl.BlockSpec((1, L // TQ, 2 * hd, TQ), lambda b, h, l: (b, l, h, 0)),
        out_shape=jax.ShapeDtypeStruct((B, S // TQ, H * 2 * hd, TQ), BF16),
        compiler_params=_cparams(("arbitrary",) * 3), name="moba_select",
    )(qgT, kmean)


def _flash_kernel(q_ref, k_ref, v_ref, o_ref, s_scr, acc_scr, *, heads, use_exp2):
    H = heads
    tq = q_ref.shape[3]
    tk = k_ref.shape[2]
    dk = q_ref.shape[2] // H
    dv = v_ref.shape[2] // H
    ratio = tk // tq
    ex = jnp.exp2 if use_exp2 else jnp.exp
    i = pl.program_id(1)
    jd = i // ratio
    kpos = jd * tk + lax.broadcasted_iota(jnp.int32, (tk, tq), 0)
    qpos = i * tq + lax.broadcasted_iota(jnp.int32, (tk, tq), 1)
    causal = kpos <= qpos
    ones = jnp.ones((ONES_ROWS, tk), BF16)

    def scores(jt, h, masked):
        s = _dot(k_ref[0, jt, :, h * dk:(h + 1) * dk], q_ref[0, 0, h * dk:(h + 1) * dk, :])
        if masked:
            s = jnp.where(causal, s, NEG)
        s_scr[h] = s
        return jnp.max(s, axis=0, keepdims=True)

    def update(jt, h, cmax, m):
        m_new = jnp.maximum(m, cmax)
        alpha = ex(m - m_new)
        p = ex(s_scr[h] - m_new).astype(BF16)
        vt = jnp.concatenate([v_ref[0, jt, h * dv:(h + 1) * dv, :], ones], axis=0)
        acc_scr[h] = acc_scr[h] * alpha + _dot(vt, p)
        return m_new

    acc_scr[...] = jnp.zeros_like(acc_scr)
    ms = [jnp.full((1, tq), NEG, F32) for _ in range(H)]
    ahead = FLASH_AHEAD
    cmax = [scores(jd, h, True) for h in range(ahead)]
    for h in range(H):
        nh = h + ahead
        cmax.append(scores(jd, nh, True) if nh < H else scores(0, nh - H, False))
        ms[h] = update(jd, h, cmax.pop(0), ms[h])

    def tile_step(j, carry):
        ms, cmax = carry
        ms, cmax = list(ms), list(cmax)
        jn = jnp.minimum(j + 1, jd - 1)
        for h in range(H):
            nh = h + ahead
            cmax.append(scores(j, nh, False) if nh < H else scores(jn, nh - H, False))
            ms[h] = update(j, h, cmax.pop(0), ms[h])
        return tuple(ms), tuple(cmax)

    carry = (tuple(ms), tuple(cmax))
    done = 0
    for unroll in FLASH_UNROLLS:
        def body(t, c, unroll=unroll, done=done):
            for u in range(unroll):
                c = tile_step(done + unroll * t + u, c)
            return c

        trips = (jd - done) // unroll
        carry = lax.fori_loop(0, trips, body, carry)
        done = done + trips * unroll
    outs = []
    for h in range(H):
        acc = acc_scr[h]
        outs.append(acc[:dv] / acc[dv:dv + 1])
    o_ref[...] = jnp.concatenate(outs, axis=0).T.astype(BF16)


def _flash(qT, k, vT, heads, use_exp2, name):
    B, nq, hdk, tq = qT.shape
    nk, tk = k.shape[1:3]
    hdv = vT.shape[2]
    dva = hdv // heads + ONES_ROWS
    whole = lambda shp: pl.BlockSpec((1,) + shp, lambda b, i: (b, 0, 0, 0),
                                     pipeline_mode=pl.Buffered(1))
    return pl.pallas_call(
        functools.partial(_flash_kernel, heads=heads, use_exp2=use_exp2),
        grid=(B, nq),
        in_specs=[pl.BlockSpec((1, 1, hdk, tq), lambda b, i: (b, i, 0, 0)),
                  whole((nk, tk, hdk)), whole((nk, hdv, tk))],
        out_specs=pl.BlockSpec((tq, hdv), lambda b, i: (b * nq + i, 0)),
        out_shape=jax.ShapeDtypeStruct((B * nq * tq, hdv), BF16),
        scratch_shapes=[pltpu.VMEM((heads, tk, tq), F32), pltpu.VMEM((heads, dva, tq), F32)],
        compiler_params=_cparams(("arbitrary", "arbitrary")), name=name,
    )(qT, k, vT)


def _merge_kernel(x_ref, ya_ref, yb_ref, yc_ref, yd_ref, g_ref, wg_ref, wb_ref, wo_ref,
                  fg_ref, rw_ref, rb_ref, x1_ref, h2_ref, rt_ref, cnt_ref, h2_scr):
    @pl.when(pl.program_id(0) == 0)
    def _():
        h2_scr[...] = jnp.zeros_like(h2_scr)

    router = _router_stages(h2_scr, rw_ref, rb_ref, rt_ref, cnt_ref)
    x = x_ref[...]
    h = _rms(x, g_ref[...]).astype(BF16)
    merged = None
    for bi, y_ref in enumerate((ya_ref, yb_ref, yc_ref, yd_ref)):
        t = jax.nn.sigmoid(_dot(h, wg_ref[bi])) * _dot(y_ref[...], wb_ref[bi])
        merged = t if merged is None else merged + t
        next(router)
    x1 = x + _dot(merged.astype(BF16), wo_ref[...])
    x1_ref[...] = x1
    h2_new = _rms(x1, fg_ref[...])
    h2_ref[...] = h2_new.astype(BF16)
    for _ in router:
        pass
    h2_scr[...] = h2_new


def _router_stages(h2_scr, rw_ref, rb_ref, rt_ref, cnt_ref):
    h2 = h2_scr[...]
    nt = (((1,), (1,)), ((), ()))
    h2_hi = h2.astype(BF16)
    h2_lo = (h2 - h2_hi.astype(F32)).astype(BF16)
    rw = rw_ref[...]
    rw_hi = rw.astype(BF16)
    rw_lo = (rw - rw_hi.astype(F32)).astype(BF16)
    logits = (lax.dot_general(rw_hi, h2_hi, nt, preferred_element_type=F32)
              + lax.dot_general(rw_hi, h2_lo, nt, preferred_element_type=F32)
              + lax.dot_general(rw_lo, h2_hi, nt, preferred_element_type=F32))
    yield
    scores = jax.nn.sigmoid(logits)
    biased = scores + rb_ref[...]
    s_rows = [scores[e:e + 1, :] for e in range(N_EXPERTS)]
    b_rows = [biased[e:e + 1, :] for e in range(N_EXPERTS)]

    def top2(vals):
        m1 = vals[0]
        i1 = jnp.zeros_like(m1, dtype=jnp.int32)
        for j in range(1, len(vals)):
            better = vals[j] > m1
            i1 = jnp.where(better, j, i1)
            m1 = jnp.where(better, vals[j], m1)
        m2 = None
        i2 = None
        for j in range(len(vals)):
            cand = jnp.where(i1 == j, NEG, vals[j])
            if m2 is None:
                m2, i2 = cand, jnp.zeros_like(i1)
            else:
                better = cand > m2
                i2 = jnp.where(better, j, i2)
                m2 = jnp.where(better, cand, m2)
        return m1, i1, m2, i2

    best = None
    for gi in range(N_GROUPS):
        bm1, bi1, bm2, bi2 = top2(b_rows[gi * EPG:(gi + 1) * EPG])
        gs = bm1 + bm2
        e1 = bi1 + gi * EPG
        e2 = bi2 + gi * EPG
        if best is None:
            best = (gs, e1, e2)
        else:
            better = gs > best[0]
            best = (jnp.where(better, gs, best[0]), jnp.where(better, e1, best[1]),
                    jnp.where(better, e2, best[2]))
        if gi % 2 == 1:
            yield
    _, e1, e2 = best
    w1 = jnp.zeros_like(s_rows[0])
    w2 = jnp.zeros_like(s_rows[0])
    for e in range(N_EXPERTS):
        w1 = jnp.where(e1 == e, s_rows[e], w1)
        w2 = jnp.where(e2 == e, s_rows[e], w2)
    tot = w1 + w2
    w1 = w1 / tot
    w2 = w2 / tot
    r8 = lax.broadcasted_iota(jnp.int32, (8, scores.shape[1]), 0)
    rt_ref[...] = jnp.where(r8 == 0, e1.astype(F32), jnp.where(r8 == 1, e2.astype(F32),
                            jnp.where(r8 == 2, w1, jnp.where(r8 == 3, w2, 0.0))))
    row = lax.broadcasted_iota(jnp.int32, scores.shape, 0)
    assigned = jnp.where(row == e1, 1.0, jnp.where(row == e2, 1.0, 0.0))
    lane = lax.broadcasted_iota(jnp.int32, (N_EXPERTS, LANES), 1)
    cnt = jnp.zeros((N_EXPERTS, LANES), F32)
    for k in range(scores.shape[1] // MOE_SUB):
        part = jnp.sum(assigned[:, k * MOE_SUB:(k + 1) * MOE_SUB], axis=1, keepdims=True)
        cnt = jnp.where(lane == k, part, cnt)
    cnt_ref[0] = cnt.astype(jnp.int32)
    yield


def _merge(x2, ya, yb, yc, yd, lw, router_w_t, router_b):
    T, D = x2.shape
    tm = TM_MERGE
    n = T // tm
    tok = lambda w: pl.BlockSpec((tm, w), lambda i: (jnp.minimum(i, n - 1), 0))
    consts = [lw["attn_norm"], lw["w_gate"], lw["w_branch"], lw["w_out"], lw["ffn_norm"],
              router_w_t, router_b]
    return pl.pallas_call(
        _merge_kernel, grid=(n + 1,),
        in_specs=[tok(D), tok(256), tok(256), tok(256), tok(256)] + [_full(a.shape) for a in consts],
        out_specs=[tok(D), tok(D), pl.BlockSpec((8, tm), lambda i: (0, jnp.maximum(i - 1, 0))),
                   pl.BlockSpec((1, N_EXPERTS, LANES), lambda i: (jnp.maximum(i - 1, 0), 0, 0))],
        out_shape=[jax.ShapeDtypeStruct((T, D), F32), jax.ShapeDtypeStruct((T, D), BF16),
                   jax.ShapeDtypeStruct((8, T), F32),
                   jax.ShapeDtypeStruct((n, N_EXPERTS, LANES), jnp.int32)],
        scratch_shapes=[pltpu.VMEM((tm, D), F32)],
        compiler_params=_cparams(("arbitrary",)), name="merge",
    )(x2, ya, yb, yc, yd, *consts)


def _moe_kernel(cnt_ref, x1_ref, h2_ref, rt_ref, tri_ref, sg_ref, su_ref, sd_ref,
                wg_ref, wu_ref, wd_ref, o_ref, rank_scr, gate_scr):
    i = pl.program_id(0)
    e = pl.program_id(1)
    tm = h2_ref.shape[0]

    def ffn(rows, scale, wg, wu, wd):
        a = jax.nn.silu(_dot(rows, wg)) * _dot(rows, wu)
        if scale is not None:
            a = a * scale
        return _dot(a.astype(BF16), wd)

    sub = tri_ref.shape[0]
    nsub = tm // sub
    spans = [slice(k * sub, (k + 1) * sub) for k in range(nsub)]

    @pl.when(e == 0)
    def _():
        for r in range(0, tm, MOE_SHARED_ROWS):
            rs = slice(r, r + MOE_SHARED_ROWS)
            o_ref[rs, :] = x1_ref[rs, :] + ffn(h2_ref[rs, :], None, sg_ref[...], su_ref[...],
                                               sd_ref[...])
        row = lax.broadcasted_iota(jnp.int32, (N_EXPERTS, tm), 0).astype(F32)
        e1, e2 = rt_ref[0:1, :], rt_ref[1:2, :]
        w1, w2 = rt_ref[2:3, :], rt_ref[3:4, :]
        assigned = jnp.where(row == e1, 1.0, jnp.where(row == e2, 1.0, 0.0))
        for sp in spans:
            before = _dot(assigned[:, sp].astype(BF16), tri_ref[...])
            rank_scr[:, sp] = jnp.where(assigned[:, sp] > 0.0, before, -1.0)
        gate_scr[...] = jnp.where(row == e1, w1, jnp.where(row == e2, w2, 0.0))

    @pl.when(e > 0)
    def _():
        subs = range(MOE_EXPERTS_PER_STEP)
        ex = [(e - 1) * MOE_EXPERTS_PER_STEP + s for s in subs]
        cnt = [cnt_ref[(i * nsub + k) * N_EXPERTS + x] for x in ex for k in range(nsub)]
        rank_row = [rank_scr[pl.ds(x, 1), :] for x in ex]
        gate_row = [gate_scr[pl.ds(x, 1), :] for x in ex]
        slot = lax.broadcasted_iota(jnp.int32, (MOE_CHUNK, sub), 0).astype(F32)

        def chunk(c, carry):
            base = slot + (c * MOE_CHUNK).astype(F32)
            onehot, rows, scale = [], [], []
            for sp in spans:
                hit = [rank_row[s][:, sp] == base for s in subs]
                onehot.append(jnp.concatenate(
                    [jnp.where(h, 1.0, 0.0).astype(BF16) for h in hit], axis=0))
                rows.append(_dot(onehot[-1], h2_ref[sp, :]).astype(BF16))
                scale.append([jnp.sum(jnp.where(hit[s], gate_row[s][:, sp], 0.0),
                                      axis=1, keepdims=True) for s in subs])
            y = []
            for s in subs:
                mine = slice(s * MOE_CHUNK, (s + 1) * MOE_CHUNK)
                y.append(ffn(jnp.concatenate([r[mine] for r in rows], axis=0),
                             jnp.concatenate([sc[s] for sc in scale], axis=0),
                             wg_ref[0, s], wu_ref[0, s], wd_ref[0, s]).astype(BF16))
            for k, sp in enumerate(spans):
                rows_k = slice(k * MOE_CHUNK, (k + 1) * MOE_CHUNK)
                o_ref[sp, :] += lax.dot_general(
                    onehot[k], jnp.concatenate([y[s][rows_k] for s in subs], axis=0),
                    (((0,), (0,)), ((), ())), preferred_element_type=F32)
            return carry

        most = functools.reduce(jnp.maximum, cnt)
        lax.fori_loop(0, (most + MOE_CHUNK - 1) // MOE_CHUNK, chunk, 0)


def _moe(x1, h2, route, counts, tri, shared, layer, wg, wu, wd):
    T, D = x1.shape
    tm = min(TM_MOE, T)
    sub = tri.shape[0]
    _, ne, _, ff = wg.shape
    const = lambda shp: pl.BlockSpec(shp, lambda i, e, c: (0, 0), pipeline_mode=pl.Buffered(1))
    eps = MOE_EXPERTS_PER_STEP
    expert = lambda shp: pl.BlockSpec((1, eps) + shp,
                                      lambda i, e, c: (layer, jnp.maximum(e - 1, 0), 0, 0))
    grid_spec = pltpu.PrefetchScalarGridSpec(
        num_scalar_prefetch=1, grid=(T // tm, ne // eps + 1),
        in_specs=[pl.BlockSpec((tm, D), lambda i, e, c: (i, 0), pipeline_mode=pl.Buffered(1)),
                  pl.BlockSpec((tm, D), lambda i, e, c: (i, 0), pipeline_mode=pl.Buffered(1)),
                  pl.BlockSpec((8, tm), lambda i, e, c: (0, i)),
                  const((sub, sub)), const((D, ff)), const((D, ff)), const((ff, D)),
                  expert((D, ff)), expert((D, ff)), expert((ff, D))],
        out_specs=pl.BlockSpec((tm, D), lambda i, e, c: (i, 0), pipeline_mode=pl.Buffered(1)),
        scratch_shapes=[pltpu.VMEM((N_EXPERTS, tm), F32), pltpu.VMEM((N_EXPERTS, tm), F32)])
    return pl.pallas_call(
        _moe_kernel, grid_spec=grid_spec,
        out_shape=jax.ShapeDtypeStruct((T, D), F32),
        compiler_params=_cparams(("arbitrary", "arbitrary")), name="moe",
    )(counts, x1, h2, route, tri, *shared, wg, wu, wd)


def _constants(S):
    half = MLA_ROPE // 2
    inv = ROPE_THETA ** (-np.arange(half, dtype=np.float64) / half)
    ang = np.arange(S, dtype=np.float64)[:, None] * inv[None, :]
    cos = np.ones((S, LANES), np.float32)
    cos[:, MLA_NOPE:MLA_NOPE + half] = np.cos(ang)
    cos[:, MLA_NOPE + half:MLA_QK] = np.cos(ang)
    sin = np.zeros((S, LANES), np.float32)
    sin[:, MLA_NOPE:MLA_NOPE + half] = -np.sin(ang)
    sin[:, MLA_NOPE + half:MLA_QK] = np.sin(ang)

    nkb = S // MOBA_BLOCK
    kpos = np.arange(S)
    kc = np.zeros((S, MOBA_HD), np.float32)
    blk = kpos // MOBA_BLOCK
    m = blk < min(nkb, MOBA_HD) - 1
    kc[kpos[m], blk[m]] = 1.0
    kc[:, MOBA_HD - 1] = kpos % MOBA_BLOCK
    kc4 = np.zeros((S, MOBA_HEADS, 2 * MOBA_HD), np.float32)
    kc4[:, :, MOBA_HD:] = kc[:, None, :]
    swap = np.zeros((LANES, LANES), np.float32)
    for j in range(half):
        swap[MLA_NOPE + half + j, MLA_NOPE + j] = 1.0
        swap[MLA_NOPE + j, MLA_NOPE + half + j] = 1.0
    return {"cos": jnp.asarray(cos), "sin": jnp.asarray(sin), "swap": jnp.asarray(swap, BF16),
            "kc": jnp.asarray(kc4.reshape(S, -1), BF16)}


def _head_pad(v):
    return jnp.concatenate([v, jnp.zeros((LANES - MLA_QK,), F32)])[None, :]


def _layer_weights(l, p):
    D = p["w_in"].shape[1]
    w_in = p["w_in"][l]
    o0 = MLA_Q_LORA + MLA_KV_LORA
    o1 = o0 + MLA_ROPE
    w = jnp.concatenate([w_in[:, :o0], jnp.zeros((D, MLA_NOPE), F32), w_in[:, o0:o1],
                         jnp.zeros((D, LANES - MLA_QK), F32), w_in[:, o1:]], axis=1)

    wq = p["mla_w_q_up"][l].reshape(MLA_Q_LORA, MLA_HEADS, MLA_QK)
    wq = jnp.pad(wq, ((0, 0), (0, 0), (0, LANES - MLA_QK))).reshape(MLA_Q_LORA, MLA_HEADS * LANES)
    wkv = p["mla_w_kv_up"][l].reshape(MLA_KV_LORA, MLA_HEADS, MLA_NOPE + MLA_V)
    wk = jnp.pad(wkv[:, :, :MLA_NOPE], ((0, 0), (0, 0), (0, LANES - MLA_NOPE)))
    wk = wk.reshape(MLA_KV_LORA, MLA_HEADS * LANES)
    wv = wkv[:, :, MLA_NOPE:].reshape(MLA_KV_LORA, MLA_HEADS * MLA_V)

    tri = jnp.tril(jnp.ones((SGU_CHUNK, SGU_CHUNK), dtype=bool))
    sgu_w = jnp.where(tri[None], p["sgu_w"][l], 0.0)
    sgu_b = jnp.repeat(p["sgu_b"][l].T, SGU_W // SGU_GROUPS, axis=1)
    pool_w = jax.scipy.linalg.block_diag(*[p["pool_w"][l, gi] for gi in range(len(POOL_WINDOWS))])

    return {
        "attn_norm": p["attn_norm"][l][None, :], "w_in": w.astype(BF16),
        "mla_q_norm": p["mla_q_norm"][l][None, :], "wq": wq.astype(BF16),
        "mla_kv_norm": p["mla_kv_norm"][l][None, :],
        "wkv": jnp.concatenate([wk, wv], axis=1).astype(BF16),
        "mla_qg": _head_pad(p["mla_q_gain"][l]), "mla_kg": _head_pad(p["mla_k_gain"][l]),
        "sgu_norm": p["sgu_norm"][l][None, :], "sgu_w": sgu_w.astype(BF16), "sgu_b": sgu_b,
        "pool_w": pool_w.astype(BF16), "pool_b": p["pool_b"][l][None, :],
        "pool_scale": p["pool_scale"][l][None, :],
        "moba_qg": jnp.tile(p["moba_q_gain"][l], 2)[None, :],
        "moba_kg": jnp.tile(p["moba_k_gain"][l], 2)[None, :],
        "w_gate": p["w_gate"][l].astype(BF16), "w_branch": p["w_branch"][l].astype(BF16),
        "w_out": p["w_out"][l].astype(BF16), "ffn_norm": p["ffn_norm"][l][None, :],
        "shared": (p["shared_w_gate"][l].astype(BF16), p["shared_w_up"][l].astype(BF16),
                   p["shared_w_down"][l].astype(BF16)),
    }


def kernel(x, attn_norm, w_in, mla_q_norm, mla_w_q_up, mla_kv_norm, mla_w_kv_up, mla_q_gain, mla_k_gain, sgu_norm, sgu_w, sgu_b, pool_w, pool_b, pool_scale, moba_q_gain, moba_k_gain, w_gate, w_branch, w_out, ffn_norm, router_w, router_bias, moe_w_gate, moe_w_up, moe_w_down, shared_w_gate, shared_w_up, shared_w_down):
    p = dict(attn_norm=attn_norm, w_in=w_in, mla_q_norm=mla_q_norm, mla_w_q_up=mla_w_q_up,
             mla_kv_norm=mla_kv_norm, mla_w_kv_up=mla_w_kv_up, mla_q_gain=mla_q_gain,
             mla_k_gain=mla_k_gain, sgu_norm=sgu_norm, sgu_w=sgu_w, sgu_b=sgu_b, pool_w=pool_w,
             pool_b=pool_b, pool_scale=pool_scale, moba_q_gain=moba_q_gain,
             moba_k_gain=moba_k_gain, w_gate=w_gate, w_branch=w_branch, w_out=w_out,
             ffn_norm=ffn_norm, moe_w_gate=moe_w_gate, moe_w_up=moe_w_up, moe_w_down=moe_w_down,
             shared_w_gate=shared_w_gate, shared_w_up=shared_w_up, shared_w_down=shared_w_down)
    B, S, D = x.shape
    T = B * S
    depth = attn_norm.shape[0]
    nkb = S // MOBA_BLOCK
    assert TQ == TK and S % TK == 0 and T % TM_MERGE == 0 and nkb <= MOBA_HD
    consts = _constants(S)
    router_w_t = router_w.T
    router_b = router_bias[:, None]
    assert min(TM_MOE, T) % TM_MERGE == 0 and N_EXPERTS % MOE_EXPERTS_PER_STEP == 0
    tri = jnp.asarray(np.triu(np.ones((MOE_SUB, MOE_SUB), np.float32), 1), BF16)

    ffn_w = (moe_w_gate.astype(BF16), moe_w_up.astype(BF16), moe_w_down.astype(BF16))
    x2 = x.reshape(T, D)
    for l in range(depth):
        lw = _layer_weights(l, p)
        qa, ka, va, yb, yc, qd, kd, vd, km = _in_proj(x2, lw, consts, S)
        ya = _flash(qa, ka.reshape(B, S // TK, TK, -1), va, MLA_HEADS, True, "mla_attn")

        kmean = km.reshape(B, nkb, MOBA_HEADS, MOBA_HD).transpose(0, 2, 1, 3)
        kmean = jnp.pad(kmean, ((0, 0), (0, 0), (0, MOBA_HD - nkb), (0, 0)))
        q_aug = _moba_select(qd, kmean)
        yd = _flash(q_aug, kd.reshape(B, S // TK, TK, -1), vd, MOBA_HEADS, False, "moba_attn")

        x1, h2, route, cnt = _merge(x2, ya, yb, yc, yd, lw, router_w_t, router_b)
        counts = cnt[:, :, :TM_MERGE // MOE_SUB].transpose(0, 2, 1).reshape(-1)
        x2 = _moe(x1, h2, route, counts, tri, lw["shared"], l, *ffn_w)
    return x2.reshape(B, S, D)
```

```python
import functools
import math

import jax
import jax.numpy as jnp
import numpy as np
from jax import lax
from jax.experimental import pallas as pl
from jax.experimental.pallas import tpu as pltpu

F32 = jnp.float32
BF16 = jnp.bfloat16
EPS = 1e-6
NEG = -1e30

LANES = 128
VMEM_LIMIT = 56 * 1024 * 1024

MLA_HEADS = 4
MLA_NOPE = 64
MLA_ROPE = 32
MLA_V = 64
MLA_Q_LORA = 256
MLA_KV_LORA = 128
ROPE_THETA = 10000.0
MLA_QK = MLA_NOPE + MLA_ROPE

SGU_GROUPS = 4
SGU_CHUNK = 128
SGU_W = 256

POOL_WINDOWS = (2, 4, 8, 16)
POOL_MAXW = 16
POOL_PAD = 8
assert POOL_WINDOWS == tuple(2 << k for k in range(len(POOL_WINDOWS))) and POOL_MAXW >= max(POOL_WINDOWS)
POOL_W = 256

MOBA_HEADS = 4
MOBA_HD = 64
MOBA_BLOCK = 256
MOBA_TOPK = 3

N_EXPERTS = 16
N_GROUPS = 4
EPG = 4

C_MLA = 0
C_SGU = 512
C_POOL = 1024
C_MOBA = 1280
C_END = 2048

TQ = 512
TK = 512
TM_IN = TQ
TM_MERGE = 512
TM_MOE = 2048
SEL_L = 2048
FLASH_UNROLLS = (4, 2, 1)
FLASH_AHEAD = 2
MOE_EXPERTS_PER_STEP = 4
MOE_SHARED_ROWS = 512
MOE_SUB = 256
MOE_CHUNK = 48
ONES_ROWS = 16


def _cparams(sem):
    return pltpu.CompilerParams(dimension_semantics=sem, vmem_limit_bytes=VMEM_LIMIT)


def _full(shape):
    n = len(shape)
    return pl.BlockSpec(shape, lambda *_: (0,) * n, pipeline_mode=pl.Buffered(1))


def _rms(x, g):
    return x * lax.rsqrt(jnp.mean(x * x, axis=-1, keepdims=True) + EPS) * g


def _dot(a, b):
    return jnp.dot(a, b, preferred_element_type=F32)


def _in_proj_kernel(x_ref, g_ref, w_ref, qn_ref, wq_ref, kvn_ref, wkv_ref,
                    qg_ref, kg_ref, swap_ref, cos_ref, sin_ref,
                    sn_ref, sw_ref, sb_ref, pw_ref, pb_ref, ps_ref,
                    mqg_ref, mkg_ref, kc_ref,
                    qa_ref, ka_ref, va_ref, yb_ref, yc_ref, qd_ref, kd_ref, vd_ref, km_ref,
                    halo_ref, buf_ref, tmp_ref, proj_scr, *, tiles_per_seq, q_scale):
    tm = x_ref.shape[0]
    i = pl.program_id(0)

    @pl.when(i % tiles_per_seq == 0)
    def _():
        halo_ref[...] = jnp.zeros_like(halo_ref)

    x = x_ref[...]
    h = _rms(x, g_ref[...]).astype(BF16)
    lane = lax.broadcasted_iota(jnp.int32, (tm, LANES), 1)

    cm = _dot(h, w_ref[:, C_MLA:C_SGU])
    c_q = _rms(cm[:, :MLA_Q_LORA], qn_ref[...]).astype(BF16)
    c_kv = _rms(cm[:, MLA_Q_LORA:MLA_Q_LORA + MLA_KV_LORA], kvn_ref[...]).astype(BF16)
    kpe_grp = cm[:, MLA_Q_LORA + MLA_KV_LORA:]
    q = _dot(c_q, wq_ref[...])
    kv = _dot(c_kv, wkv_ref[...])
    va_ref[0, 0] = kv[:, 512:].T.astype(BF16)
    proj_scr[...] = _dot(h, w_ref[:, C_SGU:C_END])
    cos = cos_ref[...]
    sin = sin_ref[...]
    inv_d = 1.0 / MLA_QK
    rope_lo = lane < MLA_NOPE + MLA_ROPE // 2

    def head_norm_rope(t, gain):
        ms = jnp.sum(t * t, axis=-1, keepdims=True) * inv_d
        t = t * lax.rsqrt(ms + EPS) * gain
        rot = _dot(t.astype(BF16), swap_ref[...])
        return t * cos + rot * sin

    heads = [slice(hh * LANES, (hh + 1) * LANES) for hh in range(MLA_HEADS)]
    qh = [head_norm_rope(q[:, sl], qg_ref[...]) * q_scale for sl in heads]
    qa_ref[0, 0] = jnp.concatenate(qh, axis=1).T.astype(BF16)
    kh = [head_norm_rope(kv[:, sl] + kpe_grp, kg_ref[...]) for sl in heads]
    ka_ref[...] = jnp.concatenate(kh, axis=1).astype(BF16)

    z = jax.nn.gelu(proj_scr[:, 0:C_POOL - C_SGU])
    u = z[:, :SGU_W]
    vv = _rms(z[:, SGU_W:], sn_ref[...]).astype(BF16)
    lane_grp = lax.broadcasted_iota(jnp.int32, (SGU_CHUNK, SGU_W), 1) // (SGU_W // SGU_GROUPS)
    for c in range(tm // SGU_CHUNK):
        vc = vv[c * SGU_CHUNK:(c + 1) * SGU_CHUNK, :]
        y = sb_ref[...]
        for gi in range(SGU_GROUPS):
            y = y + jnp.where(lane_grp == gi, _dot(sw_ref[gi], vc), 0.0)
        yb_ref[c * SGU_CHUNK:(c + 1) * SGU_CHUNK, :] = (
            u[c * SGU_CHUNK:(c + 1) * SGU_CHUNK, :] * y).astype(BF16)

    xp = proj_scr[:, C_POOL - C_SGU:C_MOBA - C_SGU]
    lead = POOL_PAD + POOL_MAXW
    ext = tm + POOL_MAXW
    buf_ref[0:POOL_PAD, :] = jnp.zeros((POOL_PAD, POOL_W), F32)
    tmp_ref[0:POOL_PAD, :] = jnp.zeros((POOL_PAD, POOL_W), F32)
    buf_ref[POOL_PAD:lead, :] = halo_ref[...]
    buf_ref[lead:, :] = xp
    halo_ref[...] = xp[tm - POOL_MAXW:, :]
    pos = (i % tiles_per_seq) * tm + lax.broadcasted_iota(jnp.int32, (tm, POOL_W), 0)
    lane_w = lax.broadcasted_iota(jnp.int32, (tm, POOL_W), 1) // (POOL_W // len(POOL_WINDOWS))
    pooled = jnp.zeros((tm, POOL_W), F32)
    src, dst = buf_ref, tmp_ref
    for gi, w in enumerate(POOL_WINDOWS):
        half_w = w // 2
        acc = src[POOL_PAD:POOL_PAD + ext, :] + src[POOL_PAD - half_w:POOL_PAD - half_w + ext, :]
        cnt = jnp.minimum(pos + 1, w).astype(F32)
        pooled = jnp.where(lane_w == gi, acc[POOL_MAXW:, :] / cnt, pooled)
        if gi + 1 < len(POOL_WINDOWS):
            dst[POOL_PAD:POOL_PAD + ext, :] = acc
            src, dst = dst, src
    p = (pooled - xp).astype(BF16)
    yc_ref[...] = ((_dot(p, pw_ref[...]) + pb_ref[...]) * ps_ref[...]).astype(BF16)

    cd = proj_scr[:, C_MOBA - C_SGU:]
    inv_hd = 1.0 / MOBA_HD
    lo = lane < MOBA_HD

    def pair_norm(t, gain):
        sq = t * t
        s_all = jnp.sum(sq, axis=-1, keepdims=True)
        s_lo = jnp.sum(jnp.where(lo, sq, 0.0), axis=-1, keepdims=True)
        ms = jnp.where(lo, s_lo, s_all - s_lo) * inv_hd
        return t * lax.rsqrt(ms + EPS) * gain

    pairs = [slice(g * LANES, (g + 1) * LANES) for g in range(MOBA_HEADS // 2)]
    qd = jnp.concatenate([pair_norm(cd[:, sl], mqg_ref[...]) for sl in pairs], axis=1)
    qd_ref[0] = qd.T
    kp = [pair_norm(cd[:, 256 + g * LANES:256 + (g + 1) * LANES], mkg_ref[...])
          for g in range(MOBA_HEADS // 2)]
    kd = jnp.concatenate(kp, axis=1)
    kh = []
    for hh in range(MOBA_HEADS):
        src = kp[hh // 2] if hh % 2 == 0 else pltpu.roll(kp[hh // 2], MOBA_HD, 1)
        kh.append(jnp.where(lo, src, kc_ref[:, hh * LANES:(hh + 1) * LANES].astype(F32)))
    kd_ref[...] = jnp.concatenate(kh, axis=1).astype(BF16)
    vd_ref[0, 0] = cd[:, 512:].T.astype(BF16)
    for c in range(tm // MOBA_BLOCK):
        km_ref[c] = jnp.mean(kd[c * MOBA_BLOCK:(c + 1) * MOBA_BLOCK, :], axis=0, keepdims=True)


def _in_proj(x2, lw, consts, S):
    T, D = x2.shape
    tm = TM_IN
    nt = T // tm
    tok = lambda w: pl.BlockSpec((tm, w), lambda i: (i, 0))
    pos_spec = pl.BlockSpec((tm, LANES), lambda i: (i % (S // tm), 0))
    ins = [
        (x2, tok(D)), (lw["attn_norm"], None), (lw["w_in"], None),
        (lw["mla_q_norm"], None), (lw["wq"], None), (lw["mla_kv_norm"], None), (lw["wkv"], None),
        (lw["mla_qg"], None), (lw["mla_kg"], None), (consts["swap"], None),
        (consts["cos"], pos_spec), (consts["sin"], pos_spec),
        (lw["sgu_norm"], None), (lw["sgu_w"], None), (lw["sgu_b"], None),
        (lw["pool_w"], None), (lw["pool_b"], None), (lw["pool_scale"], None),
        (lw["moba_qg"], None), (lw["moba_kg"], None),
        (consts["kc"], pl.BlockSpec((tm, 4 * LANES), lambda i: (i % (S // tm), 0))),
    ]
    arrays = [a for a, _ in ins]
    specs = [s if s is not None else _full(a.shape) for a, s in ins]
    nb = tm // MOBA_BLOCK
    B = T // S
    ns = S // tm
    tr = lambda f: pl.BlockSpec((1, 1, f, tm), lambda i: (i // ns, i % ns, 0, 0))
    out_shape = [
        jax.ShapeDtypeStruct((B, ns, 512, tm), BF16), jax.ShapeDtypeStruct((T, 512), BF16),
        jax.ShapeDtypeStruct((B, ns, 256, tm), BF16), jax.ShapeDtypeStruct((T, 256), BF16),
        jax.ShapeDtypeStruct((T, 256), BF16), jax.ShapeDtypeStruct((B, 256, S), F32),
        jax.ShapeDtypeStruct((T, 512), BF16), jax.ShapeDtypeStruct((B, ns, 256, tm), BF16),
        jax.ShapeDtypeStruct((T // MOBA_BLOCK, 1, 256), F32),
    ]
    out_specs = [tr(512), tok(512), tr(256), tok(256), tok(256),
                 pl.BlockSpec((1, 256, tm), lambda i: (i // ns, 0, i % ns)),
                 tok(512), tr(256), pl.BlockSpec((nb, 1, 256), lambda i: (i, 0, 0))]
    q_scale = (MLA_QK ** -0.5) * math.log2(math.e)
    return pl.pallas_call(
        functools.partial(_in_proj_kernel, tiles_per_seq=S // tm, q_scale=q_scale),
        grid=(nt,), in_specs=specs, out_specs=out_specs, out_shape=out_shape,
        scratch_shapes=[pltpu.VMEM((POOL_MAXW, POOL_W), F32),
                        pltpu.VMEM((POOL_PAD + POOL_MAXW + tm, POOL_W), F32),
                        pltpu.VMEM((POOL_PAD + POOL_MAXW + tm, POOL_W), F32),
                        pltpu.VMEM((tm, C_END - C_SGU), F32)],
        compiler_params=_cparams(("arbitrary",)), name="in_proj",
    )(*arrays)


def _moba_select_kernel(q_ref, km_ref, o_ref, *, slope_log2_step):
    hh = pl.program_id(1)
    li = pl.program_id(2)
    q = q_ref[0]
    L = q.shape[1]
    nrow = km_ref.shape[2]
    g = jnp.dot(km_ref[0, 0], q, preferred_element_type=F32, precision=lax.Precision.HIGHEST)
    n_i = lax.broadcasted_iota(jnp.int32, (nrow, L), 0)
    own = (li * L + lax.broadcasted_iota(jnp.int32, (nrow, L), 1)) // MOBA_BLOCK
    g = jnp.where(n_i < own, g, NEG)
    sel = jnp.zeros((nrow, L), F32)
    for _ in range(MOBA_TOPK):
        mx = jnp.max(g, axis=0, keepdims=True)
        first = jnp.min(jnp.where(g == mx, n_i, nrow), axis=0, keepdims=True)
        hit = n_i == first
        sel = jnp.where(hit & (mx > 0.5 * NEG), 1.0, sel)
        g = jnp.where(hit, NEG, g)
    slope = jnp.exp2(-slope_log2_step * (hh + 1).astype(F32))
    rel = (n_i - own).astype(F32) * (slope * MOBA_BLOCK)
    bias = jnp.where((sel > 0.0) | (n_i == own), rel, NEG)
    bias = jnp.where(n_i == nrow - 1, slope, bias)
    qs = (q * (MOBA_HD ** -0.5)).astype(BF16)
    bias = bias.astype(BF16)
    tq = o_ref.shape[3]
    for t in range(L // tq):
        o_ref[0, t, 0:MOBA_HD, :] = qs[:, t * tq:(t + 1) * tq]
        o_ref[0, t, MOBA_HD:, :] = bias[:, t * tq:(t + 1) * tq]


def _moba_select(qgT, kmean):
    B, _, S = qgT.shape
    H, hd = MOBA_HEADS, MOBA_HD
    L = min(SEL_L, S)
    return pl.pallas_call(
        functools.partial(_moba_select_kernel, slope_log2_step=8.0 / MOBA_HEADS),
        grid=(B, H, S // L),
        in_specs=[pl.BlockSpec((1, hd, L), lambda b, h, l: (b, h, l)),
                  pl.BlockSpec((1, 1, 64, hd), lambda b, h, l: (b, h, 0, 0))],
        out_specs=pl.BlockSpec((1, L // TQ, 2 * hd, TQ), lambda b, h, l: (b, l, h, 0)),
        out_shape=jax.ShapeDtypeStruct((B, S // TQ, H * 2 * hd, TQ), BF16),
        compiler_params=_cparams(("arbitrary",) * 3), name="moba_select",
    )(qgT, kmean)


def _flash_kernel(q_ref, k_ref, v_ref, o_ref, s_scr, acc_scr, *, heads, use_exp2):
    H = heads
    tq = q_ref.shape[3]
    tk = k_ref.shape[2]
    dk = q_ref.shape[2] // H
    dv = v_ref.shape[2] // H
    ratio = tk // tq
    ex = jnp.exp2 if use_exp2 else jnp.exp
    i = pl.program_id(1)
    jd = i // ratio
    kpos = jd * tk + lax.broadcasted_iota(jnp.int32, (tk, tq), 0)
    qpos = i * tq + lax.broadcasted_iota(jnp.int32, (tk, tq), 1)
    causal = kpos <= qpos
    ones = jnp.ones((ONES_ROWS, tk), BF16)

    def scores(jt, h, masked):
        s = _dot(k_ref[0, jt, :, h * dk:(h + 1) * dk], q_ref[0, 0, h * dk:(h + 1) * dk, :])
        if masked:
            s = jnp.where(causal, s, NEG)
        s_scr[h] = s
        return jnp.max(s, axis=0, keepdims=True)

    def update(jt, h, cmax, m):
        m_new = jnp.maximum(m, cmax)
        alpha = ex(m - m_new)
        p = ex(s_scr[h] - m_new).astype(BF16)
        vt = jnp.concatenate([v_ref[0, jt, h * dv:(h + 1) * dv, :], ones], axis=0)
        acc_scr[h] = acc_scr[h] * alpha + _dot(vt, p)
        return m_new

    acc_scr[...] = jnp.zeros_like(acc_scr)
    ms = [jnp.full((1, tq), NEG, F32) for _ in range(H)]
    ahead = FLASH_AHEAD
    cmax = [scores(jd, h, True) for h in range(ahead)]
    for h in range(H):
        nh = h + ahead
        cmax.append(scores(jd, nh, True) if nh < H else scores(0, nh - H, False))
        ms[h] = update(jd, h, cmax.pop(0), ms[h])

    def tile_step(j, carry):
        ms, cmax = carry
        ms, cmax = list(ms), list(cmax)
        jn = jnp.minimum(j + 1, jd - 1)
        for h in range(H):
            nh = h + ahead
            cmax.append(scores(j, nh, False) if nh < H else scores(jn, nh - H, False))
            ms[h] = update(j, h, cmax.pop(0), ms[h])
        return tuple(ms), tuple(cmax)

    carry = (tuple(ms), tuple(cmax))
    done = 0
    for unroll in FLASH_UNROLLS:
        def body(t, c, unroll=unroll, done=done):
            for u in range(unroll):
                c = tile_step(done + unroll * t + u, c)
            return c

        trips = (jd - done) // unroll
        carry = lax.fori_loop(0, trips, body, carry)
        done = done + trips * unroll
    outs = []
    for h in range(H):
        acc = acc_scr[h]
        outs.append(acc[:dv] / acc[dv:dv + 1])
    o_ref[...] = jnp.concatenate(outs, axis=0).T.astype(BF16)


def _flash(qT, k, vT, heads, use_exp2, name):
    B, nq, hdk, tq = qT.shape
    nk, tk = k.shape[1:3]
    hdv = vT.shape[2]
    dva = hdv // heads + ONES_ROWS
    whole = lambda shp: pl.BlockSpec((1,) + shp, lambda b, i: (b, 0, 0, 0),
                                     pipeline_mode=pl.Buffered(1))
    return pl.pallas_call(
        functools.partial(_flash_kernel, heads=heads, use_exp2=use_exp2),
        grid=(B, nq),
        in_specs=[pl.BlockSpec((1, 1, hdk, tq), lambda b, i: (b, i, 0, 0)),
                  whole((nk, tk, hdk)), whole((nk, hdv, tk))],
        out_specs=pl.BlockSpec((tq, hdv), lambda b, i: (b * nq + i, 0)),
        out_shape=jax.ShapeDtypeStruct((B * nq * tq, hdv), BF16),
        scratch_shapes=[pltpu.VMEM((heads, tk, tq), F32), pltpu.VMEM((heads, dva, tq), F32)],
        compiler_params=_cparams(("arbitrary", "arbitrary")), name=name,
    )(qT, k, vT)


def _merge_kernel(x_ref, ya_ref, yb_ref, yc_ref, yd_ref, g_ref, wg_ref, wb_ref, wo_ref,
                  fg_ref, rw_ref, rb_ref, x1_ref, h2_ref, rt_ref, cnt_ref, h2_scr):
    @pl.when(pl.program_id(0) == 0)
    def _():
        h2_scr[...] = jnp.zeros_like(h2_scr)

    router = _router_stages(h2_scr, rw_ref, rb_ref, rt_ref, cnt_ref)
    x = x_ref[...]
    h = _rms(x, g_ref[...]).astype(BF16)
    merged = None
    for bi, y_ref in enumerate((ya_ref, yb_ref, yc_ref, yd_ref)):
        t = jax.nn.sigmoid(_dot(h, wg_ref[bi])) * _dot(y_ref[...], wb_ref[bi])
        merged = t if merged is None else merged + t
        next(router)
    x1 = x + _dot(merged.astype(BF16), wo_ref[...])
    x1_ref[...] = x1
    h2_new = _rms(x1, fg_ref[...])
    h2_ref[...] = h2_new.astype(BF16)
    for _ in router:
        pass
    h2_scr[...] = h2_new


def _router_stages(h2_scr, rw_ref, rb_ref, rt_ref, cnt_ref):
    h2 = h2_scr[...]
    nt = (((1,), (1,)), ((), ()))
    h2_hi = h2.astype(BF16)
    h2_lo = (h2 - h2_hi.astype(F32)).astype(BF16)
    rw = rw_ref[...]
    rw_hi = rw.astype(BF16)
    rw_lo = (rw - rw_hi.astype(F32)).astype(BF16)
    logits = (lax.dot_general(rw_hi, h2_hi, nt, preferred_element_type=F32)
              + lax.dot_general(rw_hi, h2_lo, nt, preferred_element_type=F32)
              + lax.dot_general(rw_lo, h2_hi, nt, preferred_element_type=F32))
    yield
    scores = jax.nn.sigmoid(logits)
    biased = scores + rb_ref[...]
    s_rows = [scores[e:e + 1, :] for e in range(N_EXPERTS)]
    b_rows = [biased[e:e + 1, :] for e in range(N_EXPERTS)]

    def top2(vals):
        m1 = vals[0]
        i1 = jnp.zeros_like(m1, dtype=jnp.int32)
        for j in range(1, len(vals)):
            better = vals[j] > m1
            i1 = jnp.where(better, j, i1)
            m1 = jnp.where(better, vals[j], m1)
        m2 = None
        i2 = None
        for j in range(len(vals)):
            cand = jnp.where(i1 == j, NEG, vals[j])
            if m2 is None:
                m2, i2 = cand, jnp.zeros_like(i1)
            else:
                better = cand > m2
                i2 = jnp.where(better, j, i2)
                m2 = jnp.where(better, cand, m2)
        return m1, i1, m2, i2

    best = None
    for gi in range(N_GROUPS):
        bm1, bi1, bm2, bi2 = top2(b_rows[gi * EPG:(gi + 1) * EPG])
        gs = bm1 + bm2
        e1 = bi1 + gi * EPG
        e2 = bi2 + gi * EPG
        if best is None:
            best = (gs, e1, e2)
        else:
            better = gs > best[0]
            best = (jnp.where(better, gs, best[0]), jnp.where(better, e1, best[1]),
                    jnp.where(better, e2, best[2]))
        if gi % 2 == 1:
            yield
    _, e1, e2 = best
    w1 = jnp.zeros_like(s_rows[0])
    w2 = jnp.zeros_like(s_rows[0])
    for e in range(N_EXPERTS):
        w1 = jnp.where(e1 == e, s_rows[e], w1)
        w2 = jnp.where(e2 == e, s_rows[e], w2)
    tot = w1 + w2
    w1 = w1 / tot
    w2 = w2 / tot
    r8 = lax.broadcasted_iota(jnp.int32, (8, scores.shape[1]), 0)
    rt_ref[...] = jnp.where(r8 == 0, e1.astype(F32), jnp.where(r8 == 1, e2.astype(F32),
                            jnp.where(r8 == 2, w1, jnp.where(r8 == 3, w2, 0.0))))
    row = lax.broadcasted_iota(jnp.int32, scores.shape, 0)
    assigned = jnp.where(row == e1, 1.0, jnp.where(row == e2, 1.0, 0.0))
    lane = lax.broadcasted_iota(jnp.int32, (N_EXPERTS, LANES), 1)
    cnt = jnp.zeros((N_EXPERTS, LANES), F32)
    for k in range(scores.shape[1] // MOE_SUB):
        part = jnp.sum(assigned[:, k * MOE_SUB:(k + 1) * MOE_SUB], axis=1, keepdims=True)
        cnt = jnp.where(lane == k, part, cnt)
    cnt_ref[0] = cnt.astype(jnp.int32)
    yield


def _merge(x2, ya, yb, yc, yd, lw, router_w_t, router_b):
    T, D = x2.shape
    tm = TM_MERGE
    n = T // tm
    tok = lambda w: pl.BlockSpec((tm, w), lambda i: (jnp.minimum(i, n - 1), 0))
    consts = [lw["attn_norm"], lw["w_gate"], lw["w_branch"], lw["w_out"], lw["ffn_norm"],
              router_w_t, router_b]
    return pl.pallas_call(
        _merge_kernel, grid=(n + 1,),
        in_specs=[tok(D), tok(256), tok(256), tok(256), tok(256)] + [_full(a.shape) for a in consts],
        out_specs=[tok(D), tok(D), pl.BlockSpec((8, tm), lambda i: (0, jnp.maximum(i - 1, 0))),
                   pl.BlockSpec((1, N_EXPERTS, LANES), lambda i: (jnp.maximum(i - 1, 0), 0, 0))],
        out_shape=[jax.ShapeDtypeStruct((T, D), F32), jax.ShapeDtypeStruct((T, D), BF16),
                   jax.ShapeDtypeStruct((8, T), F32),
                   jax.ShapeDtypeStruct((n, N_EXPERTS, LANES), jnp.int32)],
        scratch_shapes=[pltpu.VMEM((tm, D), F32)],
        compiler_params=_cparams(("arbitrary",)), name="merge",
    )(x2, ya, yb, yc, yd, *consts)


def _moe_kernel(cnt_ref, x1_ref, h2_ref, rt_ref, tri_ref, sg_ref, su_ref, sd_ref,
                wg_ref, wu_ref, wd_ref, o_ref, rank_scr, gate_scr):
    i = pl.program_id(0)
    e = pl.program_id(1)
    tm = h2_ref.shape[0]

    def ffn(rows, scale, wg, wu, wd):
        a = jax.nn.silu(_dot(rows, wg)) * _dot(rows, wu)
        if scale is not None:
            a = a * scale
        return _dot(a.astype(BF16), wd)

    sub = tri_ref.shape[0]
    nsub = tm // sub
    spans = [slice(k * sub, (k + 1) * sub) for k in range(nsub)]

    @pl.when(e == 0)
    def _():
        for r in range(0, tm, MOE_SHARED_ROWS):
            rs = slice(r, r + MOE_SHARED_ROWS)
            o_ref[rs, :] = x1_ref[rs, :] + ffn(h2_ref[rs, :], None, sg_ref[...], su_ref[...],
                                               sd_ref[...])
        row = lax.broadcasted_iota(jnp.int32, (N_EXPERTS, tm), 0).astype(F32)
        e1, e2 = rt_ref[0:1, :], rt_ref[1:2, :]
        w1, w2 = rt_ref[2:3, :], rt_ref[3:4, :]
        assigned = jnp.where(row == e1, 1.0, jnp.where(row == e2, 1.0, 0.0))
        for sp in spans:
            before = _dot(assigned[:, sp].astype(BF16), tri_ref[...])
            rank_scr[:, sp] = jnp.where(assigned[:, sp] > 0.0, before, -1.0)
        gate_scr[...] = jnp.where(row == e1, w1, jnp.where(row == e2, w2, 0.0))

    @pl.when(e > 0)
    def _():
        subs = range(MOE_EXPERTS_PER_STEP)
        ex = [(e - 1) * MOE_EXPERTS_PER_STEP + s for s in subs]
        cnt = [cnt_ref[(i * nsub + k) * N_EXPERTS + x] for x in ex for k in range(nsub)]
        rank_row = [rank_scr[pl.ds(x, 1), :] for x in ex]
        gate_row = [gate_scr[pl.ds(x, 1), :] for x in ex]
        slot = lax.broadcasted_iota(jnp.int32, (MOE_CHUNK, sub), 0).astype(F32)

        def chunk(c, carry):
            base = slot + (c * MOE_CHUNK).astype(F32)
            onehot, rows, scale = [], [], []
            for sp in spans:
                hit = [rank_row[s][:, sp] == base for s in subs]
                onehot.append(jnp.concatenate(
                    [jnp.where(h, 1.0, 0.0).astype(BF16) for h in hit], axis=0))
                rows.append(_dot(onehot[-1], h2_ref[sp, :]).astype(BF16))
                scale.append([jnp.sum(jnp.where(hit[s], gate_row[s][:, sp], 0.0),
                                      axis=1, keepdims=True) for s in subs])
            y = []
            for s in subs:
                mine = slice(s * MOE_CHUNK, (s + 1) * MOE_CHUNK)
                y.append(ffn(jnp.concatenate([r[mine] for r in rows], axis=0),
                             jnp.concatenate([sc[s] for sc in scale], axis=0),
                             wg_ref[0, s], wu_ref[0, s], wd_ref[0, s]).astype(BF16))
            for k, sp in enumerate(spans):
                rows_k = slice(k * MOE_CHUNK, (k + 1) * MOE_CHUNK)
                o_ref[sp, :] += lax.dot_general(
                    onehot[k], jnp.concatenate([y[s][rows_k] for s in subs], axis=0),
                    (((0,), (0,)), ((), ())), preferred_element_type=F32)
            return carry

        most = functools.reduce(jnp.maximum, cnt)
        lax.fori_loop(0, (most + MOE_CHUNK - 1) // MOE_CHUNK, chunk, 0)


def _moe(x1, h2, route, counts, tri, shared, layer, wg, wu, wd):
    T, D = x1.shape
    tm = min(TM_MOE, T)
    sub = tri.shape[0]
    _, ne, _, ff = wg.shape
    const = lambda shp: pl.BlockSpec(shp, lambda i, e, c: (0, 0), pipeline_mode=pl.Buffered(1))
    eps = MOE_EXPERTS_PER_STEP
    expert = lambda shp: pl.BlockSpec((1, eps) + shp,
                                      lambda i, e, c: (layer, jnp.maximum(e - 1, 0), 0, 0))
    grid_spec = pltpu.PrefetchScalarGridSpec(
        num_scalar_prefetch=1, grid=(T // tm, ne // eps + 1),
        in_specs=[pl.BlockSpec((tm, D), lambda i, e, c: (i, 0), pipeline_mode=pl.Buffered(1)),
                  pl.BlockSpec((tm, D), lambda i, e, c: (i, 0), pipeline_mode=pl.Buffered(1)),
                  pl.BlockSpec((8, tm), lambda i, e, c: (0, i)),
                  const((sub, sub)), const((D, ff)), const((D, ff)), const((ff, D)),
                  expert((D, ff)), expert((D, ff)), expert((ff, D))],
        out_specs=pl.BlockSpec((tm, D), lambda i, e, c: (i, 0), pipeline_mode=pl.Buffered(1)),
        scratch_shapes=[pltpu.VMEM((N_EXPERTS, tm), F32), pltpu.VMEM((N_EXPERTS, tm), F32)])
    return pl.pallas_call(
        _moe_kernel, grid_spec=grid_spec,
        out_shape=jax.ShapeDtypeStruct((T, D), F32),
        compiler_params=_cparams(("arbitrary", "arbitrary")), name="moe",
    )(counts, x1, h2, route, tri, *shared, wg, wu, wd)


def _constants(S):
    half = MLA_ROPE // 2
    inv = ROPE_THETA ** (-np.arange(half, dtype=np.float64) / half)
    ang = np.arange(S, dtype=np.float64)[:, None] * inv[None, :]
    cos = np.ones((S, LANES), np.float32)
    cos[:, MLA_NOPE:MLA_NOPE + half] = np.cos(ang)
    cos[:, MLA_NOPE + half:MLA_QK] = np.cos(ang)
    sin = np.zeros((S, LANES), np.float32)
    sin[:, MLA_NOPE:MLA_NOPE + half] = -np.sin(ang)
    sin[:, MLA_NOPE + half:MLA_QK] = np.sin(ang)

    nkb = S // MOBA_BLOCK
    kpos = np.arange(S)
    kc = np.zeros((S, MOBA_HD), np.float32)
    blk = kpos // MOBA_BLOCK
    m = blk < min(nkb, MOBA_HD) - 1
    kc[kpos[m], blk[m]] = 1.0
    kc[:, MOBA_HD - 1] = kpos % MOBA_BLOCK
    kc4 = np.zeros((S, MOBA_HEADS, 2 * MOBA_HD), np.float32)
    kc4[:, :, MOBA_HD:] = kc[:, None, :]
    swap = np.zeros((LANES, LANES), np.float32)
    for j in range(half):
        swap[MLA_NOPE + half + j, MLA_NOPE + j] = 1.0
        swap[MLA_NOPE + j, MLA_NOPE + half + j] = 1.0
    return {"cos": jnp.asarray(cos), "sin": jnp.asarray(sin), "swap": jnp.asarray(swap, BF16),
            "kc": jnp.asarray(kc4.reshape(S, -1), BF16)}


def _head_pad(v):
    return jnp.concatenate([v, jnp.zeros((LANES - MLA_QK,), F32)])[None, :]


def _layer_weights(l, p):
    D = p["w_in"].shape[1]
    w_in = p["w_in"][l]
    o0 = MLA_Q_LORA + MLA_KV_LORA
    o1 = o0 + MLA_ROPE
    w = jnp.concatenate([w_in[:, :o0], jnp.zeros((D, MLA_NOPE), F32), w_in[:, o0:o1],
                         jnp.zeros((D, LANES - MLA_QK), F32), w_in[:, o1:]], axis=1)

    wq = p["mla_w_q_up"][l].reshape(MLA_Q_LORA, MLA_HEADS, MLA_QK)
    wq = jnp.pad(wq, ((0, 0), (0, 0), (0, LANES - MLA_QK))).reshape(MLA_Q_LORA, MLA_HEADS * LANES)
    wkv = p["mla_w_kv_up"][l].reshape(MLA_KV_LORA, MLA_HEADS, MLA_NOPE + MLA_V)
    wk = jnp.pad(wkv[:, :, :MLA_NOPE], ((0, 0), (0, 0), (0, LANES - MLA_NOPE)))
    wk = wk.reshape(MLA_KV_LORA, MLA_HEADS * LANES)
    wv = wkv[:, :, MLA_NOPE:].reshape(MLA_KV_LORA, MLA_HEADS * MLA_V)

    tri = jnp.tril(jnp.ones((SGU_CHUNK, SGU_CHUNK), dtype=bool))
    sgu_w = jnp.where(tri[None], p["sgu_w"][l], 0.0)
    sgu_b = jnp.repeat(p["sgu_b"][l].T, SGU_W // SGU_GROUPS, axis=1)
    pool_w = jax.scipy.linalg.block_diag(*[p["pool_w"][l, gi] for gi in range(len(POOL_WINDOWS))])

    return {
        "attn_norm": p["attn_norm"][l][None, :], "w_in": w.astype(BF16),
        "mla_q_norm": p["mla_q_norm"][l][None, :], "wq": wq.astype(BF16),
        "mla_kv_norm": p["mla_kv_norm"][l][None, :],
        "wkv": jnp.concatenate([wk, wv], axis=1).astype(BF16),
        "mla_qg": _head_pad(p["mla_q_gain"][l]), "mla_kg": _head_pad(p["mla_k_gain"][l]),
        "sgu_norm": p["sgu_norm"][l][None, :], "sgu_w": sgu_w.astype(BF16), "sgu_b": sgu_b,
        "pool_w": pool_w.astype(BF16), "pool_b": p["pool_b"][l][None, :],
        "pool_scale": p["pool_scale"][l][None, :],
        "moba_qg": jnp.tile(p["moba_q_gain"][l], 2)[None, :],
        "moba_kg": jnp.tile(p["moba_k_gain"][l], 2)[None, :],
        "w_gate": p["w_gate"][l].astype(BF16), "w_branch": p["w_branch"][l].astype(BF16),
        "w_out": p["w_out"][l].astype(BF16), "ffn_norm": p["ffn_norm"][l][None, :],
        "shared": (p["shared_w_gate"][l].astype(BF16), p["shared_w_up"][l].astype(BF16),
                   p["shared_w_down"][l].astype(BF16)),
    }


def kernel(x, attn_norm, w_in, mla_q_norm, mla_w_q_up, mla_kv_norm, mla_w_kv_up, mla_q_gain, mla_k_gain, sgu_norm, sgu_w, sgu_b, pool_w, pool_b, pool_scale, moba_q_gain, moba_k_gain, w_gate, w_branch, w_out, ffn_norm, router_w, router_bias, moe_w_gate, moe_w_up, moe_w_down, shared_w_gate, shared_w_up, shared_w_down):
    p = dict(attn_norm=attn_norm, w_in=w_in, mla_q_norm=mla_q_norm, mla_w_q_up=mla_w_q_up,
             mla_kv_norm=mla_kv_norm, mla_w_kv_up=mla_w_kv_up, mla_q_gain=mla_q_gain,
             mla_k_gain=mla_k_gain, sgu_norm=sgu_norm, sgu_w=sgu_w, sgu_b=sgu_b, pool_w=pool_w,
             pool_b=pool_b, pool_scale=pool_scale, moba_q_gain=moba_q_gain,
             moba_k_gain=moba_k_gain, w_gate=w_gate, w_branch=w_branch, w_out=w_out,
             ffn_norm=ffn_norm, moe_w_gate=moe_w_gate, moe_w_up=moe_w_up, moe_w_down=moe_w_down,
             shared_w_gate=shared_w_gate, shared_w_up=shared_w_up, shared_w_down=shared_w_down)
    B, S, D = x.shape
    T = B * S
    depth = attn_norm.shape[0]
    nkb = S // MOBA_BLOCK
    assert TQ == TK and S % TK == 0 and T % TM_MERGE == 0 and nkb <= MOBA_HD
    consts = _constants(S)
    router_w_t = router_w.T
    router_b = router_bias[:, None]
    assert min(TM_MOE, T) % TM_MERGE == 0 and N_EXPERTS % MOE_EXPERTS_PER_STEP == 0
    tri = jnp.asarray(np.triu(np.ones((MOE_SUB, MOE_SUB), np.float32), 1), BF16)

    ffn_w = (moe_w_gate.astype(BF16), moe_w_up.astype(BF16), moe_w_down.astype(BF16))
    x2 = x.reshape(T, D)
    for l in range(depth):
        lw = _layer_weights(l, p)
        qa, ka, va, yb, yc, qd, kd, vd, km = _in_proj(x2, lw, consts, S)
        ya = _flash(qa, ka.reshape(B, S // TK, TK, -1), va, MLA_HEADS, True, "mla_attn")

        kmean = km.reshape(B, nkb, MOBA_HEADS, MOBA_HD).transpose(0, 2, 1, 3)
        kmean = jnp.pad(kmean, ((0, 0), (0, 0), (0, MOBA_HD - nkb), (0, 0)))
        q_aug = _moba_select(qd, kmean)
        yd = _flash(q_aug, kd.reshape(B, S // TK, TK, -1), vd, MOBA_HEADS, False, "moba_attn")

        x1, h2, route, cnt = _merge(x2, ya, yb, yc, yd, lw, router_w_t, router_b)
        counts = cnt[:, :, :TM_MERGE // MOE_SUB].transpose(0, 2, 1).reshape(-1)
        x2 = _moe(x1, h2, route, counts, tri, lw["shared"], l, *ffn_w)
    return x2.reshape(B, S, D)
```

```python
import functools
import math

import jax
import jax.numpy as jnp
import numpy as np
from jax import lax
from jax.experimental import pallas as pl
from jax.experimental.pallas import tpu as pltpu

F32 = jnp.float32
BF16 = jnp.bfloat16
EPS = 1e-6
NEG = -1e30

LANES = 128
VMEM_LIMIT = 56 * 1024 * 1024

MLA_HEADS = 4
MLA_NOPE = 64
MLA_ROPE = 32
MLA_V = 64
MLA_Q_LORA = 256
MLA_KV_LORA = 128
ROPE_THETA = 10000.0
MLA_QK = MLA_NOPE + MLA_ROPE

SGU_GROUPS = 4
SGU_CHUNK = 128
SGU_W = 256

POOL_WINDOWS = (2, 4, 8, 16)
POOL_MAXW = 16
POOL_PAD = 8
assert POOL_WINDOWS == tuple(2 << k for k in range(len(POOL_WINDOWS))) and POOL_MAXW >= max(POOL_WINDOWS)
POOL_W = 256

MOBA_HEADS = 4
MOBA_HD = 64
MOBA_BLOCK = 256
MOBA_TOPK = 3

N_EXPERTS = 16
N_GROUPS = 4
EPG = 4

C_MLA = 0
C_SGU = 512
C_POOL = 1024
C_MOBA = 1280
C_END = 2048

TQ = 512
TK = 512
TM_IN = TQ
TM_MERGE = 512
TM_MOE = 2048
SEL_L = 2048
FLASH_UNROLLS = (4, 2, 1)
FLASH_AHEAD = 2
MOE_EXPERTS_PER_STEP = 4
MOE_SHARED_ROWS = 512
MOE_SUB = 256
MOE_CHUNK = 48
ONES_ROWS = 16


def _cparams(sem):
    return pltpu.CompilerParams(dimension_semantics=sem, vmem_limit_bytes=VMEM_LIMIT)


def _full(shape):
    n = len(shape)
    return pl.BlockSpec(shape, lambda *_: (0,) * n, pipeline_mode=pl.Buffered(1))


def _rms(x, g):
    return x * lax.rsqrt(jnp.mean(x * x, axis=-1, keepdims=True) + EPS) * g


def _dot(a, b):
    return jnp.dot(a, b, preferred_element_type=F32)


def _in_proj_kernel(x_ref, g_ref, w_ref, qn_ref, wq_ref, kvn_ref, wkv_ref,
                    qg_ref, kg_ref, swap_ref, cos_ref, sin_ref,
                    sn_ref, sw_ref, sb_ref, pw_ref, pb_ref, ps_ref,
                    mqg_ref, mkg_ref, kc_ref,
                    qa_ref, ka_ref, va_ref, yb_ref, yc_ref, qd_ref, kd_ref, vd_ref, km_ref,
                    halo_ref, buf_ref, tmp_ref, proj_scr, *, tiles_per_seq):
    tm = x_ref.shape[0]
    i = pl.program_id(0)

    @pl.when(i % tiles_per_seq == 0)
    def _():
        halo_ref[...] = jnp.zeros_like(halo_ref)

    x = x_ref[...]
    h = _rms(x, g_ref[...]).astype(BF16)
    lane = lax.broadcasted_iota(jnp.int32, (tm, LANES), 1)

    cm = _dot(h, w_ref[:, C_MLA:C_SGU])
    c_q = _rms(cm[:, :MLA_Q_LORA], qn_ref[...]).astype(BF16)
    c_kv = _rms(cm[:, MLA_Q_LORA:MLA_Q_LORA + MLA_KV_LORA], kvn_ref[...]).astype(BF16)
    kpe_grp = cm[:, MLA_Q_LORA + MLA_KV_LORA:]
    q = _dot(c_q, wq_ref[...])
    kv = _dot(c_kv, wkv_ref[...])
    va_ref[0, 0] = kv[:, 512:].T.astype(BF16)
    proj_scr[...] = _dot(h, w_ref[:, C_SGU:C_END])
    cos = cos_ref[...]
    sin = sin_ref[...]
    inv_d = 1.0 / MLA_QK
    rope_lo = lane < MLA_NOPE + MLA_ROPE // 2

    def head_norm_rope(t, gain):
        ms = jnp.sum(t * t, axis=-1, keepdims=True) * inv_d
        t = t * lax.rsqrt(ms + EPS) * gain
        rot = _dot(t.astype(BF16), swap_ref[...])
        return t * cos + rot * sin

    heads = [slice(hh * LANES, (hh + 1) * LANES) for hh in range(MLA_HEADS)]
    qh = [head_norm_rope(q[:, sl], qg_ref[...]) for sl in heads]
    qa_ref[0, 0] = jnp.concatenate(qh, axis=1).T.astype(BF16)
    kh = [head_norm_rope(kv[:, sl] + kpe_grp, kg_ref[...]) for sl in heads]
    ka_ref[...] = jnp.concatenate(kh, axis=1).astype(BF16)

    z = jax.nn.gelu(proj_scr[:, 0:C_POOL - C_SGU])
    u = z[:, :SGU_W]
    vv = _rms(z[:, SGU_W:], sn_ref[...]).astype(BF16)
    lane_grp = lax.broadcasted_iota(jnp.int32, (SGU_CHUNK, SGU_W), 1) // (SGU_W // SGU_GROUPS)
    for c in range(tm // SGU_CHUNK):
        vc = vv[c * SGU_CHUNK:(c + 1) * SGU_CHUNK, :]
        y = sb_ref[...]
        for gi in range(SGU_GROUPS):
            y = y + jnp.where(lane_grp == gi, _dot(sw_ref[gi], vc), 0.0)
        yb_ref[c * SGU_CHUNK:(c + 1) * SGU_CHUNK, :] = (
            u[c * SGU_CHUNK:(c + 1) * SGU_CHUNK, :] * y).astype(BF16)

    xp = proj_scr[:, C_POOL - C_SGU:C_MOBA - C_SGU]
    lead = POOL_PAD + POOL_MAXW
    ext = tm + POOL_MAXW
    buf_ref[0:POOL_PAD, :] = jnp.zeros((POOL_PAD, POOL_W), F32)
    tmp_ref[0:POOL_PAD, :] = jnp.zeros((POOL_PAD, POOL_W), F32)
    buf_ref[POOL_PAD:lead, :] = halo_ref[...]
    buf_ref[lead:, :] = xp
    halo_ref[...] = xp[tm - POOL_MAXW:, :]
    pos = (i % tiles_per_seq) * tm + lax.broadcasted_iota(jnp.int32, (tm, POOL_W), 0)
    lane_w = lax.broadcasted_iota(jnp.int32, (tm, POOL_W), 1) // (POOL_W // len(POOL_WINDOWS))
    pooled = jnp.zeros((tm, POOL_W), F32)
    src, dst = buf_ref, tmp_ref
    for gi, w in enumerate(POOL_WINDOWS):
        half_w = w // 2
        acc = src[POOL_PAD:POOL_PAD + ext, :] + src[POOL_PAD - half_w:POOL_PAD - half_w + ext, :]
        cnt = jnp.minimum(pos + 1, w).astype(F32)
        pooled = jnp.where(lane_w == gi, acc[POOL_MAXW:, :] / cnt, pooled)
        if gi + 1 < len(POOL_WINDOWS):
            dst[POOL_PAD:POOL_PAD + ext, :] = acc
            src, dst = dst, src
    p = (pooled - xp).astype(BF16)
    yc_ref[...] = ((_dot(p, pw_ref[...]) + pb_ref[...]) * ps_ref[...]).astype(BF16)

    cd = proj_scr[:, C_MOBA - C_SGU:]
    inv_hd = 1.0 / MOBA_HD
    lo = lane < MOBA_HD

    def pair_norm(t, gain):
        sq = t * t
        s_all = jnp.sum(sq, axis=-1, keepdims=True)
        s_lo = jnp.sum(jnp.where(lo, sq, 0.0), axis=-1, keepdims=True)
        ms = jnp.where(lo, s_lo, s_all - s_lo) * inv_hd
        return t * lax.rsqrt(ms + EPS) * gain

    pairs = [slice(g * LANES, (g + 1) * LANES) for g in range(MOBA_HEADS // 2)]
    qd = jnp.concatenate([pair_norm(cd[:, sl], mqg_ref[...]) for sl in pairs], axis=1)
    qd_ref[0] = qd.T
    kp = [pair_norm(cd[:, 256 + g * LANES:256 + (g + 1) * LANES], mkg_ref[...])
          for g in range(MOBA_HEADS // 2)]
    kd = jnp.concatenate(kp, axis=1)
    kh = []
    for hh in range(MOBA_HEADS):
        src = kp[hh // 2] if hh % 2 == 0 else pltpu.roll(kp[hh // 2], MOBA_HD, 1)
        kh.append(jnp.where(lo, src, kc_ref[:, hh * LANES:(hh + 1) * LANES].astype(F32)))
    kd_ref[...] = jnp.concatenate(kh, axis=1).astype(BF16)
    vd_ref[0, 0] = cd[:, 512:].T.astype(BF16)
    for c in range(tm // MOBA_BLOCK):
        km_ref[c] = jnp.mean(kd[c * MOBA_BLOCK:(c + 1) * MOBA_BLOCK, :], axis=0, keepdims=True)


def _in_proj(x2, lw, consts, S):
    T, D = x2.shape
    tm = TM_IN
    nt = T // tm
    tok = lambda w: pl.BlockSpec((tm, w), lambda i: (i, 0))
    pos_spec = pl.BlockSpec((tm, LANES), lambda i: (i % (S // tm), 0))
    ins = [
        (x2, tok(D)), (lw["attn_norm"], None), (lw["w_in"], None),
        (lw["mla_q_norm"], None), (lw["wq"], None), (lw["mla_kv_norm"], None), (lw["wkv"], None),
        (lw["mla_qg"], None), (lw["mla_kg"], None), (consts["swap"], None),
        (consts["cos"], pos_spec), (consts["sin"], pos_spec),
        (lw["sgu_norm"], None), (lw["sgu_w"], None), (lw["sgu_b"], None),
        (lw["pool_w"], None), (lw["pool_b"], None), (lw["pool_scale"], None),
        (lw["moba_qg"], None), (lw["moba_kg"], None),
        (consts["kc"], pl.BlockSpec((tm, 4 * LANES), lambda i: (i % (S // tm), 0))),
    ]
    arrays = [a for a, _ in ins]
    specs = [s if s is not None else _full(a.shape) for a, s in ins]
    nb = tm // MOBA_BLOCK
    B = T // S
    ns = S // tm
    tr = lambda f: pl.BlockSpec((1, 1, f, tm), lambda i: (i // ns, i % ns, 0, 0))
    out_shape = [
        jax.ShapeDtypeStruct((B, ns, 512, tm), BF16), jax.ShapeDtypeStruct((T, 512), BF16),
        jax.ShapeDtypeStruct((B, ns, 256, tm), BF16), jax.ShapeDtypeStruct((T, 256), BF16),
        jax.ShapeDtypeStruct((T, 256), BF16), jax.ShapeDtypeStruct((B, 256, S), F32),
        jax.ShapeDtypeStruct((T, 512), BF16), jax.ShapeDtypeStruct((B, ns, 256, tm), BF16),
        jax.ShapeDtypeStruct((T // MOBA_BLOCK, 1, 256), F32),
    ]
    out_specs = [tr(512), tok(512), tr(256), tok(256), tok(256),
                 pl.BlockSpec((1, 256, tm), lambda i: (i // ns, 0, i % ns)),
                 tok(512), tr(256), pl.BlockSpec((nb, 1, 256), lambda i: (i, 0, 0))]
    return pl.pallas_call(
        functools.partial(_in_proj_kernel, tiles_per_seq=S // tm),
        grid=(nt,), in_specs=specs, out_specs=out_specs, out_shape=out_shape,
        scratch_shapes=[pltpu.VMEM((POOL_MAXW, POOL_W), F32),
                        pltpu.VMEM((POOL_PAD + POOL_MAXW + tm, POOL_W), F32),
                        pltpu.VMEM((POOL_PAD + POOL_MAXW + tm, POOL_W), F32),
                        pltpu.VMEM((tm, C_END - C_SGU), F32)],
        compiler_params=_cparams(("arbitrary",)), name="in_proj",
    )(*arrays)


def _moba_select_kernel(q_ref, km_ref, o_ref, *, slope_log2_step):
    hh = pl.program_id(1)
    li = pl.program_id(2)
    q = q_ref[0]
    L = q.shape[1]
    nrow = km_ref.shape[2]
    g = jnp.dot(km_ref[0, 0], q, preferred_element_type=F32, precision=lax.Precision.HIGHEST)
    n_i = lax.broadcasted_iota(jnp.int32, (nrow, L), 0)
    own = (li * L + lax.broadcasted_iota(jnp.int32, (nrow, L), 1)) // MOBA_BLOCK
    g = jnp.where(n_i < own, g, NEG)
    sel = jnp.zeros((nrow, L), F32)
    for _ in range(MOBA_TOPK):
        mx = jnp.max(g, axis=0, keepdims=True)
        first = jnp.min(jnp.where(g == mx, n_i, nrow), axis=0, keepdims=True)
        hit = n_i == first
        sel = jnp.where(hit & (mx > 0.5 * NEG), 1.0, sel)
        g = jnp.where(hit, NEG, g)
    slope = jnp.exp2(-slope_log2_step * (hh + 1).astype(F32))
    rel = (n_i - own).astype(F32) * (slope * MOBA_BLOCK)
    bias = jnp.where((sel > 0.0) | (n_i == own), rel, NEG)
    bias = jnp.where(n_i == nrow - 1, slope, bias)
    qs = (q * (MOBA_HD ** -0.5)).astype(BF16)
    bias = bias.astype(BF16)
    tq = o_ref.shape[3]
    for t in range(L // tq):
        o_ref[0, t, 0:MOBA_HD, :] = qs[:, t * tq:(t + 1) * tq]
        o_ref[0, t, MOBA_HD:, :] = bias[:, t * tq:(t + 1) * tq]


def _moba_select(qgT, kmean):
    B, _, S = qgT.shape
    H, hd = MOBA_HEADS, MOBA_HD
    L = min(SEL_L, S)
    return pl.pallas_call(
        functools.partial(_moba_select_kernel, slope_log2_step=8.0 / MOBA_HEADS),
        grid=(B, H, S // L),
        in_specs=[pl.BlockSpec((1, hd, L), lambda b, h, l: (b, h, l)),
                  pl.BlockSpec((1, 1, 64, hd), lambda b, h, l: (b, h, 0, 0))],
        out_specs=pl.BlockSpec((1, L // TQ, 2 * hd, TQ), lambda b, h, l: (b, l, h, 0)),
        out_shape=jax.ShapeDtypeStruct((B, S // TQ, H * 2 * hd, TQ), BF16),
        compiler_params=_cparams(("arbitrary",) * 3), name="moba_select",
    )(qgT, kmean)


def _flash_kernel(q_ref, k_ref, v_ref, o_ref, s_scr, acc_scr, *, heads, use_exp2):
    H = heads
    tq = q_ref.shape[3]
    tk = k_ref.shape[2]
    dk = q_ref.shape[2] // H
    dv = v_ref.shape[2] // H
    ratio = tk // tq
    ex = jnp.exp2 if use_exp2 else jnp.exp
    i = pl.program_id(1)
    jd = i // ratio
    kpos = jd * tk + lax.broadcasted_iota(jnp.int32, (tk, tq), 0)
    qpos = i * tq + lax.broadcasted_iota(jnp.int32, (tk, tq), 1)
    causal = kpos <= qpos
    ones = jnp.ones((ONES_ROWS, tk), BF16)

    def scores(jt, h, masked):
        s = _dot(k_ref[0, jt, :, h * dk:(h + 1) * dk], q_ref[0, 0, h * dk:(h + 1) * dk, :])
        if masked:
            s = jnp.where(causal, s, NEG)
        s_scr[h] = s
        return jnp.max(s, axis=0, keepdims=True)

    def update(jt, h, cmax, m):
        m_new = jnp.maximum(m, cmax)
        alpha = ex(m - m_new)
        p = ex(s_scr[h] - m_new).astype(BF16)
        vt = jnp.concatenate([v_ref[0, jt, h * dv:(h + 1) * dv, :], ones], axis=0)
        acc_scr[h] = acc_scr[h] * alpha + _dot(vt, p)
        return m_new

    acc_scr[...] = jnp.zeros_like(acc_scr)
    ms = [jnp.full((1, tq), NEG, F32) for _ in range(H)]
    ahead = FLASH_AHEAD
    cmax = [scores(jd, h, True) for h in range(ahead)]
    for h in range(H):
        nh = h + ahead
        cmax.append(scores(jd, nh, True) if nh < H else scores(0, nh - H, False))
        ms[h] = update(jd, h, cmax.pop(0), ms[h])

    def tile_step(j, carry):
        ms, cmax = carry
        ms, cmax = list(ms), list(cmax)
        jn = jnp.minimum(j + 1, jd - 1)
        for h in range(H):
            nh = h + ahead
            cmax.append(scores(j, nh, False) if nh < H else scores(jn, nh - H, False))
            ms[h] = update(j, h, cmax.pop(0), ms[h])
        return tuple(ms), tuple(cmax)

    carry = (tuple(ms), tuple(cmax))
    done = 0
    for unroll in FLASH_UNROLLS:
        def body(t, c, unroll=unroll, done=done):
            for u in range(unroll):
                c = tile_step(done + unroll * t + u, c)
            return c

        trips = (jd - done) // unroll
        carry = lax.fori_loop(0, trips, body, carry)
        done = done + trips * unroll
    outs = []
    for h in range(H):
        acc = acc_scr[h]
        outs.append(acc[:dv] / acc[dv:dv + 1])
    o_ref[...] = jnp.concatenate(outs, axis=0).T.astype(BF16)


def _flash(qT, k, vT, heads, use_exp2, name):
    B, nq, hdk, tq = qT.shape
    nk, tk = k.shape[1:3]
    hdv = vT.shape[2]
    dva = hdv // heads + ONES_ROWS
    whole = lambda shp: pl.BlockSpec((1,) + shp, lambda b, i: (b, 0, 0, 0),
                                     pipeline_mode=pl.Buffered(1))
    return pl.pallas_call(
        functools.partial(_flash_kernel, heads=heads, use_exp2=use_exp2),
        grid=(B, nq),
        in_specs=[pl.BlockSpec((1, 1, hdk, tq), lambda b, i: (b, i, 0, 0)),
                  whole((nk, tk, hdk)), whole((nk, hdv, tk))],
        out_specs=pl.BlockSpec((tq, hdv), lambda b, i: (b * nq + i, 0)),
        out_shape=jax.ShapeDtypeStruct((B * nq * tq, hdv), BF16),
        scratch_shapes=[pltpu.VMEM((heads, tk, tq), F32), pltpu.VMEM((heads, dva, tq), F32)],
        compiler_params=_cparams(("arbitrary", "arbitrary")), name=name,
    )(qT, k, vT)


def _merge_kernel(x_ref, ya_ref, yb_ref, yc_ref, yd_ref, g_ref, wg_ref, wb_ref, wo_ref,
                  fg_ref, rw_ref, rb_ref, x1_ref, h2_ref, rt_ref, cnt_ref, h2_scr):
    @pl.when(pl.program_id(0) == 0)
    def _():
        h2_scr[...] = jnp.zeros_like(h2_scr)

    router = _router_stages(h2_scr, rw_ref, rb_ref, rt_ref, cnt_ref)
    x = x_ref[...]
    h = _rms(x, g_ref[...]).astype(BF16)
    merged = None
    for bi, y_ref in enumerate((ya_ref, yb_ref, yc_ref, yd_ref)):
        gate = 0.5 * jnp.tanh(0.5 * _dot(h, wg_ref[bi])) + 0.5
        t = gate * _dot(y_ref[...], wb_ref[bi])
        merged = t if merged is None else merged + t
        next(router)
    x1 = x + _dot(merged.astype(BF16), wo_ref[...])
    x1_ref[...] = x1
    h2_new = _rms(x1, fg_ref[...])
    h2_ref[...] = h2_new.astype(BF16)
    for _ in router:
        pass
    h2_scr[...] = h2_new


def _router_stages(h2_scr, rw_ref, rb_ref, rt_ref, cnt_ref):
    h2 = h2_scr[...]
    nt = (((1,), (1,)), ((), ()))
    h2_hi = h2.astype(BF16)
    h2_lo = (h2 - h2_hi.astype(F32)).astype(BF16)
    rw = rw_ref[...]
    rw_hi = rw.astype(BF16)
    rw_lo = (rw - rw_hi.astype(F32)).astype(BF16)
    logits = (lax.dot_general(rw_hi, h2_hi, nt, preferred_element_type=F32)
              + lax.dot_general(rw_hi, h2_lo, nt, preferred_element_type=F32)
              + lax.dot_general(rw_lo, h2_hi, nt, preferred_element_type=F32))
    yield
    scores = jax.nn.sigmoid(logits)
    biased = scores + rb_ref[...]
    s_rows = [scores[e:e + 1, :] for e in range(N_EXPERTS)]
    b_rows = [biased[e:e + 1, :] for e in range(N_EXPERTS)]

    def top2(vals):
        m1 = vals[0]
        i1 = jnp.zeros_like(m1, dtype=jnp.int32)
        for j in range(1, len(vals)):
            better = vals[j] > m1
            i1 = jnp.where(better, j, i1)
            m1 = jnp.where(better, vals[j], m1)
        m2 = None
        i2 = None
        for j in range(len(vals)):
            cand = jnp.where(i1 == j, NEG, vals[j])
            if m2 is None:
                m2, i2 = cand, jnp.zeros_like(i1)
            else:
                better = cand > m2
                i2 = jnp.where(better, j, i2)
                m2 = jnp.where(better, cand, m2)
        return m1, i1, m2, i2

    best = None
    for gi in range(N_GROUPS):
        bm1, bi1, bm2, bi2 = top2(b_rows[gi * EPG:(gi + 1) * EPG])
        gs = bm1 + bm2
        e1 = bi1 + gi * EPG
        e2 = bi2 + gi * EPG
        if best is None:
            best = (gs, e1, e2)
        else:
            better = gs > best[0]
            best = (jnp.where(better, gs, best[0]), jnp.where(better, e1, best[1]),
                    jnp.where(better, e2, best[2]))
        if gi % 2 == 1:
            yield
    _, e1, e2 = best
    w1 = jnp.zeros_like(s_rows[0])
    w2 = jnp.zeros_like(s_rows[0])
    for e in range(N_EXPERTS):
        w1 = jnp.where(e1 == e, s_rows[e], w1)
        w2 = jnp.where(e2 == e, s_rows[e], w2)
    tot = w1 + w2
    w1 = w1 / tot
    w2 = w2 / tot
    r8 = lax.broadcasted_iota(jnp.int32, (8, scores.shape[1]), 0)
    rt_ref[...] = jnp.where(r8 == 0, e1.astype(F32), jnp.where(r8 == 1, e2.astype(F32),
                            jnp.where(r8 == 2, w1, jnp.where(r8 == 3, w2, 0.0))))
    row = lax.broadcasted_iota(jnp.int32, scores.shape, 0)
    assigned = jnp.where(row == e1, 1.0, jnp.where(row == e2, 1.0, 0.0))
    lane = lax.broadcasted_iota(jnp.int32, (N_EXPERTS, LANES), 1)
    cnt = jnp.zeros((N_EXPERTS, LANES), F32)
    for k in range(scores.shape[1] // MOE_SUB):
        part = jnp.sum(assigned[:, k * MOE_SUB:(k + 1) * MOE_SUB], axis=1, keepdims=True)
        cnt = jnp.where(lane == k, part, cnt)
    cnt_ref[0] = cnt.astype(jnp.int32)
    yield


def _merge(x2, ya, yb, yc, yd, lw, router_w_t, router_b):
    T, D = x2.shape
    tm = TM_MERGE
    n = T // tm
    tok = lambda w: pl.BlockSpec((tm, w), lambda i: (jnp.minimum(i, n - 1), 0))
    consts = [lw["attn_norm"], lw["w_gate"], lw["w_branch"], lw["w_out"], lw["ffn_norm"],
              router_w_t, router_b]
    return pl.pallas_call(
        _merge_kernel, grid=(n + 1,),
        in_specs=[tok(D), tok(256), tok(256), tok(256), tok(256)] + [_full(a.shape) for a in consts],
        out_specs=[tok(D), tok(D), pl.BlockSpec((8, tm), lambda i: (0, jnp.maximum(i - 1, 0))),
                   pl.BlockSpec((1, N_EXPERTS, LANES), lambda i: (jnp.maximum(i - 1, 0), 0, 0))],
        out_shape=[jax.ShapeDtypeStruct((T, D), F32), jax.ShapeDtypeStruct((T, D), BF16),
                   jax.ShapeDtypeStruct((8, T), F32),
                   jax.ShapeDtypeStruct((n, N_EXPERTS, LANES), jnp.int32)],
        scratch_shapes=[pltpu.VMEM((tm, D), F32)],
        compiler_params=_cparams(("arbitrary",)), name="merge",
    )(x2, ya, yb, yc, yd, *consts)


def _moe_kernel(cnt_ref, x1_ref, h2_ref, rt_ref, tri_ref, sg_ref, su_ref, sd_ref,
                wg_ref, wu_ref, wd_ref, o_ref, rank_scr, gate_scr):
    i = pl.program_id(0)
    e = pl.program_id(1)
    tm = h2_ref.shape[0]

    def ffn(rows, scale, wg, wu, wd):
        a = jax.nn.silu(_dot(rows, wg)) * _dot(rows, wu)
        if scale is not None:
            a = a * scale
        return _dot(a.astype(BF16), wd)

    sub = tri_ref.shape[0]
    nsub = tm // sub
    spans = [slice(k * sub, (k + 1) * sub) for k in range(nsub)]

    @pl.when(e == 0)
    def _():
        for r in range(0, tm, MOE_SHARED_ROWS):
            rs = slice(r, r + MOE_SHARED_ROWS)
            o_ref[rs, :] = x1_ref[rs, :] + ffn(h2_ref[rs, :], None, sg_ref[...], su_ref[...],
                                               sd_ref[...])
        row = lax.broadcasted_iota(jnp.int32, (N_EXPERTS, tm), 0).astype(F32)
        e1, e2 = rt_ref[0:1, :], rt_ref[1:2, :]
        w1, w2 = rt_ref[2:3, :], rt_ref[3:4, :]
        assigned = jnp.where(row == e1, 1.0, jnp.where(row == e2, 1.0, 0.0))
        for sp in spans:
            before = _dot(assigned[:, sp].astype(BF16), tri_ref[...])
            rank_scr[:, sp] = jnp.where(assigned[:, sp] > 0.0, before, -1.0)
        gate_scr[...] = jnp.where(row == e1, w1, jnp.where(row == e2, w2, 0.0))

    @pl.when(e > 0)
    def _():
        subs = range(MOE_EXPERTS_PER_STEP)
        ex = [(e - 1) * MOE_EXPERTS_PER_STEP + s for s in subs]
        cnt = [cnt_ref[(i * nsub + k) * N_EXPERTS + x] for x in ex for k in range(nsub)]
        rank_row = [rank_scr[pl.ds(x, 1), :] for x in ex]
        gate_row = [gate_scr[pl.ds(x, 1), :] for x in ex]
        slot = lax.broadcasted_iota(jnp.int32, (MOE_CHUNK, sub), 0).astype(F32)

        def chunk(c, carry):
            base = slot + (c * MOE_CHUNK).astype(F32)
            onehot, rows, scale = [], [], []
            for sp in spans:
                hit = [rank_row[s][:, sp] == base for s in subs]
                onehot.append(jnp.concatenate(
                    [jnp.where(h, 1.0, 0.0).astype(BF16) for h in hit], axis=0))
                rows.append(_dot(onehot[-1], h2_ref[sp, :]).astype(BF16))
                scale.append([jnp.sum(jnp.where(hit[s], gate_row[s][:, sp], 0.0),
                                      axis=1, keepdims=True) for s in subs])
            y = []
            for s in subs:
                mine = slice(s * MOE_CHUNK, (s + 1) * MOE_CHUNK)
                y.append(ffn(jnp.concatenate([r[mine] for r in rows], axis=0),
                             jnp.concatenate([sc[s] for sc in scale], axis=0),
                             wg_ref[0, s], wu_ref[0, s], wd_ref[0, s]).astype(BF16))
            for k, sp in enumerate(spans):
                rows_k = slice(k * MOE_CHUNK, (k + 1) * MOE_CHUNK)
                o_ref[sp, :] += lax.dot_general(
                    onehot[k], jnp.concatenate([y[s][rows_k] for s in subs], axis=0),
                    (((0,), (0,)), ((), ())), preferred_element_type=F32)
            return carry

        most = functools.reduce(jnp.maximum, cnt)
        lax.fori_loop(0, (most + MOE_CHUNK - 1) // MOE_CHUNK, chunk, 0)


def _moe(x1, h2, route, counts, tri, shared, layer, wg, wu, wd):
    T, D = x1.shape
    tm = min(TM_MOE, T)
    sub = tri.shape[0]
    _, ne, _, ff = wg.shape
    const = lambda shp: pl.BlockSpec(shp, lambda i, e, c: (0, 0), pipeline_mode=pl.Buffered(1))
    eps = MOE_EXPERTS_PER_STEP
    expert = lambda shp: pl.BlockSpec((1, eps) + shp,
                                      lambda i, e, c: (layer, jnp.maximum(e - 1, 0), 0, 0))
    grid_spec = pltpu.PrefetchScalarGridSpec(
        num_scalar_prefetch=1, grid=(T // tm, ne // eps + 1),
        in_specs=[pl.BlockSpec((tm, D), lambda i, e, c: (i, 0), pipeline_mode=pl.Buffered(1)),
                  pl.BlockSpec((tm, D), lambda i, e, c: (i, 0), pipeline_mode=pl.Buffered(1)),
                  pl.BlockSpec((8, tm), lambda i, e, c: (0, i)),
                  const((sub, sub)), const((D, ff)), const((D, ff)), const((ff, D)),
                  expert((D, ff)), expert((D, ff)), expert((ff, D))],
        out_specs=pl.BlockSpec((tm, D), lambda i, e, c: (i, 0), pipeline_mode=pl.Buffered(1)),
        scratch_shapes=[pltpu.VMEM((N_EXPERTS, tm), F32), pltpu.VMEM((N_EXPERTS, tm), F32)])
    return pl.pallas_call(
        _moe_kernel, grid_spec=grid_spec,
        out_shape=jax.ShapeDtypeStruct((T, D), F32),
        compiler_params=_cparams(("arbitrary", "arbitrary")), name="moe",
    )(counts, x1, h2, route, tri, *shared, wg, wu, wd)


def _constants(S):
    half = MLA_ROPE // 2
    inv = ROPE_THETA ** (-np.arange(half, dtype=np.float64) / half)
    ang = np.arange(S, dtype=np.float64)[:, None] * inv[None, :]
    cos = np.ones((S, LANES), np.float32)
    cos[:, MLA_NOPE:MLA_NOPE + half] = np.cos(ang)
    cos[:, MLA_NOPE + half:MLA_QK] = np.cos(ang)
    sin = np.zeros((S, LANES), np.float32)
    sin[:, MLA_NOPE:MLA_NOPE + half] = -np.sin(ang)
    sin[:, MLA_NOPE + half:MLA_QK] = np.sin(ang)

    nkb = S // MOBA_BLOCK
    kpos = np.arange(S)
    kc = np.zeros((S, MOBA_HD), np.float32)
    blk = kpos // MOBA_BLOCK
    m = blk < min(nkb, MOBA_HD) - 1
    kc[kpos[m], blk[m]] = 1.0
    kc[:, MOBA_HD - 1] = kpos % MOBA_BLOCK
    kc4 = np.zeros((S, MOBA_HEADS, 2 * MOBA_HD), np.float32)
    kc4[:, :, MOBA_HD:] = kc[:, None, :]
    swap = np.zeros((LANES, LANES), np.float32)
    for j in range(half):
        swap[MLA_NOPE + half + j, MLA_NOPE + j] = 1.0
        swap[MLA_NOPE + j, MLA_NOPE + half + j] = 1.0
    return {"cos": jnp.asarray(cos), "sin": jnp.asarray(sin), "swap": jnp.asarray(swap, BF16),
            "kc": jnp.asarray(kc4.reshape(S, -1), BF16)}


def _head_pad(v):
    return jnp.concatenate([v, jnp.zeros((LANES - MLA_QK,), F32)])[None, :]


def _layer_weights(l, p):
    D = p["w_in"].shape[1]
    w_in = p["w_in"][l]
    o0 = MLA_Q_LORA + MLA_KV_LORA
    o1 = o0 + MLA_ROPE
    w = jnp.concatenate([w_in[:, :o0], jnp.zeros((D, MLA_NOPE), F32), w_in[:, o0:o1],
                         jnp.zeros((D, LANES - MLA_QK), F32), w_in[:, o1:]], axis=1)

    wq = p["mla_w_q_up"][l].reshape(MLA_Q_LORA, MLA_HEADS, MLA_QK)
    wq = jnp.pad(wq, ((0, 0), (0, 0), (0, LANES - MLA_QK))).reshape(MLA_Q_LORA, MLA_HEADS * LANES)
    wkv = p["mla_w_kv_up"][l].reshape(MLA_KV_LORA, MLA_HEADS, MLA_NOPE + MLA_V)
    wk = jnp.pad(wkv[:, :, :MLA_NOPE], ((0, 0), (0, 0), (0, LANES - MLA_NOPE)))
    wk = wk.reshape(MLA_KV_LORA, MLA_HEADS * LANES)
    wv = wkv[:, :, MLA_NOPE:].reshape(MLA_KV_LORA, MLA_HEADS * MLA_V)

    tri = jnp.tril(jnp.ones((SGU_CHUNK, SGU_CHUNK), dtype=bool))
    sgu_w = jnp.where(tri[None], p["sgu_w"][l], 0.0)
    sgu_b = jnp.repeat(p["sgu_b"][l].T, SGU_W // SGU_GROUPS, axis=1)
    pool_w = jax.scipy.linalg.block_diag(*[p["pool_w"][l, gi] for gi in range(len(POOL_WINDOWS))])

    return {
        "attn_norm": p["attn_norm"][l][None, :], "w_in": w.astype(BF16),
        "mla_q_norm": p["mla_q_norm"][l][None, :], "wq": wq.astype(BF16),
        "mla_kv_norm": p["mla_kv_norm"][l][None, :],
        "wkv": jnp.concatenate([wk, wv], axis=1).astype(BF16),
        "mla_qg": _head_pad(p["mla_q_gain"][l] * ((MLA_QK ** -0.5) * math.log2(math.e))),
        "mla_kg": _head_pad(p["mla_k_gain"][l]),
        "sgu_norm": p["sgu_norm"][l][None, :], "sgu_w": sgu_w.astype(BF16), "sgu_b": sgu_b,
        "pool_w": pool_w.astype(BF16), "pool_b": p["pool_b"][l][None, :],
        "pool_scale": p["pool_scale"][l][None, :],
        "moba_qg": jnp.tile(p["moba_q_gain"][l], 2)[None, :],
        "moba_kg": jnp.tile(p["moba_k_gain"][l], 2)[None, :],
        "w_gate": p["w_gate"][l].astype(BF16), "w_branch": p["w_branch"][l].astype(BF16),
        "w_out": p["w_out"][l].astype(BF16), "ffn_norm": p["ffn_norm"][l][None, :],
        "shared": (p["shared_w_gate"][l].astype(BF16), p["shared_w_up"][l].astype(BF16),
                   p["shared_w_down"][l].astype(BF16)),
    }


def kernel(x, attn_norm, w_in, mla_q_norm, mla_w_q_up, mla_kv_norm, mla_w_kv_up, mla_q_gain, mla_k_gain, sgu_norm, sgu_w, sgu_b, pool_w, pool_b, pool_scale, moba_q_gain, moba_k_gain, w_gate, w_branch, w_out, ffn_norm, router_w, router_bias, moe_w_gate, moe_w_up, moe_w_down, shared_w_gate, shared_w_up, shared_w_down):
    p = dict(attn_norm=attn_norm, w_in=w_in, mla_q_norm=mla_q_norm, mla_w_q_up=mla_w_q_up,
             mla_kv_norm=mla_kv_norm, mla_w_kv_up=mla_w_kv_up, mla_q_gain=mla_q_gain,
             mla_k_gain=mla_k_gain, sgu_norm=sgu_norm, sgu_w=sgu_w, sgu_b=sgu_b, pool_w=pool_w,
             pool_b=pool_b, pool_scale=pool_scale, moba_q_gain=moba_q_gain,
             moba_k_gain=moba_k_gain, w_gate=w_gate, w_branch=w_branch, w_out=w_out,
             ffn_norm=ffn_norm, moe_w_gate=moe_w_gate, moe_w_up=moe_w_up, moe_w_down=moe_w_down,
             shared_w_gate=shared_w_gate, shared_w_up=shared_w_up, shared_w_down=shared_w_down)
    B, S, D = x.shape
    T = B * S
    depth = attn_norm.shape[0]
    nkb = S // MOBA_BLOCK
    assert TQ == TK and S % TK == 0 and T % TM_MERGE == 0 and nkb <= MOBA_HD
    consts = _constants(S)
    router_w_t = router_w.T
    router_b = router_bias[:, None]
    assert min(TM_MOE, T) % TM_MERGE == 0 and N_EXPERTS % MOE_EXPERTS_PER_STEP == 0
    tri = jnp.asarray(np.triu(np.ones((MOE_SUB, MOE_SUB), np.float32), 1), BF16)

    ffn_w = (moe_w_gate.astype(BF16), moe_w_up.astype(BF16), moe_w_down.astype(BF16))
    x2 = x.reshape(T, D)
    for l in range(depth):
        lw = _layer_weights(l, p)
        qa, ka, va, yb, yc, qd, kd, vd, km = _in_proj(x2, lw, consts, S)
        ya = _flash(qa, ka.reshape(B, S // TK, TK, -1), va, MLA_HEADS, True, "mla_attn")

        kmean = km.reshape(B, nkb, MOBA_HEADS, MOBA_HD).transpose(0, 2, 1, 3)
        kmean = jnp.pad(kmean, ((0, 0), (0, 0), (0, MOBA_HD - nkb), (0, 0)))
        q_aug = _moba_select(qd, kmean)
        yd = _flash(q_aug, kd.reshape(B, S // TK, TK, -1), vd, MOBA_HEADS, False, "moba_attn")

        x1, h2, route, cnt = _merge(x2, ya, yb, yc, yd, lw, router_w_t, router_b)
        counts = cnt[:, :, :TM_MERGE // MOE_SUB].transpose(0, 2, 1).reshape(-1)
        x2 = _moe(x1, h2, route, counts, tri, lw["shared"], l, *ffn_w)
    return x2.reshape(B, S, D)
```

```python
import functools
import math

import jax
import jax.numpy as jnp
import numpy as np
from jax import lax
from jax.experimental import pallas as pl
from jax.experimental.pallas import tpu as pltpu

F32 = jnp.float32
BF16 = jnp.bfloat16
EPS = 1e-6
NEG = -1e30

LANES = 128
VMEM_LIMIT = 56 * 1024 * 1024

MLA_HEADS = 4
MLA_NOPE = 64
MLA_ROPE = 32
MLA_V = 64
MLA_Q_LORA = 256
MLA_KV_LORA = 128
ROPE_THETA = 10000.0
MLA_QK = MLA_NOPE + MLA_ROPE

SGU_GROUPS = 4
SGU_CHUNK = 128
SGU_W = 256

POOL_WINDOWS = (2, 4, 8, 16)
POOL_MAXW = 16
POOL_PAD = 8
assert POOL_WINDOWS == tuple(2 << k for k in range(len(POOL_WINDOWS))) and POOL_MAXW >= max(POOL_WINDOWS)
POOL_W = 256

MOBA_HEADS = 4
MOBA_HD = 64
MOBA_BLOCK = 256
MOBA_TOPK = 3

N_EXPERTS = 16
N_GROUPS = 4
EPG = 4

C_MLA = 0
C_SGU = 512
C_POOL = 1024
C_MOBA = 1280
C_END = 2048

TQ = 512
TK = 512
TM_IN = TQ
TM_MERGE = 512
TM_MOE = 2048
SEL_L = 2048
FLASH_UNROLLS = (4, 2, 1)
FLASH_AHEAD = 2
MOE_EXPERTS_PER_STEP = 4
MOE_SHARED_ROWS = 512
MOE_SUB = 256
MOE_CHUNK = 48
ONES_ROWS = 16


def _cparams(sem):
    return pltpu.CompilerParams(dimension_semantics=sem, vmem_limit_bytes=VMEM_LIMIT)


def _full(shape):
    n = len(shape)
    return pl.BlockSpec(shape, lambda *_: (0,) * n, pipeline_mode=pl.Buffered(1))


def _rms(x, g):
    return x * lax.rsqrt(jnp.mean(x * x, axis=-1, keepdims=True) + EPS) * g


def _dot(a, b):
    return jnp.dot(a, b, preferred_element_type=F32)


def _in_proj_kernel(x_ref, g_ref, w_ref, qn_ref, wq_ref, kvn_ref, wkv_ref,
                    qg_ref, kg_ref, swap_ref, cos_ref, sin_ref,
                    sn_ref, sw_ref, sb_ref, pw_ref, pb_ref, ps_ref,
                    mqg_ref, mkg_ref, kc_ref,
                    qa_ref, ka_ref, va_ref, yb_ref, yc_ref, qd_ref, kd_ref, vd_ref, km_ref,
                    halo_ref, buf_ref, tmp_ref, proj_scr, *, tiles_per_seq):
    tm = x_ref.shape[0]
    i = pl.program_id(0)

    @pl.when(i % tiles_per_seq == 0)
    def _():
        halo_ref[...] = jnp.zeros_like(halo_ref)

    x = x_ref[...]
    h = _rms(x, g_ref[...]).astype(BF16)
    lane = lax.broadcasted_iota(jnp.int32, (tm, LANES), 1)

    cm = _dot(h, w_ref[:, C_MLA:C_SGU])
    c_q = _rms(cm[:, :MLA_Q_LORA], qn_ref[...]).astype(BF16)
    c_kv = _rms(cm[:, MLA_Q_LORA:MLA_Q_LORA + MLA_KV_LORA], kvn_ref[...]).astype(BF16)
    kpe_grp = cm[:, MLA_Q_LORA + MLA_KV_LORA:]
    q = _dot(c_q, wq_ref[...])
    kv = _dot(c_kv, wkv_ref[...])
    va_ref[0, 0] = kv[:, 512:].T.astype(BF16)
    proj_scr[...] = _dot(h, w_ref[:, C_SGU:C_END])
    cos = cos_ref[...]
    sin = sin_ref[...]
    inv_d = 1.0 / MLA_QK
    rope_lo = lane < MLA_NOPE + MLA_ROPE // 2

    def head_norm_rope(t, gain):
        ms = jnp.sum(t * t, axis=-1, keepdims=True) * inv_d
        t = t * lax.rsqrt(ms + EPS) * gain
        rot = _dot(t.astype(BF16), swap_ref[...])
        return t * cos + rot * sin

    heads = [slice(hh * LANES, (hh + 1) * LANES) for hh in range(MLA_HEADS)]
    qh = [head_norm_rope(q[:, sl], qg_ref[...]) for sl in heads]
    qa_ref[0, 0] = jnp.concatenate(qh, axis=1).T.astype(BF16)
    kh = [head_norm_rope(kv[:, sl] + kpe_grp, kg_ref[...]) for sl in heads]
    ka_ref[...] = jnp.concatenate(kh, axis=1).astype(BF16)

    z = jax.nn.gelu(proj_scr[:, 0:C_POOL - C_SGU])
    u = z[:, :SGU_W]
    vv = _rms(z[:, SGU_W:], sn_ref[...]).astype(BF16)
    lane_grp = lax.broadcasted_iota(jnp.int32, (SGU_CHUNK, SGU_W), 1) // (SGU_W // SGU_GROUPS)
    for c in range(tm // SGU_CHUNK):
        vc = vv[c * SGU_CHUNK:(c + 1) * SGU_CHUNK, :]
        y = sb_ref[...]
        for gi in range(SGU_GROUPS):
            y = y + jnp.where(lane_grp == gi, _dot(sw_ref[gi], vc), 0.0)
        yb_ref[c * SGU_CHUNK:(c + 1) * SGU_CHUNK, :] = (
            u[c * SGU_CHUNK:(c + 1) * SGU_CHUNK, :] * y).astype(BF16)

    xp = proj_scr[:, C_POOL - C_SGU:C_MOBA - C_SGU]
    lead = POOL_PAD + POOL_MAXW
    ext = tm + POOL_MAXW
    buf_ref[0:POOL_PAD, :] = jnp.zeros((POOL_PAD, POOL_W), F32)
    tmp_ref[0:POOL_PAD, :] = jnp.zeros((POOL_PAD, POOL_W), F32)
    buf_ref[POOL_PAD:lead, :] = halo_ref[...]
    buf_ref[lead:, :] = xp
    halo_ref[...] = xp[tm - POOL_MAXW:, :]
    pos = (i % tiles_per_seq) * tm + lax.broadcasted_iota(jnp.int32, (tm, POOL_W), 0)
    lane_w = lax.broadcasted_iota(jnp.int32, (tm, POOL_W), 1) // (POOL_W // len(POOL_WINDOWS))
    pooled = jnp.zeros((tm, POOL_W), F32)
    src, dst = buf_ref, tmp_ref
    for gi, w in enumerate(POOL_WINDOWS):
        half_w = w // 2
        acc = src[POOL_PAD:POOL_PAD + ext, :] + src[POOL_PAD - half_w:POOL_PAD - half_w + ext, :]
        cnt = jnp.minimum(pos + 1, w).astype(F32)
        pooled = jnp.where(lane_w == gi, acc[POOL_MAXW:, :] / cnt, pooled)
        if gi + 1 < len(POOL_WINDOWS):
            dst[POOL_PAD:POOL_PAD + ext, :] = acc
            src, dst = dst, src
    p = (pooled - xp).astype(BF16)
    yc_ref[...] = ((_dot(p, pw_ref[...]) + pb_ref[...]) * ps_ref[...]).astype(BF16)

    cd = proj_scr[:, C_MOBA - C_SGU:]
    inv_hd = 1.0 / MOBA_HD
    lo = lane < MOBA_HD

    def pair_norm(t, gain):
        sq = t * t
        s_all = jnp.sum(sq, axis=-1, keepdims=True)
        s_lo = jnp.sum(jnp.where(lo, sq, 0.0), axis=-1, keepdims=True)
        ms = jnp.where(lo, s_lo, s_all - s_lo) * inv_hd
        return t * lax.rsqrt(ms + EPS) * gain

    pairs = [slice(g * LANES, (g + 1) * LANES) for g in range(MOBA_HEADS // 2)]
    qd = jnp.concatenate([pair_norm(cd[:, sl], mqg_ref[...]) for sl in pairs], axis=1)
    qd_ref[0] = qd.T
    kp = [pair_norm(cd[:, 256 + g * LANES:256 + (g + 1) * LANES], mkg_ref[...])
          for g in range(MOBA_HEADS // 2)]
    kd = jnp.concatenate(kp, axis=1)
    kh = []
    for hh in range(MOBA_HEADS):
        src = kp[hh // 2] if hh % 2 == 0 else pltpu.roll(kp[hh // 2], MOBA_HD, 1)
        kh.append(jnp.where(lo, src, kc_ref[:, hh * LANES:(hh + 1) * LANES].astype(F32)))
    kd_ref[...] = jnp.concatenate(kh, axis=1).astype(BF16)
    vd_ref[0, 0] = cd[:, 512:].T.astype(BF16)
    for c in range(tm // MOBA_BLOCK):
        km_ref[c] = jnp.mean(kd[c * MOBA_BLOCK:(c + 1) * MOBA_BLOCK, :], axis=0, keepdims=True)


def _in_proj(x2, lw, consts, S):
    T, D = x2.shape
    tm = TM_IN
    nt = T // tm
    tok = lambda w: pl.BlockSpec((tm, w), lambda i: (i, 0))
    pos_spec = pl.BlockSpec((tm, LANES), lambda i: (i % (S // tm), 0))
    ins = [
        (x2, tok(D)), (lw["attn_norm"], None), (lw["w_in"], None),
        (lw["mla_q_norm"], None), (lw["wq"], None), (lw["mla_kv_norm"], None), (lw["wkv"], None),
        (lw["mla_qg"], None), (lw["mla_kg"], None), (consts["swap"], None),
        (consts["cos"], pos_spec), (consts["sin"], pos_spec),
        (lw["sgu_norm"], None), (lw["sgu_w"], None), (lw["sgu_b"], None),
        (lw["pool_w"], None), (lw["pool_b"], None), (lw["pool_scale"], None),
        (lw["moba_qg"], None), (lw["moba_kg"], None),
        (consts["kc"], pl.BlockSpec((tm, 4 * LANES), lambda i: (i % (S // tm), 0))),
    ]
    arrays = [a for a, _ in ins]
    specs = [s if s is not None else _full(a.shape) for a, s in ins]
    nb = tm // MOBA_BLOCK
    B = T // S
    ns = S // tm
    tr = lambda f: pl.BlockSpec((1, 1, f, tm), lambda i: (i // ns, i % ns, 0, 0))
    out_shape = [
        jax.ShapeDtypeStruct((B, ns, 512, tm), BF16), jax.ShapeDtypeStruct((T, 512), BF16),
        jax.ShapeDtypeStruct((B, ns, 256, tm), BF16), jax.ShapeDtypeStruct((T, 256), BF16),
        jax.ShapeDtypeStruct((T, 256), BF16), jax.ShapeDtypeStruct((B, 256, S), F32),
        jax.ShapeDtypeStruct((T, 512), BF16), jax.ShapeDtypeStruct((B, ns, 256, tm), BF16),
        jax.ShapeDtypeStruct((T // MOBA_BLOCK, 1, 256), F32),
    ]
    out_specs = [tr(512), tok(512), tr(256), tok(256), tok(256),
                 pl.BlockSpec((1, 256, tm), lambda i: (i // ns, 0, i % ns)),
                 tok(512), tr(256), pl.BlockSpec((nb, 1, 256), lambda i: (i, 0, 0))]
    return pl.pallas_call(
        functools.partial(_in_proj_kernel, tiles_per_seq=S // tm),
        grid=(nt,), in_specs=specs, out_specs=out_specs, out_shape=out_shape,
        scratch_shapes=[pltpu.VMEM((POOL_MAXW, POOL_W), F32),
                        pltpu.VMEM((POOL_PAD + POOL_MAXW + tm, POOL_W), F32),
                        pltpu.VMEM((POOL_PAD + POOL_MAXW + tm, POOL_W), F32),
                        pltpu.VMEM((tm, C_END - C_SGU), F32)],
        compiler_params=_cparams(("arbitrary",)), name="in_proj",
    )(*arrays)


def _moba_select_kernel(q_ref, km_ref, o_ref, *, slope_log2_step):
    hh = pl.program_id(1)
    li = pl.program_id(2)
    q = q_ref[0]
    L = q.shape[1]
    nrow = km_ref.shape[2]
    g = jnp.dot(km_ref[0, 0], q, preferred_element_type=F32, precision=lax.Precision.HIGHEST)
    n_i = lax.broadcasted_iota(jnp.int32, (nrow, L), 0)
    own = (li * L + lax.broadcasted_iota(jnp.int32, (nrow, L), 1)) // MOBA_BLOCK
    g = jnp.where(n_i < own, g, NEG)
    sel = jnp.zeros((nrow, L), F32)
    for _ in range(MOBA_TOPK):
        mx = jnp.max(g, axis=0, keepdims=True)
        first = jnp.min(jnp.where(g == mx, n_i, nrow), axis=0, keepdims=True)
        hit = n_i == first
        sel = jnp.where(hit & (mx > 0.5 * NEG), 1.0, sel)
        g = jnp.where(hit, NEG, g)
    slope = jnp.exp2(-slope_log2_step * (hh + 1).astype(F32))
    rel = (n_i - own).astype(F32) * (slope * MOBA_BLOCK)
    bias = jnp.where((sel > 0.0) | (n_i == own), rel, NEG)
    bias = jnp.where(n_i == nrow - 1, slope, bias)
    qs = (q * (MOBA_HD ** -0.5)).astype(BF16)
    bias = bias.astype(BF16)
    tq = o_ref.shape[3]
    for t in range(L // tq):
        o_ref[0, t, 0:MOBA_HD, :] = qs[:, t * tq:(t + 1) * tq]
        o_ref[0, t, MOBA_HD:, :] = bias[:, t * tq:(t + 1) * tq]


def _moba_select(qgT, kmean):
    B, _, S = qgT.shape
    H, hd = MOBA_HEADS, MOBA_HD
    L = min(SEL_L, S)
    return pl.pallas_call(
        functools.partial(_moba_select_kernel, slope_log2_step=8.0 / MOBA_HEADS),
        grid=(B, H, S // L),
        in_specs=[pl.BlockSpec((1, hd, L), lambda b, h, l: (b, h, l)),
                  pl.BlockSpec((1, 1, 64, hd), lambda b, h, l: (b, h, 0, 0))],
        out_specs=pl.BlockSpec((1, L // TQ, 2 * hd, TQ), lambda b, h, l: (b, l, h, 0)),
        out_shape=jax.ShapeDtypeStruct((B, S // TQ, H * 2 * hd, TQ), BF16),
        compiler_params=_cparams(("arbitrary",) * 3), name="moba_select",
    )(qgT, kmean)


def _flash_kernel(q_ref, k_ref, v_ref, o_ref, s_scr, acc_scr, *, heads, use_exp2):
    H = heads
    tq = q_ref.shape[3]
    tk = k_ref.shape[2]
    dk = q_ref.shape[2] // H
    dv = v_ref.shape[2] // H
    ratio = tk // tq
    ex = jnp.exp2 if use_exp2 else jnp.exp
    i = pl.program_id(1)
    jd = i // ratio
    kpos = jd * tk + lax.broadcasted_iota(jnp.int32, (tk, tq), 0)
    qpos = i * tq + lax.broadcasted_iota(jnp.int32, (tk, tq), 1)
    causal = kpos <= qpos
    ones = jnp.ones((ONES_ROWS, tk), BF16)

    def scores(jt, h, masked):
        s = _dot(k_ref[0, jt, :, h * dk:(h + 1) * dk], q_ref[0, 0, h * dk:(h + 1) * dk, :])
        if masked:
            s = jnp.where(causal, s, NEG)
        s_scr[h] = s
        return jnp.max(s, axis=0, keepdims=True)

    def update(jt, h, cmax, m):
        m_new = jnp.maximum(m, cmax)
        alpha = ex(m - m_new)
        p = ex(s_scr[h] - m_new).astype(BF16)
        vt = jnp.concatenate([v_ref[0, jt, h * dv:(h + 1) * dv, :], ones], axis=0)
        acc_scr[h] = acc_scr[h] * alpha + _dot(vt, p)
        return m_new

    acc_scr[...] = jnp.zeros_like(acc_scr)
    ms = [jnp.full((1, tq), NEG, F32) for _ in range(H)]
    ahead = FLASH_AHEAD
    cmax = [scores(jd, h, True) for h in range(ahead)]
    for h in range(H):
        nh = h + ahead
        cmax.append(scores(jd, nh, True) if nh < H else scores(0, nh - H, False))
        ms[h] = update(jd, h, cmax.pop(0), ms[h])

    def tile_step(j, carry):
        ms, cmax = carry
        ms, cmax = list(ms), list(cmax)
        jn = jnp.minimum(j + 1, jd - 1)
        for h in range(H):
            nh = h + ahead
            cmax.append(scores(j, nh, False) if nh < H else scores(jn, nh - H, False))
            ms[h] = update(j, h, cmax.pop(0), ms[h])
        return tuple(ms), tuple(cmax)

    carry = (tuple(ms), tuple(cmax))
    done = 0
    for unroll in FLASH_UNROLLS:
        def body(t, c, unroll=unroll, done=done):
            for u in range(unroll):
                c = tile_step(done + unroll * t + u, c)
            return c

        trips = (jd - done) // unroll
        carry = lax.fori_loop(0, trips, body, carry)
        done = done + trips * unroll
    outs = []
    for h in range(H):
        acc = acc_scr[h]
        outs.append(acc[:dv] / acc[dv:dv + 1])
    o_ref[...] = jnp.concatenate(outs, axis=0).T.astype(BF16)


def _flash(qT, k, vT, heads, use_exp2, name):
    B, nq, hdk, tq = qT.shape
    nk, tk = k.shape[1:3]
    hdv = vT.shape[2]
    dva = hdv // heads + ONES_ROWS
    whole = lambda shp: pl.BlockSpec((1,) + shp, lambda b, i: (b, 0, 0, 0),
                                     pipeline_mode=pl.Buffered(1))
    return pl.pallas_call(
        functools.partial(_flash_kernel, heads=heads, use_exp2=use_exp2),
        grid=(B, nq),
        in_specs=[pl.BlockSpec((1, 1, hdk, tq), lambda b, i: (b, i, 0, 0)),
                  whole((nk, tk, hdk)), whole((nk, hdv, tk))],
        out_specs=pl.BlockSpec((tq, hdv), lambda b, i: (b * nq + i, 0)),
        out_shape=jax.ShapeDtypeStruct((B * nq * tq, hdv), BF16),
        scratch_shapes=[pltpu.VMEM((heads, tk, tq), F32), pltpu.VMEM((heads, dva, tq), F32)],
        compiler_params=_cparams(("arbitrary", "arbitrary")), name=name,
    )(qT, k, vT)


def _merge_kernel(x_ref, ya_ref, yb_ref, yc_ref, yd_ref, g_ref, wg_ref, wb_ref, wo_ref,
                  fg_ref, rw_ref, rb_ref, x1_ref, h2_ref, rt_ref, cnt_ref, h2_scr):
    @pl.when(pl.program_id(0) == 0)
    def _():
        h2_scr[...] = jnp.zeros_like(h2_scr)

    router = _router_stages(h2_scr, rw_ref, rb_ref, rt_ref, cnt_ref)
    x = x_ref[...]
    h = _rms(x, g_ref[...]).astype(BF16)
    merged = None
    for bi, y_ref in enumerate((ya_ref, yb_ref, yc_ref, yd_ref)):
        gate = 0.5 * jnp.tanh(0.5 * _dot(h, wg_ref[bi])) + 0.5
        t = gate * _dot(y_ref[...], wb_ref[bi])
        merged = t if merged is None else merged + t
        next(router)
    x1 = x + _dot(merged.astype(BF16), wo_ref[...])
    x1_ref[...] = x1
    h2_new = _rms(x1, fg_ref[...])
    h2_ref[...] = h2_new.astype(BF16)
    for _ in router:
        pass
    h2_scr[...] = h2_new


def _router_stages(h2_scr, rw_ref, rb_ref, rt_ref, cnt_ref):
    h2 = h2_scr[...]
    nt = (((1,), (1,)), ((), ()))
    h2_hi = h2.astype(BF16)
    h2_lo = (h2 - h2_hi.astype(F32)).astype(BF16)
    rw = rw_ref[...]
    rw_hi = rw.astype(BF16)
    rw_lo = (rw - rw_hi.astype(F32)).astype(BF16)
    logits = (lax.dot_general(rw_hi, h2_hi, nt, preferred_element_type=F32)
              + lax.dot_general(rw_hi, h2_lo, nt, preferred_element_type=F32)
              + lax.dot_general(rw_lo, h2_hi, nt, preferred_element_type=F32))
    yield
    scores = jax.nn.sigmoid(logits)
    biased = scores + rb_ref[...]
    s_rows = [scores[e:e + 1, :] for e in range(N_EXPERTS)]
    b_rows = [biased[e:e + 1, :] for e in range(N_EXPERTS)]

    def top2(vals):
        m1 = vals[0]
        i1 = jnp.zeros_like(m1, dtype=jnp.int32)
        for j in range(1, len(vals)):
            better = vals[j] > m1
            i1 = jnp.where(better, j, i1)
            m1 = jnp.where(better, vals[j], m1)
        m2 = None
        i2 = None
        for j in range(len(vals)):
            cand = jnp.where(i1 == j, NEG, vals[j])
            if m2 is None:
                m2, i2 = cand, jnp.zeros_like(i1)
            else:
                better = cand > m2
                i2 = jnp.where(better, j, i2)
                m2 = jnp.where(better, cand, m2)
        return m1, i1, m2, i2

    best = None
    for gi in range(N_GROUPS):
        bm1, bi1, bm2, bi2 = top2(b_rows[gi * EPG:(gi + 1) * EPG])
        gs = bm1 + bm2
        e1 = bi1 + gi * EPG
        e2 = bi2 + gi * EPG
        if best is None:
            best = (gs, e1, e2)
        else:
            better = gs > best[0]
            best = (jnp.where(better, gs, best[0]), jnp.where(better, e1, best[1]),
                    jnp.where(better, e2, best[2]))
        if gi % 2 == 1:
            yield
    _, e1, e2 = best
    w1 = jnp.zeros_like(s_rows[0])
    w2 = jnp.zeros_like(s_rows[0])
    for e in range(N_EXPERTS):
        w1 = jnp.where(e1 == e, s_rows[e], w1)
        w2 = jnp.where(e2 == e, s_rows[e], w2)
    tot = w1 + w2
    w1 = w1 / tot
    w2 = w2 / tot
    r8 = lax.broadcasted_iota(jnp.int32, (8, scores.shape[1]), 0)
    rt_ref[...] = jnp.where(r8 == 0, e1.astype(F32), jnp.where(r8 == 1, e2.astype(F32),
                            jnp.where(r8 == 2, w1, jnp.where(r8 == 3, w2, 0.0))))
    row = lax.broadcasted_iota(jnp.int32, scores.shape, 0)
    assigned = jnp.where(row == e1, 1.0, jnp.where(row == e2, 1.0, 0.0))
    lane = lax.broadcasted_iota(jnp.int32, (N_EXPERTS, LANES), 1)
    cnt = jnp.zeros((N_EXPERTS, LANES), F32)
    for k in range(scores.shape[1] // MOE_SUB):
        part = jnp.sum(assigned[:, k * MOE_SUB:(k + 1) * MOE_SUB], axis=1, keepdims=True)
        cnt = jnp.where(lane == k, part, cnt)
    cnt_ref[0] = cnt.astype(jnp.int32)
    yield


def _merge(x2, ya, yb, yc, yd, lw, router_w_t, router_b):
    T, D = x2.shape
    tm = TM_MERGE
    n = T // tm
    tok = lambda w: pl.BlockSpec((tm, w), lambda i: (jnp.minimum(i, n - 1), 0))
    consts = [lw["attn_norm"], lw["w_gate"], lw["w_branch"], lw["w_out"], lw["ffn_norm"],
              router_w_t, router_b]
    return pl.pallas_call(
        _merge_kernel, grid=(n + 1,),
        in_specs=[tok(D), tok(256), tok(256), tok(256), tok(256)] + [_full(a.shape) for a in consts],
        out_specs=[tok(D), tok(D), pl.BlockSpec((8, tm), lambda i: (0, jnp.maximum(i - 1, 0))),
                   pl.BlockSpec((1, N_EXPERTS, LANES), lambda i: (jnp.maximum(i - 1, 0), 0, 0))],
        out_shape=[jax.ShapeDtypeStruct((T, D), F32), jax.ShapeDtypeStruct((T, D), BF16),
                   jax.ShapeDtypeStruct((8, T), F32),
                   jax.ShapeDtypeStruct((n, N_EXPERTS, LANES), jnp.int32)],
        scratch_shapes=[pltpu.VMEM((tm, D), F32)],
        compiler_params=_cparams(("arbitrary",)), name="merge",
    )(x2, ya, yb, yc, yd, *consts)


def _moe_kernel(cnt_ref, x1_ref, h2_ref, rt_ref, tri_ref, sg_ref, su_ref, sd_ref,
                wg_ref, wu_ref, wd_ref, o_ref, rank_scr, gate_scr):
    i = pl.program_id(0)
    e = pl.program_id(1)
    tm = h2_ref.shape[0]

    def ffn(rows, scale, wg, wu, wd):
        g = _dot(rows, wg)
        a = g * (0.5 * jnp.tanh(0.5 * g) + 0.5) * _dot(rows, wu)
        if scale is not None:
            a = a * scale
        return _dot(a.astype(BF16), wd)

    sub = tri_ref.shape[0]
    nsub = tm // sub
    spans = [slice(k * sub, (k + 1) * sub) for k in range(nsub)]

    @pl.when(e == 0)
    def _():
        for r in range(0, tm, MOE_SHARED_ROWS):
            rs = slice(r, r + MOE_SHARED_ROWS)
            o_ref[rs, :] = x1_ref[rs, :] + ffn(h2_ref[rs, :], None, sg_ref[...], su_ref[...],
                                               sd_ref[...])
        row = lax.broadcasted_iota(jnp.int32, (N_EXPERTS, tm), 0).astype(F32)
        e1, e2 = rt_ref[0:1, :], rt_ref[1:2, :]
        w1, w2 = rt_ref[2:3, :], rt_ref[3:4, :]
        assigned = jnp.where(row == e1, 1.0, jnp.where(row == e2, 1.0, 0.0))
        for sp in spans:
            before = _dot(assigned[:, sp].astype(BF16), tri_ref[...])
            rank_scr[:, sp] = jnp.where(assigned[:, sp] > 0.0, before, -1.0)
        gate_scr[...] = jnp.where(row == e1, w1, jnp.where(row == e2, w2, 0.0))

    @pl.when(e > 0)
    def _():
        subs = range(MOE_EXPERTS_PER_STEP)
        ex = [(e - 1) * MOE_EXPERTS_PER_STEP + s for s in subs]
        cnt = [cnt_ref[(i * nsub + k) * N_EXPERTS + x] for x in ex for k in range(nsub)]
        rank_row = [rank_scr[pl.ds(x, 1), :] for x in ex]
        gate_row = [gate_scr[pl.ds(x, 1), :] for x in ex]
        slot = lax.broadcasted_iota(jnp.int32, (MOE_CHUNK, sub), 0).astype(F32)

        def chunk(c, carry):
            base = slot + (c * MOE_CHUNK).astype(F32)
            onehot, rows, scale = [], [], []
            for sp in spans:
                hit = [rank_row[s][:, sp] == base for s in subs]
                onehot.append(jnp.concatenate(
                    [jnp.where(h, 1.0, 0.0).astype(BF16) for h in hit], axis=0))
                rows.append(_dot(onehot[-1], h2_ref[sp, :]).astype(BF16))
                scale.append([jnp.sum(jnp.where(hit[s], gate_row[s][:, sp], 0.0),
                                      axis=1, keepdims=True) for s in subs])
            y = []
            for s in subs:
                mine = slice(s * MOE_CHUNK, (s + 1) * MOE_CHUNK)
                y.append(ffn(jnp.concatenate([r[mine] for r in rows], axis=0),
                             jnp.concatenate([sc[s] for sc in scale], axis=0),
                             wg_ref[0, s], wu_ref[0, s], wd_ref[0, s]).astype(BF16))
            for k, sp in enumerate(spans):
                rows_k = slice(k * MOE_CHUNK, (k + 1) * MOE_CHUNK)
                o_ref[sp, :] += lax.dot_general(
                    onehot[k], jnp.concatenate([y[s][rows_k] for s in subs], axis=0),
                    (((0,), (0,)), ((), ())), preferred_element_type=F32)
            return carry

        most = functools.reduce(jnp.maximum, cnt)
        lax.fori_loop(0, (most + MOE_CHUNK - 1) // MOE_CHUNK, chunk, 0)


def _moe(x1, h2, route, counts, tri, shared, layer, wg, wu, wd):
    T, D = x1.shape
    tm = min(TM_MOE, T)
    sub = tri.shape[0]
    _, ne, _, ff = wg.shape
    const = lambda shp: pl.BlockSpec(shp, lambda i, e, c: (0, 0), pipeline_mode=pl.Buffered(1))
    eps = MOE_EXPERTS_PER_STEP
    expert = lambda shp: pl.BlockSpec((1, eps) + shp,
                                      lambda i, e, c: (layer, jnp.maximum(e - 1, 0), 0, 0))
    grid_spec = pltpu.PrefetchScalarGridSpec(
        num_scalar_prefetch=1, grid=(T // tm, ne // eps + 1),
        in_specs=[pl.BlockSpec((tm, D), lambda i, e, c: (i, 0), pipeline_mode=pl.Buffered(1)),
                  pl.BlockSpec((tm, D), lambda i, e, c: (i, 0), pipeline_mode=pl.Buffered(1)),
                  pl.BlockSpec((8, tm), lambda i, e, c: (0, i)),
                  const((sub, sub)), const((D, ff)), const((D, ff)), const((ff, D)),
                  expert((D, ff)), expert((D, ff)), expert((ff, D))],
        out_specs=pl.BlockSpec((tm, D), lambda i, e, c: (i, 0), pipeline_mode=pl.Buffered(1)),
        scratch_shapes=[pltpu.VMEM((N_EXPERTS, tm), F32), pltpu.VMEM((N_EXPERTS, tm), F32)])
    return pl.pallas_call(
        _moe_kernel, grid_spec=grid_spec,
        out_shape=jax.ShapeDtypeStruct((T, D), F32),
        compiler_params=_cparams(("arbitrary", "arbitrary")), name="moe",
    )(counts, x1, h2, route, tri, *shared, wg, wu, wd)


def _constants(S):
    half = MLA_ROPE // 2
    inv = ROPE_THETA ** (-np.arange(half, dtype=np.float64) / half)
    ang = np.arange(S, dtype=np.float64)[:, None] * inv[None, :]
    cos = np.ones((S, LANES), np.float32)
    cos[:, MLA_NOPE:MLA_NOPE + half] = np.cos(ang)
    cos[:, MLA_NOPE + half:MLA_QK] = np.cos(ang)
    sin = np.zeros((S, LANES), np.float32)
    sin[:, MLA_NOPE:MLA_NOPE + half] = -np.sin(ang)
    sin[:, MLA_NOPE + half:MLA_QK] = np.sin(ang)

    nkb = S // MOBA_BLOCK
    kpos = np.arange(S)
    kc = np.zeros((S, MOBA_HD), np.float32)
    blk = kpos // MOBA_BLOCK
    m = blk < min(nkb, MOBA_HD) - 1
    kc[kpos[m], blk[m]] = 1.0
    kc[:, MOBA_HD - 1] = kpos % MOBA_BLOCK
    kc4 = np.zeros((S, MOBA_HEADS, 2 * MOBA_HD), np.float32)
    kc4[:, :, MOBA_HD:] = kc[:, None, :]
    swap = np.zeros((LANES, LANES), np.float32)
    for j in range(half):
        swap[MLA_NOPE + half + j, MLA_NOPE + j] = 1.0
        swap[MLA_NOPE + j, MLA_NOPE + half + j] = 1.0
    return {"cos": jnp.asarray(cos), "sin": jnp.asarray(sin), "swap": jnp.asarray(swap, BF16),
            "kc": jnp.asarray(kc4.reshape(S, -1), BF16)}


def _head_pad(v):
    return jnp.concatenate([v, jnp.zeros((LANES - MLA_QK,), F32)])[None, :]


def _layer_weights(l, p):
    D = p["w_in"].shape[1]
    w_in = p["w_in"][l]
    o0 = MLA_Q_LORA + MLA_KV_LORA
    o1 = o0 + MLA_ROPE
    w = jnp.concatenate([w_in[:, :o0], jnp.zeros((D, MLA_NOPE), F32), w_in[:, o0:o1],
                         jnp.zeros((D, LANES - MLA_QK), F32), w_in[:, o1:]], axis=1)

    wq = p["mla_w_q_up"][l].reshape(MLA_Q_LORA, MLA_HEADS, MLA_QK)
    wq = jnp.pad(wq, ((0, 0), (0, 0), (0, LANES - MLA_QK))).reshape(MLA_Q_LORA, MLA_HEADS * LANES)
    wkv = p["mla_w_kv_up"][l].reshape(MLA_KV_LORA, MLA_HEADS, MLA_NOPE + MLA_V)
    wk = jnp.pad(wkv[:, :, :MLA_NOPE], ((0, 0), (0, 0), (0, LANES - MLA_NOPE)))
    wk = wk.reshape(MLA_KV_LORA, MLA_HEADS * LANES)
    wv = wkv[:, :, MLA_NOPE:].reshape(MLA_KV_LORA, MLA_HEADS * MLA_V)

    tri = jnp.tril(jnp.ones((SGU_CHUNK, SGU_CHUNK), dtype=bool))
    sgu_w = jnp.where(tri[None], p["sgu_w"][l], 0.0)
    sgu_b = jnp.repeat(p["sgu_b"][l].T, SGU_W // SGU_GROUPS, axis=1)
    pool_w = jax.scipy.linalg.block_diag(*[p["pool_w"][l, gi] for gi in range(len(POOL_WINDOWS))])

    return {
        "attn_norm": p["attn_norm"][l][None, :], "w_in": w.astype(BF16),
        "mla_q_norm": p["mla_q_norm"][l][None, :], "wq": wq.astype(BF16),
        "mla_kv_norm": p["mla_kv_norm"][l][None, :],
        "wkv": jnp.concatenate([wk, wv], axis=1).astype(BF16),
        "mla_qg": _head_pad(p["mla_q_gain"][l] * ((MLA_QK ** -0.5) * math.log2(math.e))),
        "mla_kg": _head_pad(p["mla_k_gain"][l]),
        "sgu_norm": p["sgu_norm"][l][None, :], "sgu_w": sgu_w.astype(BF16), "sgu_b": sgu_b,
        "pool_w": pool_w.astype(BF16), "pool_b": p["pool_b"][l][None, :],
        "pool_scale": p["pool_scale"][l][None, :],
        "moba_qg": jnp.tile(p["moba_q_gain"][l], 2)[None, :],
        "moba_kg": jnp.tile(p["moba_k_gain"][l], 2)[None, :],
        "w_gate": p["w_gate"][l].astype(BF16), "w_branch": p["w_branch"][l].astype(BF16),
        "w_out": p["w_out"][l].astype(BF16), "ffn_norm": p["ffn_norm"][l][None, :],
        "shared": (p["shared_w_gate"][l].astype(BF16), p["shared_w_up"][l].astype(BF16),
                   p["shared_w_down"][l].astype(BF16)),
    }


def kernel(x, attn_norm, w_in, mla_q_norm, mla_w_q_up, mla_kv_norm, mla_w_kv_up, mla_q_gain, mla_k_gain, sgu_norm, sgu_w, sgu_b, pool_w, pool_b, pool_scale, moba_q_gain, moba_k_gain, w_gate, w_branch, w_out, ffn_norm, router_w, router_bias, moe_w_gate, moe_w_up, moe_w_down, shared_w_gate, shared_w_up, shared_w_down):
    p = dict(attn_norm=attn_norm, w_in=w_in, mla_q_norm=mla_q_norm, mla_w_q_up=mla_w_q_up,
             mla_kv_norm=mla_kv_norm, mla_w_kv_up=mla_w_kv_up, mla_q_gain=mla_q_gain,
             mla_k_gain=mla_k_gain, sgu_norm=sgu_norm, sgu_w=sgu_w, sgu_b=sgu_b, pool_w=pool_w,
             pool_b=pool_b, pool_scale=pool_scale, moba_q_gain=moba_q_gain,
             moba_k_gain=moba_k_gain, w_gate=w_gate, w_branch=w_branch, w_out=w_out,
             ffn_norm=ffn_norm, moe_w_gate=moe_w_gate, moe_w_up=moe_w_up, moe_w_down=moe_w_down,
             shared_w_gate=shared_w_gate, shared_w_up=shared_w_up, shared_w_down=shared_w_down)
    B, S, D = x.shape
    T = B * S
    depth = attn_norm.shape[0]
    nkb = S // MOBA_BLOCK
    assert TQ == TK and S % TK == 0 and T % TM_MERGE == 0 and nkb <= MOBA_HD
    consts = _constants(S)
    router_w_t = router_w.T
    router_b = router_bias[:, None]
    assert min(TM_MOE, T) % TM_MERGE == 0 and N_EXPERTS % MOE_EXPERTS_PER_STEP == 0
    tri = jnp.asarray(np.triu(np.ones((MOE_SUB, MOE_SUB), np.float32), 1), BF16)

    ffn_w = (moe_w_gate.astype(BF16), moe_w_up.astype(BF16), moe_w_down.astype(BF16))
    x2 = x.reshape(T, D)
    for l in range(depth):
        lw = _layer_weights(l, p)
        qa, ka, va, yb, yc, qd, kd, vd, km = _in_proj(x2, lw, consts, S)
        ya = _flash(qa, ka.reshape(B, S // TK, TK, -1), va, MLA_HEADS, True, "mla_attn")

        kmean = km.reshape(B, nkb, MOBA_HEADS, MOBA_HD).transpose(0, 2, 1, 3)
        kmean = jnp.pad(kmean, ((0, 0), (0, 0), (0, MOBA_HD - nkb), (0, 0)))
        q_aug = _moba_select(qd, kmean)
        yd = _flash(q_aug, kd.reshape(B, S // TK, TK, -1), vd, MOBA_HEADS, False, "moba_attn")

        x1, h2, route, cnt = _merge(x2, ya, yb, yc, yd, lw, router_w_t, router_b)
        counts = cnt[:, :, :TM_MERGE // MOE_SUB].transpose(0, 2, 1).reshape(-1)
        x2 = _moe(x1, h2, route, counts, tri, lw["shared"], l, *ffn_w)
    return x2.reshape(B, S, D)
```
